```python
import jax, jax.numpy as jnp
from jax import lax
import numpy as np

D_MODEL = 1024
BATCH = 32
SEQ = 2048
DEPTH = 1

HGRN_HEADS = 4
HGRN_DK = 128
HGRN_DV = 128
HGRN_WIDTH = HGRN_HEADS * HGRN_DK
CHUNK = 64
CONV_WIDTH = 512
CONV_K = 3
N_EXPERTS = 32
TOP_K = 4
D_FF = 1024
SWIGLU_LIMIT = 7.0
SWIGLU_ALPHA = 1.702
MOE_BLOCK = 128
EPS = 1e-6
IN_SIZES = (HGRN_WIDTH, HGRN_WIDTH, HGRN_WIDTH, HGRN_WIDTH, CONV_WIDTH, CONV_WIDTH, CONV_WIDTH, D_MODEL, D_MODEL)
IN_COLS = 4 * HGRN_WIDTH + 3 * CONV_WIDTH + 2 * D_MODEL

kernel_name = "hgrn2_shortconv_gated_moe_adaln"


def rms_norm(x, w):
    xf = x.astype(jnp.float32)
    y = xf * lax.rsqrt(jnp.mean(xf * xf, axis=-1, keepdims=True) + EPS)
    return (y * w.astype(jnp.float32)).astype(x.dtype)


def split_points():
    return [int(v) for v in np.cumsum(IN_SIZES)[:-1]]


def hgrn2_lower_bound(lb_table, layer):
    p = jax.nn.softmax(lb_table.astype(jnp.float32), axis=0)
    return jnp.cumsum(p, axis=0)[layer]


def hgrn2_mixer(q, f_pre, v, g_out, lb, norm_w):
    dt = q.dtype
    bsz, s_len, _ = q.shape
    n_chunks = s_len // CHUNK
    log_f = jnp.logaddexp(jnp.log(lb), jnp.log1p(-lb) + jax.nn.log_sigmoid(f_pre.astype(jnp.float32)))
    k = -jnp.expm1(log_f)

    def to_chunks(t, d):
        return t.astype(jnp.float32).reshape(bsz, n_chunks, CHUNK, HGRN_HEADS, d).transpose(1, 0, 3, 2, 4)

    qc, kc, gc = to_chunks(q, HGRN_DK), to_chunks(k, HGRN_DK), to_chunks(log_f, HGRN_DK)
    vc = to_chunks(v, HGRN_DV)
    causal = jnp.tril(jnp.ones((CHUNK, CHUNK), dtype=bool))[None, None, :, :, None]

    def step(state, inp):
        q_c, k_c, v_c, g_c = inp
        b = jnp.cumsum(g_c, axis=2)
        o_inter = jnp.einsum('bhtk,bhkv->bhtv', q_c * jnp.exp(b), state)
        diff = b[:, :, :, None, :] - b[:, :, None, :, :]
        decay = jnp.exp(jnp.where(causal, diff, -jnp.inf))
        scores = jnp.einsum('bhtsk,bhsk->bhts', q_c[:, :, :, None, :] * decay, k_c)
        o = o_inter + jnp.einsum('bhts,bhsv->bhtv', scores, v_c)
        b_last = b[:, :, -1:, :]
        state = jnp.exp(b_last[:, :, 0, :])[..., None] * state + jnp.einsum('bhsk,bhsv->bhkv', k_c * jnp.exp(b_last - b), v_c)
        return state, o

    state0 = jnp.zeros((bsz, HGRN_HEADS, HGRN_DK, HGRN_DV), jnp.float32)
    _, o = lax.scan(step, state0, (qc, kc, vc, gc))
    o = o.transpose(1, 0, 3, 2, 4).reshape(bsz, s_len, HGRN_HEADS, HGRN_DV)
    o = o * lax.rsqrt(jnp.mean(o * o, axis=-1, keepdims=True) + EPS) * norm_w.astype(jnp.float32)
    g = g_out.astype(jnp.float32).reshape(bsz, s_len, HGRN_HEADS, HGRN_DV)
    o = o * jax.nn.silu(g)
    return o.reshape(bsz, s_len, HGRN_WIDTH).astype(dt)


def short_conv_mixer(b_gate, c_gate, h, conv_w):
    u = c_gate * h
    rhs = conv_w.astype(u.dtype)[:, None, :]
    y = lax.conv_general_dilated(u, rhs, window_strides=(1,), padding=((CONV_K - 1, 0),),
                                 dimension_numbers=('NWC', 'WIO', 'NWC'), feature_group_count=CONV_WIDTH)
    return b_gate * y


def moe_ffn(u, w_router, b_router, w1, b1, w2, b2):
    bsz, s_len, d = u.shape
    n_tok = bsz * s_len
    n_assign = n_tok * TOP_K
    n_blocks = n_assign // MOE_BLOCK + N_EXPERTS
    n_rows = n_blocks * MOE_BLOCK
    xt = u.reshape(n_tok, d)
    logits = (xt @ w_router + b_router).astype(jnp.float32)
    top_vals, top_idx = lax.top_k(logits, TOP_K)
    probs = jax.nn.softmax(top_vals, axis=-1)
    flat_e = top_idx.reshape(n_assign).astype(jnp.int32)
    order = jnp.argsort(flat_e)
    e_sorted = flat_e[order]
    tok_sorted = (order // TOP_K).astype(jnp.int32)
    w_sorted = probs.reshape(n_assign)[order]
    sizes = jnp.bincount(flat_e, length=N_EXPERTS).astype(jnp.int32)
    padded = (sizes + MOE_BLOCK - 1) // MOE_BLOCK * MOE_BLOCK
    start = jnp.cumsum(sizes) - sizes
    pad_end = jnp.cumsum(padded)
    pad_start = pad_end - padded
    dest = pad_start[e_sorted] + jnp.arange(n_assign, dtype=jnp.int32) - start[e_sorted]
    row_tok = jnp.full((n_rows,), n_tok, jnp.int32).at[dest].set(tok_sorted)
    row_w = jnp.zeros((n_rows,), jnp.float32).at[dest].set(w_sorted)
    block_start = jnp.arange(n_blocks, dtype=jnp.int32) * MOE_BLOCK
    block_e = jnp.minimum(jnp.searchsorted(pad_end, block_start, side='right'), N_EXPERTS - 1).astype(jnp.int32)
    x_pad = jnp.concatenate([xt, jnp.zeros((1, d), xt.dtype)], axis=0)

    def expert_block(args):
        toks, e = args
        hs = x_pad[toks]
        hid = hs @ w1[e] + b1[e]
        glu, lin = jnp.split(hid, 2, axis=-1)
        glu = jnp.minimum(glu, SWIGLU_LIMIT)
        lin = jnp.clip(lin, -SWIGLU_LIMIT, SWIGLU_LIMIT)
        act = glu * jax.nn.sigmoid(SWIGLU_ALPHA * glu) * (lin + 1.0)
        return act @ w2[e] + b2[e]

    out = lax.map(expert_block, (row_tok.reshape(n_blocks, MOE_BLOCK), block_e))
    out = out.reshape(n_rows, d) * row_w[:, None].astype(out.dtype)
    y = jnp.zeros((n_tok + 1, d), out.dtype).at[row_tok].add(out)[:n_tok]
    return y.reshape(bsz, s_len, d)


def setup_inputs(seed: int = 0) -> dict:
    key = jax.random.key(seed)
    ks = jax.random.split(key, 20)
    nrm = jax.random.normal
    f32 = jnp.float32
    D = D_MODEL
    return {
        "x": nrm(ks[0], (BATCH, SEQ, D), f32),
        "c": nrm(ks[1], (BATCH, D), f32),
        "w_ada": nrm(ks[2], (DEPTH, D, 6 * D), f32) * D ** -0.5,
        "b_ada": nrm(ks[3], (DEPTH, 6 * D), f32) * 0.02,
        "norm_mix_w": 1.0 + 0.02 * nrm(ks[4], (DEPTH, D), f32),
        "w_in": nrm(ks[5], (DEPTH, D, IN_COLS), f32) * D ** -0.5,
        "hgrn_lower_bounds": nrm(ks[6], (DEPTH + 1, HGRN_WIDTH), f32) * 0.5,
        "hgrn_norm_w": 1.0 + 0.02 * nrm(ks[7], (DEPTH, HGRN_DV), f32),
        "conv_w": nrm(ks[8], (DEPTH, CONV_K, CONV_WIDTH), f32) * CONV_K ** -0.5,
        "w_hgrn_out": nrm(ks[9], (DEPTH, HGRN_WIDTH, D), f32) * HGRN_WIDTH ** -0.5,
        "w_conv_out": nrm(ks[10], (DEPTH, CONV_WIDTH, D), f32) * CONV_WIDTH ** -0.5,
        "w_mix_out": nrm(ks[11], (DEPTH, D, D), f32) * D ** -0.5,
        "norm_ffn_w": 1.0 + 0.02 * nrm(ks[12], (DEPTH, D), f32),
        "w_router": nrm(ks[13], (DEPTH, D, N_EXPERTS), f32) * D ** -0.5,
        "b_router": nrm(ks[14], (DEPTH, N_EXPERTS), f32) * 0.01,
        "w1": nrm(ks[15], (DEPTH, N_EXPERTS, D, 2 * D_FF), f32) * D ** -0.5,
        "b1": nrm(ks[16], (DEPTH, N_EXPERTS, 2 * D_FF), f32) * 0.01,
        "w2": nrm(ks[17], (DEPTH, N_EXPERTS, D_FF, D), f32) * D_FF ** -0.5,
        "b2": nrm(ks[18], (DEPTH, N_EXPERTS, D), f32) * 0.01,
        "norm_final_w": 1.0 + 0.02 * nrm(ks[19], (D,), f32),
    }


def reference(x, c, w_ada, b_ada, norm_mix_w, w_in, hgrn_lower_bounds, hgrn_norm_w, conv_w,
              w_hgrn_out, w_conv_out, w_mix_out, norm_ffn_w, w_router, b_router, w1, b1, w2, b2,
              norm_final_w):
    bsz, s_len, d = x.shape
    silu_c = jax.nn.silu(c)
    cuts = split_points()
    h = x
    for layer in range(DEPTH):
        mod = (silu_c @ w_ada[layer] + b_ada[layer]).reshape(bsz, 6, 1, d)
        shift_m, scale_m, gate_m = mod[:, 0], mod[:, 1], mod[:, 2]
        shift_f, scale_f, gate_f = mod[:, 3], mod[:, 4], mod[:, 5]

        u = rms_norm(h, norm_mix_w[layer]) * (1.0 + scale_m) + shift_m
        proj = u @ w_in[layer]
        q, f_pre, i_v, g_o, cb, cc, ch, ga, gb = jnp.split(proj, cuts, axis=-1)
        lb = hgrn2_lower_bound(hgrn_lower_bounds, layer)
        y_a = hgrn2_mixer(q, f_pre, i_v, g_o, lb, hgrn_norm_w[layer]) @ w_hgrn_out[layer]
        y_b = short_conv_mixer(cb, cc, ch, conv_w[layer]) @ w_conv_out[layer]
        merged = jax.nn.sigmoid(ga) * y_a + jax.nn.sigmoid(gb) * y_b
        h = h + gate_m * (merged @ w_mix_out[layer])

        u = rms_norm(h, norm_ffn_w[layer]) * (1.0 + scale_f) + shift_f
        h = h + gate_f * moe_ffn(u, w_router[layer], b_router[layer], w1[layer], b1[layer], w2[layer], b2[layer])
    return rms_norm(h, norm_final_w)
```

```python
import functools

import jax
import jax.numpy as jnp
from jax import lax
from jax.experimental import pallas as pl
from jax.experimental.pallas import tpu as pltpu

F32 = jnp.float32
BF16 = jnp.bfloat16

D_MODEL = 1024
HGRN_HEADS = 4
HEAD_DIM = 128
HGRN_WIDTH = HGRN_HEADS * HEAD_DIM
CONV_WIDTH = 512
CONV_K = 3
CHUNK = 64
SUB = 16
N_EXPERTS = 32
TOP_K = 4
D_FF = 1024
SWIGLU_LIMIT = 7.0
SWIGLU_ALPHA = 1.702
EPS = 1e-6
IN_COLS = 4 * HGRN_WIDTH + 3 * CONV_WIDTH + 2 * D_MODEL
LANES = 128
ROW_BLOCK = 256
VMEM_LIMIT = 56 * 1024 * 1024


def _sigmoid(x):
    return 1.0 / (1.0 + jnp.exp(-x))


def _rms(x, w):
    ms = jnp.mean(x * x, axis=-1, keepdims=True)
    return x * lax.rsqrt(ms + EPS) * w


def _nt_dot(a, b):
    return lax.dot_general(a, b, (((1,), (1,)), ((), ())), preferred_element_type=F32)


def _tn_dot(a, b):
    return lax.dot_general(a, b, (((0,), (0,)), ((), ())), preferred_element_type=F32)


def _ada_kernel(c_ref, w_ref, b_ref, o_ref):
    c = c_ref[...]
    sc = (c * _sigmoid(c)).astype(BF16)
    o_ref[...] = jnp.dot(sc, w_ref[...].astype(BF16), preferred_element_type=F32) + b_ref[...]


def _ada(c, w_ada, b_ada):
    bsz, d = c.shape
    n = w_ada.shape[1]
    return pl.pallas_call(
        _ada_kernel,
        out_shape=jax.ShapeDtypeStruct((bsz, n), F32),
        grid=(n // d,),
        in_specs=[pl.BlockSpec((bsz, d), lambda j: (0, 0)),
                  pl.BlockSpec((d, d), lambda j: (0, j)),
                  pl.BlockSpec((1, d), lambda j: (0, j))],
        out_specs=pl.BlockSpec((bsz, d), lambda j: (0, j)),
        name="ada",
    )(c, w_ada, b_ada.reshape(1, n))


def _inproj_kernel(x_ref, mod_ref, nw_ref, w_ref, o_ref, u_scr):
    y = _rms(x_ref[...], nw_ref[...])
    shift = mod_ref[0, 0:1, :]
    scale = mod_ref[0, 1:2, :]
    u_scr[...] = (y * (1.0 + scale) + shift).astype(BF16)
    step = 512
    for j in range(IN_COLS // step):
        o_ref[:, j * step:(j + 1) * step] = jnp.dot(
            u_scr[...], w_ref[:, j * step:(j + 1) * step], preferred_element_type=F32).astype(BF16)


def _inproj(x2, mod, norm_w, w_in_bf, seq, tm):
    n = x2.shape[0]
    per_b = seq // tm
    return pl.pallas_call(
        _inproj_kernel,
        out_shape=jax.ShapeDtypeStruct((n, IN_COLS), BF16),
        grid=(n // tm,),
        in_specs=[pl.BlockSpec((tm, D_MODEL), lambda i: (i, 0)),
                  pl.BlockSpec((1, 6, D_MODEL), lambda i: (i // per_b, 0, 0)),
                  pl.BlockSpec((1, D_MODEL), lambda i: (0, 0)),
                  pl.BlockSpec((D_MODEL, IN_COLS), lambda i: (0, 0))],
        out_specs=pl.BlockSpec((tm, IN_COLS), lambda i: (i, 0)),
        scratch_shapes=[pltpu.VMEM((tm, D_MODEL), BF16)],
        compiler_params=pltpu.CompilerParams(dimension_semantics=("arbitrary",),
                                             vmem_limit_bytes=VMEM_LIMIT),
        name="inproj",
    )(x2, mod, norm_w.reshape(1, D_MODEL), w_in_bf)


def _hgrn_chunk(c, lb, tri, q_ref, f_ref, v_ref, st_scr, o_scr):
    r0 = pl.multiple_of(c * CHUNK, CHUNK)
    rows = pl.ds(r0, CHUNK)
    fx = f_ref[rows, :].astype(F32)
    f = lb + (1.0 - lb) * _sigmoid(fx)
    g = jnp.log(f)
    kk = 1.0 - f
    g1 = g.astype(BF16)
    r1 = g - g1.astype(F32)
    g2 = r1.astype(BF16)
    g3 = (r1 - g2.astype(F32)).astype(BF16)
    b = (jnp.dot(tri, g1, preferred_element_type=F32)
         + jnp.dot(tri, g2, preferred_element_type=F32)
         + jnp.dot(tri, g3, preferred_element_type=F32))
    q = q_ref[rows, :].astype(F32)
    v_bf = v_ref[rows, :]
    v = v_bf.astype(F32)
    b_last = b[CHUNK - 1:CHUNK, :]
    qd = (q * jnp.exp(b)).astype(BF16)
    kdec = (kk * jnp.exp(b_last - b)).astype(BF16)

    row = lax.broadcasted_iota(jnp.int32, (CHUNK, 1), 0)
    row_in_sub = row % SUB
    band = [None] * HGRN_HEADS
    for d in range(SUB):
        if d == 0:
            p = q * kk
            vd = v
        else:
            kd = pltpu.roll(kk, d, 0)
            bd = pltpu.roll(b, d, 0)
            vd = pltpu.roll(v, d, 0)
            e = jnp.exp(jnp.where(row_in_sub >= d, b - bd, -jnp.inf))
            p = q * kd * e
        for h in range(HGRN_HEADS):
            hs = slice(h * HEAD_DIM, (h + 1) * HEAD_DIM)
            s = jnp.sum(p[:, hs], axis=-1, keepdims=True)
            t = s * vd[:, hs]
            band[h] = t if band[h] is None else band[h] + t

    n_sub = CHUNK // SUB
    for h in range(HGRN_HEADS):
        hs = slice(h * HEAD_DIM, (h + 1) * HEAD_DIM)
        st = st_scr[h]
        o_h = _nt_dot(qd[:, hs], st.astype(BF16)) + band[h]
        pieces = [o_h[0:SUB]]
        for i in range(1, n_sub):
            lo = i * SUB
            ref_b = b[lo - 1:lo, hs]
            qx = (q[lo:lo + SUB, hs] * jnp.exp(b[lo:lo + SUB, hs] - ref_b)).astype(BF16)
            kx = (kk[0:lo, hs] * jnp.exp(ref_b - b[0:lo, hs])).astype(BF16)
            sc = _nt_dot(qx, kx).astype(BF16)
            pieces.append(o_h[lo:lo + SUB] + jnp.dot(sc, v_bf[0:lo, hs], preferred_element_type=F32))
        o_scr[rows, hs] = jnp.concatenate(pieces, axis=0)
        st_scr[h] = jnp.exp(b_last[:, hs]) * st + _tn_dot(v_bf[:, hs], kdec[:, hs])


def _mix_kernel(lbt_ref, q_ref, f_ref, i_ref, g_ref, cb_ref, cc_ref, ch_ref, ga0_ref, ga1_ref,
                gb0_ref, gb1_ref, x_ref, mod_ref, hnw_ref, cw_ref, wa_ref, wb_ref, wm_ref,
                o_ref, st_scr, o_scr, carry_scr, *, layer, rows_per_step, epi_rows):
    @pl.when(pl.program_id(1) == 0)
    def _():
        st_scr[...] = jnp.zeros_like(st_scr)
        carry_scr[...] = jnp.zeros_like(carry_scr)

    tab = lbt_ref[...]
    tmax = jnp.max(tab, axis=0, keepdims=True)
    te = jnp.exp(tab - tmax)
    lb = jnp.sum(te[0:layer + 1], axis=0, keepdims=True) / jnp.sum(te, axis=0, keepdims=True)

    ri = lax.broadcasted_iota(jnp.int32, (CHUNK, CHUNK), 0)
    ci = lax.broadcasted_iota(jnp.int32, (CHUNK, CHUNK), 1)
    tri = jnp.where(ci <= ri, 1.0, 0.0).astype(BF16)

    def chunk_body(c, carry):
        _hgrn_chunk(c, lb, tri, q_ref, f_ref, i_ref, st_scr, o_scr)
        return carry

    lax.fori_loop(0, rows_per_step // CHUNK, chunk_body, 0)

    gate_m = mod_ref[0, 2:3, :]
    hnw = hnw_ref[...]
    cw0 = cw_ref[0:1, :]
    cw1 = cw_ref[1:2, :]
    cw2 = cw_ref[2:3, :]
    for r in range(rows_per_step // epi_rows):
        rs = slice(r * epi_rows, (r + 1) * epi_rows)
        parts = []
        for h in range(HGRN_HEADS):
            hs = slice(h * HEAD_DIM, (h + 1) * HEAD_DIM)
            oh = _rms(o_scr[rs, hs], hnw)
            go = g_ref[rs, hs].astype(F32)
            parts.append((oh * (go * _sigmoid(go))).astype(BF16))
        ya = jnp.dot(jnp.concatenate(parts, axis=1), wa_ref[...], preferred_element_type=F32)
        uc = cc_ref[rs, :].astype(F32) * ch_ref[rs, :].astype(F32)
        prev = carry_scr[...]
        rowi = lax.broadcasted_iota(jnp.int32, (epi_rows, 1), 0)
        s1 = jnp.where(rowi == 0, prev[7:8, :], pltpu.roll(uc, 1, 0))
        s2 = pltpu.roll(uc, 2, 0)
        s2 = jnp.where(rowi == 0, prev[6:7, :], jnp.where(rowi == 1, prev[7:8, :], s2))
        carry_scr[...] = uc[epi_rows - 8:epi_rows, :]
        yc = cb_ref[rs, :].astype(F32) * (cw2 * uc + cw1 * s1 + cw0 * s2)
        yb = jnp.dot(yc.astype(BF16), wb_ref[...], preferred_element_type=F32)
        ga = jnp.concatenate([ga0_ref[rs, :], ga1_ref[rs, :]], axis=1).astype(F32)
        gb = jnp.concatenate([gb0_ref[rs, :], gb1_ref[rs, :]], axis=1).astype(F32)
        merged = (_sigmoid(ga) * ya + _sigmoid(gb) * yb).astype(BF16)
        o_ref[rs, :] = x_ref[rs, :] + gate_m * jnp.dot(merged, wm_ref[...], preferred_element_type=F32)


def _mix(proj, x2, mod, lb_table, hgrn_norm_w, conv_w, wa, wb, wm, layer, bsz, seq, tt):
    n = x2.shape[0]
    per_b = seq // tt

    def col(col_block):
        return pl.BlockSpec((tt, 512), lambda b, t: (b * per_b + t, col_block))

    const = lambda shape: pl.BlockSpec(shape, lambda b, t: (0,) * len(shape))
    in_specs = [
        const(lb_table.shape),
        col(0), col(1), col(2), col(3),
        col(4), col(5), col(6),
        col(7), col(8), col(9), col(10),
        pl.BlockSpec((tt, D_MODEL), lambda b, t: (b * per_b + t, 0)),
        pl.BlockSpec((1, 6, D_MODEL), lambda b, t: (b, 0, 0)),
        const((1, HEAD_DIM)), const((CONV_K, CONV_WIDTH)),
        const((HGRN_WIDTH, D_MODEL)), const((CONV_WIDTH, D_MODEL)), const((D_MODEL, D_MODEL)),
    ]
    kern = functools.partial(_mix_kernel, layer=layer, rows_per_step=tt, epi_rows=min(tt, 256))
    return pl.pallas_call(
        kern,
        out_shape=jax.ShapeDtypeStruct((n, D_MODEL), F32),
        grid=(bsz, per_b),
        in_specs=in_specs,
        out_specs=pl.BlockSpec((tt, D_MODEL), lambda b, t: (b * per_b + t, 0)),
        scratch_shapes=[pltpu.VMEM((HGRN_HEADS, HEAD_DIM, HEAD_DIM), F32),
                        pltpu.VMEM((tt, HGRN_WIDTH), F32),
                        pltpu.VMEM((8, CONV_WIDTH), F32)],
        compiler_params=pltpu.CompilerParams(dimension_semantics=("arbitrary", "arbitrary"),
                                             vmem_limit_bytes=VMEM_LIMIT),
        name="mix",
    )(lb_table, *([proj] * 11),
      x2, mod, hgrn_norm_w.reshape(1, HEAD_DIM), conv_w, wa, wb, wm)


def _route_kernel(h_ref, mod_ref, nw_ref, wr_ref, br_ref, u_ref, meta_ref, pw_ref, cnt_ref, *, tr):
    shift = mod_ref[0, 3:4, :]
    scale = mod_ref[0, 4:5, :]
    u = _rms(h_ref[...], nw_ref[...]) * (1.0 + scale) + shift
    u_ref[...] = u
    lane = lax.broadcasted_iota(jnp.int32, (tr, LANES), 1)
    logits = jnp.dot(u.astype(BF16), wr_ref[...], preferred_element_type=F32) + br_ref[...]
    logits = jnp.where(lane < N_EXPERTS, logits, -jnp.inf)
    idx, val = [], []
    cur = logits
    for _ in range(TOP_K):
        m = jnp.max(cur, axis=-1, keepdims=True)
        i = jnp.min(jnp.where(cur == m, lane, LANES), axis=-1, keepdims=True)
        idx.append(i)
        val.append(m)
        cur = jnp.where(lane == i, -jnp.inf, cur)
    ex = [jnp.exp(v - val[0]) for v in val]
    den = ex[0] + ex[1] + ex[2] + ex[3]
    onehot = jnp.zeros((tr, LANES), F32)
    for i in idx:
        onehot = onehot + jnp.where(lane == i, 1.0, 0.0)
    ri = lax.broadcasted_iota(jnp.int32, (tr, tr), 0)
    ci = lax.broadcasted_iota(jnp.int32, (tr, tr), 1)
    tri = jnp.where(ci < ri, 1.0, 0.0).astype(BF16)
    pref = jnp.dot(tri, onehot.astype(BF16), preferred_element_type=F32)
    meta = jnp.zeros((tr, LANES), jnp.int32)
    pw = jnp.zeros((tr, LANES), F32)
    for j in range(TOP_K):
        rank = jnp.sum(jnp.where(lane == idx[j], pref, 0.0), axis=-1, keepdims=True).astype(jnp.int32)
        meta = jnp.where(lane == j, idx[j], meta)
        meta = jnp.where(lane == TOP_K + j, rank, meta)
        pw = jnp.where(lane == j, ex[j] / den, pw)
    meta_ref[...] = meta
    pw_ref[...] = pw
    cnt_ref[0] = jnp.sum(onehot, axis=0, keepdims=True).astype(jnp.int32)


def _route(h1, mod, norm_w, wr_pad, br_pad, seq, tr):
    n = h1.shape[0]
    per_b = seq // tr
    nt = n // tr
    return pl.pallas_call(
        functools.partial(_route_kernel, tr=tr),
        out_shape=(jax.ShapeDtypeStruct((n, D_MODEL), F32),
                   jax.ShapeDtypeStruct((n, LANES), jnp.int32),
                   jax.ShapeDtypeStruct((n, LANES), F32),
                   jax.ShapeDtypeStruct((nt, 1, LANES), jnp.int32)),
        grid=(nt,),
        in_specs=[pl.BlockSpec((tr, D_MODEL), lambda i: (i, 0)),
                  pl.BlockSpec((1, 6, D_MODEL), lambda i: (i // per_b, 0, 0)),
                  pl.BlockSpec((1, D_MODEL), lambda i: (0, 0)),
                  pl.BlockSpec((D_MODEL, LANES), lambda i: (0, 0)),
                  pl.BlockSpec((1, LANES), lambda i: (0, 0))],
        out_specs=(pl.BlockSpec((tr, D_MODEL), lambda i: (i, 0)),
                   pl.BlockSpec((tr, LANES), lambda i: (i, 0)),
                   pl.BlockSpec((tr, LANES), lambda i: (i, 0)),
                   pl.BlockSpec((1, 1, LANES), lambda i: (i, 0, 0))),
        compiler_params=pltpu.CompilerParams(dimension_semantics=("arbitrary",),
                                             vmem_limit_bytes=VMEM_LIMIT),
        name="route",
    )(h1, mod, norm_w.reshape(1, D_MODEL), wr_pad, br_pad)


def _dest_kernel(meta_ref, base_ref, o_ref, *, tr):
    lane = lax.broadcasted_iota(jnp.int32, (tr, LANES), 1)
    meta = meta_ref[...]
    base = base_ref[0]
    out = jnp.zeros((tr, LANES), jnp.int32)
    for j in range(TOP_K):
        e = jnp.sum(jnp.where(lane == j, meta, 0), axis=-1, keepdims=True)
        rank = jnp.sum(jnp.where(lane == TOP_K + j, meta, 0), axis=-1, keepdims=True)
        start = jnp.sum(jnp.where(lane == e, base, 0), axis=-1, keepdims=True)
        out = jnp.where(lane == j, start + rank, out)
    o_ref[...] = out


def _dest(meta, base, tr):
    n = meta.shape[0]
    return pl.pallas_call(
        functools.partial(_dest_kernel, tr=tr),
        out_shape=jax.ShapeDtypeStruct((n, LANES), jnp.int32),
        grid=(n // tr,),
        in_specs=[pl.BlockSpec((tr, LANES), lambda i: (i, 0)),
                  pl.BlockSpec((1, 1, LANES), lambda i: (i, 0, 0))],
        out_specs=pl.BlockSpec((tr, LANES), lambda i: (i, 0)),
        name="dest",
    )(meta, base)


def _dispatch_kernel(dest_ref, u_ref, xs_in_ref, xs_ref, sem, *, td):
    del xs_in_ref

    def row_copy(t, j):
        return pltpu.make_async_copy(u_ref.at[pl.ds(t, 1), :],
                                     xs_ref.at[pl.ds(dest_ref[t * TOP_K + j], 1), :], sem)

    def issue(t, carry):
        for j in range(TOP_K):
            row_copy(t, j).start()
        return carry

    lax.fori_loop(0, td, issue, 0)

    def drain(t, carry):
        for j in range(TOP_K):
            row_copy(t, j).wait()
        return carry

    lax.fori_loop(0, td, drain, 0)


def _dispatch(dest_flat, u2, n_rows, td):
    n = u2.shape[0]
    xs0 = jnp.zeros((n_rows, D_MODEL), F32)
    return pl.pallas_call(
        functools.partial(_dispatch_kernel, td=td),
        out_shape=jax.ShapeDtypeStruct((n_rows, D_MODEL), F32),
        grid=(n // td,),
        in_specs=[pl.BlockSpec((td * TOP_K,), lambda i: (i,), memory_space=pltpu.SMEM),
                  pl.BlockSpec((td, D_MODEL), lambda i: (i, 0)),
                  pl.BlockSpec(memory_space=pl.ANY)],
        out_specs=pl.BlockSpec(memory_space=pl.ANY),
        scratch_shapes=[pltpu.SemaphoreType.DMA],
        input_output_aliases={2: 0},
        compiler_params=pltpu.CompilerParams(dimension_semantics=("arbitrary",)),
        name="dispatch",
    )(dest_flat, u2, xs0)


def _experts_kernel(be_ref, nu_ref, x_ref, w1_ref, b1_ref, w2_ref, b2_ref, o_ref):
    del be_ref

    @pl.when(pl.program_id(0) < nu_ref[0])
    def _():
        x = x_ref[...].astype(BF16)
        acc = None
        step = 512
        for j in range(D_FF // step):
            cs = slice(j * step, (j + 1) * step)
            ls = slice(D_FF + j * step, D_FF + (j + 1) * step)
            glu = jnp.dot(x, w1_ref[0, :, cs], preferred_element_type=F32) + b1_ref[0, :, cs]
            lin = jnp.dot(x, w1_ref[0, :, ls], preferred_element_type=F32) + b1_ref[0, :, ls]
            glu = jnp.minimum(glu, SWIGLU_LIMIT)
            lin = jnp.clip(lin, -SWIGLU_LIMIT, SWIGLU_LIMIT)
            act = (glu * _sigmoid(SWIGLU_ALPHA * glu) * (lin + 1.0)).astype(BF16)
            part = jnp.dot(act, w2_ref[0, cs, :], preferred_element_type=F32)
            acc = part if acc is None else acc + part
        o_ref[...] = acc + b2_ref[0]

    @pl.when(pl.program_id(0) >= nu_ref[0])
    def _():
        o_ref[...] = jnp.zeros_like(o_ref)


def _experts(block_e, n_used, xs, w1_bf, b1, w2_bf, b2):
    n_rows = xs.shape[0]
    nb = n_rows // ROW_BLOCK
    grid_spec = pltpu.PrefetchScalarGridSpec(
        num_scalar_prefetch=2,
        grid=(nb,),
        in_specs=[pl.BlockSpec((ROW_BLOCK, D_MODEL), lambda i, be, nu: (i, 0)),
                  pl.BlockSpec((1, D_MODEL, 2 * D_FF), lambda i, be, nu: (be[i], 0, 0)),
                  pl.BlockSpec((1, 1, 2 * D_FF), lambda i, be, nu: (be[i], 0, 0)),
                  pl.BlockSpec((1, D_FF, D_MODEL), lambda i, be, nu: (be[i], 0, 0)),
                  pl.BlockSpec((1, 1, D_MODEL), lambda i, be, nu: (be[i], 0, 0))],
        out_specs=pl.BlockSpec((ROW_BLOCK, D_MODEL), lambda i, be, nu: (i, 0)),
    )
    return pl.pallas_call(
        _experts_kernel,
        out_shape=jax.ShapeDtypeStruct((n_rows, D_MODEL), F32),
        grid_spec=grid_spec,
        compiler_params=pltpu.CompilerParams(dimension_semantics=("arbitrary",),
                                             vmem_limit_bytes=VMEM_LIMIT),
        name="experts",
    )(block_e, n_used, xs, w1_bf, b1.reshape(N_EXPERTS, 1, 2 * D_FF), w2_bf,
      b2.reshape(N_EXPERTS, 1, D_MODEL))


def _combine_kernel(dest_ref, h_ref, pw_ref, mod_ref, nw_ref, ys_ref, o_ref, buf, sem, *, tc):
    def row_copy(t, j):
        return pltpu.make_async_copy(ys_ref.at[pl.ds(dest_ref[t * TOP_K + j], 1), :],
                                     buf.at[j, pl.ds(t, 1), :], sem)

    def issue(t, carry):
        for j in range(TOP_K):
            row_copy(t, j).start()
        return carry

    lax.fori_loop(0, tc, issue, 0)

    def drain(t, carry):
        for j in range(TOP_K):
            row_copy(t, j).wait()
        return carry

    lax.fori_loop(0, tc, drain, 0)

    pw = pw_ref[...]
    moe = pw[:, 0:1] * buf[0]
    for j in range(1, TOP_K):
        moe = moe + pw[:, j:j + 1] * buf[j]
    gate_f = mod_ref[0, 5:6, :]
    o_ref[...] = _rms(h_ref[...] + gate_f * moe, nw_ref[...])


def _combine(dest_flat, h1, pw, mod, norm_w, ys, seq, tc):
    n = h1.shape[0]
    per_b = seq // tc
    return pl.pallas_call(
        functools.partial(_combine_kernel, tc=tc),
        out_shape=jax.ShapeDtypeStruct((n, D_MODEL), F32),
        grid=(n // tc,),
        in_specs=[pl.BlockSpec((tc * TOP_K,), lambda i: (i,), memory_space=pltpu.SMEM),
                  pl.BlockSpec((tc, D_MODEL), lambda i: (i, 0)),
                  pl.BlockSpec((tc, LANES), lambda i: (i, 0)),
                  pl.BlockSpec((1, 6, D_MODEL), lambda i: (i // per_b, 0, 0)),
                  pl.BlockSpec((1, D_MODEL), lambda i: (0, 0)),
                  pl.BlockSpec(memory_space=pl.ANY)],
        out_specs=pl.BlockSpec((tc, D_MODEL), lambda i: (i, 0)),
        scratch_shapes=[pltpu.VMEM((TOP_K, tc, D_MODEL), F32), pltpu.SemaphoreType.DMA],
        compiler_params=pltpu.CompilerParams(dimension_semantics=("arbitrary",),
                                             vmem_limit_bytes=VMEM_LIMIT),
        name="combine",
    )(dest_flat, h1, pw, mod, norm_w.reshape(1, D_MODEL), ys)


def _moe_plan(counts, n_assign):
    cnt = counts[:, 0, :N_EXPERTS]
    sizes = jnp.sum(cnt, axis=0)
    padded = (sizes + ROW_BLOCK - 1) // ROW_BLOCK * ROW_BLOCK
    pad_end = jnp.cumsum(padded)
    pad_start = pad_end - padded
    tile_base = pad_start[None, :] + jnp.cumsum(cnt, axis=0) - cnt
    base = jnp.zeros((cnt.shape[0], 1, LANES), jnp.int32).at[:, 0, :N_EXPERTS].set(tile_base)
    nb = n_assign // ROW_BLOCK + N_EXPERTS
    block_start = jnp.arange(nb, dtype=jnp.int32) * ROW_BLOCK
    block_e = jnp.minimum(jnp.searchsorted(pad_end, block_start, side="right"),
                          N_EXPERTS - 1).astype(jnp.int32)
    n_used = (pad_end[-1] // ROW_BLOCK).astype(jnp.int32).reshape(1)
    return base, block_e, n_used, nb * ROW_BLOCK


def kernel(x, c, w_ada, b_ada, norm_mix_w, w_in, hgrn_lower_bounds, hgrn_norm_w, conv_w,
           w_hgrn_out, w_conv_out, w_mix_out, norm_ffn_w, w_router, b_router, w1, b1, w2, b2,
           norm_final_w):
    bsz, seq, d = x.shape
    assert d == D_MODEL and seq % CHUNK == 0
    n = bsz * seq
    depth = w_ada.shape[0]
    tile = min(512, seq)
    assert seq % tile == 0 and (n * TOP_K) % ROW_BLOCK == 0
    h = x.reshape(n, d)
    wr_pad = jnp.zeros((depth, D_MODEL, LANES), BF16).at[:, :, :N_EXPERTS].set(w_router.astype(BF16))
    br_pad = jnp.zeros((depth, 1, LANES), F32).at[:, 0, :N_EXPERTS].set(b_router)
    for layer in range(depth):
        mod = _ada(c, w_ada[layer], b_ada[layer]).reshape(bsz, 6, d)
        proj = _inproj(h, mod, norm_mix_w[layer], w_in[layer].astype(BF16), seq, tile)
        h = _mix(proj, h, mod, hgrn_lower_bounds, hgrn_norm_w[layer], conv_w[layer],
                 w_hgrn_out[layer].astype(BF16), w_conv_out[layer].astype(BF16),
                 w_mix_out[layer].astype(BF16), layer, bsz, seq, tile)
        u2, meta, pw, counts = _route(h, mod, norm_ffn_w[layer], wr_pad[layer], br_pad[layer], seq, tile)
        base, block_e, n_used, n_rows = _moe_plan(counts, n * TOP_K)
        dest = _dest(meta, base, tile)[:, :TOP_K].reshape(n * TOP_K)
        xs = _dispatch(dest, u2, n_rows, tile)
        ys = _experts(block_e, n_used, xs, w1[layer].astype(BF16), b1[layer], w2[layer].astype(BF16),
                      b2[layer])
        last = layer == depth - 1
        fin_w = norm_final_w if last else jnp.ones((d,), F32)
        assert last, "only the last layer's combine applies the final norm"
        h = _combine(dest, h, pw, mod, fin_w, ys, seq, tile)
    return h.reshape(bsz, seq, d)
```

```python
import functools

import jax
import jax.numpy as jnp
from jax import lax
from jax.experimental import pallas as pl
from jax.experimental.pallas import tpu as pltpu

F32 = jnp.float32
BF16 = jnp.bfloat16

D_MODEL = 1024
HGRN_HEADS = 4
HEAD_DIM = 128
HGRN_WIDTH = HGRN_HEADS * HEAD_DIM
CONV_WIDTH = 512
CONV_K = 3
CHUNK = 64
N_EXPERTS = 32
TOP_K = 4
D_FF = 1024
SWIGLU_LIMIT = 7.0
SWIGLU_ALPHA = 1.702
EPS = 1e-6
IN_COLS = 4 * HGRN_WIDTH + 3 * CONV_WIDTH + 2 * D_MODEL
LANES = 128
ROW_BLOCK = 256
VMEM_LIMIT = 56 * 1024 * 1024


def _sigmoid(x):
    return 1.0 / (1.0 + jnp.exp(-x))


def _rms(x, w):
    ms = jnp.mean(x * x, axis=-1, keepdims=True)
    return x * lax.rsqrt(ms + EPS) * w


def _nt_dot(a, b):
    return lax.dot_general(a, b, (((1,), (1,)), ((), ())), preferred_element_type=F32)


def _tn_dot(a, b):
    return lax.dot_general(a, b, (((0,), (0,)), ((), ())), preferred_element_type=F32)


def _ada_kernel(c_ref, w_ref, b_ref, o_ref):
    c = c_ref[...]
    sc = (c * _sigmoid(c)).astype(BF16)
    o_ref[...] = jnp.dot(sc, w_ref[...].astype(BF16), preferred_element_type=F32) + b_ref[...]


def _ada(c, w_ada, b_ada):
    bsz, d = c.shape
    n = w_ada.shape[1]
    return pl.pallas_call(
        _ada_kernel,
        out_shape=jax.ShapeDtypeStruct((bsz, n), F32),
        grid=(n // d,),
        in_specs=[pl.BlockSpec((bsz, d), lambda j: (0, 0)),
                  pl.BlockSpec((d, d), lambda j: (0, j)),
                  pl.BlockSpec((1, d), lambda j: (0, j))],
        out_specs=pl.BlockSpec((bsz, d), lambda j: (0, j)),
        name="ada",
    )(c, w_ada, b_ada.reshape(1, n))


def _inproj_kernel(x_ref, mod_ref, nw_ref, w_ref, o_ref, u_scr):
    y = _rms(x_ref[...], nw_ref[...])
    shift = mod_ref[0, 0:1, :]
    scale = mod_ref[0, 1:2, :]
    u_scr[...] = (y * (1.0 + scale) + shift).astype(BF16)
    step = 512
    for j in range(IN_COLS // step):
        o_ref[:, j * step:(j + 1) * step] = jnp.dot(
            u_scr[...], w_ref[:, j * step:(j + 1) * step], preferred_element_type=F32).astype(BF16)


def _inproj(x2, mod, norm_w, w_in_bf, seq, tm):
    n = x2.shape[0]
    per_b = seq // tm
    return pl.pallas_call(
        _inproj_kernel,
        out_shape=jax.ShapeDtypeStruct((n, IN_COLS), BF16),
        grid=(n // tm,),
        in_specs=[pl.BlockSpec((tm, D_MODEL), lambda i: (i, 0)),
                  pl.BlockSpec((1, 6, D_MODEL), lambda i: (i // per_b, 0, 0)),
                  pl.BlockSpec((1, D_MODEL), lambda i: (0, 0)),
                  pl.BlockSpec((D_MODEL, IN_COLS), lambda i: (0, 0))],
        out_specs=pl.BlockSpec((tm, IN_COLS), lambda i: (i, 0)),
        scratch_shapes=[pltpu.VMEM((tm, D_MODEL), BF16)],
        compiler_params=pltpu.CompilerParams(dimension_semantics=("arbitrary",),
                                             vmem_limit_bytes=VMEM_LIMIT),
        name="inproj",
    )(x2, mod, norm_w.reshape(1, D_MODEL), w_in_bf)


_HEADS = [slice(h * HEAD_DIM, (h + 1) * HEAD_DIM) for h in range(HGRN_HEADS)]


def _chunk_rows(c):
    if isinstance(c, int):
        return pl.ds(c * CHUNK, CHUNK)
    return pl.ds(pl.multiple_of(c * CHUNK, CHUNK), CHUNK)


def _hgrn_gates(c, lb, tri, f_ref, b_scr, k_scr, f_scr):
    fx = f_ref[_chunk_rows(c), :].astype(F32)
    f = lb + (1.0 - lb) * _sigmoid(fx)
    g = jnp.log(f)
    g1 = g.astype(BF16)
    r1 = g - g1.astype(F32)
    g2 = r1.astype(BF16)
    g3 = (r1 - g2.astype(F32)).astype(BF16)
    b_scr[...] = (jnp.dot(tri, g1, preferred_element_type=F32)
                  + jnp.dot(tri, g2, preferred_element_type=F32)
                  + jnp.dot(tri, g3, preferred_element_type=F32))
    f_scr[...] = f
    k_scr[...] = 1.0 - f


def _hgrn_scores(c, q_ref, b_scr, k_scr, f_scr):
    rows = _chunk_rows(c)
    qs = [q_ref[rows, hs].astype(F32) for hs in _HEADS]
    s_mats = [_level_scores(1, qs[h], _HEADS[h], b_scr, k_scr, f_scr) for h in range(HGRN_HEADS)]
    for lvl in range(2, 7):
        for h in range(HGRN_HEADS):
            s_mats[h] = s_mats[h] + _level_scores(lvl, qs[h], _HEADS[h], b_scr, k_scr, f_scr)
    return s_mats


def _hgrn_outputs(c, s_mats, q_ref, v_ref, st_scr, o_scr, b_scr, k_scr):
    rows = _chunk_rows(c)
    for h, hs in enumerate(_HEADS):
        q = q_ref[rows, hs].astype(F32)
        v_bf = v_ref[rows, hs]
        b = b_scr[:, hs]
        kk = k_scr[:, hs]
        b_last = b_scr[CHUNK - 1:CHUNK, hs]
        st = st_scr[h]
        qd = (q * jnp.exp(b)).astype(BF16)
        kdec = (kk * jnp.exp(b_last - b)).astype(BF16)
        diag = jnp.sum(q * kk, axis=-1, keepdims=True)
        o_scr[rows, hs] = (_nt_dot(qd, st.astype(BF16))
                           + jnp.dot(s_mats[h].astype(BF16), v_bf, preferred_element_type=F32)
                           + diag * v_bf.astype(F32))
        st_scr[h] = jnp.exp(b_last) * st + _tn_dot(v_bf, kdec)


def _level_scores(lvl, q, hs, b_scr, k_scr, f_scr):
    row = lax.broadcasted_iota(jnp.int32, (CHUNK, 1), 0)
    col = lax.broadcasted_iota(jnp.int32, (1, CHUNK), 1)
    b = b_scr[:, hs]
    kk = k_scr[:, hs]
    blk = 1 << lvl
    half = blk // 2
    if lvl == 1:
        odd = (row & 1) == 1
        qx = jnp.where(odd, q * f_scr[:, hs], 0.0).astype(BF16)
        kx = jnp.where(odd, 0.0, kk).astype(BF16)
        return jnp.where((row >> 1) == (col >> 1), _nt_dot(qx, kx), 0.0)
    if half < 8:
        groups = []
        sub = lax.broadcasted_iota(jnp.int32, (8, 1), 0)
        for j in range(CHUNK // 8):
            rj = None
            for k in reversed(range(8 // blk)):
                m = 8 * j + k * blk + half - 1
                bm = jnp.broadcast_to(b_scr[m:m + 1, hs], (8, HEAD_DIM))
                rj = bm if rj is None else jnp.where(sub < (k + 1) * blk, bm, rj)
            groups.append(rj)
        ref = jnp.concatenate(groups, axis=0)
        second = (row & (blk - 1)) >= half
        qx = (q * jnp.exp(jnp.where(second, b - ref, -jnp.inf))).astype(BF16)
        kx = (kk * jnp.exp(jnp.where(second, -jnp.inf, ref - b))).astype(BF16)
        return jnp.where((row >> lvl) == (col >> lvl), _nt_dot(qx, kx), 0.0)
    n_blk = CHUNK // blk
    qparts, kparts = [], []
    for j in range(n_blk):
        m = j * blk + half - 1
        bm = b_scr[m:m + 1, hs]
        tq = slice(j * blk + half, (j + 1) * blk)
        tk = slice(j * blk, j * blk + half)
        qparts.append(q[tq] * jnp.exp(b[tq] - bm))
        kparts.append(kk[tk] * jnp.exp(bm - b[tk]))
        kparts.append(jnp.zeros((half, HEAD_DIM), F32))
    qx = jnp.concatenate(qparts, axis=0).astype(BF16)
    kx = jnp.concatenate(kparts, axis=0).astype(BF16)
    sc = _nt_dot(qx, kx)
    if n_blk > 1:
        crow = lax.broadcasted_iota(jnp.int32, (CHUNK // 2, 1), 0)
        sc = jnp.where((crow // half) == (col >> lvl), sc, 0.0)
    pieces = []
    for j in range(n_blk):
        pieces.append(jnp.zeros((half, CHUNK), F32))
        pieces.append(sc[j * half:(j + 1) * half])
    return jnp.concatenate(pieces, axis=0)


def _mix_kernel(lbt_ref, q_ref, f_ref, i_ref, g_ref, cb_ref, cc_ref, ch_ref, ga0_ref, ga1_ref,
                gb0_ref, gb1_ref, x_ref, mod_ref, hnw_ref, cw_ref, wa_ref, wb_ref, wm_ref,
                o_ref, st_scr, o_scr, carry_scr, b_scr, k_scr, f_scr, *, layer, rows_per_step, epi_rows):
    @pl.when(pl.program_id(1) == 0)
    def _():
        st_scr[...] = jnp.zeros_like(st_scr)
        carry_scr[...] = jnp.zeros_like(carry_scr)

    tab = lbt_ref[...]
    tmax = jnp.max(tab, axis=0, keepdims=True)
    te = jnp.exp(tab - tmax)
    lb = jnp.sum(te[0:layer + 1], axis=0, keepdims=True) / jnp.sum(te, axis=0, keepdims=True)

    ri = lax.broadcasted_iota(jnp.int32, (CHUNK, CHUNK), 0)
    ci = lax.broadcasted_iota(jnp.int32, (CHUNK, CHUNK), 1)
    tri = jnp.where(ci <= ri, 1.0, 0.0).astype(BF16)

    n_chunks = rows_per_step // CHUNK

    def slot(s):
        return b_scr.at[s], k_scr.at[s], f_scr.at[s]

    def chunk(c, c_next, s):
        bs, ks, fs = slot(s)
        s_mats = _hgrn_scores(c, q_ref, bs, ks, fs)
        _hgrn_gates(c_next, lb, tri, f_ref, *slot(1 - s))
        _hgrn_outputs(c, s_mats, q_ref, i_ref, st_scr, o_scr, bs, ks)

    def pair_body(i, carry):
        chunk(2 * i, 2 * i + 1, 0)
        chunk(2 * i + 1, jnp.minimum(2 * i + 2, n_chunks - 1), 1)
        return carry

    _hgrn_gates(0, lb, tri, f_ref, *slot(0))
    lax.fori_loop(0, n_chunks // 2, pair_body, 0)

    gate_m = mod_ref[0, 2:3, :]
    hnw = hnw_ref[...]
    cw0 = cw_ref[0:1, :]
    cw1 = cw_ref[1:2, :]
    cw2 = cw_ref[2:3, :]
    for r in range(rows_per_step // epi_rows):
        rs = slice(r * epi_rows, (r + 1) * epi_rows)
        parts = []
        for h in range(HGRN_HEADS):
            hs = slice(h * HEAD_DIM, (h + 1) * HEAD_DIM)
            oh = _rms(o_scr[rs, hs], hnw)
            go = g_ref[rs, hs].astype(F32)
            parts.append((oh * (go * _sigmoid(go))).astype(BF16))
        ya = jnp.dot(jnp.concatenate(parts, axis=1), wa_ref[...], preferred_element_type=F32)
        uc = cc_ref[rs, :].astype(F32) * ch_ref[rs, :].astype(F32)
        prev = carry_scr[...]
        rowi = lax.broadcasted_iota(jnp.int32, (epi_rows, 1), 0)
        s1 = jnp.where(rowi == 0, prev[7:8, :], pltpu.roll(uc, 1, 0))
        s2 = pltpu.roll(uc, 2, 0)
        s2 = jnp.where(rowi == 0, prev[6:7, :], jnp.where(rowi == 1, prev[7:8, :], s2))
        carry_scr[...] = uc[epi_rows - 8:epi_rows, :]
        yc = cb_ref[rs, :].astype(F32) * (cw2 * uc + cw1 * s1 + cw0 * s2)
        yb = jnp.dot(yc.astype(BF16), wb_ref[...], preferred_element_type=F32)
        ga = jnp.concatenate([ga0_ref[rs, :], ga1_ref[rs, :]], axis=1).astype(F32)
        gb = jnp.concatenate([gb0_ref[rs, :], gb1_ref[rs, :]], axis=1).astype(F32)
        merged = (_sigmoid(ga) * ya + _sigmoid(gb) * yb).astype(BF16)
        o_ref[rs, :] = x_ref[rs, :] + gate_m * jnp.dot(merged, wm_ref[...], preferred_element_type=F32)


def _mix(proj, x2, mod, lb_table, hgrn_norm_w, conv_w, wa, wb, wm, layer, bsz, seq, tt):
    n = x2.shape[0]
    per_b = seq // tt

    def col(col_block):
        return pl.BlockSpec((tt, 512), lambda b, t: (b * per_b + t, col_block))

    const = lambda shape: pl.BlockSpec(shape, lambda b, t: (0,) * len(shape))
    in_specs = [
        const(lb_table.shape),
        col(0), col(1), col(2), col(3),
        col(4), col(5), col(6),
        col(7), col(8), col(9), col(10),
        pl.BlockSpec((tt, D_MODEL), lambda b, t: (b * per_b + t, 0)),
        pl.BlockSpec((1, 6, D_MODEL), lambda b, t: (b, 0, 0)),
        const((1, HEAD_DIM)), const((CONV_K, CONV_WIDTH)),
        const((HGRN_WIDTH, D_MODEL)), const((CONV_WIDTH, D_MODEL)), const((D_MODEL, D_MODEL)),
    ]
    kern = functools.partial(_mix_kernel, layer=layer, rows_per_step=tt, epi_rows=min(tt, 256))
    return pl.pallas_call(
        kern,
        out_shape=jax.ShapeDtypeStruct((n, D_MODEL), F32),
        grid=(bsz, per_b),
        in_specs=in_specs,
        out_specs=pl.BlockSpec((tt, D_MODEL), lambda b, t: (b * per_b + t, 0)),
        scratch_shapes=[pltpu.VMEM((HGRN_HEADS, HEAD_DIM, HEAD_DIM), F32),
                        pltpu.VMEM((tt, HGRN_WIDTH), F32),
                        pltpu.VMEM((8, CONV_WIDTH), F32),
                        pltpu.VMEM((2, CHUNK, HGRN_WIDTH), F32),
                        pltpu.VMEM((2, CHUNK, HGRN_WIDTH), F32),
                        pltpu.VMEM((2, CHUNK, HGRN_WIDTH), F32)],
        compiler_params=pltpu.CompilerParams(dimension_semantics=("arbitrary", "arbitrary"),
                                             vmem_limit_bytes=VMEM_LIMIT),
        name="mix",
    )(lb_table, *([proj] * 11),
      x2, mod, hgrn_norm_w.reshape(1, HEAD_DIM), conv_w, wa, wb, wm)


def _route_kernel(h_ref, mod_ref, nw_ref, wr_ref, br_ref, u_ref, oh4_ref, rk_ref, pw_ref, cnt_ref, *, tr):
    shift = mod_ref[0, 3:4, :]
    scale = mod_ref[0, 4:5, :]
    u = _rms(h_ref[...], nw_ref[...]) * (1.0 + scale) + shift
    u_ref[...] = u
    lane = lax.broadcasted_iota(jnp.int32, (tr, LANES), 1)
    logits = jnp.dot(u.astype(BF16), wr_ref[...], preferred_element_type=F32) + br_ref[...]
    logits = jnp.where(lane < N_EXPERTS, logits, -jnp.inf)
    idx, val = [], []
    cur = logits
    for _ in range(TOP_K):
        m = jnp.max(cur, axis=-1, keepdims=True)
        i = jnp.min(jnp.where(cur == m, lane, LANES), axis=-1, keepdims=True)
        idx.append(i)
        val.append(m)
        cur = jnp.where(lane == i, -jnp.inf, cur)
    ex = [jnp.exp(v - val[0]) for v in val]
    den = ex[0] + ex[1] + ex[2] + ex[3]
    onehot = jnp.zeros((tr, LANES), F32)
    for i in idx:
        onehot = onehot + jnp.where(lane == i, 1.0, 0.0)
    ri = lax.broadcasted_iota(jnp.int32, (tr, tr), 0)
    ci = lax.broadcasted_iota(jnp.int32, (tr, tr), 1)
    tri = jnp.where(ci < ri, 1.0, 0.0).astype(BF16)
    pref = jnp.dot(tri, onehot.astype(BF16), preferred_element_type=F32)
    rk = jnp.zeros((tr, LANES), jnp.int32)
    pw = jnp.zeros((tr, LANES), F32)
    oh4 = jnp.zeros((tr, LANES), F32)
    for j in range(TOP_K):
        rank = jnp.sum(jnp.where(lane == idx[j], pref, 0.0), axis=-1, keepdims=True).astype(jnp.int32)
        rk = jnp.where(lane == j, rank, rk)
        pw = jnp.where(lane == j, ex[j] / den, pw)
        oh4 = oh4 + jnp.where(lane == idx[j] + N_EXPERTS * j, 1.0, 0.0)
    oh4_ref[...] = oh4.astype(BF16)
    rk_ref[...] = rk
    pw_ref[...] = pw
    cnt_ref[0] = jnp.sum(onehot, axis=0, keepdims=True).astype(jnp.int32)


def _route(h1, mod, norm_w, wr_pad, br_pad, seq, tr):
    n = h1.shape[0]
    per_b = seq // tr
    nt = n // tr
    return pl.pallas_call(
        functools.partial(_route_kernel, tr=tr),
        out_shape=(jax.ShapeDtypeStruct((n, D_MODEL), F32),
                   jax.ShapeDtypeStruct((n, LANES), BF16),
                   jax.ShapeDtypeStruct((n, LANES), jnp.int32),
                   jax.ShapeDtypeStruct((n, LANES), F32),
                   jax.ShapeDtypeStruct((nt, 1, LANES), jnp.int32)),
        grid=(nt,),
        in_specs=[pl.BlockSpec((tr, D_MODEL), lambda i: (i, 0)),
                  pl.BlockSpec((1, 6, D_MODEL), lambda i: (i // per_b, 0, 0)),
                  pl.BlockSpec((1, D_MODEL), lambda i: (0, 0)),
                  pl.BlockSpec((D_MODEL, LANES), lambda i: (0, 0)),
                  pl.BlockSpec((1, LANES), lambda i: (0, 0))],
        out_specs=(pl.BlockSpec((tr, D_MODEL), lambda i: (i, 0)),
                   pl.BlockSpec((tr, LANES), lambda i: (i, 0)),
                   pl.BlockSpec((tr, LANES), lambda i: (i, 0)),
                   pl.BlockSpec((tr, LANES), lambda i: (i, 0)),
                   pl.BlockSpec((1, 1, LANES), lambda i: (i, 0, 0))),
        compiler_params=pltpu.CompilerParams(dimension_semantics=("arbitrary",),
                                             vmem_limit_bytes=VMEM_LIMIT),
        name="route",
    )(h1, mod, norm_w.reshape(1, D_MODEL), wr_pad, br_pad)


def _dest_kernel(oh4_ref, rk_ref, bt_ref, o_ref):
    oh = oh4_ref[...]
    start = (jnp.dot(oh, bt_ref[0, 0], preferred_element_type=F32)
             + 256.0 * jnp.dot(oh, bt_ref[0, 1], preferred_element_type=F32)
             + 65536.0 * jnp.dot(oh, bt_ref[0, 2], preferred_element_type=F32))
    o_ref[...] = start.astype(jnp.int32) + rk_ref[...]


def _dest(oh4, rk, bt, tr):
    n = oh4.shape[0]
    return pl.pallas_call(
        _dest_kernel,
        out_shape=jax.ShapeDtypeStruct((n, LANES), jnp.int32),
        grid=(n // tr,),
        in_specs=[pl.BlockSpec((tr, LANES), lambda i: (i, 0)),
                  pl.BlockSpec((tr, LANES), lambda i: (i, 0)),
                  pl.BlockSpec((1, 3, LANES, LANES), lambda i: (i, 0, 0, 0))],
        out_specs=pl.BlockSpec((tr, LANES), lambda i: (i, 0)),
        name="dest",
    )(oh4, rk, bt)


def _dispatch_kernel(zrow_ref, nu_ref, dest_ref, u_ref, xs_ref, zbuf, sem, zsem, *, td, n_blocks):
    @pl.when(pl.program_id(0) == 0)
    def _():
        zbuf[...] = jnp.zeros_like(zbuf)

        def zero_block(r0):
            cp = pltpu.make_async_copy(zbuf, xs_ref.at[pl.ds(pl.multiple_of(r0, ROW_BLOCK), ROW_BLOCK), :],
                                       zsem)
            cp.start()
            cp.wait()

        for e in range(N_EXPERTS):
            @pl.when(zrow_ref[e] >= 0)
            def _():
                zero_block(zrow_ref[e])

        def tail(i, carry):
            zero_block(i * ROW_BLOCK)
            return carry

        lax.fori_loop(nu_ref[0], n_blocks, tail, 0)

    def issue(t, carry):
        for j in range(TOP_K):
            pltpu.make_async_copy(u_ref.at[pl.ds(t, 1), :],
                                  xs_ref.at[pl.ds(dest_ref[t * TOP_K + j], 1), :],
                                  sem).start(priority=j % 2)
        return carry

    lax.fori_loop(0, td, issue, 0, unroll=2)
    for _ in range(TOP_K):
        pltpu.make_async_copy(u_ref, xs_ref.at[pl.ds(0, td), :], sem).wait()


def _dispatch(zrow, n_used, dest_flat, u2, n_rows, td):
    n = u2.shape[0]
    grid_spec = pltpu.PrefetchScalarGridSpec(
        num_scalar_prefetch=2,
        grid=(n // td,),
        in_specs=[pl.BlockSpec((td * TOP_K,), lambda i, z, nu: (i,), memory_space=pltpu.SMEM),
                  pl.BlockSpec((td, D_MODEL), lambda i, z, nu: (i, 0))],
        out_specs=pl.BlockSpec(memory_space=pl.ANY),
        scratch_shapes=[pltpu.VMEM((ROW_BLOCK, D_MODEL), F32),
                        pltpu.SemaphoreType.DMA, pltpu.SemaphoreType.DMA],
    )
    return pl.pallas_call(
        functools.partial(_dispatch_kernel, td=td, n_blocks=n_rows // ROW_BLOCK),
        out_shape=jax.ShapeDtypeStruct((n_rows, D_MODEL), F32),
        grid_spec=grid_spec,
        compiler_params=pltpu.CompilerParams(dimension_semantics=("arbitrary",)),
        name="dispatch",
    )(zrow, n_used, dest_flat, u2)


def _experts_kernel(be_ref, nu_ref, x_ref, w1_ref, b1_ref, w2_ref, b2_ref, o_ref):
    del be_ref

    @pl.when(pl.program_id(0) < nu_ref[0])
    def _():
        x = x_ref[...].astype(BF16)
        acc = None
        step = 512
        for j in range(D_FF // step):
            cs = slice(j * step, (j + 1) * step)
            ls = slice(D_FF + j * step, D_FF + (j + 1) * step)
            glu = jnp.dot(x, w1_ref[0, :, cs], preferred_element_type=F32) + b1_ref[0, :, cs]
            lin = jnp.dot(x, w1_ref[0, :, ls], preferred_element_type=F32) + b1_ref[0, :, ls]
            glu = jnp.minimum(glu, SWIGLU_LIMIT)
            lin = jnp.clip(lin, -SWIGLU_LIMIT, SWIGLU_LIMIT)
            act = (glu * _sigmoid(SWIGLU_ALPHA * glu) * (lin + 1.0)).astype(BF16)
            part = jnp.dot(act, w2_ref[0, cs, :], preferred_element_type=F32)
            acc = part if acc is None else acc + part
        o_ref[...] = acc + b2_ref[0]

    @pl.when(pl.program_id(0) >= nu_ref[0])
    def _():
        o_ref[...] = jnp.zeros_like(o_ref)


def _experts(block_e, n_used, xs, w1_bf, b1, w2_bf, b2):
    n_rows = xs.shape[0]
    nb = n_rows // ROW_BLOCK
    grid_spec = pltpu.PrefetchScalarGridSpec(
        num_scalar_prefetch=2,
        grid=(nb,),
        in_specs=[pl.BlockSpec((ROW_BLOCK, D_MODEL), lambda i, be, nu: (jnp.minimum(i, nu[0] - 1), 0)),
                  pl.BlockSpec((1, D_MODEL, 2 * D_FF), lambda i, be, nu: (be[i], 0, 0)),
                  pl.BlockSpec((1, 1, 2 * D_FF), lambda i, be, nu: (be[i], 0, 0)),
                  pl.BlockSpec((1, D_FF, D_MODEL), lambda i, be, nu: (be[i], 0, 0)),
                  pl.BlockSpec((1, 1, D_MODEL), lambda i, be, nu: (be[i], 0, 0))],
        out_specs=pl.BlockSpec((ROW_BLOCK, D_MODEL), lambda i, be, nu: (i, 0)),
    )
    return pl.pallas_call(
        _experts_kernel,
        out_shape=jax.ShapeDtypeStruct((n_rows, D_MODEL), F32),
        grid_spec=grid_spec,
        compiler_params=pltpu.CompilerParams(dimension_semantics=("arbitrary",),
                                             vmem_limit_bytes=VMEM_LIMIT),
        name="experts",
    )(block_e, n_used, xs, w1_bf, b1.reshape(N_EXPERTS, 1, 2 * D_FF), w2_bf,
      b2.reshape(N_EXPERTS, 1, D_MODEL))


def _combine_kernel(dest_ref, h_ref, pw_ref, mod_ref, nw_ref, ys_ref, o_ref, buf, sem, *, tc):
    def issue(t, carry):
        for j in range(TOP_K):
            pltpu.make_async_copy(ys_ref.at[pl.ds(dest_ref[t * TOP_K + j], 1), :],
                                  buf.at[j, pl.ds(t, 1), :], sem).start(priority=j % 2)
        return carry

    lax.fori_loop(0, tc, issue, 0, unroll=2)
    for j in range(TOP_K):
        pltpu.make_async_copy(ys_ref.at[pl.ds(0, tc), :], buf.at[j], sem).wait()

    pw = pw_ref[...]
    moe = pw[:, 0:1] * buf[0]
    for j in range(1, TOP_K):
        moe = moe + pw[:, j:j + 1] * buf[j]
    gate_f = mod_ref[0, 5:6, :]
    o_ref[...] = _rms(h_ref[...] + gate_f * moe, nw_ref[...])


def _combine(dest_flat, h1, pw, mod, norm_w, ys, seq, tc):
    n = h1.shape[0]
    per_b = seq // tc
    return pl.pallas_call(
        functools.partial(_combine_kernel, tc=tc),
        out_shape=jax.ShapeDtypeStruct((n, D_MODEL), F32),
        grid=(n // tc,),
        in_specs=[pl.BlockSpec((tc * TOP_K,), lambda i: (i,), memory_space=pltpu.SMEM),
                  pl.BlockSpec((tc, D_MODEL), lambda i: (i, 0)),
                  pl.BlockSpec((tc, LANES), lambda i: (i, 0)),
                  pl.BlockSpec((1, 6, D_MODEL), lambda i: (i // per_b, 0, 0)),
                  pl.BlockSpec((1, D_MODEL), lambda i: (0, 0)),
                  pl.BlockSpec(memory_space=pl.ANY)],
        out_specs=pl.BlockSpec((tc, D_MODEL), lambda i: (i, 0)),
        scratch_shapes=[pltpu.VMEM((TOP_K, tc, D_MODEL), F32), pltpu.SemaphoreType.DMA],
        compiler_params=pltpu.CompilerParams(dimension_semantics=("arbitrary",),
                                             vmem_limit_bytes=VMEM_LIMIT),
        name="combine",
    )(dest_flat, h1, pw, mod, norm_w.reshape(1, D_MODEL), ys)


def _moe_plan(counts, n_assign):
    cnt = counts[:, 0, :N_EXPERTS]
    sizes = jnp.sum(cnt, axis=0)
    padded = (sizes + ROW_BLOCK - 1) // ROW_BLOCK * ROW_BLOCK
    pad_end = jnp.cumsum(padded)
    pad_start = pad_end - padded
    tile_base = pad_start[None, :] + jnp.cumsum(cnt, axis=0) - cnt
    digits = jnp.stack([tile_base % 256, (tile_base // 256) % 256, tile_base // 65536], axis=1)
    rows = jnp.tile(digits, (1, 1, TOP_K))
    slot_of_row = jnp.arange(LANES, dtype=jnp.int32) // N_EXPERTS
    col = jnp.arange(LANES, dtype=jnp.int32)
    bt = jnp.where(slot_of_row[:, None] == col[None, :], rows[..., None], 0).astype(BF16)
    nb = n_assign // ROW_BLOCK + N_EXPERTS
    block_start = jnp.arange(nb, dtype=jnp.int32) * ROW_BLOCK
    block_e = jnp.minimum(jnp.sum(pad_end[None, :] <= block_start[:, None], axis=1),
                          N_EXPERTS - 1).astype(jnp.int32)
    n_used = (pad_end[-1] // ROW_BLOCK).astype(jnp.int32).reshape(1)
    zrow = jnp.where(padded > 0, pad_end - ROW_BLOCK, -1).astype(jnp.int32)
    return bt, block_e, n_used, zrow, nb * ROW_BLOCK


def kernel(x, c, w_ada, b_ada, norm_mix_w, w_in, hgrn_lower_bounds, hgrn_norm_w, conv_w,
           w_hgrn_out, w_conv_out, w_mix_out, norm_ffn_w, w_router, b_router, w1, b1, w2, b2,
           norm_final_w):
    bsz, seq, d = x.shape
    assert d == D_MODEL and seq % CHUNK == 0
    n = bsz * seq
    depth = w_ada.shape[0]
    tile = min(512, seq)
    assert seq % tile == 0 and (n * TOP_K) % ROW_BLOCK == 0
    h = x.reshape(n, d)
    wr_pad = jnp.zeros((depth, D_MODEL, LANES), BF16).at[:, :, :N_EXPERTS].set(w_router.astype(BF16))
    br_pad = jnp.zeros((depth, 1, LANES), F32).at[:, 0, :N_EXPERTS].set(b_router)
    for layer in range(depth):
        mod = _ada(c, w_ada[layer], b_ada[layer]).reshape(bsz, 6, d)
        proj = _inproj(h, mod, norm_mix_w[layer], w_in[layer].astype(BF16), seq, tile)
        h = _mix(proj, h, mod, hgrn_lower_bounds, hgrn_norm_w[layer], conv_w[layer],
                 w_hgrn_out[layer].astype(BF16), w_conv_out[layer].astype(BF16),
                 w_mix_out[layer].astype(BF16), layer, bsz, seq, tile)
        u2, oh4, rk, pw, counts = _route(h, mod, norm_ffn_w[layer], wr_pad[layer], br_pad[layer], seq, tile)
        bt, block_e, n_used, zrow, n_rows = _moe_plan(counts, n * TOP_K)
        dest = _dest(oh4, rk, bt, tile)[:, :TOP_K].reshape(n * TOP_K)
        xs = _dispatch(zrow, n_used, dest, u2, n_rows, tile)
        ys = _experts(block_e, n_used, xs, w1[layer].astype(BF16), b1[layer], w2[layer].astype(BF16),
                      b2[layer])
        last = layer == depth - 1
        fin_w = norm_final_w if last else jnp.ones((d,), F32)
        assert last, "only the last layer's combine applies the final norm"
        h = _combine(dest, h, pw, mod, fin_w, ys, seq, tile)
    return h.reshape(bsz, seq, d)
```

```python
import functools

import jax
import jax.numpy as jnp
from jax import lax
from jax.experimental import pallas as pl
from jax.experimental.pallas import tpu as pltpu
from jax.experimental.pallas import tpu_sc as plsc

F32 = jnp.float32
BF16 = jnp.bfloat16

D_MODEL = 1024
HGRN_HEADS = 4
HEAD_DIM = 128
HGRN_WIDTH = HGRN_HEADS * HEAD_DIM
CONV_WIDTH = 512
CONV_K = 3
CHUNK = 64
N_EXPERTS = 32
TOP_K = 4
D_FF = 1024
SWIGLU_LIMIT = 7.0
SWIGLU_ALPHA = 1.702
EPS = 1e-6
IN_COLS = 4 * HGRN_WIDTH + 3 * CONV_WIDTH + 2 * D_MODEL
LANES = 128
ROW_BLOCK = 256
VMEM_LIMIT = 56 * 1024 * 1024


def _sigmoid(x):
    return 1.0 / (1.0 + jnp.exp(-x))


def _rms(x, w):
    ms = jnp.mean(x * x, axis=-1, keepdims=True)
    return x * lax.rsqrt(ms + EPS) * w


def _nt_dot(a, b):
    return lax.dot_general(a, b, (((1,), (1,)), ((), ())), preferred_element_type=F32)


def _tn_dot(a, b):
    return lax.dot_general(a, b, (((0,), (0,)), ((), ())), preferred_element_type=F32)


def _ada_kernel(c_ref, w_ref, b_ref, o_ref):
    c = c_ref[...]
    sc = (c * _sigmoid(c)).astype(BF16)
    o_ref[...] = jnp.dot(sc, w_ref[...].astype(BF16), preferred_element_type=F32) + b_ref[...]


def _ada(c, w_ada, b_ada):
    bsz, d = c.shape
    n = w_ada.shape[1]
    return pl.pallas_call(
        _ada_kernel,
        out_shape=jax.ShapeDtypeStruct((bsz, n), F32),
        grid=(n // d,),
        in_specs=[pl.BlockSpec((bsz, d), lambda j: (0, 0)),
                  pl.BlockSpec((d, d), lambda j: (0, j)),
                  pl.BlockSpec((1, d), lambda j: (0, j))],
        out_specs=pl.BlockSpec((bsz, d), lambda j: (0, j)),
        name="ada",
    )(c, w_ada, b_ada.reshape(1, n))


def _inproj_kernel(x_ref, mod_ref, nw_ref, w_ref, o_ref, u_scr):
    y = _rms(x_ref[...], nw_ref[...])
    shift = mod_ref[0, 0:1, :]
    scale = mod_ref[0, 1:2, :]
    u_scr[...] = (y * (1.0 + scale) + shift).astype(BF16)
    step = 512
    for j in range(IN_COLS // step):
        o_ref[:, j * step:(j + 1) * step] = jnp.dot(
            u_scr[...], w_ref[:, j * step:(j + 1) * step], preferred_element_type=F32).astype(BF16)


def _inproj(x2, mod, norm_w, w_in_bf, seq, tm):
    n = x2.shape[0]
    per_b = seq // tm
    return pl.pallas_call(
        _inproj_kernel,
        out_shape=jax.ShapeDtypeStruct((n, IN_COLS), BF16),
        grid=(n // tm,),
        in_specs=[pl.BlockSpec((tm, D_MODEL), lambda i: (i, 0)),
                  pl.BlockSpec((1, 6, D_MODEL), lambda i: (i // per_b, 0, 0)),
                  pl.BlockSpec((1, D_MODEL), lambda i: (0, 0)),
                  pl.BlockSpec((D_MODEL, IN_COLS), lambda i: (0, 0))],
        out_specs=pl.BlockSpec((tm, IN_COLS), lambda i: (i, 0)),
        scratch_shapes=[pltpu.VMEM((tm, D_MODEL), BF16)],
        compiler_params=pltpu.CompilerParams(dimension_semantics=("arbitrary",),
                                             vmem_limit_bytes=VMEM_LIMIT),
        name="inproj",
    )(x2, mod, norm_w.reshape(1, D_MODEL), w_in_bf)


_HEADS = [slice(h * HEAD_DIM, (h + 1) * HEAD_DIM) for h in range(HGRN_HEADS)]


def _chunk_rows(c):
    if isinstance(c, int):
        return pl.ds(c * CHUNK, CHUNK)
    return pl.ds(pl.multiple_of(c * CHUNK, CHUNK), CHUNK)


def _hgrn_gates(c, lb, tri, f_ref, b_scr, k_scr, f_scr):
    fx = f_ref[_chunk_rows(c), :].astype(F32)
    f = lb + (1.0 - lb) * _sigmoid(fx)
    g = jnp.log(f)
    g1 = g.astype(BF16)
    r1 = g - g1.astype(F32)
    g2 = r1.astype(BF16)
    g3 = (r1 - g2.astype(F32)).astype(BF16)
    b_scr[...] = (jnp.dot(tri, g1, preferred_element_type=F32)
                  + jnp.dot(tri, g2, preferred_element_type=F32)
                  + jnp.dot(tri, g3, preferred_element_type=F32))
    f_scr[...] = f
    k_scr[...] = 1.0 - f


def _hgrn_scores(c, q_ref, b_scr, k_scr, f_scr):
    rows = _chunk_rows(c)
    qs = [q_ref[rows, hs].astype(F32) for hs in _HEADS]
    s_mats = [_level_scores(1, qs[h], _HEADS[h], b_scr, k_scr, f_scr) for h in range(HGRN_HEADS)]
    for lvl in range(2, 7):
        for h in range(HGRN_HEADS):
            s_mats[h] = s_mats[h] + _level_scores(lvl, qs[h], _HEADS[h], b_scr, k_scr, f_scr)
    return s_mats


def _hgrn_outputs(c, s_mats, q_ref, v_ref, st_scr, o_scr, b_scr, k_scr):
    rows = _chunk_rows(c)
    for h, hs in enumerate(_HEADS):
        q = q_ref[rows, hs].astype(F32)
        v_bf = v_ref[rows, hs]
        b = b_scr[:, hs]
        kk = k_scr[:, hs]
        b_last = b_scr[CHUNK - 1:CHUNK, hs]
        st = st_scr[h]
        qd = (q * jnp.exp(b)).astype(BF16)
        kdec = (kk * jnp.exp(b_last - b)).astype(BF16)
        diag = jnp.sum(q * kk, axis=-1, keepdims=True)
        o_scr[rows, hs] = (_nt_dot(qd, st.astype(BF16))
                           + jnp.dot(s_mats[h].astype(BF16), v_bf, preferred_element_type=F32)
                           + diag * v_bf.astype(F32))
        st_scr[h] = jnp.exp(b_last) * st + _tn_dot(v_bf, kdec)


def _level_scores(lvl, q, hs, b_scr, k_scr, f_scr):
    row = lax.broadcasted_iota(jnp.int32, (CHUNK, 1), 0)
    col = lax.broadcasted_iota(jnp.int32, (1, CHUNK), 1)
    b = b_scr[:, hs]
    kk = k_scr[:, hs]
    blk = 1 << lvl
    half = blk // 2
    if lvl == 1:
        odd = (row & 1) == 1
        qx = jnp.where(odd, q * f_scr[:, hs], 0.0).astype(BF16)
        kx = jnp.where(odd, 0.0, kk).astype(BF16)
        return jnp.where((row >> 1) == (col >> 1), _nt_dot(qx, kx), 0.0)
    if half < 8:
        groups = []
        sub = lax.broadcasted_iota(jnp.int32, (8, 1), 0)
        for j in range(CHUNK // 8):
            rj = None
            for k in reversed(range(8 // blk)):
                m = 8 * j + k * blk + half - 1
                bm = jnp.broadcast_to(b_scr[m:m + 1, hs], (8, HEAD_DIM))
                rj = bm if rj is None else jnp.where(sub < (k + 1) * blk, bm, rj)
            groups.append(rj)
        ref = jnp.concatenate(groups, axis=0)
        second = (row & (blk - 1)) >= half
        qx = (q * jnp.exp(jnp.where(second, b - ref, -jnp.inf))).astype(BF16)
        kx = (kk * jnp.exp(jnp.where(second, -jnp.inf, ref - b))).astype(BF16)
        return jnp.where((row >> lvl) == (col >> lvl), _nt_dot(qx, kx), 0.0)
    n_blk = CHUNK // blk
    qparts, kparts = [], []
    for j in range(n_blk):
        m = j * blk + half - 1
        bm = b_scr[m:m + 1, hs]
        tq = slice(j * blk + half, (j + 1) * blk)
        tk = slice(j * blk, j * blk + half)
        qparts.append(q[tq] * jnp.exp(b[tq] - bm))
        kparts.append(kk[tk] * jnp.exp(bm - b[tk]))
        kparts.append(jnp.zeros((half, HEAD_DIM), F32))
    qx = jnp.concatenate(qparts, axis=0).astype(BF16)
    kx = jnp.concatenate(kparts, axis=0).astype(BF16)
    sc = _nt_dot(qx, kx)
    if n_blk > 1:
        crow = lax.broadcasted_iota(jnp.int32, (CHUNK // 2, 1), 0)
        sc = jnp.where((crow // half) == (col >> lvl), sc, 0.0)
    pieces = []
    for j in range(n_blk):
        pieces.append(jnp.zeros((half, CHUNK), F32))
        pieces.append(sc[j * half:(j + 1) * half])
    return jnp.concatenate(pieces, axis=0)


def _mix_kernel(lbt_ref, q_ref, f_ref, i_ref, g_ref, cb_ref, cc_ref, ch_ref, ga0_ref, ga1_ref,
                gb0_ref, gb1_ref, x_ref, mod_ref, hnw_ref, cw_ref, wa_ref, wb_ref, wm_ref,
                o_ref, st_scr, o_scr, carry_scr, b_scr, k_scr, f_scr, *, layer, rows_per_step, epi_rows):
    @pl.when(pl.program_id(1) == 0)
    def _():
        st_scr[...] = jnp.zeros_like(st_scr)
        carry_scr[...] = jnp.zeros_like(carry_scr)

    tab = lbt_ref[...]
    tmax = jnp.max(tab, axis=0, keepdims=True)
    te = jnp.exp(tab - tmax)
    lb = jnp.sum(te[0:layer + 1], axis=0, keepdims=True) / jnp.sum(te, axis=0, keepdims=True)

    ri = lax.broadcasted_iota(jnp.int32, (CHUNK, CHUNK), 0)
    ci = lax.broadcasted_iota(jnp.int32, (CHUNK, CHUNK), 1)
    tri = jnp.where(ci <= ri, 1.0, 0.0).astype(BF16)

    n_chunks = rows_per_step // CHUNK

    def slot(s):
        return b_scr.at[s], k_scr.at[s], f_scr.at[s]

    def chunk(c, c_next, s):
        bs, ks, fs = slot(s)
        s_mats = _hgrn_scores(c, q_ref, bs, ks, fs)
        _hgrn_gates(c_next, lb, tri, f_ref, *slot(1 - s))
        _hgrn_outputs(c, s_mats, q_ref, i_ref, st_scr, o_scr, bs, ks)

    def pair_body(i, carry):
        chunk(2 * i, 2 * i + 1, 0)
        chunk(2 * i + 1, jnp.minimum(2 * i + 2, n_chunks - 1), 1)
        return carry

    _hgrn_gates(0, lb, tri, f_ref, *slot(0))
    lax.fori_loop(0, n_chunks // 2, pair_body, 0)

    gate_m = mod_ref[0, 2:3, :]
    hnw = hnw_ref[...]
    cw0 = cw_ref[0:1, :]
    cw1 = cw_ref[1:2, :]
    cw2 = cw_ref[2:3, :]
    for r in range(rows_per_step // epi_rows):
        rs = slice(r * epi_rows, (r + 1) * epi_rows)
        parts = []
        for h in range(HGRN_HEADS):
            hs = slice(h * HEAD_DIM, (h + 1) * HEAD_DIM)
            oh = _rms(o_scr[rs, hs], hnw)
            go = g_ref[rs, hs].astype(F32)
            parts.append((oh * (go * _sigmoid(go))).astype(BF16))
        ya = jnp.dot(jnp.concatenate(parts, axis=1), wa_ref[...], preferred_element_type=F32)
        uc = cc_ref[rs, :].astype(F32) * ch_ref[rs, :].astype(F32)
        prev = carry_scr[...]
        rowi = lax.broadcasted_iota(jnp.int32, (epi_rows, 1), 0)
        s1 = jnp.where(rowi == 0, prev[7:8, :], pltpu.roll(uc, 1, 0))
        s2 = pltpu.roll(uc, 2, 0)
        s2 = jnp.where(rowi == 0, prev[6:7, :], jnp.where(rowi == 1, prev[7:8, :], s2))
        carry_scr[...] = uc[epi_rows - 8:epi_rows, :]
        yc = cb_ref[rs, :].astype(F32) * (cw2 * uc + cw1 * s1 + cw0 * s2)
        yb = jnp.dot(yc.astype(BF16), wb_ref[...], preferred_element_type=F32)
        ga = jnp.concatenate([ga0_ref[rs, :], ga1_ref[rs, :]], axis=1).astype(F32)
        gb = jnp.concatenate([gb0_ref[rs, :], gb1_ref[rs, :]], axis=1).astype(F32)
        merged = (_sigmoid(ga) * ya + _sigmoid(gb) * yb).astype(BF16)
        o_ref[rs, :] = x_ref[rs, :] + gate_m * jnp.dot(merged, wm_ref[...], preferred_element_type=F32)


def _mix(proj, x2, mod, lb_table, hgrn_norm_w, conv_w, wa, wb, wm, layer, bsz, seq, tt):
    n = x2.shape[0]
    per_b = seq // tt

    def col(col_block):
        return pl.BlockSpec((tt, 512), lambda b, t: (b * per_b + t, col_block))

    const = lambda shape: pl.BlockSpec(shape, lambda b, t: (0,) * len(shape))
    in_specs = [
        const(lb_table.shape),
        col(0), col(1), col(2), col(3),
        col(4), col(5), col(6),
        col(7), col(8), col(9), col(10),
        pl.BlockSpec((tt, D_MODEL), lambda b, t: (b * per_b + t, 0)),
        pl.BlockSpec((1, 6, D_MODEL), lambda b, t: (b, 0, 0)),
        const((1, HEAD_DIM)), const((CONV_K, CONV_WIDTH)),
        const((HGRN_WIDTH, D_MODEL)), const((CONV_WIDTH, D_MODEL)), const((D_MODEL, D_MODEL)),
    ]
    kern = functools.partial(_mix_kernel, layer=layer, rows_per_step=tt, epi_rows=min(tt, 256))
    return pl.pallas_call(
        kern,
        out_shape=jax.ShapeDtypeStruct((n, D_MODEL), F32),
        grid=(bsz, per_b),
        in_specs=in_specs,
        out_specs=pl.BlockSpec((tt, D_MODEL), lambda b, t: (b * per_b + t, 0)),
        scratch_shapes=[pltpu.VMEM((HGRN_HEADS, HEAD_DIM, HEAD_DIM), F32),
                        pltpu.VMEM((tt, HGRN_WIDTH), F32),
                        pltpu.VMEM((8, CONV_WIDTH), F32),
                        pltpu.VMEM((2, CHUNK, HGRN_WIDTH), F32),
                        pltpu.VMEM((2, CHUNK, HGRN_WIDTH), F32),
                        pltpu.VMEM((2, CHUNK, HGRN_WIDTH), F32)],
        compiler_params=pltpu.CompilerParams(dimension_semantics=("arbitrary", "arbitrary"),
                                             vmem_limit_bytes=VMEM_LIMIT),
        name="mix",
    )(lb_table, *([proj] * 11),
      x2, mod, hgrn_norm_w.reshape(1, HEAD_DIM), conv_w, wa, wb, wm)


def _route_kernel(h_ref, mod_ref, nw_ref, wr_ref, br_ref, u_ref, oh4_ref, rk_ref, pw_ref, cnt_ref, *, tr):
    shift = mod_ref[0, 3:4, :]
    scale = mod_ref[0, 4:5, :]
    u = _rms(h_ref[...], nw_ref[...]) * (1.0 + scale) + shift
    u_ref[...] = u
    lane = lax.broadcasted_iota(jnp.int32, (tr, LANES), 1)
    logits = jnp.dot(u.astype(BF16), wr_ref[...], preferred_element_type=F32) + br_ref[...]
    logits = jnp.where(lane < N_EXPERTS, logits, -jnp.inf)
    idx, val = [], []
    cur = logits
    for _ in range(TOP_K):
        m = jnp.max(cur, axis=-1, keepdims=True)
        i = jnp.min(jnp.where(cur == m, lane, LANES), axis=-1, keepdims=True)
        idx.append(i)
        val.append(m)
        cur = jnp.where(lane == i, -jnp.inf, cur)
    ex = [jnp.exp(v - val[0]) for v in val]
    den = ex[0] + ex[1] + ex[2] + ex[3]
    onehot = jnp.zeros((tr, LANES), F32)
    for i in idx:
        onehot = onehot + jnp.where(lane == i, 1.0, 0.0)
    ri = lax.broadcasted_iota(jnp.int32, (tr, tr), 0)
    ci = lax.broadcasted_iota(jnp.int32, (tr, tr), 1)
    tri = jnp.where(ci < ri, 1.0, 0.0).astype(BF16)
    pref = jnp.dot(tri, onehot.astype(BF16), preferred_element_type=F32)
    rk = jnp.zeros((tr, LANES), jnp.int32)
    pw = jnp.zeros((tr, LANES), F32)
    oh4 = jnp.zeros((tr, LANES), F32)
    for j in range(TOP_K):
        rank = jnp.sum(jnp.where(lane == idx[j], pref, 0.0), axis=-1, keepdims=True).astype(jnp.int32)
        rk = jnp.where(lane == j, rank, rk)
        pw = jnp.where(lane == j, ex[j] / den, pw)
        oh4 = oh4 + jnp.where(lane == idx[j] + N_EXPERTS * j, 1.0, 0.0)
    oh4_ref[...] = oh4.astype(BF16)
    rk_ref[...] = rk
    pw_ref[...] = pw
    cnt_ref[0] = jnp.sum(onehot, axis=0, keepdims=True).astype(jnp.int32)


def _route(h1, mod, norm_w, wr_pad, br_pad, seq, tr):
    n = h1.shape[0]
    per_b = seq // tr
    nt = n // tr
    return pl.pallas_call(
        functools.partial(_route_kernel, tr=tr),
        out_shape=(jax.ShapeDtypeStruct((n, D_MODEL), F32),
                   jax.ShapeDtypeStruct((n, LANES), BF16),
                   jax.ShapeDtypeStruct((n, LANES), jnp.int32),
                   jax.ShapeDtypeStruct((n, LANES), F32),
                   jax.ShapeDtypeStruct((nt, 1, LANES), jnp.int32)),
        grid=(nt,),
        in_specs=[pl.BlockSpec((tr, D_MODEL), lambda i: (i, 0)),
                  pl.BlockSpec((1, 6, D_MODEL), lambda i: (i // per_b, 0, 0)),
                  pl.BlockSpec((1, D_MODEL), lambda i: (0, 0)),
                  pl.BlockSpec((D_MODEL, LANES), lambda i: (0, 0)),
                  pl.BlockSpec((1, LANES), lambda i: (0, 0))],
        out_specs=(pl.BlockSpec((tr, D_MODEL), lambda i: (i, 0)),
                   pl.BlockSpec((tr, LANES), lambda i: (i, 0)),
                   pl.BlockSpec((tr, LANES), lambda i: (i, 0)),
                   pl.BlockSpec((tr, LANES), lambda i: (i, 0)),
                   pl.BlockSpec((1, 1, LANES), lambda i: (i, 0, 0))),
        compiler_params=pltpu.CompilerParams(dimension_semantics=("arbitrary",),
                                             vmem_limit_bytes=VMEM_LIMIT),
        name="route",
    )(h1, mod, norm_w.reshape(1, D_MODEL), wr_pad, br_pad)


def _dest_kernel(oh4_ref, rk_ref, bt_ref, o_ref):
    oh = oh4_ref[...]
    start = (jnp.dot(oh, bt_ref[0, 0], preferred_element_type=F32)
             + 256.0 * jnp.dot(oh, bt_ref[0, 1], preferred_element_type=F32)
             + 65536.0 * jnp.dot(oh, bt_ref[0, 2], preferred_element_type=F32))
    o_ref[...] = start.astype(jnp.int32) + rk_ref[...]


def _dest(oh4, rk, bt, tr):
    n = oh4.shape[0]
    return pl.pallas_call(
        _dest_kernel,
        out_shape=jax.ShapeDtypeStruct((n, LANES), jnp.int32),
        grid=(n // tr,),
        in_specs=[pl.BlockSpec((tr, LANES), lambda i: (i, 0)),
                  pl.BlockSpec((tr, LANES), lambda i: (i, 0)),
                  pl.BlockSpec((1, 3, LANES, LANES), lambda i: (i, 0, 0, 0))],
        out_specs=pl.BlockSpec((tr, LANES), lambda i: (i, 0)),
        name="dest",
    )(oh4, rk, bt)


def _dispatch_kernel(zrow_ref, nu_ref, dest_ref, u_ref, xs_ref, zbuf, sem, zsem, *, td, n_blocks):
    @pl.when(pl.program_id(0) == 0)
    def _():
        zbuf[...] = jnp.zeros_like(zbuf)

        def zero_block(r0):
            cp = pltpu.make_async_copy(zbuf, xs_ref.at[pl.ds(pl.multiple_of(r0, ROW_BLOCK), ROW_BLOCK), :],
                                       zsem)
            cp.start()
            cp.wait()

        for e in range(N_EXPERTS):
            @pl.when(zrow_ref[e] >= 0)
            def _():
                zero_block(zrow_ref[e])

        def tail(i, carry):
            zero_block(i * ROW_BLOCK)
            return carry

        lax.fori_loop(nu_ref[0], n_blocks, tail, 0)

    def issue(t, carry):
        for j in range(TOP_K):
            pltpu.make_async_copy(u_ref.at[pl.ds(t, 1), :],
                                  xs_ref.at[pl.ds(dest_ref[t * TOP_K + j], 1), :],
                                  sem).start(priority=j % 2)
        return carry

    lax.fori_loop(0, td, issue, 0, unroll=2)
    for _ in range(TOP_K):
        pltpu.make_async_copy(u_ref, xs_ref.at[pl.ds(0, td), :], sem).wait()


def _dispatch(zrow, n_used, dest_flat, u2, n_rows, td):
    n = u2.shape[0]
    grid_spec = pltpu.PrefetchScalarGridSpec(
        num_scalar_prefetch=2,
        grid=(n // td,),
        in_specs=[pl.BlockSpec((td * TOP_K,), lambda i, z, nu: (i,), memory_space=pltpu.SMEM),
                  pl.BlockSpec((td, D_MODEL), lambda i, z, nu: (i, 0))],
        out_specs=pl.BlockSpec(memory_space=pl.ANY),
        scratch_shapes=[pltpu.VMEM((ROW_BLOCK, D_MODEL), F32),
                        pltpu.SemaphoreType.DMA, pltpu.SemaphoreType.DMA],
    )
    return pl.pallas_call(
        functools.partial(_dispatch_kernel, td=td, n_blocks=n_rows // ROW_BLOCK),
        out_shape=jax.ShapeDtypeStruct((n_rows, D_MODEL), F32),
        grid_spec=grid_spec,
        compiler_params=pltpu.CompilerParams(dimension_semantics=("arbitrary",)),
        name="dispatch",
    )(zrow, n_used, dest_flat, u2)


def _experts_kernel(be_ref, nu_ref, x_ref, w1_ref, b1_ref, w2_ref, b2_ref, o_ref):
    del be_ref

    @pl.when(pl.program_id(0) < nu_ref[0])
    def _():
        x = x_ref[...].astype(BF16)
        acc = None
        step = 512
        for j in range(D_FF // step):
            cs = slice(j * step, (j + 1) * step)
            ls = slice(D_FF + j * step, D_FF + (j + 1) * step)
            glu = jnp.dot(x, w1_ref[0, :, cs], preferred_element_type=F32) + b1_ref[0, :, cs]
            lin = jnp.dot(x, w1_ref[0, :, ls], preferred_element_type=F32) + b1_ref[0, :, ls]
            glu = jnp.minimum(glu, SWIGLU_LIMIT)
            lin = jnp.clip(lin, -SWIGLU_LIMIT, SWIGLU_LIMIT)
            act = (glu * _sigmoid(SWIGLU_ALPHA * glu) * (lin + 1.0)).astype(BF16)
            part = jnp.dot(act, w2_ref[0, cs, :], preferred_element_type=F32)
            acc = part if acc is None else acc + part
        o_ref[...] = acc + b2_ref[0]

    @pl.when(pl.program_id(0) >= nu_ref[0])
    def _():
        o_ref[...] = jnp.zeros_like(o_ref)


def _experts(block_e, n_used, xs, w1_bf, b1, w2_bf, b2):
    n_rows = xs.shape[0]
    nb = n_rows // ROW_BLOCK
    grid_spec = pltpu.PrefetchScalarGridSpec(
        num_scalar_prefetch=2,
        grid=(nb,),
        in_specs=[pl.BlockSpec((ROW_BLOCK, D_MODEL), lambda i, be, nu: (jnp.minimum(i, nu[0] - 1), 0)),
                  pl.BlockSpec((1, D_MODEL, 2 * D_FF), lambda i, be, nu: (be[i], 0, 0)),
                  pl.BlockSpec((1, 1, 2 * D_FF), lambda i, be, nu: (be[i], 0, 0)),
                  pl.BlockSpec((1, D_FF, D_MODEL), lambda i, be, nu: (be[i], 0, 0)),
                  pl.BlockSpec((1, 1, D_MODEL), lambda i, be, nu: (be[i], 0, 0))],
        out_specs=pl.BlockSpec((ROW_BLOCK, D_MODEL), lambda i, be, nu: (i, 0)),
    )
    return pl.pallas_call(
        _experts_kernel,
        out_shape=jax.ShapeDtypeStruct((n_rows, D_MODEL), F32),
        grid_spec=grid_spec,
        compiler_params=pltpu.CompilerParams(dimension_semantics=("arbitrary",),
                                             vmem_limit_bytes=VMEM_LIMIT),
        name="experts",
    )(block_e, n_used, xs, w1_bf, b1.reshape(N_EXPERTS, 1, 2 * D_FF), w2_bf,
      b2.reshape(N_EXPERTS, 1, D_MODEL))


def _sc_gather_rows(table, idx_flat, chunk):
    info = plsc.get_sparse_core_info()
    n_workers = info.num_cores * info.num_subcores
    n_idx = idx_flat.shape[0]
    d = table.shape[1]
    per_worker = n_idx // n_workers
    assert per_worker * n_workers == n_idx and per_worker % chunk == 0 and chunk % 8 == 0
    mesh = plsc.VectorSubcoreMesh(core_axis_name="c", subcore_axis_name="s")

    @functools.partial(
        pl.kernel, mesh=mesh,
        out_type=jax.ShapeDtypeStruct((n_idx, d), table.dtype),
        scratch_types=[pltpu.VMEM((chunk,), jnp.int32),
                       pltpu.VMEM((chunk, d), table.dtype),
                       pltpu.SemaphoreType.DMA],
        name="sc_gather",
    )
    def gather(table_hbm, idx_hbm, out_hbm, idx_v, rows_v, sem):
        wid = lax.axis_index("s") * info.num_cores + lax.axis_index("c")
        base = wid * per_worker

        def body(i, carry):
            off = pl.multiple_of(base + i * chunk, 8)
            pltpu.sync_copy(idx_hbm.at[pl.ds(off, chunk)], idx_v)
            pltpu.async_copy(table_hbm.at[idx_v], rows_v, sem).wait()
            pltpu.sync_copy(rows_v, out_hbm.at[pl.ds(off, chunk)])
            return carry

        lax.fori_loop(0, per_worker // chunk, body, 0)

    return gather(table, idx_flat)


def _finish_kernel(h_ref, pw_ref, mod_ref, nw_ref, y0_ref, y1_ref, y2_ref, y3_ref, o_ref):
    pw = pw_ref[...]
    moe = pw[:, 0:1] * y0_ref[0]
    for j, y_ref in enumerate((y1_ref, y2_ref, y3_ref), start=1):
        moe = moe + pw[:, j:j + 1] * y_ref[0]
    gate_f = mod_ref[0, 5:6, :]
    o_ref[...] = _rms(h_ref[...] + gate_f * moe, nw_ref[...])


def _finish(h1, pw, mod, norm_w, y4, seq, tc):
    n = h1.shape[0]
    per_b = seq // tc
    slot = lambda j: pl.BlockSpec((1, tc, D_MODEL), lambda i: (j, i, 0))
    return pl.pallas_call(
        _finish_kernel,
        out_shape=jax.ShapeDtypeStruct((n, D_MODEL), F32),
        grid=(n // tc,),
        in_specs=[pl.BlockSpec((tc, D_MODEL), lambda i: (i, 0)),
                  pl.BlockSpec((tc, LANES), lambda i: (i, 0)),
                  pl.BlockSpec((1, 6, D_MODEL), lambda i: (i // per_b, 0, 0)),
                  pl.BlockSpec((1, D_MODEL), lambda i: (0, 0)),
                  slot(0), slot(1), slot(2), slot(3)],
        out_specs=pl.BlockSpec((tc, D_MODEL), lambda i: (i, 0)),
        compiler_params=pltpu.CompilerParams(dimension_semantics=("arbitrary",),
                                             vmem_limit_bytes=VMEM_LIMIT),
        name="finish",
    )(h1, pw, mod, norm_w.reshape(1, D_MODEL), y4, y4, y4, y4)


def _moe_plan(counts, n_assign):
    cnt = counts[:, 0, :N_EXPERTS]
    sizes = jnp.sum(cnt, axis=0)
    padded = (sizes + ROW_BLOCK - 1) // ROW_BLOCK * ROW_BLOCK
    pad_end = jnp.cumsum(padded)
    pad_start = pad_end - padded
    tile_base = pad_start[None, :] + jnp.cumsum(cnt, axis=0) - cnt
    digits = jnp.stack([tile_base % 256, (tile_base // 256) % 256, tile_base // 65536], axis=1)
    rows = jnp.tile(digits, (1, 1, TOP_K))
    slot_of_row = jnp.arange(LANES, dtype=jnp.int32) // N_EXPERTS
    col = jnp.arange(LANES, dtype=jnp.int32)
    bt = jnp.where(slot_of_row[:, None] == col[None, :], rows[..., None], 0).astype(BF16)
    nb = n_assign // ROW_BLOCK + N_EXPERTS
    block_start = jnp.arange(nb, dtype=jnp.int32) * ROW_BLOCK
    block_e = jnp.minimum(jnp.sum(pad_end[None, :] <= block_start[:, None], axis=1),
                          N_EXPERTS - 1).astype(jnp.int32)
    n_used = (pad_end[-1] // ROW_BLOCK).astype(jnp.int32).reshape(1)
    zrow = jnp.where(padded > 0, pad_end - ROW_BLOCK, -1).astype(jnp.int32)
    return bt, block_e, n_used, zrow, nb * ROW_BLOCK


def kernel(x, c, w_ada, b_ada, norm_mix_w, w_in, hgrn_lower_bounds, hgrn_norm_w, conv_w,
           w_hgrn_out, w_conv_out, w_mix_out, norm_ffn_w, w_router, b_router, w1, b1, w2, b2,
           norm_final_w):
    bsz, seq, d = x.shape
    assert d == D_MODEL and seq % CHUNK == 0
    n = bsz * seq
    depth = w_ada.shape[0]
    tile = min(512, seq)
    assert seq % tile == 0 and (n * TOP_K) % ROW_BLOCK == 0
    h = x.reshape(n, d)
    wr_pad = jnp.zeros((depth, D_MODEL, LANES), BF16).at[:, :, :N_EXPERTS].set(w_router.astype(BF16))
    br_pad = jnp.zeros((depth, 1, LANES), F32).at[:, 0, :N_EXPERTS].set(b_router)
    for layer in range(depth):
        mod = _ada(c, w_ada[layer], b_ada[layer]).reshape(bsz, 6, d)
        proj = _inproj(h, mod, norm_mix_w[layer], w_in[layer].astype(BF16), seq, tile)
        h = _mix(proj, h, mod, hgrn_lower_bounds, hgrn_norm_w[layer], conv_w[layer],
                 w_hgrn_out[layer].astype(BF16), w_conv_out[layer].astype(BF16),
                 w_mix_out[layer].astype(BF16), layer, bsz, seq, tile)
        u2, oh4, rk, pw, counts = _route(h, mod, norm_ffn_w[layer], wr_pad[layer], br_pad[layer], seq, tile)
        bt, block_e, n_used, zrow, n_rows = _moe_plan(counts, n * TOP_K)
        dest2 = _dest(oh4, rk, bt, tile)[:, :TOP_K]
        xs = _dispatch(zrow, n_used, dest2.reshape(n * TOP_K), u2, n_rows, tile)
        ys = _experts(block_e, n_used, xs, w1[layer].astype(BF16), b1[layer], w2[layer].astype(BF16),
                      b2[layer])
        assert layer == depth - 1, "only the last layer applies the final norm"
        y4 = _sc_gather_rows(ys, dest2.T.reshape(TOP_K * n), 64).reshape(TOP_K, n, d)
        h = _finish(h, pw, mod, norm_final_w, y4, seq, tile)
    return h.reshape(bsz, seq, d)
```

```python
import functools

import jax
import jax.numpy as jnp
from jax import lax
from jax.experimental import pallas as pl
from jax.experimental.pallas import tpu as pltpu
from jax.experimental.pallas import tpu_sc as plsc

F32 = jnp.float32
BF16 = jnp.bfloat16

D_MODEL = 1024
HGRN_HEADS = 4
HEAD_DIM = 128
HGRN_WIDTH = HGRN_HEADS * HEAD_DIM
CONV_WIDTH = 512
CONV_K = 3
CHUNK = 64
N_EXPERTS = 32
TOP_K = 4
D_FF = 1024
SWIGLU_LIMIT = 7.0
SWIGLU_ALPHA = 1.702
EPS = 1e-6
IN_COLS = 4 * HGRN_WIDTH + 3 * CONV_WIDTH + 2 * D_MODEL
LANES = 128
ROW_BLOCK = 256
PACKED = D_MODEL // 2
SC_CHUNK = 64
PAD_SLOTS = N_EXPERTS * ROW_BLOCK
VMEM_LIMIT = 56 * 1024 * 1024


def _sigmoid(x):
    return 1.0 / (1.0 + jnp.exp(-x))


def _rms(x, w):
    ms = jnp.mean(x * x, axis=-1, keepdims=True)
    return x * lax.rsqrt(ms + EPS) * w


def _pack_rows(x):
    w = x.shape[1] // 2
    lo = lax.bitcast_convert_type(x[:, :w].astype(BF16).astype(F32), jnp.uint32)
    hi = lax.bitcast_convert_type(x[:, w:].astype(BF16).astype(F32), jnp.uint32)
    return lax.bitcast_convert_type((lo >> 16) | (hi & jnp.uint32(0xFFFF0000)), jnp.int32)


def _unpack_rows(p):
    u = lax.bitcast_convert_type(p, jnp.uint32)
    lo = lax.bitcast_convert_type(u << 16, F32)
    hi = lax.bitcast_convert_type(u & jnp.uint32(0xFFFF0000), F32)
    return jnp.concatenate([lo, hi], axis=1)


def _nt_dot(a, b):
    return lax.dot_general(a, b, (((1,), (1,)), ((), ())), preferred_element_type=F32)


def _tn_dot(a, b):
    return lax.dot_general(a, b, (((0,), (0,)), ((), ())), preferred_element_type=F32)


def _ada_kernel(c_ref, w_ref, b_ref, o_ref):
    c = c_ref[...]
    sc = (c * _sigmoid(c)).astype(BF16)
    o_ref[...] = jnp.dot(sc, w_ref[...].astype(BF16), preferred_element_type=F32) + b_ref[...]


def _ada(c, w_ada, b_ada):
    bsz, d = c.shape
    n = w_ada.shape[1]
    return pl.pallas_call(
        _ada_kernel,
        out_shape=jax.ShapeDtypeStruct((bsz, n), F32),
        grid=(n // d,),
        in_specs=[pl.BlockSpec((bsz, d), lambda j: (0, 0)),
                  pl.BlockSpec((d, d), lambda j: (0, j)),
                  pl.BlockSpec((1, d), lambda j: (0, j))],
        out_specs=pl.BlockSpec((bsz, d), lambda j: (0, j)),
        name="ada",
    )(c, w_ada, b_ada.reshape(1, n))


def _inproj_kernel(x_ref, mod_ref, nw_ref, w_ref, o_ref, u_scr):
    y = _rms(x_ref[...], nw_ref[...])
    shift = mod_ref[0, 0:1, :]
    scale = mod_ref[0, 1:2, :]
    u_scr[...] = (y * (1.0 + scale) + shift).astype(BF16)
    step = 512
    for j in range(IN_COLS // step):
        o_ref[:, j * step:(j + 1) * step] = jnp.dot(
            u_scr[...], w_ref[:, j * step:(j + 1) * step], preferred_element_type=F32).astype(BF16)


def _inproj(x2, mod, norm_w, w_in_bf, seq, tm):
    n = x2.shape[0]
    per_b = seq // tm
    return pl.pallas_call(
        _inproj_kernel,
        out_shape=jax.ShapeDtypeStruct((n, IN_COLS), BF16),
        grid=(n // tm,),
        in_specs=[pl.BlockSpec((tm, D_MODEL), lambda i: (i, 0)),
                  pl.BlockSpec((1, 6, D_MODEL), lambda i: (i // per_b, 0, 0)),
                  pl.BlockSpec((1, D_MODEL), lambda i: (0, 0)),
                  pl.BlockSpec((D_MODEL, IN_COLS), lambda i: (0, 0))],
        out_specs=pl.BlockSpec((tm, IN_COLS), lambda i: (i, 0)),
        scratch_shapes=[pltpu.VMEM((tm, D_MODEL), BF16)],
        compiler_params=pltpu.CompilerParams(dimension_semantics=("arbitrary",),
                                             vmem_limit_bytes=VMEM_LIMIT),
        name="inproj",
    )(x2, mod, norm_w.reshape(1, D_MODEL), w_in_bf)


_HEADS = [slice(h * HEAD_DIM, (h + 1) * HEAD_DIM) for h in range(HGRN_HEADS)]


def _chunk_rows(c):
    if isinstance(c, int):
        return pl.ds(c * CHUNK, CHUNK)
    return pl.ds(pl.multiple_of(c * CHUNK, CHUNK), CHUNK)


def _hgrn_gates(c, lb, tri, f_ref, b_scr, k_scr, f_scr):
    fx = f_ref[_chunk_rows(c), :].astype(F32)
    f = lb + (1.0 - lb) * _sigmoid(fx)
    g = jnp.log(f)
    g1 = g.astype(BF16)
    r1 = g - g1.astype(F32)
    g2 = r1.astype(BF16)
    g3 = (r1 - g2.astype(F32)).astype(BF16)
    b_scr[...] = (jnp.dot(tri, g1, preferred_element_type=F32)
                  + jnp.dot(tri, g2, preferred_element_type=F32)
                  + jnp.dot(tri, g3, preferred_element_type=F32))
    f_scr[...] = f
    k_scr[...] = 1.0 - f


def _hgrn_scores(c, q_ref, b_scr, k_scr, f_scr):
    rows = _chunk_rows(c)
    qs = [q_ref[rows, hs].astype(F32) for hs in _HEADS]
    s_mats = [_level_scores(1, qs[h], _HEADS[h], b_scr, k_scr, f_scr) for h in range(HGRN_HEADS)]
    for lvl in range(2, 7):
        for h in range(HGRN_HEADS):
            s_mats[h] = s_mats[h] + _level_scores(lvl, qs[h], _HEADS[h], b_scr, k_scr, f_scr)
    return s_mats


def _hgrn_outputs(c, s_mats, q_ref, v_ref, st_scr, o_scr, b_scr, k_scr):
    rows = _chunk_rows(c)
    for h, hs in enumerate(_HEADS):
        q = q_ref[rows, hs].astype(F32)
        v_bf = v_ref[rows, hs]
        b = b_scr[:, hs]
        kk = k_scr[:, hs]
        b_last = b_scr[CHUNK - 1:CHUNK, hs]
        st = st_scr[h]
        qd = (q * jnp.exp(b)).astype(BF16)
        kdec = (kk * jnp.exp(b_last - b)).astype(BF16)
        diag = jnp.sum(q * kk, axis=-1, keepdims=True)
        o_scr[rows, hs] = (_nt_dot(qd, st.astype(BF16))
                           + jnp.dot(s_mats[h].astype(BF16), v_bf, preferred_element_type=F32)
                           + diag * v_bf.astype(F32))
        st_scr[h] = jnp.exp(b_last) * st + _tn_dot(v_bf, kdec)


def _level_scores(lvl, q, hs, b_scr, k_scr, f_scr):
    row = lax.broadcasted_iota(jnp.int32, (CHUNK, 1), 0)
    col = lax.broadcasted_iota(jnp.int32, (1, CHUNK), 1)
    b = b_scr[:, hs]
    kk = k_scr[:, hs]
    blk = 1 << lvl
    half = blk // 2
    if lvl == 1:
        odd = (row & 1) == 1
        qx = jnp.where(odd, q * f_scr[:, hs], 0.0).astype(BF16)
        kx = jnp.where(odd, 0.0, kk).astype(BF16)
        return jnp.where((row >> 1) == (col >> 1), _nt_dot(qx, kx), 0.0)
    if half < 8:
        groups = []
        sub = lax.broadcasted_iota(jnp.int32, (8, 1), 0)
        for j in range(CHUNK // 8):
            rj = None
            for k in reversed(range(8 // blk)):
                m = 8 * j + k * blk + half - 1
                bm = jnp.broadcast_to(b_scr[m:m + 1, hs], (8, HEAD_DIM))
                rj = bm if rj is None else jnp.where(sub < (k + 1) * blk, bm, rj)
            groups.append(rj)
        ref = jnp.concatenate(groups, axis=0)
        second = (row & (blk - 1)) >= half
        qx = (q * jnp.exp(jnp.where(second, b - ref, -jnp.inf))).astype(BF16)
        kx = (kk * jnp.exp(jnp.where(second, -jnp.inf, ref - b))).astype(BF16)
        return jnp.where((row >> lvl) == (col >> lvl), _nt_dot(qx, kx), 0.0)
    n_blk = CHUNK // blk
    qparts, kparts = [], []
    for j in range(n_blk):
        m = j * blk + half - 1
        bm = b_scr[m:m + 1, hs]
        tq = slice(j * blk + half, (j + 1) * blk)
        tk = slice(j * blk, j * blk + half)
        qparts.append(q[tq] * jnp.exp(b[tq] - bm))
        kparts.append(kk[tk] * jnp.exp(bm - b[tk]))
        kparts.append(jnp.zeros((half, HEAD_DIM), F32))
    qx = jnp.concatenate(qparts, axis=0).astype(BF16)
    kx = jnp.concatenate(kparts, axis=0).astype(BF16)
    sc = _nt_dot(qx, kx)
    if n_blk > 1:
        crow = lax.broadcasted_iota(jnp.int32, (CHUNK // 2, 1), 0)
        sc = jnp.where((crow // half) == (col >> lvl), sc, 0.0)
    pieces = []
    for j in range(n_blk):
        pieces.append(jnp.zeros((half, CHUNK), F32))
        pieces.append(sc[j * half:(j + 1) * half])
    return jnp.concatenate(pieces, axis=0)


def _mix_kernel(lbt_ref, q_ref, f_ref, i_ref, g_ref, cb_ref, cc_ref, ch_ref, ga0_ref, ga1_ref,
                gb0_ref, gb1_ref, x_ref, mod_ref, hnw_ref, cw_ref, wa_ref, wb_ref, wm_ref,
                o_ref, st_scr, o_scr, carry_scr, b_scr, k_scr, f_scr, *, layer, rows_per_step, epi_rows):
    @pl.when(pl.program_id(1) == 0)
    def _():
        st_scr[...] = jnp.zeros_like(st_scr)
        carry_scr[...] = jnp.zeros_like(carry_scr)

    tab = lbt_ref[...]
    tmax = jnp.max(tab, axis=0, keepdims=True)
    te = jnp.exp(tab - tmax)
    lb = jnp.sum(te[0:layer + 1], axis=0, keepdims=True) / jnp.sum(te, axis=0, keepdims=True)

    ri = lax.broadcasted_iota(jnp.int32, (CHUNK, CHUNK), 0)
    ci = lax.broadcasted_iota(jnp.int32, (CHUNK, CHUNK), 1)
    tri = jnp.where(ci <= ri, 1.0, 0.0).astype(BF16)

    n_chunks = rows_per_step // CHUNK

    def slot(s):
        return b_scr.at[s], k_scr.at[s], f_scr.at[s]

    def chunk(c, c_next, s):
        bs, ks, fs = slot(s)
        s_mats = _hgrn_scores(c, q_ref, bs, ks, fs)
        _hgrn_gates(c_next, lb, tri, f_ref, *slot(1 - s))
        _hgrn_outputs(c, s_mats, q_ref, i_ref, st_scr, o_scr, bs, ks)

    def pair_body(i, carry):
        chunk(2 * i, 2 * i + 1, 0)
        chunk(2 * i + 1, jnp.minimum(2 * i + 2, n_chunks - 1), 1)
        return carry

    _hgrn_gates(0, lb, tri, f_ref, *slot(0))
    lax.fori_loop(0, n_chunks // 2, pair_body, 0)

    gate_m = mod_ref[0, 2:3, :]
    hnw = hnw_ref[...]
    cw0 = cw_ref[0:1, :]
    cw1 = cw_ref[1:2, :]
    cw2 = cw_ref[2:3, :]
    for r in range(rows_per_step // epi_rows):
        rs = slice(r * epi_rows, (r + 1) * epi_rows)
        parts = []
        for h in range(HGRN_HEADS):
            hs = slice(h * HEAD_DIM, (h + 1) * HEAD_DIM)
            oh = _rms(o_scr[rs, hs], hnw)
            go = g_ref[rs, hs].astype(F32)
            parts.append((oh * (go * _sigmoid(go))).astype(BF16))
        ya = jnp.dot(jnp.concatenate(parts, axis=1), wa_ref[...], preferred_element_type=F32)
        uc = cc_ref[rs, :].astype(F32) * ch_ref[rs, :].astype(F32)
        prev = carry_scr[...]
        rowi = lax.broadcasted_iota(jnp.int32, (epi_rows, 1), 0)
        s1 = jnp.where(rowi == 0, prev[7:8, :], pltpu.roll(uc, 1, 0))
        s2 = pltpu.roll(uc, 2, 0)
        s2 = jnp.where(rowi == 0, prev[6:7, :], jnp.where(rowi == 1, prev[7:8, :], s2))
        carry_scr[...] = uc[epi_rows - 8:epi_rows, :]
        yc = cb_ref[rs, :].astype(F32) * (cw2 * uc + cw1 * s1 + cw0 * s2)
        yb = jnp.dot(yc.astype(BF16), wb_ref[...], preferred_element_type=F32)
        ga = jnp.concatenate([ga0_ref[rs, :], ga1_ref[rs, :]], axis=1).astype(F32)
        gb = jnp.concatenate([gb0_ref[rs, :], gb1_ref[rs, :]], axis=1).astype(F32)
        merged = (_sigmoid(ga) * ya + _sigmoid(gb) * yb).astype(BF16)
        o_ref[rs, :] = x_ref[rs, :] + gate_m * jnp.dot(merged, wm_ref[...], preferred_element_type=F32)


def _mix(proj, x2, mod, lb_table, hgrn_norm_w, conv_w, wa, wb, wm, layer, bsz, seq, tt):
    n = x2.shape[0]
    per_b = seq // tt

    def col(col_block):
        return pl.BlockSpec((tt, 512), lambda b, t: (b * per_b + t, col_block))

    const = lambda shape: pl.BlockSpec(shape, lambda b, t: (0,) * len(shape))
    in_specs = [
        const(lb_table.shape),
        col(0), col(1), col(2), col(3),
        col(4), col(5), col(6),
        col(7), col(8), col(9), col(10),
        pl.BlockSpec((tt, D_MODEL), lambda b, t: (b * per_b + t, 0)),
        pl.BlockSpec((1, 6, D_MODEL), lambda b, t: (b, 0, 0)),
        const((1, HEAD_DIM)), const((CONV_K, CONV_WIDTH)),
        const((HGRN_WIDTH, D_MODEL)), const((CONV_WIDTH, D_MODEL)), const((D_MODEL, D_MODEL)),
    ]
    kern = functools.partial(_mix_kernel, layer=layer, rows_per_step=tt, epi_rows=min(tt, 256))
    return pl.pallas_call(
        kern,
        out_shape=jax.ShapeDtypeStruct((n, D_MODEL), F32),
        grid=(bsz, per_b),
        in_specs=in_specs,
        out_specs=pl.BlockSpec((tt, D_MODEL), lambda b, t: (b * per_b + t, 0)),
        scratch_shapes=[pltpu.VMEM((HGRN_HEADS, HEAD_DIM, HEAD_DIM), F32),
                        pltpu.VMEM((tt, HGRN_WIDTH), F32),
                        pltpu.VMEM((8, CONV_WIDTH), F32),
                        pltpu.VMEM((2, CHUNK, HGRN_WIDTH), F32),
                        pltpu.VMEM((2, CHUNK, HGRN_WIDTH), F32),
                        pltpu.VMEM((2, CHUNK, HGRN_WIDTH), F32)],
        compiler_params=pltpu.CompilerParams(dimension_semantics=("arbitrary", "arbitrary"),
                                             vmem_limit_bytes=VMEM_LIMIT),
        name="mix",
    )(lb_table, *([proj] * 11),
      x2, mod, hgrn_norm_w.reshape(1, HEAD_DIM), conv_w, wa, wb, wm)


def _route_kernel(h_ref, mod_ref, nw_ref, wr_ref, br_ref, u_ref, oh4_ref, rk_ref, pw_ref, cnt_ref, *, tr):
    shift = mod_ref[0, 3:4, :]
    scale = mod_ref[0, 4:5, :]
    u = _rms(h_ref[...], nw_ref[...]) * (1.0 + scale) + shift
    u_ref[...] = _pack_rows(u)
    lane = lax.broadcasted_iota(jnp.int32, (tr, LANES), 1)
    logits = jnp.dot(u.astype(BF16), wr_ref[...], preferred_element_type=F32) + br_ref[...]
    logits = jnp.where(lane < N_EXPERTS, logits, -jnp.inf)
    idx, val = [], []
    cur = logits
    for _ in range(TOP_K):
        m = jnp.max(cur, axis=-1, keepdims=True)
        i = jnp.min(jnp.where(cur == m, lane, LANES), axis=-1, keepdims=True)
        idx.append(i)
        val.append(m)
        cur = jnp.where(lane == i, -jnp.inf, cur)
    ex = [jnp.exp(v - val[0]) for v in val]
    den = ex[0] + ex[1] + ex[2] + ex[3]
    onehot = jnp.zeros((tr, LANES), F32)
    for i in idx:
        onehot = onehot + jnp.where(lane == i, 1.0, 0.0)
    ri = lax.broadcasted_iota(jnp.int32, (tr, tr), 0)
    ci = lax.broadcasted_iota(jnp.int32, (tr, tr), 1)
    tri = jnp.where(ci < ri, 1.0, 0.0).astype(BF16)
    pref = jnp.dot(tri, onehot.astype(BF16), preferred_element_type=F32)
    rk = jnp.zeros((tr, LANES), jnp.int32)
    pw = jnp.zeros((tr, LANES), F32)
    oh4 = jnp.zeros((tr, LANES), F32)
    for j in range(TOP_K):
        rank = jnp.sum(jnp.where(lane == idx[j], pref, 0.0), axis=-1, keepdims=True).astype(jnp.int32)
        rk = jnp.where(lane == j, rank, rk)
        pw = jnp.where(lane == j, ex[j] / den, pw)
        oh4 = oh4 + jnp.where(lane == idx[j] + N_EXPERTS * j, 1.0, 0.0)
    oh4_ref[...] = oh4.astype(BF16)
    rk_ref[...] = rk
    pw_ref[...] = pw
    cnt_ref[0] = jnp.sum(onehot, axis=0, keepdims=True).astype(jnp.int32)


def _route(h1, mod, norm_w, wr_pad, br_pad, seq, tr):
    n = h1.shape[0]
    per_b = seq // tr
    nt = n // tr
    return pl.pallas_call(
        functools.partial(_route_kernel, tr=tr),
        out_shape=(jax.ShapeDtypeStruct((n, PACKED), jnp.int32),
                   jax.ShapeDtypeStruct((n, LANES), BF16),
                   jax.ShapeDtypeStruct((n, LANES), jnp.int32),
                   jax.ShapeDtypeStruct((n, LANES), F32),
                   jax.ShapeDtypeStruct((nt, 1, LANES), jnp.int32)),
        grid=(nt,),
        in_specs=[pl.BlockSpec((tr, D_MODEL), lambda i: (i, 0)),
                  pl.BlockSpec((1, 6, D_MODEL), lambda i: (i // per_b, 0, 0)),
                  pl.BlockSpec((1, D_MODEL), lambda i: (0, 0)),
                  pl.BlockSpec((D_MODEL, LANES), lambda i: (0, 0)),
                  pl.BlockSpec((1, LANES), lambda i: (0, 0))],
        out_specs=(pl.BlockSpec((tr, PACKED), lambda i: (i, 0)),
                   pl.BlockSpec((tr, LANES), lambda i: (i, 0)),
                   pl.BlockSpec((tr, LANES), lambda i: (i, 0)),
                   pl.BlockSpec((tr, LANES), lambda i: (i, 0)),
                   pl.BlockSpec((1, 1, LANES), lambda i: (i, 0, 0))),
        compiler_params=pltpu.CompilerParams(dimension_semantics=("arbitrary",),
                                             vmem_limit_bytes=VMEM_LIMIT),
        name="route",
    )(h1, mod, norm_w.reshape(1, D_MODEL), wr_pad, br_pad)


def _dest_kernel(oh4_ref, rk_ref, bt_ref, o_ref):
    oh = oh4_ref[...]
    start = (jnp.dot(oh, bt_ref[0, 0], preferred_element_type=F32)
             + 256.0 * jnp.dot(oh, bt_ref[0, 1], preferred_element_type=F32)
             + 65536.0 * jnp.dot(oh, bt_ref[0, 2], preferred_element_type=F32))
    o_ref[...] = start.astype(jnp.int32) + rk_ref[...]


def _dest(oh4, rk, bt, tr):
    n = oh4.shape[0]
    return pl.pallas_call(
        _dest_kernel,
        out_shape=jax.ShapeDtypeStruct((n, LANES), jnp.int32),
        grid=(n // tr,),
        in_specs=[pl.BlockSpec((tr, LANES), lambda i: (i, 0)),
                  pl.BlockSpec((tr, LANES), lambda i: (i, 0)),
                  pl.BlockSpec((1, 3, LANES, LANES), lambda i: (i, 0, 0, 0))],
        out_specs=pl.BlockSpec((tr, LANES), lambda i: (i, 0)),
        name="dest",
    )(oh4, rk, bt)


def _sc_workers():
    info = plsc.get_sparse_core_info()
    return info.num_cores, info.num_cores * info.num_subcores


def _sc_scatter_rows(rows, idx_slots, pad_idx, n_out):
    n_cores, n_workers = _sc_workers()
    n, w = rows.shape
    k = idx_slots.shape[0] // n
    per_worker = n // n_workers
    pad_per_worker = pad_idx.shape[0] // n_workers
    assert per_worker % SC_CHUNK == 0 and pad_per_worker % SC_CHUNK == 0
    mesh = plsc.VectorSubcoreMesh(core_axis_name="c", subcore_axis_name="s")
    zeros = jnp.zeros((SC_CHUNK, w), rows.dtype)

    @functools.partial(
        pl.kernel, mesh=mesh,
        out_type=jax.ShapeDtypeStruct((n_out, w), rows.dtype),
        scratch_types=[pltpu.VMEM((SC_CHUNK,), jnp.int32),
                       pltpu.VMEM((SC_CHUNK, w), rows.dtype),
                       pltpu.SemaphoreType.DMA],
        name="sc_scatter",
    )
    def scatter(rows_hbm, idx_hbm, pad_hbm, zeros_hbm, out_hbm, idx_v, rows_v, sem):
        wid = lax.axis_index("s") * n_cores + lax.axis_index("c")

        def body(i, carry):
            off = pl.multiple_of(wid * per_worker + i * SC_CHUNK, 8)
            pltpu.sync_copy(rows_hbm.at[pl.ds(off, SC_CHUNK)], rows_v)
            for j in range(k):
                pltpu.sync_copy(idx_hbm.at[pl.ds(pl.multiple_of(j * n + off, 8), SC_CHUNK)], idx_v)
                pltpu.async_copy(rows_v, out_hbm.at[idx_v], sem).wait()
            return carry

        lax.fori_loop(0, per_worker // SC_CHUNK, body, 0)
        pltpu.sync_copy(zeros_hbm, rows_v)

        def pad_body(i, carry):
            off = pl.multiple_of(wid * pad_per_worker + i * SC_CHUNK, 8)
            pltpu.sync_copy(pad_hbm.at[pl.ds(off, SC_CHUNK)], idx_v)
            pltpu.async_copy(rows_v, out_hbm.at[idx_v], sem).wait()
            return carry

        lax.fori_loop(0, pad_per_worker // SC_CHUNK, pad_body, 0)

    return scatter(rows, idx_slots, pad_idx, zeros)


def _experts_kernel(be_ref, nu_ref, x_ref, w1_ref, b1_ref, w2_ref, b2_ref, o_ref):
    del be_ref

    @pl.when(pl.program_id(0) < nu_ref[0])
    def _():
        x = _unpack_rows(x_ref[...]).astype(BF16)
        acc = None
        step = 512
        for j in range(D_FF // step):
            cs = slice(j * step, (j + 1) * step)
            ls = slice(D_FF + j * step, D_FF + (j + 1) * step)
            glu = jnp.dot(x, w1_ref[0, :, cs], preferred_element_type=F32) + b1_ref[0, :, cs]
            lin = jnp.dot(x, w1_ref[0, :, ls], preferred_element_type=F32) + b1_ref[0, :, ls]
            glu = jnp.minimum(glu, SWIGLU_LIMIT)
            lin = jnp.clip(lin, -SWIGLU_LIMIT, SWIGLU_LIMIT)
            act = (glu * _sigmoid(SWIGLU_ALPHA * glu) * (lin + 1.0)).astype(BF16)
            part = jnp.dot(act, w2_ref[0, cs, :], preferred_element_type=F32)
            acc = part if acc is None else acc + part
        o_ref[...] = _pack_rows(acc + b2_ref[0])

    @pl.when(pl.program_id(0) >= nu_ref[0])
    def _():
        o_ref[...] = jnp.zeros_like(o_ref)


def _experts(block_e, n_used, xs, w1_bf, b1, w2_bf, b2):
    n_rows = xs.shape[0]
    nb = n_rows // ROW_BLOCK
    grid_spec = pltpu.PrefetchScalarGridSpec(
        num_scalar_prefetch=2,
        grid=(nb,),
        in_specs=[pl.BlockSpec((ROW_BLOCK, PACKED), lambda i, be, nu: (jnp.minimum(i, nu[0] - 1), 0)),
                  pl.BlockSpec((1, D_MODEL, 2 * D_FF), lambda i, be, nu: (be[i], 0, 0)),
                  pl.BlockSpec((1, 1, 2 * D_FF), lambda i, be, nu: (be[i], 0, 0)),
                  pl.BlockSpec((1, D_FF, D_MODEL), lambda i, be, nu: (be[i], 0, 0)),
                  pl.BlockSpec((1, 1, D_MODEL), lambda i, be, nu: (be[i], 0, 0))],
        out_specs=pl.BlockSpec((ROW_BLOCK, PACKED), lambda i, be, nu: (i, 0)),
    )
    return pl.pallas_call(
        _experts_kernel,
        out_shape=jax.ShapeDtypeStruct((n_rows, PACKED), jnp.int32),
        grid_spec=grid_spec,
        compiler_params=pltpu.CompilerParams(dimension_semantics=("arbitrary",),
                                             vmem_limit_bytes=VMEM_LIMIT),
        name="experts",
    )(block_e, n_used, xs, w1_bf, b1.reshape(N_EXPERTS, 1, 2 * D_FF), w2_bf,
      b2.reshape(N_EXPERTS, 1, D_MODEL))


def _sc_gather_rows(table, idx_flat):
    n_cores, n_workers = _sc_workers()
    n_idx = idx_flat.shape[0]
    w = table.shape[1]
    per_worker = n_idx // n_workers
    n_chunks = per_worker // SC_CHUNK
    assert per_worker * n_workers == n_idx and n_chunks * SC_CHUNK == per_worker and n_chunks % 2 == 0
    mesh = plsc.VectorSubcoreMesh(core_axis_name="c", subcore_axis_name="s")

    @functools.partial(
        pl.kernel, mesh=mesh,
        out_type=jax.ShapeDtypeStruct((n_idx, w), table.dtype),
        scratch_types=[pltpu.VMEM((SC_CHUNK,), jnp.int32), pltpu.VMEM((SC_CHUNK,), jnp.int32),
                       pltpu.VMEM((SC_CHUNK, w), table.dtype), pltpu.VMEM((SC_CHUNK, w), table.dtype),
                       pltpu.SemaphoreType.DMA, pltpu.SemaphoreType.DMA],
        name="sc_gather",
    )
    def gather(table_hbm, idx_hbm, out_hbm, idx0, idx1, rows0, rows1, sem0, sem1):
        wid = lax.axis_index("s") * n_cores + lax.axis_index("c")
        bufs = ((idx0, rows0, sem0), (idx1, rows1, sem1))

        def chunk_off(c):
            return pl.multiple_of(wid * per_worker + c * SC_CHUNK, 8)

        def start(c, b):
            idx_v, rows_v, sem = bufs[b]
            pltpu.sync_copy(idx_hbm.at[pl.ds(chunk_off(c), SC_CHUNK)], idx_v)
            pltpu.async_copy(table_hbm.at[idx_v], rows_v, sem)

        def finish(c, b):
            idx_v, rows_v, sem = bufs[b]
            pltpu.make_async_copy(table_hbm.at[idx_v], rows_v, sem).wait()
            pltpu.sync_copy(rows_v, out_hbm.at[pl.ds(chunk_off(c), SC_CHUNK)])

        start(0, 0)

        def body(i, carry):
            start(2 * i + 1, 1)
            finish(2 * i, 0)

            @pl.when(i < n_chunks // 2 - 1)
            def _():
                start(2 * i + 2, 0)

            finish(2 * i + 1, 1)
            return carry

        lax.fori_loop(0, n_chunks // 2, body, 0)

    return gather(table, idx_flat)


def _finish_kernel(h_ref, pw_ref, mod_ref, nw_ref, y0_ref, y1_ref, y2_ref, y3_ref, o_ref):
    pw = pw_ref[...]
    moe = pw[:, 0:1] * _unpack_rows(y0_ref[0])
    for j, y_ref in enumerate((y1_ref, y2_ref, y3_ref), start=1):
        moe = moe + pw[:, j:j + 1] * _unpack_rows(y_ref[0])
    gate_f = mod_ref[0, 5:6, :]
    o_ref[...] = _rms(h_ref[...] + gate_f * moe, nw_ref[...])


def _finish(h1, pw, mod, norm_w, y4, seq, tc):
    n = h1.shape[0]
    per_b = seq // tc
    slot = lambda j: pl.BlockSpec((1, tc, PACKED), lambda i: (j, i, 0))
    return pl.pallas_call(
        _finish_kernel,
        out_shape=jax.ShapeDtypeStruct((n, D_MODEL), F32),
        grid=(n // tc,),
        in_specs=[pl.BlockSpec((tc, D_MODEL), lambda i: (i, 0)),
                  pl.BlockSpec((tc, LANES), lambda i: (i, 0)),
                  pl.BlockSpec((1, 6, D_MODEL), lambda i: (i // per_b, 0, 0)),
                  pl.BlockSpec((1, D_MODEL), lambda i: (0, 0)),
                  slot(0), slot(1), slot(2), slot(3)],
        out_specs=pl.BlockSpec((tc, D_MODEL), lambda i: (i, 0)),
        compiler_params=pltpu.CompilerParams(dimension_semantics=("arbitrary",),
                                             vmem_limit_bytes=VMEM_LIMIT),
        name="finish",
    )(h1, pw, mod, norm_w.reshape(1, D_MODEL), y4, y4, y4, y4)


def _moe_plan(counts, n_assign):
    cnt = counts[:, 0, :N_EXPERTS]
    sizes = jnp.sum(cnt, axis=0)
    padded = (sizes + ROW_BLOCK - 1) // ROW_BLOCK * ROW_BLOCK
    pad_end = jnp.cumsum(padded)
    pad_start = pad_end - padded
    tile_base = pad_start[None, :] + jnp.cumsum(cnt, axis=0) - cnt
    digits = jnp.stack([tile_base % 256, (tile_base // 256) % 256, tile_base // 65536], axis=1)
    rows = jnp.tile(digits, (1, 1, TOP_K))
    slot_of_row = jnp.arange(LANES, dtype=jnp.int32) // N_EXPERTS
    col = jnp.arange(LANES, dtype=jnp.int32)
    bt = jnp.where(slot_of_row[:, None] == col[None, :], rows[..., None], 0).astype(BF16)
    nb = n_assign // ROW_BLOCK + N_EXPERTS
    block_start = jnp.arange(nb, dtype=jnp.int32) * ROW_BLOCK
    block_e = jnp.minimum(jnp.sum(pad_end[None, :] <= block_start[:, None], axis=1),
                          N_EXPERTS - 1).astype(jnp.int32)
    n_used = (pad_end[-1] // ROW_BLOCK).astype(jnp.int32).reshape(1)
    r = jnp.arange(ROW_BLOCK, dtype=jnp.int32)[None, :]
    pad_idx = jnp.where(r < (padded - sizes)[:, None], (pad_start + sizes)[:, None] + r,
                        nb * ROW_BLOCK - 1).astype(jnp.int32).reshape(PAD_SLOTS)
    return bt, block_e, n_used, pad_idx, nb * ROW_BLOCK


def kernel(x, c, w_ada, b_ada, norm_mix_w, w_in, hgrn_lower_bounds, hgrn_norm_w, conv_w,
           w_hgrn_out, w_conv_out, w_mix_out, norm_ffn_w, w_router, b_router, w1, b1, w2, b2,
           norm_final_w):
    bsz, seq, d = x.shape
    assert d == D_MODEL and seq % CHUNK == 0
    n = bsz * seq
    depth = w_ada.shape[0]
    tile = min(512, seq)
    assert seq % tile == 0 and (n * TOP_K) % ROW_BLOCK == 0
    h = x.reshape(n, d)
    wr_pad = jnp.zeros((depth, D_MODEL, LANES), BF16).at[:, :, :N_EXPERTS].set(w_router.astype(BF16))
    br_pad = jnp.zeros((depth, 1, LANES), F32).at[:, 0, :N_EXPERTS].set(b_router)
    for layer in range(depth):
        mod = _ada(c, w_ada[layer], b_ada[layer]).reshape(bsz, 6, d)
        proj = _inproj(h, mod, norm_mix_w[layer], w_in[layer].astype(BF16), seq, tile)
        h = _mix(proj, h, mod, hgrn_lower_bounds, hgrn_norm_w[layer], conv_w[layer],
                 w_hgrn_out[layer].astype(BF16), w_conv_out[layer].astype(BF16),
                 w_mix_out[layer].astype(BF16), layer, bsz, seq, tile)
        u2, oh4, rk, pw, counts = _route(h, mod, norm_ffn_w[layer], wr_pad[layer], br_pad[layer], seq, tile)
        bt, block_e, n_used, pad_idx, n_rows = _moe_plan(counts, n * TOP_K)
        dest_slots = _dest(oh4, rk, bt, tile)[:, :TOP_K].T.reshape(TOP_K * n)
        xs = _sc_scatter_rows(u2, dest_slots, pad_idx, n_rows)
        ys = _experts(block_e, n_used, xs, w1[layer].astype(BF16), b1[layer], w2[layer].astype(BF16),
                      b2[layer])
        assert layer == depth - 1, "only the last layer applies the final norm"
        y4 = _sc_gather_rows(ys, dest_slots).reshape(TOP_K, n, PACKED)
        h = _finish(h, pw, mod, norm_final_w, y4, seq, tile)
    return h.reshape(bsz, seq, d)
```

```python
import functools

import jax
import jax.numpy as jnp
from jax import lax
from jax.experimental import pallas as pl
from jax.experimental.pallas import tpu as pltpu
from jax.experimental.pallas import tpu_sc as plsc

F32 = jnp.float32
BF16 = jnp.bfloat16

D_MODEL = 1024
HGRN_HEADS = 4
HEAD_DIM = 128
HGRN_WIDTH = HGRN_HEADS * HEAD_DIM
CONV_WIDTH = 512
CONV_K = 3
CHUNK = 64
N_EXPERTS = 32
TOP_K = 4
D_FF = 1024
SWIGLU_LIMIT = 7.0
SWIGLU_ALPHA = 1.702
EPS = 1e-6
IN_COLS = 4 * HGRN_WIDTH + 3 * CONV_WIDTH + 2 * D_MODEL
LANES = 128
ROW_BLOCK = 512
EXPERT_ROWS = 256
PACKED = D_MODEL // 2
SC_CHUNK = 64
PAD_SLOTS = N_EXPERTS * ROW_BLOCK
VMEM_LIMIT = 56 * 1024 * 1024


def _sigmoid(x):
    return 1.0 / (1.0 + jnp.exp(-x))


def _rms(x, w):
    ms = jnp.mean(x * x, axis=-1, keepdims=True)
    return x * lax.rsqrt(ms + EPS) * w


def _pack_rows(x):
    w = x.shape[1] // 2
    lo = lax.bitcast_convert_type(x[:, :w].astype(BF16).astype(F32), jnp.uint32)
    hi = lax.bitcast_convert_type(x[:, w:].astype(BF16).astype(F32), jnp.uint32)
    return lax.bitcast_convert_type((lo >> 16) | (hi & jnp.uint32(0xFFFF0000)), jnp.int32)


def _unpack_rows(p):
    u = lax.bitcast_convert_type(p, jnp.uint32)
    lo = lax.bitcast_convert_type(u << 16, F32)
    hi = lax.bitcast_convert_type(u & jnp.uint32(0xFFFF0000), F32)
    return jnp.concatenate([lo, hi], axis=1)


def _nt_dot(a, b):
    return lax.dot_general(a, b, (((1,), (1,)), ((), ())), preferred_element_type=F32)


def _tn_dot(a, b):
    return lax.dot_general(a, b, (((0,), (0,)), ((), ())), preferred_element_type=F32)


def _ada_kernel(c_ref, w_ref, b_ref, o_ref):
    c = c_ref[...]
    sc = (c * _sigmoid(c)).astype(BF16)
    o_ref[...] = jnp.dot(sc, w_ref[...].astype(BF16), preferred_element_type=F32) + b_ref[...]


def _ada(c, w_ada, b_ada):
    bsz, d = c.shape
    n = w_ada.shape[1]
    return pl.pallas_call(
        _ada_kernel,
        out_shape=jax.ShapeDtypeStruct((bsz, n), F32),
        grid=(n // d,),
        in_specs=[pl.BlockSpec((bsz, d), lambda j: (0, 0)),
                  pl.BlockSpec((d, d), lambda j: (0, j)),
                  pl.BlockSpec((1, d), lambda j: (0, j))],
        out_specs=pl.BlockSpec((bsz, d), lambda j: (0, j)),
        name="ada",
    )(c, w_ada, b_ada.reshape(1, n))


def _inproj_kernel(x_ref, mod_ref, nw_ref, w_ref, o_ref, u_scr):
    y = _rms(x_ref[...], nw_ref[...])
    shift = mod_ref[0, 0:1, :]
    scale = mod_ref[0, 1:2, :]
    u_scr[...] = (y * (1.0 + scale) + shift).astype(BF16)
    step = 512
    for j in range(IN_COLS // step):
        o_ref[:, j * step:(j + 1) * step] = jnp.dot(
            u_scr[...], w_ref[:, j * step:(j + 1) * step], preferred_element_type=F32).astype(BF16)


def _inproj(x2, mod, norm_w, w_in_bf, seq, tm):
    n = x2.shape[0]
    per_b = seq // tm
    return pl.pallas_call(
        _inproj_kernel,
        out_shape=jax.ShapeDtypeStruct((n, IN_COLS), BF16),
        grid=(n // tm,),
        in_specs=[pl.BlockSpec((tm, D_MODEL), lambda i: (i, 0)),
                  pl.BlockSpec((1, 6, D_MODEL), lambda i: (i // per_b, 0, 0)),
                  pl.BlockSpec((1, D_MODEL), lambda i: (0, 0)),
                  pl.BlockSpec((D_MODEL, IN_COLS), lambda i: (0, 0))],
        out_specs=pl.BlockSpec((tm, IN_COLS), lambda i: (i, 0)),
        scratch_shapes=[pltpu.VMEM((tm, D_MODEL), BF16)],
        compiler_params=pltpu.CompilerParams(dimension_semantics=("arbitrary",),
                                             vmem_limit_bytes=VMEM_LIMIT),
        name="inproj",
    )(x2, mod, norm_w.reshape(1, D_MODEL), w_in_bf)


_HEADS = [slice(h * HEAD_DIM, (h + 1) * HEAD_DIM) for h in range(HGRN_HEADS)]


def _chunk_rows(c):
    if isinstance(c, int):
        return pl.ds(c * CHUNK, CHUNK)
    return pl.ds(pl.multiple_of(c * CHUNK, CHUNK), CHUNK)


def _hgrn_gates(c, lb, tri, f_ref, b_scr, k_scr, f_scr):
    fx = f_ref[_chunk_rows(c), :].astype(F32)
    f = lb + (1.0 - lb) * _sigmoid(fx)
    g = jnp.log(f)
    g1 = g.astype(BF16)
    r1 = g - g1.astype(F32)
    g2 = r1.astype(BF16)
    g3 = (r1 - g2.astype(F32)).astype(BF16)
    b_scr[...] = (jnp.dot(tri, g1, preferred_element_type=F32)
                  + jnp.dot(tri, g2, preferred_element_type=F32)
                  + jnp.dot(tri, g3, preferred_element_type=F32))
    f_scr[...] = f
    k_scr[...] = 1.0 - f


def _hgrn_scores(c, q_ref, b_scr, k_scr, f_scr):
    rows = _chunk_rows(c)
    qs = [q_ref[rows, hs].astype(F32) for hs in _HEADS]
    s_mats = [_level_scores(1, qs[h], _HEADS[h], b_scr, k_scr, f_scr) for h in range(HGRN_HEADS)]
    for lvl in range(2, 7):
        for h in range(HGRN_HEADS):
            s_mats[h] = s_mats[h] + _level_scores(lvl, qs[h], _HEADS[h], b_scr, k_scr, f_scr)
    return s_mats


def _hgrn_outputs(c, s_mats, q_ref, v_ref, st_scr, o_scr, b_scr, k_scr):
    rows = _chunk_rows(c)
    for h, hs in enumerate(_HEADS):
        q = q_ref[rows, hs].astype(F32)
        v_bf = v_ref[rows, hs]
        b = b_scr[:, hs]
        kk = k_scr[:, hs]
        b_last = b_scr[CHUNK - 1:CHUNK, hs]
        st = st_scr[h]
        qd = (q * jnp.exp(b)).astype(BF16)
        kdec = (kk * jnp.exp(b_last - b)).astype(BF16)
        diag = jnp.sum(q * kk, axis=-1, keepdims=True)
        o_scr[rows, hs] = (_nt_dot(qd, st.astype(BF16))
                           + jnp.dot(s_mats[h].astype(BF16), v_bf, preferred_element_type=F32)
                           + diag * v_bf.astype(F32))
        st_scr[h] = jnp.exp(b_last) * st + _tn_dot(v_bf, kdec)


def _level_scores(lvl, q, hs, b_scr, k_scr, f_scr):
    row = lax.broadcasted_iota(jnp.int32, (CHUNK, 1), 0)
    col = lax.broadcasted_iota(jnp.int32, (1, CHUNK), 1)
    b = b_scr[:, hs]
    kk = k_scr[:, hs]
    blk = 1 << lvl
    half = blk // 2
    if lvl == 1:
        odd = (row & 1) == 1
        qx = jnp.where(odd, q * f_scr[:, hs], 0.0).astype(BF16)
        kx = jnp.where(odd, 0.0, kk).astype(BF16)
        return jnp.where((row >> 1) == (col >> 1), _nt_dot(qx, kx), 0.0)
    if half < 8:
        groups = []
        sub = lax.broadcasted_iota(jnp.int32, (8, 1), 0)
        for j in range(CHUNK // 8):
            rj = None
            for k in reversed(range(8 // blk)):
                m = 8 * j + k * blk + half - 1
                bm = jnp.broadcast_to(b_scr[m:m + 1, hs], (8, HEAD_DIM))
                rj = bm if rj is None else jnp.where(sub < (k + 1) * blk, bm, rj)
            groups.append(rj)
        ref = jnp.concatenate(groups, axis=0)
        second = (row & (blk - 1)) >= half
        qx = (q * jnp.exp(jnp.where(second, b - ref, -jnp.inf))).astype(BF16)
        kx = (kk * jnp.exp(jnp.where(second, -jnp.inf, ref - b))).astype(BF16)
        return jnp.where((row >> lvl) == (col >> lvl), _nt_dot(qx, kx), 0.0)
    n_blk = CHUNK // blk
    qparts, kparts = [], []
    for j in range(n_blk):
        m = j * blk + half - 1
        bm = b_scr[m:m + 1, hs]
        tq = slice(j * blk + half, (j + 1) * blk)
        tk = slice(j * blk, j * blk + half)
        qparts.append(q[tq] * jnp.exp(b[tq] - bm))
        kparts.append(kk[tk] * jnp.exp(bm - b[tk]))
        kparts.append(jnp.zeros((half, HEAD_DIM), F32))
    qx = jnp.concatenate(qparts, axis=0).astype(BF16)
    kx = jnp.concatenate(kparts, axis=0).astype(BF16)
    sc = _nt_dot(qx, kx)
    if n_blk > 1:
        crow = lax.broadcasted_iota(jnp.int32, (CHUNK // 2, 1), 0)
        sc = jnp.where((crow // half) == (col >> lvl), sc, 0.0)
    pieces = []
    for j in range(n_blk):
        pieces.append(jnp.zeros((half, CHUNK), F32))
        pieces.append(sc[j * half:(j + 1) * half])
    return jnp.concatenate(pieces, axis=0)


def _mix_kernel(lbt_ref, q_ref, f_ref, i_ref, g_ref, cb_ref, cc_ref, ch_ref, ga0_ref, ga1_ref,
                gb0_ref, gb1_ref, x_ref, mod_ref, hnw_ref, cw_ref, wa_ref, wb_ref, wm_ref,
                o_ref, st_scr, o_scr, carry_scr, b_scr, k_scr, f_scr, *, layer, rows_per_step, epi_rows):
    @pl.when(pl.program_id(1) == 0)
    def _():
        st_scr[...] = jnp.zeros_like(st_scr)
        carry_scr[...] = jnp.zeros_like(carry_scr)

    tab = lbt_ref[...]
    tmax = jnp.max(tab, axis=0, keepdims=True)
    te = jnp.exp(tab - tmax)
    lb = jnp.sum(te[0:layer + 1], axis=0, keepdims=True) / jnp.sum(te, axis=0, keepdims=True)

    ri = lax.broadcasted_iota(jnp.int32, (CHUNK, CHUNK), 0)
    ci = lax.broadcasted_iota(jnp.int32, (CHUNK, CHUNK), 1)
    tri = jnp.where(ci <= ri, 1.0, 0.0).astype(BF16)

    n_chunks = rows_per_step // CHUNK

    def slot(s):
        return b_scr.at[s], k_scr.at[s], f_scr.at[s]

    def chunk(c, c_next, s):
        bs, ks, fs = slot(s)
        s_mats = _hgrn_scores(c, q_ref, bs, ks, fs)
        _hgrn_gates(c_next, lb, tri, f_ref, *slot(1 - s))
        _hgrn_outputs(c, s_mats, q_ref, i_ref, st_scr, o_scr, bs, ks)

    def pair_body(i, carry):
        chunk(2 * i, 2 * i + 1, 0)
        chunk(2 * i + 1, jnp.minimum(2 * i + 2, n_chunks - 1), 1)
        return carry

    _hgrn_gates(0, lb, tri, f_ref, *slot(0))
    lax.fori_loop(0, n_chunks // 2, pair_body, 0)

    gate_m = mod_ref[0, 2:3, :]
    hnw = hnw_ref[...]
    cw0 = cw_ref[0:1, :]
    cw1 = cw_ref[1:2, :]
    cw2 = cw_ref[2:3, :]
    for r in range(rows_per_step // epi_rows):
        rs = slice(r * epi_rows, (r + 1) * epi_rows)
        parts = []
        for h in range(HGRN_HEADS):
            hs = slice(h * HEAD_DIM, (h + 1) * HEAD_DIM)
            oh = _rms(o_scr[rs, hs], hnw)
            go = g_ref[rs, hs].astype(F32)
            parts.append((oh * (go * _sigmoid(go))).astype(BF16))
        ya = jnp.dot(jnp.concatenate(parts, axis=1), wa_ref[...], preferred_element_type=F32)
        uc = cc_ref[rs, :].astype(F32) * ch_ref[rs, :].astype(F32)
        prev = carry_scr[...]
        rowi = lax.broadcasted_iota(jnp.int32, (epi_rows, 1), 0)
        s1 = jnp.where(rowi == 0, prev[7:8, :], pltpu.roll(uc, 1, 0))
        s2 = pltpu.roll(uc, 2, 0)
        s2 = jnp.where(rowi == 0, prev[6:7, :], jnp.where(rowi == 1, prev[7:8, :], s2))
        carry_scr[...] = uc[epi_rows - 8:epi_rows, :]
        yc = cb_ref[rs, :].astype(F32) * (cw2 * uc + cw1 * s1 + cw0 * s2)
        yb = jnp.dot(yc.astype(BF16), wb_ref[...], preferred_element_type=F32)
        ga = jnp.concatenate([ga0_ref[rs, :], ga1_ref[rs, :]], axis=1).astype(F32)
        gb = jnp.concatenate([gb0_ref[rs, :], gb1_ref[rs, :]], axis=1).astype(F32)
        merged = (_sigmoid(ga) * ya + _sigmoid(gb) * yb).astype(BF16)
        o_ref[rs, :] = x_ref[rs, :] + gate_m * jnp.dot(merged, wm_ref[...], preferred_element_type=F32)


def _mix(proj, x2, mod, lb_table, hgrn_norm_w, conv_w, wa, wb, wm, layer, bsz, seq, tt):
    n = x2.shape[0]
    per_b = seq // tt

    def col(col_block):
        return pl.BlockSpec((tt, 512), lambda b, t: (b * per_b + t, col_block))

    const = lambda shape: pl.BlockSpec(shape, lambda b, t: (0,) * len(shape))
    in_specs = [
        const(lb_table.shape),
        col(0), col(1), col(2), col(3),
        col(4), col(5), col(6),
        col(7), col(8), col(9), col(10),
        pl.BlockSpec((tt, D_MODEL), lambda b, t: (b * per_b + t, 0)),
        pl.BlockSpec((1, 6, D_MODEL), lambda b, t: (b, 0, 0)),
        const((1, HEAD_DIM)), const((CONV_K, CONV_WIDTH)),
        const((HGRN_WIDTH, D_MODEL)), const((CONV_WIDTH, D_MODEL)), const((D_MODEL, D_MODEL)),
    ]
    kern = functools.partial(_mix_kernel, layer=layer, rows_per_step=tt, epi_rows=min(tt, 256))
    return pl.pallas_call(
        kern,
        out_shape=jax.ShapeDtypeStruct((n, D_MODEL), F32),
        grid=(bsz, per_b),
        in_specs=in_specs,
        out_specs=pl.BlockSpec((tt, D_MODEL), lambda b, t: (b * per_b + t, 0)),
        scratch_shapes=[pltpu.VMEM((HGRN_HEADS, HEAD_DIM, HEAD_DIM), F32),
                        pltpu.VMEM((tt, HGRN_WIDTH), F32),
                        pltpu.VMEM((8, CONV_WIDTH), F32),
                        pltpu.VMEM((2, CHUNK, HGRN_WIDTH), F32),
                        pltpu.VMEM((2, CHUNK, HGRN_WIDTH), F32),
                        pltpu.VMEM((2, CHUNK, HGRN_WIDTH), F32)],
        compiler_params=pltpu.CompilerParams(dimension_semantics=("arbitrary", "arbitrary"),
                                             vmem_limit_bytes=VMEM_LIMIT),
        name="mix",
    )(lb_table, *([proj] * 11),
      x2, mod, hgrn_norm_w.reshape(1, HEAD_DIM), conv_w, wa, wb, wm)


def _route_kernel(h_ref, mod_ref, nw_ref, wr_ref, br_ref, u_ref, oh4_ref, rk_ref, pw_ref, cnt_ref, *, tr):
    shift = mod_ref[0, 3:4, :]
    scale = mod_ref[0, 4:5, :]
    u = _rms(h_ref[...], nw_ref[...]) * (1.0 + scale) + shift
    u_ref[...] = _pack_rows(u)
    lane = lax.broadcasted_iota(jnp.int32, (tr, LANES), 1)
    logits = jnp.dot(u.astype(BF16), wr_ref[...], preferred_element_type=F32) + br_ref[...]
    logits = jnp.where(lane < N_EXPERTS, logits, -jnp.inf)
    idx, val = [], []
    cur = logits
    for _ in range(TOP_K):
        m = jnp.max(cur, axis=-1, keepdims=True)
        i = jnp.min(jnp.where(cur == m, lane, LANES), axis=-1, keepdims=True)
        idx.append(i)
        val.append(m)
        cur = jnp.where(lane == i, -jnp.inf, cur)
    ex = [jnp.exp(v - val[0]) for v in val]
    den = ex[0] + ex[1] + ex[2] + ex[3]
    onehot = jnp.zeros((tr, LANES), F32)
    for i in idx:
        onehot = onehot + jnp.where(lane == i, 1.0, 0.0)
    ri = lax.broadcasted_iota(jnp.int32, (tr, tr), 0)
    ci = lax.broadcasted_iota(jnp.int32, (tr, tr), 1)
    tri = jnp.where(ci < ri, 1.0, 0.0).astype(BF16)
    pref = jnp.dot(tri, onehot.astype(BF16), preferred_element_type=F32)
    rk = jnp.zeros((tr, LANES), jnp.int32)
    pw = jnp.zeros((tr, LANES), F32)
    oh4 = jnp.zeros((tr, LANES), F32)
    for j in range(TOP_K):
        rank = jnp.sum(jnp.where(lane == idx[j], pref, 0.0), axis=-1, keepdims=True).astype(jnp.int32)
        rk = jnp.where(lane == j, rank, rk)
        pw = jnp.where(lane == j, ex[j] / den, pw)
        oh4 = oh4 + jnp.where(lane == idx[j] + N_EXPERTS * j, 1.0, 0.0)
    oh4_ref[...] = oh4.astype(BF16)
    rk_ref[...] = rk
    pw_ref[...] = pw
    cnt_ref[0] = jnp.sum(onehot, axis=0, keepdims=True).astype(jnp.int32)


def _route(h1, mod, norm_w, wr_pad, br_pad, seq, tr):
    n = h1.shape[0]
    per_b = seq // tr
    nt = n // tr
    return pl.pallas_call(
        functools.partial(_route_kernel, tr=tr),
        out_shape=(jax.ShapeDtypeStruct((n, PACKED), jnp.int32),
                   jax.ShapeDtypeStruct((n, LANES), BF16),
                   jax.ShapeDtypeStruct((n, LANES), jnp.int32),
                   jax.ShapeDtypeStruct((n, LANES), F32),
                   jax.ShapeDtypeStruct((nt, 1, LANES), jnp.int32)),
        grid=(nt,),
        in_specs=[pl.BlockSpec((tr, D_MODEL), lambda i: (i, 0)),
                  pl.BlockSpec((1, 6, D_MODEL), lambda i: (i // per_b, 0, 0)),
                  pl.BlockSpec((1, D_MODEL), lambda i: (0, 0)),
                  pl.BlockSpec((D_MODEL, LANES), lambda i: (0, 0)),
                  pl.BlockSpec((1, LANES), lambda i: (0, 0))],
        out_specs=(pl.BlockSpec((tr, PACKED), lambda i: (i, 0)),
                   pl.BlockSpec((tr, LANES), lambda i: (i, 0)),
                   pl.BlockSpec((tr, LANES), lambda i: (i, 0)),
                   pl.BlockSpec((tr, LANES), lambda i: (i, 0)),
                   pl.BlockSpec((1, 1, LANES), lambda i: (i, 0, 0))),
        compiler_params=pltpu.CompilerParams(dimension_semantics=("arbitrary",),
                                             vmem_limit_bytes=VMEM_LIMIT),
        name="route",
    )(h1, mod, norm_w.reshape(1, D_MODEL), wr_pad, br_pad)


def _dest_kernel(oh4_ref, rk_ref, bt_ref, o_ref):
    oh = oh4_ref[...]
    start = (jnp.dot(oh, bt_ref[0, 0], preferred_element_type=F32)
             + 256.0 * jnp.dot(oh, bt_ref[0, 1], preferred_element_type=F32)
             + 65536.0 * jnp.dot(oh, bt_ref[0, 2], preferred_element_type=F32))
    o_ref[...] = start.astype(jnp.int32) + rk_ref[...]


def _dest(oh4, rk, bt, tr):
    n = oh4.shape[0]
    return pl.pallas_call(
        _dest_kernel,
        out_shape=jax.ShapeDtypeStruct((n, LANES), jnp.int32),
        grid=(n // tr,),
        in_specs=[pl.BlockSpec((tr, LANES), lambda i: (i, 0)),
                  pl.BlockSpec((tr, LANES), lambda i: (i, 0)),
                  pl.BlockSpec((1, 3, LANES, LANES), lambda i: (i, 0, 0, 0))],
        out_specs=pl.BlockSpec((tr, LANES), lambda i: (i, 0)),
        name="dest",
    )(oh4, rk, bt)


def _sc_workers():
    info = plsc.get_sparse_core_info()
    return info.num_cores, info.num_cores * info.num_subcores


def _sc_scatter_rows(rows, idx_slots, pad_idx, n_out):
    n_cores, n_workers = _sc_workers()
    n, w = rows.shape
    k = idx_slots.shape[0] // n
    per_worker = n // n_workers
    pad_per_worker = pad_idx.shape[0] // n_workers
    assert per_worker % SC_CHUNK == 0 and pad_per_worker % SC_CHUNK == 0
    mesh = plsc.VectorSubcoreMesh(core_axis_name="c", subcore_axis_name="s")
    zeros = jnp.zeros((SC_CHUNK, w), rows.dtype)

    n_chunks = per_worker // SC_CHUNK
    assert n_chunks % 2 == 0

    @functools.partial(
        pl.kernel, mesh=mesh,
        out_type=jax.ShapeDtypeStruct((n_out, w), rows.dtype),
        scratch_types=[pltpu.VMEM((SC_CHUNK,), jnp.int32)] * k
        + [pltpu.VMEM((SC_CHUNK, w), rows.dtype)] * 2
        + [pltpu.SemaphoreType.DMA] * 2,
        name="sc_scatter",
    )
    def scatter(rows_hbm, idx_hbm, pad_hbm, zeros_hbm, out_hbm, *scratch):
        idx_bufs = scratch[:k]
        row_bufs = scratch[k:k + 2]
        sem_rows, sem_out = scratch[k + 2:]
        wid = lax.axis_index("s") * n_cores + lax.axis_index("c")

        def chunk_off(c):
            return pl.multiple_of(wid * per_worker + c * SC_CHUNK, 8)

        def load(c, b):
            pltpu.async_copy(rows_hbm.at[pl.ds(chunk_off(c), SC_CHUNK)], row_bufs[b], sem_rows)

        def wait_load(b):
            pltpu.make_async_copy(rows_hbm.at[pl.ds(0, SC_CHUNK)], row_bufs[b], sem_rows).wait()

        def scatter_chunk(c, b):
            for j in range(k):
                pltpu.sync_copy(idx_hbm.at[pl.ds(pl.multiple_of(j * n + chunk_off(c), 8), SC_CHUNK)],
                                idx_bufs[j])
            copies = [pltpu.async_copy(row_bufs[b], out_hbm.at[idx_bufs[j]], sem_out) for j in range(k)]
            for cp in copies:
                cp.wait()

        load(0, 0)

        def body(i, carry):
            wait_load(0)
            load(2 * i + 1, 1)
            scatter_chunk(2 * i, 0)
            wait_load(1)

            @pl.when(i < n_chunks // 2 - 1)
            def _():
                load(2 * i + 2, 0)

            scatter_chunk(2 * i + 1, 1)
            return carry

        lax.fori_loop(0, n_chunks // 2, body, 0)
        pltpu.sync_copy(zeros_hbm, row_bufs[0])

        def pad_body(i, carry):
            off = pl.multiple_of(wid * pad_per_worker + i * SC_CHUNK, 8)
            pltpu.sync_copy(pad_hbm.at[pl.ds(off, SC_CHUNK)], idx_bufs[0])
            pltpu.async_copy(row_bufs[0], out_hbm.at[idx_bufs[0]], sem_out).wait()
            return carry

        lax.fori_loop(0, pad_per_worker // SC_CHUNK, pad_body, 0)

    return scatter(rows, idx_slots, pad_idx, zeros)


def _experts_kernel(be_ref, nu_ref, x_ref, w1_ref, b1_ref, w2_ref, b2_ref, o_ref, w1_bf, w2_bf):
    i = pl.program_id(0)

    @pl.when((i == 0) | (be_ref[i] != be_ref[jnp.maximum(i - 1, 0)]))
    def _():
        cw = 256
        for c in range(2 * D_FF // cw):
            w1_bf[:, c * cw:(c + 1) * cw] = w1_ref[0, :, c * cw:(c + 1) * cw].astype(BF16)
        for c in range(D_MODEL // cw):
            w2_bf[:, c * cw:(c + 1) * cw] = w2_ref[0, :, c * cw:(c + 1) * cw].astype(BF16)

    @pl.when(i < nu_ref[0])
    def _():
        step = 512
        for r in range(ROW_BLOCK // EXPERT_ROWS):
            rs = slice(r * EXPERT_ROWS, (r + 1) * EXPERT_ROWS)
            x = _unpack_rows(x_ref[rs, :]).astype(BF16)
            acc = None
            for j in range(D_FF // step):
                cs = slice(j * step, (j + 1) * step)
                ls = slice(D_FF + j * step, D_FF + (j + 1) * step)
                glu = jnp.dot(x, w1_bf[:, cs], preferred_element_type=F32) + b1_ref[0, :, cs]
                lin = jnp.dot(x, w1_bf[:, ls], preferred_element_type=F32) + b1_ref[0, :, ls]
                glu = jnp.minimum(glu, SWIGLU_LIMIT)
                lin = jnp.clip(lin, -SWIGLU_LIMIT, SWIGLU_LIMIT)
                act = (glu * _sigmoid(SWIGLU_ALPHA * glu) * (lin + 1.0)).astype(BF16)
                part = jnp.dot(act, w2_bf[cs, :], preferred_element_type=F32)
                acc = part if acc is None else acc + part
            o_ref[rs, :] = _pack_rows(acc + b2_ref[0])

    @pl.when(i >= nu_ref[0])
    def _():
        o_ref[...] = jnp.zeros_like(o_ref)


def _experts(block_e, n_used, xs, w1_bf, b1, w2_bf, b2):
    n_rows = xs.shape[0]
    nb = n_rows // ROW_BLOCK
    grid_spec = pltpu.PrefetchScalarGridSpec(
        num_scalar_prefetch=2,
        grid=(nb,),
        in_specs=[pl.BlockSpec((ROW_BLOCK, PACKED), lambda i, be, nu: (jnp.minimum(i, nu[0] - 1), 0)),
                  pl.BlockSpec((1, D_MODEL, 2 * D_FF), lambda i, be, nu: (be[i], 0, 0)),
                  pl.BlockSpec((1, 1, 2 * D_FF), lambda i, be, nu: (be[i], 0, 0)),
                  pl.BlockSpec((1, D_FF, D_MODEL), lambda i, be, nu: (be[i], 0, 0)),
                  pl.BlockSpec((1, 1, D_MODEL), lambda i, be, nu: (be[i], 0, 0))],
        out_specs=pl.BlockSpec((ROW_BLOCK, PACKED), lambda i, be, nu: (i, 0)),
        scratch_shapes=[pltpu.VMEM((D_MODEL, 2 * D_FF), BF16), pltpu.VMEM((D_FF, D_MODEL), BF16)],
    )
    return pl.pallas_call(
        _experts_kernel,
        out_shape=jax.ShapeDtypeStruct((n_rows, PACKED), jnp.int32),
        grid_spec=grid_spec,
        compiler_params=pltpu.CompilerParams(dimension_semantics=("arbitrary",),
                                             vmem_limit_bytes=VMEM_LIMIT),
        name="experts",
    )(block_e, n_used, xs, w1_bf, b1.reshape(N_EXPERTS, 1, 2 * D_FF), w2_bf,
      b2.reshape(N_EXPERTS, 1, D_MODEL))


def _sc_gather_rows(table, idx_flat):
    n_cores, n_workers = _sc_workers()
    n_idx = idx_flat.shape[0]
    w = table.shape[1]
    per_worker = n_idx // n_workers
    n_chunks = per_worker // SC_CHUNK
    assert per_worker * n_workers == n_idx and n_chunks * SC_CHUNK == per_worker and n_chunks % 2 == 0
    mesh = plsc.VectorSubcoreMesh(core_axis_name="c", subcore_axis_name="s")

    @functools.partial(
        pl.kernel, mesh=mesh,
        out_type=jax.ShapeDtypeStruct((n_idx, w), table.dtype),
        scratch_types=[pltpu.VMEM((SC_CHUNK,), jnp.int32), pltpu.VMEM((SC_CHUNK,), jnp.int32),
                       pltpu.VMEM((SC_CHUNK, w), table.dtype), pltpu.VMEM((SC_CHUNK, w), table.dtype),
                       pltpu.SemaphoreType.DMA, pltpu.SemaphoreType.DMA],
        name="sc_gather",
    )
    def gather(table_hbm, idx_hbm, out_hbm, idx0, idx1, rows0, rows1, sem0, sem1):
        wid = lax.axis_index("s") * n_cores + lax.axis_index("c")
        bufs = ((idx0, rows0, sem0), (idx1, rows1, sem1))

        def chunk_off(c):
            return pl.multiple_of(wid * per_worker + c * SC_CHUNK, 8)

        def start(c, b):
            idx_v, rows_v, sem = bufs[b]
            pltpu.sync_copy(idx_hbm.at[pl.ds(chunk_off(c), SC_CHUNK)], idx_v)
            pltpu.async_copy(table_hbm.at[idx_v], rows_v, sem)

        def finish(c, b):
            idx_v, rows_v, sem = bufs[b]
            pltpu.make_async_copy(table_hbm.at[idx_v], rows_v, sem).wait()
            pltpu.sync_copy(rows_v, out_hbm.at[pl.ds(chunk_off(c), SC_CHUNK)])

        start(0, 0)

        def body(i, carry):
            start(2 * i + 1, 1)
            finish(2 * i, 0)

            @pl.when(i < n_chunks // 2 - 1)
            def _():
                start(2 * i + 2, 0)

            finish(2 * i + 1, 1)
            return carry

        lax.fori_loop(0, n_chunks // 2, body, 0)

    return gather(table, idx_flat)


def _finish_kernel(h_ref, pw_ref, mod_ref, nw_ref, y0_ref, y1_ref, y2_ref, y3_ref, o_ref):
    pw = pw_ref[...]
    moe = pw[:, 0:1] * _unpack_rows(y0_ref[0])
    for j, y_ref in enumerate((y1_ref, y2_ref, y3_ref), start=1):
        moe = moe + pw[:, j:j + 1] * _unpack_rows(y_ref[0])
    gate_f = mod_ref[0, 5:6, :]
    o_ref[...] = _rms(h_ref[...] + gate_f * moe, nw_ref[...])


def _finish(h1, pw, mod, norm_w, y4, seq, tc):
    n = h1.shape[0]
    per_b = seq // tc
    slot = lambda j: pl.BlockSpec((1, tc, PACKED), lambda i: (j, i, 0))
    return pl.pallas_call(
        _finish_kernel,
        out_shape=jax.ShapeDtypeStruct((n, D_MODEL), F32),
        grid=(n // tc,),
        in_specs=[pl.BlockSpec((tc, D_MODEL), lambda i: (i, 0)),
                  pl.BlockSpec((tc, LANES), lambda i: (i, 0)),
                  pl.BlockSpec((1, 6, D_MODEL), lambda i: (i // per_b, 0, 0)),
                  pl.BlockSpec((1, D_MODEL), lambda i: (0, 0)),
                  slot(0), slot(1), slot(2), slot(3)],
        out_specs=pl.BlockSpec((tc, D_MODEL), lambda i: (i, 0)),
        compiler_params=pltpu.CompilerParams(dimension_semantics=("arbitrary",),
                                             vmem_limit_bytes=VMEM_LIMIT),
        name="finish",
    )(h1, pw, mod, norm_w.reshape(1, D_MODEL), y4, y4, y4, y4)


def _moe_plan(counts, n_assign):
    cnt = counts[:, 0, :N_EXPERTS]
    sizes = jnp.sum(cnt, axis=0)
    padded = (sizes + ROW_BLOCK - 1) // ROW_BLOCK * ROW_BLOCK
    pad_end = jnp.cumsum(padded)
    pad_start = pad_end - padded
    tile_base = pad_start[None, :] + jnp.cumsum(cnt, axis=0) - cnt
    digits = jnp.stack([tile_base % 256, (tile_base // 256) % 256, tile_base // 65536], axis=1)
    rows = jnp.tile(digits, (1, 1, TOP_K))
    slot_of_row = jnp.arange(LANES, dtype=jnp.int32) // N_EXPERTS
    col = jnp.arange(LANES, dtype=jnp.int32)
    bt = jnp.where(slot_of_row[:, None] == col[None, :], rows[..., None], 0).astype(BF16)
    nb = n_assign // ROW_BLOCK + N_EXPERTS
    block_start = jnp.arange(nb, dtype=jnp.int32) * ROW_BLOCK
    block_e = jnp.minimum(jnp.sum(pad_end[None, :] <= block_start[:, None], axis=1),
                          N_EXPERTS - 1).astype(jnp.int32)
    n_used = (pad_end[-1] // ROW_BLOCK).astype(jnp.int32).reshape(1)
    r = jnp.arange(ROW_BLOCK, dtype=jnp.int32)[None, :]
    pad_idx = jnp.where(r < (padded - sizes)[:, None], (pad_start + sizes)[:, None] + r,
                        nb * ROW_BLOCK - 1).astype(jnp.int32).reshape(PAD_SLOTS)
    return bt, block_e, n_used, pad_idx, nb * ROW_BLOCK


def kernel(x, c, w_ada, b_ada, norm_mix_w, w_in, hgrn_lower_bounds, hgrn_norm_w, conv_w,
           w_hgrn_out, w_conv_out, w_mix_out, norm_ffn_w, w_router, b_router, w1, b1, w2, b2,
           norm_final_w):
    bsz, seq, d = x.shape
    assert d == D_MODEL and seq % CHUNK == 0
    n = bsz * seq
    depth = w_ada.shape[0]
    tile = min(512, seq)
    assert seq % tile == 0 and (n * TOP_K) % ROW_BLOCK == 0
    h = x.reshape(n, d)
    wr_pad = jnp.zeros((depth, D_MODEL, LANES), BF16).at[:, :, :N_EXPERTS].set(w_router.astype(BF16))
    br_pad = jnp.zeros((depth, 1, LANES), F32).at[:, 0, :N_EXPERTS].set(b_router)
    for layer in range(depth):
        mod = _ada(c, w_ada[layer], b_ada[layer]).reshape(bsz, 6, d)
        proj = _inproj(h, mod, norm_mix_w[layer], w_in[layer].astype(BF16), seq, tile)
        h = _mix(proj, h, mod, hgrn_lower_bounds, hgrn_norm_w[layer], conv_w[layer],
                 w_hgrn_out[layer].astype(BF16), w_conv_out[layer].astype(BF16),
                 w_mix_out[layer].astype(BF16), layer, bsz, seq, tile)
        u2, oh4, rk, pw, counts = _route(h, mod, norm_ffn_w[layer], wr_pad[layer], br_pad[layer], seq, tile)
        bt, block_e, n_used, pad_idx, n_rows = _moe_plan(counts, n * TOP_K)
        dest_slots = _dest(oh4, rk, bt, tile)[:, :TOP_K].T.reshape(TOP_K * n)
        xs = _sc_scatter_rows(u2, dest_slots, pad_idx, n_rows)
        ys = _experts(block_e, n_used, xs, w1[layer], b1[layer], w2[layer], b2[layer])
        assert layer == depth - 1, "only the last layer applies the final norm"
        y4 = _sc_gather_rows(ys, dest_slots).reshape(TOP_K, n, PACKED)
        h = _finish(h, pw, mod, norm_final_w, y4, seq, tile)
    return h.reshape(bsz, seq, d)
```

```python
import functools

import jax
import jax.numpy as jnp
from jax import lax
from jax.experimental import pallas as pl
from jax.experimental.pallas import tpu as pltpu
from jax.experimental.pallas import tpu_sc as plsc

F32 = jnp.float32
BF16 = jnp.bfloat16

D_MODEL = 1024
HGRN_HEADS = 4
HEAD_DIM = 128
HGRN_WIDTH = HGRN_HEADS * HEAD_DIM
CONV_WIDTH = 512
CONV_K = 3
CHUNK = 64
N_EXPERTS = 32
TOP_K = 4
D_FF = 1024
SWIGLU_LIMIT = 7.0
SWIGLU_ALPHA = 1.702
EPS = 1e-6
IN_COLS = 4 * HGRN_WIDTH + 3 * CONV_WIDTH + 2 * D_MODEL
LANES = 128
ROW_BLOCK = 512
EXPERT_ROWS = 256
PACKED = D_MODEL // 2
MOE_GROUPS = 2
SC_CHUNK = 64
PAD_SLOTS = N_EXPERTS * ROW_BLOCK
VMEM_LIMIT = 56 * 1024 * 1024


def _sigmoid(x):
    return 1.0 / (1.0 + jnp.exp(-x))


def _rms(x, w):
    ms = jnp.mean(x * x, axis=-1, keepdims=True)
    return x * lax.rsqrt(ms + EPS) * w


def _pack_rows(x):
    w = x.shape[1] // 2
    lo = lax.bitcast_convert_type(x[:, :w].astype(BF16).astype(F32), jnp.uint32)
    hi = lax.bitcast_convert_type(x[:, w:].astype(BF16).astype(F32), jnp.uint32)
    return lax.bitcast_convert_type((lo >> 16) | (hi & jnp.uint32(0xFFFF0000)), jnp.int32)


def _unpack_rows(p):
    u = lax.bitcast_convert_type(p, jnp.uint32)
    lo = lax.bitcast_convert_type(u << 16, F32)
    hi = lax.bitcast_convert_type(u & jnp.uint32(0xFFFF0000), F32)
    return jnp.concatenate([lo, hi], axis=1)


def _nt_dot(a, b):
    return lax.dot_general(a, b, (((1,), (1,)), ((), ())), preferred_element_type=F32)


def _tn_dot(a, b):
    return lax.dot_general(a, b, (((0,), (0,)), ((), ())), preferred_element_type=F32)


def _ada_kernel(c_ref, w_ref, b_ref, o_ref):
    c = c_ref[...]
    sc = (c * _sigmoid(c)).astype(BF16)
    o_ref[...] = jnp.dot(sc, w_ref[...].astype(BF16), preferred_element_type=F32) + b_ref[...]


def _ada(c, w_ada, b_ada):
    bsz, d = c.shape
    n = w_ada.shape[1]
    return pl.pallas_call(
        _ada_kernel,
        out_shape=jax.ShapeDtypeStruct((bsz, n), F32),
        grid=(n // d,),
        in_specs=[pl.BlockSpec((bsz, d), lambda j: (0, 0)),
                  pl.BlockSpec((d, d), lambda j: (0, j)),
                  pl.BlockSpec((1, d), lambda j: (0, j))],
        out_specs=pl.BlockSpec((bsz, d), lambda j: (0, j)),
        name="ada",
    )(c, w_ada, b_ada.reshape(1, n))


def _inproj_kernel(x_ref, mod_ref, nw_ref, w_ref, o_ref, u_scr):
    y = _rms(x_ref[...], nw_ref[...])
    shift = mod_ref[0, 0:1, :]
    scale = mod_ref[0, 1:2, :]
    u_scr[...] = (y * (1.0 + scale) + shift).astype(BF16)
    step = 512
    for j in range(IN_COLS // step):
        o_ref[:, j * step:(j + 1) * step] = jnp.dot(
            u_scr[...], w_ref[:, j * step:(j + 1) * step], preferred_element_type=F32).astype(BF16)


def _inproj(x2, mod, norm_w, w_in_bf, seq, tm):
    n = x2.shape[0]
    per_b = seq // tm
    return pl.pallas_call(
        _inproj_kernel,
        out_shape=jax.ShapeDtypeStruct((n, IN_COLS), BF16),
        grid=(n // tm,),
        in_specs=[pl.BlockSpec((tm, D_MODEL), lambda i: (i, 0)),
                  pl.BlockSpec((1, 6, D_MODEL), lambda i: (i // per_b, 0, 0)),
                  pl.BlockSpec((1, D_MODEL), lambda i: (0, 0)),
                  pl.BlockSpec((D_MODEL, IN_COLS), lambda i: (0, 0))],
        out_specs=pl.BlockSpec((tm, IN_COLS), lambda i: (i, 0)),
        scratch_shapes=[pltpu.VMEM((tm, D_MODEL), BF16)],
        compiler_params=pltpu.CompilerParams(dimension_semantics=("arbitrary",),
                                             vmem_limit_bytes=VMEM_LIMIT),
        name="inproj",
    )(x2, mod, norm_w.reshape(1, D_MODEL), w_in_bf)


_HEADS = [slice(h * HEAD_DIM, (h + 1) * HEAD_DIM) for h in range(HGRN_HEADS)]


def _chunk_rows(c):
    if isinstance(c, int):
        return pl.ds(c * CHUNK, CHUNK)
    return pl.ds(pl.multiple_of(c * CHUNK, CHUNK), CHUNK)


def _hgrn_gates(c, lb, tri, f_ref, b_scr, k_scr, f_scr):
    fx = f_ref[_chunk_rows(c), :].astype(F32)
    f = lb + (1.0 - lb) * _sigmoid(fx)
    g = jnp.log(f)
    g1 = g.astype(BF16)
    r1 = g - g1.astype(F32)
    g2 = r1.astype(BF16)
    g3 = (r1 - g2.astype(F32)).astype(BF16)
    b_scr[...] = (jnp.dot(tri, g1, preferred_element_type=F32)
                  + jnp.dot(tri, g2, preferred_element_type=F32)
                  + jnp.dot(tri, g3, preferred_element_type=F32))
    f_scr[...] = f
    k_scr[...] = 1.0 - f


def _hgrn_scores(c, q_ref, b_scr, k_scr, f_scr):
    rows = _chunk_rows(c)
    qs = [q_ref[rows, hs].astype(F32) for hs in _HEADS]
    s_mats = [_level_scores(1, qs[h], _HEADS[h], b_scr, k_scr, f_scr) for h in range(HGRN_HEADS)]
    for lvl in range(2, 7):
        for h in range(HGRN_HEADS):
            s_mats[h] = s_mats[h] + _level_scores(lvl, qs[h], _HEADS[h], b_scr, k_scr, f_scr)
    return s_mats


def _hgrn_outputs(c, s_mats, q_ref, v_ref, st_scr, o_scr, b_scr, k_scr):
    rows = _chunk_rows(c)
    for h, hs in enumerate(_HEADS):
        q = q_ref[rows, hs].astype(F32)
        v_bf = v_ref[rows, hs]
        b = b_scr[:, hs]
        kk = k_scr[:, hs]
        b_last = b_scr[CHUNK - 1:CHUNK, hs]
        st = st_scr[h]
        qd = (q * jnp.exp(b)).astype(BF16)
        kdec = (kk * jnp.exp(b_last - b)).astype(BF16)
        diag = jnp.sum(q * kk, axis=-1, keepdims=True)
        o_scr[rows, hs] = (_nt_dot(qd, st.astype(BF16))
                           + jnp.dot(s_mats[h].astype(BF16), v_bf, preferred_element_type=F32)
                           + diag * v_bf.astype(F32))
        st_scr[h] = jnp.exp(b_last) * st + _tn_dot(v_bf, kdec)


def _level_scores(lvl, q, hs, b_scr, k_scr, f_scr):
    row = lax.broadcasted_iota(jnp.int32, (CHUNK, 1), 0)
    col = lax.broadcasted_iota(jnp.int32, (1, CHUNK), 1)
    b = b_scr[:, hs]
    kk = k_scr[:, hs]
    blk = 1 << lvl
    half = blk // 2
    if lvl == 1:
        odd = (row & 1) == 1
        qx = jnp.where(odd, q * f_scr[:, hs], 0.0).astype(BF16)
        kx = jnp.where(odd, 0.0, kk).astype(BF16)
        return jnp.where((row >> 1) == (col >> 1), _nt_dot(qx, kx), 0.0)
    if half < 8:
        groups = []
        sub = lax.broadcasted_iota(jnp.int32, (8, 1), 0)
        for j in range(CHUNK // 8):
            rj = None
            for k in reversed(range(8 // blk)):
                m = 8 * j + k * blk + half - 1
                bm = jnp.broadcast_to(b_scr[m:m + 1, hs], (8, HEAD_DIM))
                rj = bm if rj is None else jnp.where(sub < (k + 1) * blk, bm, rj)
            groups.append(rj)
        ref = jnp.concatenate(groups, axis=0)
        second = (row & (blk - 1)) >= half
        qx = (q * jnp.exp(jnp.where(second, b - ref, -jnp.inf))).astype(BF16)
        kx = (kk * jnp.exp(jnp.where(second, -jnp.inf, ref - b))).astype(BF16)
        return jnp.where((row >> lvl) == (col >> lvl), _nt_dot(qx, kx), 0.0)
    n_blk = CHUNK // blk
    qparts, kparts = [], []
    for j in range(n_blk):
        m = j * blk + half - 1
        bm = b_scr[m:m + 1, hs]
        tq = slice(j * blk + half, (j + 1) * blk)
        tk = slice(j * blk, j * blk + half)
        qparts.append(q[tq] * jnp.exp(b[tq] - bm))
        kparts.append(kk[tk] * jnp.exp(bm - b[tk]))
        kparts.append(jnp.zeros((half, HEAD_DIM), F32))
    qx = jnp.concatenate(qparts, axis=0).astype(BF16)
    kx = jnp.concatenate(kparts, axis=0).astype(BF16)
    sc = _nt_dot(qx, kx)
    if n_blk > 1:
        crow = lax.broadcasted_iota(jnp.int32, (CHUNK // 2, 1), 0)
        sc = jnp.where((crow // half) == (col >> lvl), sc, 0.0)
    pieces = []
    for j in range(n_blk):
        pieces.append(jnp.zeros((half, CHUNK), F32))
        pieces.append(sc[j * half:(j + 1) * half])
    return jnp.concatenate(pieces, axis=0)


def _mix_kernel(lbt_ref, q_ref, f_ref, i_ref, g_ref, cb_ref, cc_ref, ch_ref, ga0_ref, ga1_ref,
                gb0_ref, gb1_ref, x_ref, mod_ref, hnw_ref, cw_ref, wa_ref, wb_ref, wm_ref,
                o_ref, st_scr, o_scr, carry_scr, b_scr, k_scr, f_scr, *, layer, rows_per_step, epi_rows):
    @pl.when(pl.program_id(1) == 0)
    def _():
        st_scr[...] = jnp.zeros_like(st_scr)
        carry_scr[...] = jnp.zeros_like(carry_scr)

    tab = lbt_ref[...]
    tmax = jnp.max(tab, axis=0, keepdims=True)
    te = jnp.exp(tab - tmax)
    lb = jnp.sum(te[0:layer + 1], axis=0, keepdims=True) / jnp.sum(te, axis=0, keepdims=True)

    ri = lax.broadcasted_iota(jnp.int32, (CHUNK, CHUNK), 0)
    ci = lax.broadcasted_iota(jnp.int32, (CHUNK, CHUNK), 1)
    tri = jnp.where(ci <= ri, 1.0, 0.0).astype(BF16)

    n_chunks = rows_per_step // CHUNK

    def slot(s):
        return b_scr.at[s], k_scr.at[s], f_scr.at[s]

    def chunk(c, c_next, s):
        bs, ks, fs = slot(s)
        s_mats = _hgrn_scores(c, q_ref, bs, ks, fs)
        _hgrn_gates(c_next, lb, tri, f_ref, *slot(1 - s))
        _hgrn_outputs(c, s_mats, q_ref, i_ref, st_scr, o_scr, bs, ks)

    def pair_body(i, carry):
        chunk(2 * i, 2 * i + 1, 0)
        chunk(2 * i + 1, jnp.minimum(2 * i + 2, n_chunks - 1), 1)
        return carry

    _hgrn_gates(0, lb, tri, f_ref, *slot(0))
    lax.fori_loop(0, n_chunks // 2, pair_body, 0)

    gate_m = mod_ref[0, 2:3, :]
    hnw = hnw_ref[...]
    cw0 = cw_ref[0:1, :]
    cw1 = cw_ref[1:2, :]
    cw2 = cw_ref[2:3, :]
    for r in range(rows_per_step // epi_rows):
        rs = slice(r * epi_rows, (r + 1) * epi_rows)
        parts = []
        for h in range(HGRN_HEADS):
            hs = slice(h * HEAD_DIM, (h + 1) * HEAD_DIM)
            oh = _rms(o_scr[rs, hs], hnw)
            go = g_ref[rs, hs].astype(F32)
            parts.append((oh * (go * _sigmoid(go))).astype(BF16))
        ya = jnp.dot(jnp.concatenate(parts, axis=1), wa_ref[...], preferred_element_type=F32)
        uc = cc_ref[rs, :].astype(F32) * ch_ref[rs, :].astype(F32)
        prev = carry_scr[...]
        rowi = lax.broadcasted_iota(jnp.int32, (epi_rows, 1), 0)
        s1 = jnp.where(rowi == 0, prev[7:8, :], pltpu.roll(uc, 1, 0))
        s2 = pltpu.roll(uc, 2, 0)
        s2 = jnp.where(rowi == 0, prev[6:7, :], jnp.where(rowi == 1, prev[7:8, :], s2))
        carry_scr[...] = uc[epi_rows - 8:epi_rows, :]
        yc = cb_ref[rs, :].astype(F32) * (cw2 * uc + cw1 * s1 + cw0 * s2)
        yb = jnp.dot(yc.astype(BF16), wb_ref[...], preferred_element_type=F32)
        ga = jnp.concatenate([ga0_ref[rs, :], ga1_ref[rs, :]], axis=1).astype(F32)
        gb = jnp.concatenate([gb0_ref[rs, :], gb1_ref[rs, :]], axis=1).astype(F32)
        merged = (_sigmoid(ga) * ya + _sigmoid(gb) * yb).astype(BF16)
        o_ref[rs, :] = x_ref[rs, :] + gate_m * jnp.dot(merged, wm_ref[...], preferred_element_type=F32)


def _mix(proj, x2, mod, lb_table, hgrn_norm_w, conv_w, wa, wb, wm, layer, bsz, seq, tt):
    n = x2.shape[0]
    per_b = seq // tt

    def col(col_block):
        return pl.BlockSpec((tt, 512), lambda b, t: (b * per_b + t, col_block))

    const = lambda shape: pl.BlockSpec(shape, lambda b, t: (0,) * len(shape))
    in_specs = [
        const(lb_table.shape),
        col(0), col(1), col(2), col(3),
        col(4), col(5), col(6),
        col(7), col(8), col(9), col(10),
        pl.BlockSpec((tt, D_MODEL), lambda b, t: (b * per_b + t, 0)),
        pl.BlockSpec((1, 6, D_MODEL), lambda b, t: (b, 0, 0)),
        const((1, HEAD_DIM)), const((CONV_K, CONV_WIDTH)),
        const((HGRN_WIDTH, D_MODEL)), const((CONV_WIDTH, D_MODEL)), const((D_MODEL, D_MODEL)),
    ]
    kern = functools.partial(_mix_kernel, layer=layer, rows_per_step=tt, epi_rows=min(tt, 256))
    return pl.pallas_call(
        kern,
        out_shape=jax.ShapeDtypeStruct((n, D_MODEL), F32),
        grid=(bsz, per_b),
        in_specs=in_specs,
        out_specs=pl.BlockSpec((tt, D_MODEL), lambda b, t: (b * per_b + t, 0)),
        scratch_shapes=[pltpu.VMEM((HGRN_HEADS, HEAD_DIM, HEAD_DIM), F32),
                        pltpu.VMEM((tt, HGRN_WIDTH), F32),
                        pltpu.VMEM((8, CONV_WIDTH), F32),
                        pltpu.VMEM((2, CHUNK, HGRN_WIDTH), F32),
                        pltpu.VMEM((2, CHUNK, HGRN_WIDTH), F32),
                        pltpu.VMEM((2, CHUNK, HGRN_WIDTH), F32)],
        compiler_params=pltpu.CompilerParams(dimension_semantics=("arbitrary", "arbitrary"),
                                             vmem_limit_bytes=VMEM_LIMIT),
        name="mix",
    )(lb_table, *([proj] * 11),
      x2, mod, hgrn_norm_w.reshape(1, HEAD_DIM), conv_w, wa, wb, wm)


def _route_kernel(h_ref, mod_ref, nw_ref, wr_ref, br_ref, u_ref, oh4_ref, rk_ref, pw_ref, cnt_ref, *, tr):
    shift = mod_ref[0, 3:4, :]
    scale = mod_ref[0, 4:5, :]
    u = _rms(h_ref[...], nw_ref[...]) * (1.0 + scale) + shift
    u_ref[...] = _pack_rows(u)
    lane = lax.broadcasted_iota(jnp.int32, (tr, LANES), 1)
    logits = jnp.dot(u.astype(BF16), wr_ref[...], preferred_element_type=F32) + br_ref[...]
    logits = jnp.where(lane < N_EXPERTS, logits, -jnp.inf)
    idx, val = [], []
    cur = logits
    for _ in range(TOP_K):
        m = jnp.max(cur, axis=-1, keepdims=True)
        i = jnp.min(jnp.where(cur == m, lane, LANES), axis=-1, keepdims=True)
        idx.append(i)
        val.append(m)
        cur = jnp.where(lane == i, -jnp.inf, cur)
    ex = [jnp.exp(v - val[0]) for v in val]
    den = ex[0] + ex[1] + ex[2] + ex[3]
    onehot = jnp.zeros((tr, LANES), F32)
    for i in idx:
        onehot = onehot + jnp.where(lane == i, 1.0, 0.0)
    ri = lax.broadcasted_iota(jnp.int32, (tr, tr), 0)
    ci = lax.broadcasted_iota(jnp.int32, (tr, tr), 1)
    tri = jnp.where(ci < ri, 1.0, 0.0).astype(BF16)
    pref = jnp.dot(tri, onehot.astype(BF16), preferred_element_type=F32)
    rk = jnp.zeros((tr, LANES), jnp.int32)
    pw = jnp.zeros((tr, LANES), F32)
    oh4 = jnp.zeros((tr, LANES), F32)
    for j in range(TOP_K):
        rank = jnp.sum(jnp.where(lane == idx[j], pref, 0.0), axis=-1, keepdims=True).astype(jnp.int32)
        rk = jnp.where(lane == j, rank, rk)
        pw = jnp.where(lane == j, ex[j] / den, pw)
        oh4 = oh4 + jnp.where(lane == idx[j] + N_EXPERTS * j, 1.0, 0.0)
    oh4_ref[...] = oh4.astype(BF16)
    rk_ref[...] = rk
    pw_ref[...] = pw
    cnt_ref[0] = jnp.sum(onehot, axis=0, keepdims=True).astype(jnp.int32)


def _route(h1, mod, norm_w, wr_pad, br_pad, seq, tr, tile0, n):
    per_b = seq // tr
    nt = n // tr
    return pl.pallas_call(
        functools.partial(_route_kernel, tr=tr),
        out_shape=(jax.ShapeDtypeStruct((n, PACKED), jnp.int32),
                   jax.ShapeDtypeStruct((n, LANES), BF16),
                   jax.ShapeDtypeStruct((n, LANES), jnp.int32),
                   jax.ShapeDtypeStruct((n, LANES), F32),
                   jax.ShapeDtypeStruct((nt, 1, LANES), jnp.int32)),
        grid=(nt,),
        in_specs=[pl.BlockSpec((tr, D_MODEL), lambda i: (tile0 + i, 0)),
                  pl.BlockSpec((1, 6, D_MODEL), lambda i: ((tile0 + i) // per_b, 0, 0)),
                  pl.BlockSpec((1, D_MODEL), lambda i: (0, 0)),
                  pl.BlockSpec((D_MODEL, LANES), lambda i: (0, 0)),
                  pl.BlockSpec((1, LANES), lambda i: (0, 0))],
        out_specs=(pl.BlockSpec((tr, PACKED), lambda i: (i, 0)),
                   pl.BlockSpec((tr, LANES), lambda i: (i, 0)),
                   pl.BlockSpec((tr, LANES), lambda i: (i, 0)),
                   pl.BlockSpec((tr, LANES), lambda i: (i, 0)),
                   pl.BlockSpec((1, 1, LANES), lambda i: (i, 0, 0))),
        compiler_params=pltpu.CompilerParams(dimension_semantics=("arbitrary",),
                                             vmem_limit_bytes=VMEM_LIMIT),
        name="route",
    )(h1, mod, norm_w.reshape(1, D_MODEL), wr_pad, br_pad)


def _dest_kernel(oh4_ref, rk_ref, bt_ref, o_ref):
    oh = oh4_ref[...]
    start = (jnp.dot(oh, bt_ref[0, 0], preferred_element_type=F32)
             + 256.0 * jnp.dot(oh, bt_ref[0, 1], preferred_element_type=F32)
             + 65536.0 * jnp.dot(oh, bt_ref[0, 2], preferred_element_type=F32))
    o_ref[...] = start.astype(jnp.int32) + rk_ref[...]


def _dest(oh4, rk, bt, tr):
    n = oh4.shape[0]
    return pl.pallas_call(
        _dest_kernel,
        out_shape=jax.ShapeDtypeStruct((n, LANES), jnp.int32),
        grid=(n // tr,),
        in_specs=[pl.BlockSpec((tr, LANES), lambda i: (i, 0)),
                  pl.BlockSpec((tr, LANES), lambda i: (i, 0)),
                  pl.BlockSpec((1, 3, LANES, LANES), lambda i: (i, 0, 0, 0))],
        out_specs=pl.BlockSpec((tr, LANES), lambda i: (i, 0)),
        name="dest",
    )(oh4, rk, bt)


def _sc_workers():
    info = plsc.get_sparse_core_info()
    return info.num_cores, info.num_cores * info.num_subcores


def _sc_scatter_rows(rows, idx_slots, pad_idx, n_out):
    n_cores, n_workers = _sc_workers()
    n, w = rows.shape
    k = idx_slots.shape[0] // n
    per_worker = n // n_workers
    pad_per_worker = pad_idx.shape[0] // n_workers
    assert per_worker % SC_CHUNK == 0 and pad_per_worker % SC_CHUNK == 0
    mesh = plsc.VectorSubcoreMesh(core_axis_name="c", subcore_axis_name="s")
    zeros = jnp.zeros((SC_CHUNK, w), rows.dtype)

    @functools.partial(
        pl.kernel, mesh=mesh,
        out_type=jax.ShapeDtypeStruct((n_out, w), rows.dtype),
        scratch_types=[pltpu.VMEM((SC_CHUNK,), jnp.int32),
                       pltpu.VMEM((SC_CHUNK, w), rows.dtype),
                       pltpu.SemaphoreType.DMA],
        name="sc_scatter",
    )
    def scatter(rows_hbm, idx_hbm, pad_hbm, zeros_hbm, out_hbm, idx_v, rows_v, sem):
        wid = lax.axis_index("s") * n_cores + lax.axis_index("c")

        def body(i, carry):
            off = pl.multiple_of(wid * per_worker + i * SC_CHUNK, 8)
            pltpu.sync_copy(rows_hbm.at[pl.ds(off, SC_CHUNK)], rows_v)
            for j in range(k):
                pltpu.sync_copy(idx_hbm.at[pl.ds(pl.multiple_of(j * n + off, 8), SC_CHUNK)], idx_v)
                pltpu.async_copy(rows_v, out_hbm.at[idx_v], sem).wait()
            return carry

        lax.fori_loop(0, per_worker // SC_CHUNK, body, 0)
        pltpu.sync_copy(zeros_hbm, rows_v)

        def pad_body(i, carry):
            off = pl.multiple_of(wid * pad_per_worker + i * SC_CHUNK, 8)
            pltpu.sync_copy(pad_hbm.at[pl.ds(off, SC_CHUNK)], idx_v)
            pltpu.async_copy(rows_v, out_hbm.at[idx_v], sem).wait()
            return carry

        lax.fori_loop(0, pad_per_worker // SC_CHUNK, pad_body, 0)

    return scatter(rows, idx_slots, pad_idx, zeros)


def _experts_kernel(be_ref, nu_ref, x_ref, w1_ref, b1_ref, w2_ref, b2_ref, o_ref, w1_bf, w2_bf):
    i = pl.program_id(0)

    @pl.when((i == 0) | (be_ref[i] != be_ref[jnp.maximum(i - 1, 0)]))
    def _():
        cw = 256
        for c in range(2 * D_FF // cw):
            w1_bf[:, c * cw:(c + 1) * cw] = w1_ref[0, :, c * cw:(c + 1) * cw].astype(BF16)
        for c in range(D_MODEL // cw):
            w2_bf[:, c * cw:(c + 1) * cw] = w2_ref[0, :, c * cw:(c + 1) * cw].astype(BF16)

    @pl.when(i < nu_ref[0])
    def _():
        step = 512
        for r in range(ROW_BLOCK // EXPERT_ROWS):
            rs = slice(r * EXPERT_ROWS, (r + 1) * EXPERT_ROWS)
            x = _unpack_rows(x_ref[rs, :]).astype(BF16)
            acc = None
            for j in range(D_FF // step):
                cs = slice(j * step, (j + 1) * step)
                ls = slice(D_FF + j * step, D_FF + (j + 1) * step)
                glu = jnp.dot(x, w1_bf[:, cs], preferred_element_type=F32) + b1_ref[0, :, cs]
                lin = jnp.dot(x, w1_bf[:, ls], preferred_element_type=F32) + b1_ref[0, :, ls]
                glu = jnp.minimum(glu, SWIGLU_LIMIT)
                lin = jnp.clip(lin, -SWIGLU_LIMIT, SWIGLU_LIMIT)
                act = (glu * _sigmoid(SWIGLU_ALPHA * glu) * (lin + 1.0)).astype(BF16)
                part = jnp.dot(act, w2_bf[cs, :], preferred_element_type=F32)
                acc = part if acc is None else acc + part
            o_ref[rs, :] = _pack_rows(acc + b2_ref[0])

    @pl.when(i >= nu_ref[0])
    def _():
        o_ref[...] = jnp.zeros_like(o_ref)


def _experts(block_e, n_used, xs, w1_bf, b1, w2_bf, b2):
    n_rows = xs.shape[0]
    nb = n_rows // ROW_BLOCK
    grid_spec = pltpu.PrefetchScalarGridSpec(
        num_scalar_prefetch=2,
        grid=(nb,),
        in_specs=[pl.BlockSpec((ROW_BLOCK, PACKED), lambda i, be, nu: (jnp.minimum(i, nu[0] - 1), 0)),
                  pl.BlockSpec((1, D_MODEL, 2 * D_FF), lambda i, be, nu: (be[i], 0, 0)),
                  pl.BlockSpec((1, 1, 2 * D_FF), lambda i, be, nu: (be[i], 0, 0)),
                  pl.BlockSpec((1, D_FF, D_MODEL), lambda i, be, nu: (be[i], 0, 0)),
                  pl.BlockSpec((1, 1, D_MODEL), lambda i, be, nu: (be[i], 0, 0))],
        out_specs=pl.BlockSpec((ROW_BLOCK, PACKED), lambda i, be, nu: (i, 0)),
        scratch_shapes=[pltpu.VMEM((D_MODEL, 2 * D_FF), BF16), pltpu.VMEM((D_FF, D_MODEL), BF16)],
    )
    return pl.pallas_call(
        _experts_kernel,
        out_shape=jax.ShapeDtypeStruct((n_rows, PACKED), jnp.int32),
        grid_spec=grid_spec,
        compiler_params=pltpu.CompilerParams(dimension_semantics=("arbitrary",),
                                             vmem_limit_bytes=VMEM_LIMIT),
        name="experts",
    )(block_e, n_used, xs, w1_bf, b1.reshape(N_EXPERTS, 1, 2 * D_FF), w2_bf,
      b2.reshape(N_EXPERTS, 1, D_MODEL))


def _sc_gather_rows(table, idx_flat):
    n_cores, n_workers = _sc_workers()
    n_idx = idx_flat.shape[0]
    w = table.shape[1]
    per_worker = n_idx // n_workers
    n_chunks = per_worker // SC_CHUNK
    assert per_worker * n_workers == n_idx and n_chunks * SC_CHUNK == per_worker and n_chunks % 2 == 0
    mesh = plsc.VectorSubcoreMesh(core_axis_name="c", subcore_axis_name="s")

    @functools.partial(
        pl.kernel, mesh=mesh,
        out_type=jax.ShapeDtypeStruct((n_idx, w), table.dtype),
        scratch_types=[pltpu.VMEM((SC_CHUNK,), jnp.int32), pltpu.VMEM((SC_CHUNK,), jnp.int32),
                       pltpu.VMEM((SC_CHUNK, w), table.dtype), pltpu.VMEM((SC_CHUNK, w), table.dtype),
                       pltpu.SemaphoreType.DMA, pltpu.SemaphoreType.DMA],
        name="sc_gather",
    )
    def gather(table_hbm, idx_hbm, out_hbm, idx0, idx1, rows0, rows1, sem0, sem1):
        wid = lax.axis_index("s") * n_cores + lax.axis_index("c")
        bufs = ((idx0, rows0, sem0), (idx1, rows1, sem1))

        def chunk_off(c):
            return pl.multiple_of(wid * per_worker + c * SC_CHUNK, 8)

        def start(c, b):
            idx_v, rows_v, sem = bufs[b]
            pltpu.sync_copy(idx_hbm.at[pl.ds(chunk_off(c), SC_CHUNK)], idx_v)
            pltpu.async_copy(table_hbm.at[idx_v], rows_v, sem)

        def finish(c, b):
            idx_v, rows_v, sem = bufs[b]
            pltpu.make_async_copy(table_hbm.at[idx_v], rows_v, sem).wait()
            pltpu.sync_copy(rows_v, out_hbm.at[pl.ds(chunk_off(c), SC_CHUNK)])

        start(0, 0)

        def body(i, carry):
            start(2 * i + 1, 1)
            finish(2 * i, 0)

            @pl.when(i < n_chunks // 2 - 1)
            def _():
                start(2 * i + 2, 0)

            finish(2 * i + 1, 1)
            return carry

        lax.fori_loop(0, n_chunks // 2, body, 0)

    return gather(table, idx_flat)


def _finish_kernel(h_ref, pw_ref, mod_ref, nw_ref, y0_ref, y1_ref, y2_ref, y3_ref, *rest):
    o_ref = rest[-1]
    pw = pw_ref[...]
    moe = pw[:, 0:1] * _unpack_rows(y0_ref[0])
    for j, y_ref in enumerate((y1_ref, y2_ref, y3_ref), start=1):
        moe = moe + pw[:, j:j + 1] * _unpack_rows(y_ref[0])
    gate_f = mod_ref[0, 5:6, :]
    o_ref[...] = _rms(h_ref[...] + gate_f * moe, nw_ref[...])


def _finish(h1, pw, mod, norm_w, y4, seq, tc, tile0, prev_out):
    n = pw.shape[0]
    per_b = seq // tc
    slot = lambda j: pl.BlockSpec((1, tc, PACKED), lambda i: (j, i, 0))
    in_specs = [pl.BlockSpec((tc, D_MODEL), lambda i: (tile0 + i, 0)),
                pl.BlockSpec((tc, LANES), lambda i: (i, 0)),
                pl.BlockSpec((1, 6, D_MODEL), lambda i: ((tile0 + i) // per_b, 0, 0)),
                pl.BlockSpec((1, D_MODEL), lambda i: (0, 0)),
                slot(0), slot(1), slot(2), slot(3)]
    args = [h1, pw, mod, norm_w.reshape(1, D_MODEL), y4, y4, y4, y4]
    aliases = {}
    if prev_out is not None:
        in_specs.append(pl.BlockSpec(memory_space=pl.ANY))
        args.append(prev_out)
        aliases = {len(args) - 1: 0}
    return pl.pallas_call(
        _finish_kernel,
        out_shape=jax.ShapeDtypeStruct(h1.shape, F32),
        grid=(n // tc,),
        in_specs=in_specs,
        out_specs=pl.BlockSpec((tc, D_MODEL), lambda i: (tile0 + i, 0)),
        input_output_aliases=aliases,
        compiler_params=pltpu.CompilerParams(dimension_semantics=("arbitrary",),
                                             vmem_limit_bytes=VMEM_LIMIT),
        name="finish",
    )(*args)


def _moe_plan(counts, n_assign):
    cnt = counts[:, 0, :N_EXPERTS]
    sizes = jnp.sum(cnt, axis=0)
    padded = (sizes + ROW_BLOCK - 1) // ROW_BLOCK * ROW_BLOCK
    pad_end = jnp.cumsum(padded)
    pad_start = pad_end - padded
    tile_base = pad_start[None, :] + jnp.cumsum(cnt, axis=0) - cnt
    digits = jnp.stack([tile_base % 256, (tile_base // 256) % 256, tile_base // 65536], axis=1)
    rows = jnp.tile(digits, (1, 1, TOP_K))
    slot_of_row = jnp.arange(LANES, dtype=jnp.int32) // N_EXPERTS
    col = jnp.arange(LANES, dtype=jnp.int32)
    bt = jnp.where(slot_of_row[:, None] == col[None, :], rows[..., None], 0).astype(BF16)
    nb = n_assign // ROW_BLOCK + N_EXPERTS
    block_start = jnp.arange(nb, dtype=jnp.int32) * ROW_BLOCK
    block_e = jnp.minimum(jnp.sum(pad_end[None, :] <= block_start[:, None], axis=1),
                          N_EXPERTS - 1).astype(jnp.int32)
    n_used = (pad_end[-1] // ROW_BLOCK).astype(jnp.int32).reshape(1)
    r = jnp.arange(ROW_BLOCK, dtype=jnp.int32)[None, :]
    pad_idx = jnp.where(r < (padded - sizes)[:, None], (pad_start + sizes)[:, None] + r,
                        nb * ROW_BLOCK - 1).astype(jnp.int32).reshape(PAD_SLOTS)
    return bt, block_e, n_used, pad_idx, nb * ROW_BLOCK


def kernel(x, c, w_ada, b_ada, norm_mix_w, w_in, hgrn_lower_bounds, hgrn_norm_w, conv_w,
           w_hgrn_out, w_conv_out, w_mix_out, norm_ffn_w, w_router, b_router, w1, b1, w2, b2,
           norm_final_w):
    bsz, seq, d = x.shape
    assert d == D_MODEL and seq % CHUNK == 0
    n = bsz * seq
    depth = w_ada.shape[0]
    tile = min(512, seq)
    assert seq % tile == 0 and n % (MOE_GROUPS * tile) == 0 and (n // MOE_GROUPS * TOP_K) % ROW_BLOCK == 0
    h = x.reshape(n, d)
    wr_pad = jnp.zeros((depth, D_MODEL, LANES), BF16).at[:, :, :N_EXPERTS].set(w_router.astype(BF16))
    br_pad = jnp.zeros((depth, 1, LANES), F32).at[:, 0, :N_EXPERTS].set(b_router)
    for layer in range(depth):
        mod = _ada(c, w_ada[layer], b_ada[layer]).reshape(bsz, 6, d)
        proj = _inproj(h, mod, norm_mix_w[layer], w_in[layer].astype(BF16), seq, tile)
        h = _mix(proj, h, mod, hgrn_lower_bounds, hgrn_norm_w[layer], conv_w[layer],
                 w_hgrn_out[layer].astype(BF16), w_conv_out[layer].astype(BF16),
                 w_mix_out[layer].astype(BF16), layer, bsz, seq, tile)
        assert layer == depth - 1, "only the last layer applies the final norm"
        ng = n // MOE_GROUPS
        out = None
        for grp in range(MOE_GROUPS):
            tile0 = grp * (ng // tile)
            u2, oh4, rk, pw, counts = _route(h, mod, norm_ffn_w[layer], wr_pad[layer], br_pad[layer],
                                             seq, tile, tile0, ng)
            bt, block_e, n_used, pad_idx, n_rows = _moe_plan(counts, ng * TOP_K)
            dest_slots = _dest(oh4, rk, bt, tile)[:, :TOP_K].T.reshape(TOP_K * ng)
            xs = _sc_scatter_rows(u2, dest_slots, pad_idx, n_rows)
            ys = _experts(block_e, n_used, xs, w1[layer], b1[layer], w2[layer], b2[layer])
            y4 = _sc_gather_rows(ys, dest_slots).reshape(TOP_K, ng, PACKED)
            out = _finish(h, pw, mod, norm_final_w, y4, seq, tile, tile0, out)
        h = out
    return h.reshape(bsz, seq, d)
```

```python
import functools

import jax
import jax.numpy as jnp
from jax import lax
from jax.experimental import pallas as pl
from jax.experimental.pallas import tpu as pltpu
from jax.experimental.pallas import tpu_sc as plsc

F32 = jnp.float32
BF16 = jnp.bfloat16

D_MODEL = 1024
HGRN_HEADS = 4
HEAD_DIM = 128
HGRN_WIDTH = HGRN_HEADS * HEAD_DIM
CONV_WIDTH = 512
CONV_K = 3
CHUNK = 64
N_EXPERTS = 32
TOP_K = 4
D_FF = 1024
SWIGLU_LIMIT = 7.0
SWIGLU_ALPHA = 1.702
EPS = 1e-6
IN_COLS = 4 * HGRN_WIDTH + 3 * CONV_WIDTH + 2 * D_MODEL
LANES = 128
ROW_BLOCK = 512
EXPERT_ROWS = 256
PACKED = D_MODEL // 2
MOE_GROUPS = 2
SC_CHUNK = 64
PAD_SLOTS = N_EXPERTS * ROW_BLOCK
VMEM_LIMIT = 56 * 1024 * 1024


def _sigmoid(x):
    return 1.0 / (1.0 + jnp.exp(-x))


def _rms(x, w):
    ms = jnp.mean(x * x, axis=-1, keepdims=True)
    return x * lax.rsqrt(ms + EPS) * w


def _pack_rows(x):
    w = x.shape[1] // 2
    lo = lax.bitcast_convert_type(x[:, :w].astype(BF16).astype(F32), jnp.uint32)
    hi = lax.bitcast_convert_type(x[:, w:].astype(BF16).astype(F32), jnp.uint32)
    return lax.bitcast_convert_type((lo >> 16) | (hi & jnp.uint32(0xFFFF0000)), jnp.int32)


def _unpack_rows(p):
    u = lax.bitcast_convert_type(p, jnp.uint32)
    lo = lax.bitcast_convert_type(u << 16, F32)
    hi = lax.bitcast_convert_type(u & jnp.uint32(0xFFFF0000), F32)
    return jnp.concatenate([lo, hi], axis=1)


def _nt_dot(a, b):
    return lax.dot_general(a, b, (((1,), (1,)), ((), ())), preferred_element_type=F32)


def _tn_dot(a, b):
    return lax.dot_general(a, b, (((0,), (0,)), ((), ())), preferred_element_type=F32)


def _ada_kernel(c_ref, w_ref, b_ref, o_ref):
    c = c_ref[...]
    sc = (c * _sigmoid(c)).astype(BF16)
    o_ref[...] = jnp.dot(sc, w_ref[...].astype(BF16), preferred_element_type=F32) + b_ref[...]


def _ada(c, w_ada, b_ada):
    bsz, d = c.shape
    n = w_ada.shape[1]
    return pl.pallas_call(
        _ada_kernel,
        out_shape=jax.ShapeDtypeStruct((bsz, n), F32),
        grid=(n // d,),
        in_specs=[pl.BlockSpec((bsz, d), lambda j: (0, 0)),
                  pl.BlockSpec((d, d), lambda j: (0, j)),
                  pl.BlockSpec((1, d), lambda j: (0, j))],
        out_specs=pl.BlockSpec((bsz, d), lambda j: (0, j)),
        name="ada",
    )(c, w_ada, b_ada.reshape(1, n))


def _inproj_kernel(x_ref, mod_ref, nw_ref, w_ref, o_ref, u_scr):
    y = _rms(x_ref[...], nw_ref[...])
    shift = mod_ref[0, 0:1, :]
    scale = mod_ref[0, 1:2, :]
    u_scr[...] = (y * (1.0 + scale) + shift).astype(BF16)
    step = 512
    for j in range(IN_COLS // step):
        o_ref[:, j * step:(j + 1) * step] = jnp.dot(
            u_scr[...], w_ref[:, j * step:(j + 1) * step], preferred_element_type=F32).astype(BF16)


def _inproj(x2, mod, norm_w, w_in_bf, seq, tm):
    n = x2.shape[0]
    per_b = seq // tm
    return pl.pallas_call(
        _inproj_kernel,
        out_shape=jax.ShapeDtypeStruct((n, IN_COLS), BF16),
        grid=(n // tm,),
        in_specs=[pl.BlockSpec((tm, D_MODEL), lambda i: (i, 0)),
                  pl.BlockSpec((1, 6, D_MODEL), lambda i: (i // per_b, 0, 0)),
                  pl.BlockSpec((1, D_MODEL), lambda i: (0, 0)),
                  pl.BlockSpec((D_MODEL, IN_COLS), lambda i: (0, 0))],
        out_specs=pl.BlockSpec((tm, IN_COLS), lambda i: (i, 0)),
        scratch_shapes=[pltpu.VMEM((tm, D_MODEL), BF16)],
        compiler_params=pltpu.CompilerParams(dimension_semantics=("arbitrary",),
                                             vmem_limit_bytes=VMEM_LIMIT),
        name="inproj",
    )(x2, mod, norm_w.reshape(1, D_MODEL), w_in_bf)


_HEADS = [slice(h * HEAD_DIM, (h + 1) * HEAD_DIM) for h in range(HGRN_HEADS)]


def _chunk_rows(c):
    if isinstance(c, int):
        return pl.ds(c * CHUNK, CHUNK)
    return pl.ds(pl.multiple_of(c * CHUNK, CHUNK), CHUNK)


def _hgrn_gates(c, lb, tri, f_ref, b_scr, k_scr, f_scr):
    fx = f_ref[_chunk_rows(c), :].astype(F32)
    f = lb + (1.0 - lb) * _sigmoid(fx)
    g = jnp.log(f)
    g1 = g.astype(BF16)
    r1 = g - g1.astype(F32)
    g2 = r1.astype(BF16)
    g3 = (r1 - g2.astype(F32)).astype(BF16)
    b_scr[...] = (jnp.dot(tri, g1, preferred_element_type=F32)
                  + jnp.dot(tri, g2, preferred_element_type=F32)
                  + jnp.dot(tri, g3, preferred_element_type=F32))
    f_scr[...] = f
    k_scr[...] = 1.0 - f


def _hgrn_scores(c, q_ref, b_scr, k_scr, f_scr):
    rows = _chunk_rows(c)
    qs = [q_ref[rows, hs].astype(F32) for hs in _HEADS]
    s_mats = [_level_scores(1, qs[h], _HEADS[h], b_scr, k_scr, f_scr) for h in range(HGRN_HEADS)]
    for lvl in range(2, 7):
        for h in range(HGRN_HEADS):
            s_mats[h] = s_mats[h] + _level_scores(lvl, qs[h], _HEADS[h], b_scr, k_scr, f_scr)
    return s_mats


def _hgrn_outputs(c, s_mats, q_ref, v_ref, st_scr, o_scr, b_scr, k_scr):
    rows = _chunk_rows(c)
    for h, hs in enumerate(_HEADS):
        q = q_ref[rows, hs].astype(F32)
        v_bf = v_ref[rows, hs]
        b = b_scr[:, hs]
        kk = k_scr[:, hs]
        b_last = b_scr[CHUNK - 1:CHUNK, hs]
        st = st_scr[h]
        qd = (q * jnp.exp(b)).astype(BF16)
        kdec = (kk * jnp.exp(b_last - b)).astype(BF16)
        diag = jnp.sum(q * kk, axis=-1, keepdims=True)
        o_scr[rows, hs] = (_nt_dot(qd, st.astype(BF16))
                           + jnp.dot(s_mats[h].astype(BF16), v_bf, preferred_element_type=F32)
                           + diag * v_bf.astype(F32))
        st_scr[h] = jnp.exp(b_last) * st + _tn_dot(v_bf, kdec)


def _level_scores(lvl, q, hs, b_scr, k_scr, f_scr):
    row = lax.broadcasted_iota(jnp.int32, (CHUNK, 1), 0)
    col = lax.broadcasted_iota(jnp.int32, (1, CHUNK), 1)
    b = b_scr[:, hs]
    kk = k_scr[:, hs]
    blk = 1 << lvl
    half = blk // 2
    if lvl == 1:
        odd = (row & 1) == 1
        qx = jnp.where(odd, q * f_scr[:, hs], 0.0).astype(BF16)
        kx = jnp.where(odd, 0.0, kk).astype(BF16)
        return jnp.where((row >> 1) == (col >> 1), _nt_dot(qx, kx), 0.0)
    if half < 8:
        groups = []
        sub = lax.broadcasted_iota(jnp.int32, (8, 1), 0)
        for j in range(CHUNK // 8):
            rj = None
            for k in reversed(range(8 // blk)):
                m = 8 * j + k * blk + half - 1
                bm = jnp.broadcast_to(b_scr[m:m + 1, hs], (8, HEAD_DIM))
                rj = bm if rj is None else jnp.where(sub < (k + 1) * blk, bm, rj)
            groups.append(rj)
        ref = jnp.concatenate(groups, axis=0)
        second = (row & (blk - 1)) >= half
        qx = (q * jnp.exp(jnp.where(second, b - ref, -jnp.inf))).astype(BF16)
        kx = (kk * jnp.exp(jnp.where(second, -jnp.inf, ref - b))).astype(BF16)
        return jnp.where((row >> lvl) == (col >> lvl), _nt_dot(qx, kx), 0.0)
    n_blk = CHUNK // blk
    qparts, kparts = [], []
    for j in range(n_blk):
        m = j * blk + half - 1
        bm = b_scr[m:m + 1, hs]
        tq = slice(j * blk + half, (j + 1) * blk)
        tk = slice(j * blk, j * blk + half)
        qparts.append(q[tq] * jnp.exp(b[tq] - bm))
        kparts.append(kk[tk] * jnp.exp(bm - b[tk]))
        kparts.append(jnp.zeros((half, HEAD_DIM), F32))
    qx = jnp.concatenate(qparts, axis=0).astype(BF16)
    kx = jnp.concatenate(kparts, axis=0).astype(BF16)
    sc = _nt_dot(qx, kx)
    if n_blk > 1:
        crow = lax.broadcasted_iota(jnp.int32, (CHUNK // 2, 1), 0)
        sc = jnp.where((crow // half) == (col >> lvl), sc, 0.0)
    pieces = []
    for j in range(n_blk):
        pieces.append(jnp.zeros((half, CHUNK), F32))
        pieces.append(sc[j * half:(j + 1) * half])
    return jnp.concatenate(pieces, axis=0)


def _mix_kernel(lbt_ref, q_ref, f_ref, i_ref, g_ref, cb_ref, cc_ref, ch_ref, ga0_ref, ga1_ref,
                gb0_ref, gb1_ref, x_ref, mod_ref, hnw_ref, cw_ref, wa_ref, wb_ref, wm_ref,
                o_ref, st_scr, o_scr, carry_scr, b_scr, k_scr, f_scr, *, layer, rows_per_step, epi_rows):
    @pl.when(pl.program_id(1) == 0)
    def _():
        st_scr[...] = jnp.zeros_like(st_scr)
        carry_scr[...] = jnp.zeros_like(carry_scr)

    tab = lbt_ref[...]
    tmax = jnp.max(tab, axis=0, keepdims=True)
    te = jnp.exp(tab - tmax)
    lb = jnp.sum(te[0:layer + 1], axis=0, keepdims=True) / jnp.sum(te, axis=0, keepdims=True)

    ri = lax.broadcasted_iota(jnp.int32, (CHUNK, CHUNK), 0)
    ci = lax.broadcasted_iota(jnp.int32, (CHUNK, CHUNK), 1)
    tri = jnp.where(ci <= ri, 1.0, 0.0).astype(BF16)

    n_chunks = rows_per_step // CHUNK

    def slot(s):
        return b_scr.at[s], k_scr.at[s], f_scr.at[s]

    def chunk(c, c_next, s):
        bs, ks, fs = slot(s)
        s_mats = _hgrn_scores(c, q_ref, bs, ks, fs)
        _hgrn_gates(c_next, lb, tri, f_ref, *slot(1 - s))
        _hgrn_outputs(c, s_mats, q_ref, i_ref, st_scr, o_scr, bs, ks)

    def pair_body(i, carry):
        chunk(2 * i, 2 * i + 1, 0)
        chunk(2 * i + 1, jnp.minimum(2 * i + 2, n_chunks - 1), 1)
        return carry

    _hgrn_gates(0, lb, tri, f_ref, *slot(0))
    lax.fori_loop(0, n_chunks // 2, pair_body, 0)

    gate_m = mod_ref[0, 2:3, :]
    hnw = hnw_ref[...]
    cw0 = cw_ref[0:1, :]
    cw1 = cw_ref[1:2, :]
    cw2 = cw_ref[2:3, :]
    for r in range(rows_per_step // epi_rows):
        rs = slice(r * epi_rows, (r + 1) * epi_rows)
        parts = []
        for h in range(HGRN_HEADS):
            hs = slice(h * HEAD_DIM, (h + 1) * HEAD_DIM)
            oh = _rms(o_scr[rs, hs], hnw)
            go = g_ref[rs, hs].astype(F32)
            parts.append((oh * (go * _sigmoid(go))).astype(BF16))
        ya = jnp.dot(jnp.concatenate(parts, axis=1), wa_ref[...], preferred_element_type=F32)
        uc = cc_ref[rs, :].astype(F32) * ch_ref[rs, :].astype(F32)
        prev = carry_scr[...]
        rowi = lax.broadcasted_iota(jnp.int32, (epi_rows, 1), 0)
        s1 = jnp.where(rowi == 0, prev[7:8, :], pltpu.roll(uc, 1, 0))
        s2 = pltpu.roll(uc, 2, 0)
        s2 = jnp.where(rowi == 0, prev[6:7, :], jnp.where(rowi == 1, prev[7:8, :], s2))
        carry_scr[...] = uc[epi_rows - 8:epi_rows, :]
        yc = cb_ref[rs, :].astype(F32) * (cw2 * uc + cw1 * s1 + cw0 * s2)
        yb = jnp.dot(yc.astype(BF16), wb_ref[...], preferred_element_type=F32)
        ga = jnp.concatenate([ga0_ref[rs, :], ga1_ref[rs, :]], axis=1).astype(F32)
        gb = jnp.concatenate([gb0_ref[rs, :], gb1_ref[rs, :]], axis=1).astype(F32)
        merged = (_sigmoid(ga) * ya + _sigmoid(gb) * yb).astype(BF16)
        o_ref[rs, :] = x_ref[rs, :] + gate_m * jnp.dot(merged, wm_ref[...], preferred_element_type=F32)


def _mix(proj, x2, mod, lb_table, hgrn_norm_w, conv_w, wa, wb, wm, layer, bsz, seq, tt):
    n = x2.shape[0]
    per_b = seq // tt

    def col(col_block):
        return pl.BlockSpec((tt, 512), lambda b, t: (b * per_b + t, col_block))

    const = lambda shape: pl.BlockSpec(shape, lambda b, t: (0,) * len(shape))
    in_specs = [
        const(lb_table.shape),
        col(0), col(1), col(2), col(3),
        col(4), col(5), col(6),
        col(7), col(8), col(9), col(10),
        pl.BlockSpec((tt, D_MODEL), lambda b, t: (b * per_b + t, 0)),
        pl.BlockSpec((1, 6, D_MODEL), lambda b, t: (b, 0, 0)),
        const((1, HEAD_DIM)), const((CONV_K, CONV_WIDTH)),
        const((HGRN_WIDTH, D_MODEL)), const((CONV_WIDTH, D_MODEL)), const((D_MODEL, D_MODEL)),
    ]
    kern = functools.partial(_mix_kernel, layer=layer, rows_per_step=tt, epi_rows=min(tt, 256))
    return pl.pallas_call(
        kern,
        out_shape=jax.ShapeDtypeStruct((n, D_MODEL), F32),
        grid=(bsz, per_b),
        in_specs=in_specs,
        out_specs=pl.BlockSpec((tt, D_MODEL), lambda b, t: (b * per_b + t, 0)),
        scratch_shapes=[pltpu.VMEM((HGRN_HEADS, HEAD_DIM, HEAD_DIM), F32),
                        pltpu.VMEM((tt, HGRN_WIDTH), F32),
                        pltpu.VMEM((8, CONV_WIDTH), F32),
                        pltpu.VMEM((2, CHUNK, HGRN_WIDTH), F32),
                        pltpu.VMEM((2, CHUNK, HGRN_WIDTH), F32),
                        pltpu.VMEM((2, CHUNK, HGRN_WIDTH), F32)],
        compiler_params=pltpu.CompilerParams(dimension_semantics=("arbitrary", "arbitrary"),
                                             vmem_limit_bytes=VMEM_LIMIT),
        name="mix",
    )(lb_table, *([proj] * 11),
      x2, mod, hgrn_norm_w.reshape(1, HEAD_DIM), conv_w, wa, wb, wm)


def _route_kernel(h_ref, mod_ref, nw_ref, wr_ref, br_ref, u_ref, oh4_ref, rk_ref, pw_ref, cnt_ref, *, tr):
    shift = mod_ref[0, 3:4, :]
    scale = mod_ref[0, 4:5, :]
    u = _rms(h_ref[...], nw_ref[...]) * (1.0 + scale) + shift
    u_ref[...] = _pack_rows(u)
    lane = lax.broadcasted_iota(jnp.int32, (tr, LANES), 1)
    logits = jnp.dot(u.astype(BF16), wr_ref[...], preferred_element_type=F32) + br_ref[...]
    logits = jnp.where(lane < N_EXPERTS, logits, -jnp.inf)
    idx, val = [], []
    cur = logits
    for _ in range(TOP_K):
        m = jnp.max(cur, axis=-1, keepdims=True)
        i = jnp.min(jnp.where(cur == m, lane, LANES), axis=-1, keepdims=True)
        idx.append(i)
        val.append(m)
        cur = jnp.where(lane == i, -jnp.inf, cur)
    ex = [jnp.exp(v - val[0]) for v in val]
    den = ex[0] + ex[1] + ex[2] + ex[3]
    onehot = jnp.zeros((tr, LANES), F32)
    for i in idx:
        onehot = onehot + jnp.where(lane == i, 1.0, 0.0)
    ri = lax.broadcasted_iota(jnp.int32, (tr, tr), 0)
    ci = lax.broadcasted_iota(jnp.int32, (tr, tr), 1)
    tri = jnp.where(ci < ri, 1.0, 0.0).astype(BF16)
    pref = jnp.dot(tri, onehot.astype(BF16), preferred_element_type=F32)
    rk = jnp.zeros((tr, LANES), jnp.int32)
    pw = jnp.zeros((tr, LANES), F32)
    oh4 = jnp.zeros((tr, LANES), F32)
    for j in range(TOP_K):
        rank = jnp.sum(jnp.where(lane == idx[j], pref, 0.0), axis=-1, keepdims=True).astype(jnp.int32)
        rk = jnp.where(lane == j, rank, rk)
        pw = jnp.where(lane == j, ex[j] / den, pw)
        oh4 = oh4 + jnp.where(lane == idx[j] + N_EXPERTS * j, 1.0, 0.0)
    oh4_ref[...] = oh4.astype(BF16)
    rk_ref[...] = rk
    pw_ref[...] = pw
    cnt_ref[0] = jnp.sum(onehot, axis=0, keepdims=True).astype(jnp.int32)


def _route(h1, mod, norm_w, wr_pad, br_pad, seq, tr, tile0, n):
    per_b = seq // tr
    nt = n // tr
    return pl.pallas_call(
        functools.partial(_route_kernel, tr=tr),
        out_shape=(jax.ShapeDtypeStruct((n, PACKED), jnp.int32),
                   jax.ShapeDtypeStruct((n, LANES), BF16),
                   jax.ShapeDtypeStruct((n, LANES), jnp.int32),
                   jax.ShapeDtypeStruct((n, LANES), F32),
                   jax.ShapeDtypeStruct((nt, 1, LANES), jnp.int32)),
        grid=(nt,),
        in_specs=[pl.BlockSpec((tr, D_MODEL), lambda i: (tile0 + i, 0)),
                  pl.BlockSpec((1, 6, D_MODEL), lambda i: ((tile0 + i) // per_b, 0, 0)),
                  pl.BlockSpec((1, D_MODEL), lambda i: (0, 0)),
                  pl.BlockSpec((D_MODEL, LANES), lambda i: (0, 0)),
                  pl.BlockSpec((1, LANES), lambda i: (0, 0))],
        out_specs=(pl.BlockSpec((tr, PACKED), lambda i: (i, 0)),
                   pl.BlockSpec((tr, LANES), lambda i: (i, 0)),
                   pl.BlockSpec((tr, LANES), lambda i: (i, 0)),
                   pl.BlockSpec((tr, LANES), lambda i: (i, 0)),
                   pl.BlockSpec((1, 1, LANES), lambda i: (i, 0, 0))),
        compiler_params=pltpu.CompilerParams(dimension_semantics=("arbitrary",),
                                             vmem_limit_bytes=VMEM_LIMIT),
        name="route",
    )(h1, mod, norm_w.reshape(1, D_MODEL), wr_pad, br_pad)


def _dest_kernel(oh4_ref, rk_ref, bt_ref, o_ref):
    oh = oh4_ref[...]
    start = (jnp.dot(oh, bt_ref[0, 0], preferred_element_type=F32)
             + 256.0 * jnp.dot(oh, bt_ref[0, 1], preferred_element_type=F32)
             + 65536.0 * jnp.dot(oh, bt_ref[0, 2], preferred_element_type=F32))
    o_ref[...] = start.astype(jnp.int32) + rk_ref[...]


def _dest(oh4, rk, bt, tr):
    n = oh4.shape[0]
    return pl.pallas_call(
        _dest_kernel,
        out_shape=jax.ShapeDtypeStruct((n, LANES), jnp.int32),
        grid=(n // tr,),
        in_specs=[pl.BlockSpec((tr, LANES), lambda i: (i, 0)),
                  pl.BlockSpec((tr, LANES), lambda i: (i, 0)),
                  pl.BlockSpec((1, 3, LANES, LANES), lambda i: (i, 0, 0, 0))],
        out_specs=pl.BlockSpec((tr, LANES), lambda i: (i, 0)),
        name="dest",
    )(oh4, rk, bt)


def _sc_workers():
    info = plsc.get_sparse_core_info()
    return info.num_cores, info.num_cores * info.num_subcores


def _sc_scatter_rows(rows, idx_slots, pad_idx, n_out):
    n_cores, n_workers = _sc_workers()
    n, w = rows.shape
    k = idx_slots.shape[0] // n
    per_worker = n // n_workers
    pad_per_worker = pad_idx.shape[0] // n_workers
    assert per_worker % SC_CHUNK == 0 and pad_per_worker % SC_CHUNK == 0
    mesh = plsc.VectorSubcoreMesh(core_axis_name="c", subcore_axis_name="s")
    zeros = jnp.zeros((SC_CHUNK, w), rows.dtype)

    @functools.partial(
        pl.kernel, mesh=mesh,
        out_type=jax.ShapeDtypeStruct((n_out, w), rows.dtype),
        scratch_types=[pltpu.VMEM((SC_CHUNK,), jnp.int32),
                       pltpu.VMEM((SC_CHUNK, w), rows.dtype),
                       pltpu.SemaphoreType.DMA],
        name="sc_scatter",
    )
    def scatter(rows_hbm, idx_hbm, pad_hbm, zeros_hbm, out_hbm, idx_v, rows_v, sem):
        wid = lax.axis_index("s") * n_cores + lax.axis_index("c")

        def body(i, carry):
            off = pl.multiple_of(wid * per_worker + i * SC_CHUNK, 8)
            pltpu.sync_copy(rows_hbm.at[pl.ds(off, SC_CHUNK)], rows_v)
            for j in range(k):
                pltpu.sync_copy(idx_hbm.at[pl.ds(pl.multiple_of(j * n + off, 8), SC_CHUNK)], idx_v)
                pltpu.async_copy(rows_v, out_hbm.at[idx_v], sem).wait()
            return carry

        lax.fori_loop(0, per_worker // SC_CHUNK, body, 0)
        pltpu.sync_copy(zeros_hbm, rows_v)

        def pad_body(i, carry):
            off = pl.multiple_of(wid * pad_per_worker + i * SC_CHUNK, 8)
            pltpu.sync_copy(pad_hbm.at[pl.ds(off, SC_CHUNK)], idx_v)
            pltpu.async_copy(rows_v, out_hbm.at[idx_v], sem).wait()
            return carry

        lax.fori_loop(0, pad_per_worker // SC_CHUNK, pad_body, 0)

    return scatter(rows, idx_slots, pad_idx, zeros)


def _experts_kernel(be_ref, nu_ref, first_ref, next_ref, slot_ref, x_ref, w1_hbm, b1_ref, w2_hbm, b2_ref,
                    o_ref, w1_f32, w2_f32, w1_bf, w2_bf, sems):
    i = pl.program_id(0)
    used = i < nu_ref[0]

    def weight_copies(e, s):
        return (pltpu.make_async_copy(w1_hbm.at[e], w1_f32.at[s], sems.at[s]),
                pltpu.make_async_copy(w2_hbm.at[e], w2_f32.at[s], sems.at[s]))

    @pl.when(used & (first_ref[i] == 1))
    def _():
        s = slot_ref[i]

        @pl.when(i == 0)
        def _():
            for cp in weight_copies(be_ref[i], s):
                cp.start()

        for cp in weight_copies(be_ref[i], s):
            cp.wait()
        cw = 256
        for c in range(2 * D_FF // cw):
            w1_bf[:, c * cw:(c + 1) * cw] = w1_f32[s, :, c * cw:(c + 1) * cw].astype(BF16)
        for c in range(D_MODEL // cw):
            w2_bf[:, c * cw:(c + 1) * cw] = w2_f32[s, :, c * cw:(c + 1) * cw].astype(BF16)

        @pl.when(next_ref[i] >= 0)
        def _():
            for cp in weight_copies(next_ref[i], 1 - s):
                cp.start()

    @pl.when(used)
    def _():
        step = 512
        for r in range(ROW_BLOCK // EXPERT_ROWS):
            rs = slice(r * EXPERT_ROWS, (r + 1) * EXPERT_ROWS)
            x = _unpack_rows(x_ref[rs, :]).astype(BF16)
            acc = None
            for j in range(D_FF // step):
                cs = slice(j * step, (j + 1) * step)
                ls = slice(D_FF + j * step, D_FF + (j + 1) * step)
                glu = jnp.dot(x, w1_bf[:, cs], preferred_element_type=F32) + b1_ref[0, :, cs]
                lin = jnp.dot(x, w1_bf[:, ls], preferred_element_type=F32) + b1_ref[0, :, ls]
                glu = jnp.minimum(glu, SWIGLU_LIMIT)
                lin = jnp.clip(lin, -SWIGLU_LIMIT, SWIGLU_LIMIT)
                act = (glu * _sigmoid(SWIGLU_ALPHA * glu) * (lin + 1.0)).astype(BF16)
                part = jnp.dot(act, w2_bf[cs, :], preferred_element_type=F32)
                acc = part if acc is None else acc + part
            o_ref[rs, :] = _pack_rows(acc + b2_ref[0])

    @pl.when(jnp.logical_not(used))
    def _():
        o_ref[...] = jnp.zeros_like(o_ref)


def _experts(plan, xs, w1, b1, w2, b2):
    n_rows = xs.shape[0]
    nb = n_rows // ROW_BLOCK
    grid_spec = pltpu.PrefetchScalarGridSpec(
        num_scalar_prefetch=5,
        grid=(nb,),
        in_specs=[pl.BlockSpec((ROW_BLOCK, PACKED), lambda i, be, nu, *_: (jnp.minimum(i, nu[0] - 1), 0)),
                  pl.BlockSpec(memory_space=pl.ANY),
                  pl.BlockSpec((1, 1, 2 * D_FF), lambda i, be, *_: (be[i], 0, 0)),
                  pl.BlockSpec(memory_space=pl.ANY),
                  pl.BlockSpec((1, 1, D_MODEL), lambda i, be, *_: (be[i], 0, 0))],
        out_specs=pl.BlockSpec((ROW_BLOCK, PACKED), lambda i, *_: (i, 0)),
        scratch_shapes=[pltpu.VMEM((2, D_MODEL, 2 * D_FF), F32), pltpu.VMEM((2, D_FF, D_MODEL), F32),
                        pltpu.VMEM((D_MODEL, 2 * D_FF), BF16), pltpu.VMEM((D_FF, D_MODEL), BF16),
                        pltpu.SemaphoreType.DMA((2,))],
    )
    return pl.pallas_call(
        _experts_kernel,
        out_shape=jax.ShapeDtypeStruct((n_rows, PACKED), jnp.int32),
        grid_spec=grid_spec,
        compiler_params=pltpu.CompilerParams(dimension_semantics=("arbitrary",),
                                             vmem_limit_bytes=VMEM_LIMIT),
        name="experts",
    )(*plan, xs, w1, b1.reshape(N_EXPERTS, 1, 2 * D_FF), w2, b2.reshape(N_EXPERTS, 1, D_MODEL))


def _sc_gather_rows(table, idx_flat):
    n_cores, n_workers = _sc_workers()
    n_idx = idx_flat.shape[0]
    w = table.shape[1]
    per_worker = n_idx // n_workers
    n_chunks = per_worker // SC_CHUNK
    assert per_worker * n_workers == n_idx and n_chunks * SC_CHUNK == per_worker and n_chunks % 2 == 0
    mesh = plsc.VectorSubcoreMesh(core_axis_name="c", subcore_axis_name="s")

    @functools.partial(
        pl.kernel, mesh=mesh,
        out_type=jax.ShapeDtypeStruct((n_idx, w), table.dtype),
        scratch_types=[pltpu.VMEM((SC_CHUNK,), jnp.int32), pltpu.VMEM((SC_CHUNK,), jnp.int32),
                       pltpu.VMEM((SC_CHUNK, w), table.dtype), pltpu.VMEM((SC_CHUNK, w), table.dtype),
                       pltpu.SemaphoreType.DMA, pltpu.SemaphoreType.DMA],
        name="sc_gather",
    )
    def gather(table_hbm, idx_hbm, out_hbm, idx0, idx1, rows0, rows1, sem0, sem1):
        wid = lax.axis_index("s") * n_cores + lax.axis_index("c")
        bufs = ((idx0, rows0, sem0), (idx1, rows1, sem1))

        def chunk_off(c):
            return pl.multiple_of(wid * per_worker + c * SC_CHUNK, 8)

        def start(c, b):
            idx_v, rows_v, sem = bufs[b]
            pltpu.sync_copy(idx_hbm.at[pl.ds(chunk_off(c), SC_CHUNK)], idx_v)
            pltpu.async_copy(table_hbm.at[idx_v], rows_v, sem)

        def finish(c, b):
            idx_v, rows_v, sem = bufs[b]
            pltpu.make_async_copy(table_hbm.at[idx_v], rows_v, sem).wait()
            pltpu.sync_copy(rows_v, out_hbm.at[pl.ds(chunk_off(c), SC_CHUNK)])

        start(0, 0)

        def body(i, carry):
            start(2 * i + 1, 1)
            finish(2 * i, 0)

            @pl.when(i < n_chunks // 2 - 1)
            def _():
                start(2 * i + 2, 0)

            finish(2 * i + 1, 1)
            return carry

        lax.fori_loop(0, n_chunks // 2, body, 0)

    return gather(table, idx_flat)


def _finish_kernel(h_ref, pw_ref, mod_ref, nw_ref, y0_ref, y1_ref, y2_ref, y3_ref, *rest):
    o_ref = rest[-1]
    pw = pw_ref[...]
    moe = pw[:, 0:1] * _unpack_rows(y0_ref[0])
    for j, y_ref in enumerate((y1_ref, y2_ref, y3_ref), start=1):
        moe = moe + pw[:, j:j + 1] * _unpack_rows(y_ref[0])
    gate_f = mod_ref[0, 5:6, :]
    o_ref[...] = _rms(h_ref[...] + gate_f * moe, nw_ref[...])


def _finish(h1, pw, mod, norm_w, y4, seq, tc, tile0, prev_out):
    n = pw.shape[0]
    per_b = seq // tc
    slot = lambda j: pl.BlockSpec((1, tc, PACKED), lambda i: (j, i, 0))
    in_specs = [pl.BlockSpec((tc, D_MODEL), lambda i: (tile0 + i, 0)),
                pl.BlockSpec((tc, LANES), lambda i: (i, 0)),
                pl.BlockSpec((1, 6, D_MODEL), lambda i: ((tile0 + i) // per_b, 0, 0)),
                pl.BlockSpec((1, D_MODEL), lambda i: (0, 0)),
                slot(0), slot(1), slot(2), slot(3)]
    args = [h1, pw, mod, norm_w.reshape(1, D_MODEL), y4, y4, y4, y4]
    aliases = {}
    if prev_out is not None:
        in_specs.append(pl.BlockSpec(memory_space=pl.ANY))
        args.append(prev_out)
        aliases = {len(args) - 1: 0}
    return pl.pallas_call(
        _finish_kernel,
        out_shape=jax.ShapeDtypeStruct(h1.shape, F32),
        grid=(n // tc,),
        in_specs=in_specs,
        out_specs=pl.BlockSpec((tc, D_MODEL), lambda i: (tile0 + i, 0)),
        input_output_aliases=aliases,
        compiler_params=pltpu.CompilerParams(dimension_semantics=("arbitrary",),
                                             vmem_limit_bytes=VMEM_LIMIT),
        name="finish",
    )(*args)


def _moe_plan(counts, n_assign):
    cnt = counts[:, 0, :N_EXPERTS]
    sizes = jnp.sum(cnt, axis=0)
    padded = (sizes + ROW_BLOCK - 1) // ROW_BLOCK * ROW_BLOCK
    pad_end = jnp.cumsum(padded)
    pad_start = pad_end - padded
    tile_base = pad_start[None, :] + jnp.cumsum(cnt, axis=0) - cnt
    digits = jnp.stack([tile_base % 256, (tile_base // 256) % 256, tile_base // 65536], axis=1)
    rows = jnp.tile(digits, (1, 1, TOP_K))
    slot_of_row = jnp.arange(LANES, dtype=jnp.int32) // N_EXPERTS
    col = jnp.arange(LANES, dtype=jnp.int32)
    bt = jnp.where(slot_of_row[:, None] == col[None, :], rows[..., None], 0).astype(BF16)
    nb = n_assign // ROW_BLOCK + N_EXPERTS
    block_start = jnp.arange(nb, dtype=jnp.int32) * ROW_BLOCK
    block_e = jnp.minimum(jnp.sum(pad_end[None, :] <= block_start[:, None], axis=1),
                          N_EXPERTS - 1).astype(jnp.int32)
    n_used = (pad_end[-1] // ROW_BLOCK).astype(jnp.int32).reshape(1)
    nonempty = padded > 0
    ids = jnp.arange(N_EXPERTS, dtype=jnp.int32)
    later = nonempty[None, :] & (ids[None, :] > ids[:, None])
    next_nonempty = jnp.where(jnp.any(later, axis=1), jnp.argmax(later, axis=1), -1).astype(jnp.int32)
    slot_of = ((jnp.cumsum(nonempty) - 1) % 2).astype(jnp.int32)
    first = jnp.concatenate([jnp.ones((1,), jnp.int32),
                             (block_e[1:] != block_e[:-1]).astype(jnp.int32)])
    expert_plan = (block_e, n_used, first, next_nonempty[block_e], slot_of[block_e])
    r = jnp.arange(ROW_BLOCK, dtype=jnp.int32)[None, :]
    pad_idx = jnp.where(r < (padded - sizes)[:, None], (pad_start + sizes)[:, None] + r,
                        nb * ROW_BLOCK - 1).astype(jnp.int32).reshape(PAD_SLOTS)
    return bt, expert_plan, pad_idx, nb * ROW_BLOCK


def kernel(x, c, w_ada, b_ada, norm_mix_w, w_in, hgrn_lower_bounds, hgrn_norm_w, conv_w,
           w_hgrn_out, w_conv_out, w_mix_out, norm_ffn_w, w_router, b_router, w1, b1, w2, b2,
           norm_final_w):
    bsz, seq, d = x.shape
    assert d == D_MODEL and seq % CHUNK == 0
    n = bsz * seq
    depth = w_ada.shape[0]
    tile = min(512, seq)
    assert seq % tile == 0 and n % (MOE_GROUPS * tile) == 0 and (n // MOE_GROUPS * TOP_K) % ROW_BLOCK == 0
    h = x.reshape(n, d)
    wr_pad = jnp.zeros((depth, D_MODEL, LANES), BF16).at[:, :, :N_EXPERTS].set(w_router.astype(BF16))
    br_pad = jnp.zeros((depth, 1, LANES), F32).at[:, 0, :N_EXPERTS].set(b_router)
    for layer in range(depth):
        mod = _ada(c, w_ada[layer], b_ada[layer]).reshape(bsz, 6, d)
        proj = _inproj(h, mod, norm_mix_w[layer], w_in[layer].astype(BF16), seq, tile)
        h = _mix(proj, h, mod, hgrn_lower_bounds, hgrn_norm_w[layer], conv_w[layer],
                 w_hgrn_out[layer].astype(BF16), w_conv_out[layer].astype(BF16),
                 w_mix_out[layer].astype(BF16), layer, bsz, seq, tile)
        assert layer == depth - 1, "only the last layer applies the final norm"
        ng = n // MOE_GROUPS
        out = None
        for grp in range(MOE_GROUPS):
            tile0 = grp * (ng // tile)
            u2, oh4, rk, pw, counts = _route(h, mod, norm_ffn_w[layer], wr_pad[layer], br_pad[layer],
                                             seq, tile, tile0, ng)
            bt, expert_plan, pad_idx, n_rows = _moe_plan(counts, ng * TOP_K)
            dest_slots = _dest(oh4, rk, bt, tile)[:, :TOP_K].T.reshape(TOP_K * ng)
            xs = _sc_scatter_rows(u2, dest_slots, pad_idx, n_rows)
            ys = _experts(expert_plan, xs, w1[layer], b1[layer], w2[layer], b2[layer])
            y4 = _sc_gather_rows(ys, dest_slots).reshape(TOP_K, ng, PACKED)
            out = _finish(h, pw, mod, norm_final_w, y4, seq, tile, tile0, out)
        h = out
    return h.reshape(bsz, seq, d)
```

```python
import functools

import jax
import jax.numpy as jnp
from jax import lax
from jax.experimental import pallas as pl
from jax.experimental.pallas import tpu as pltpu
from jax.experimental.pallas import tpu_sc as plsc

F32 = jnp.float32
BF16 = jnp.bfloat16

D_MODEL = 1024
HGRN_HEADS = 4
HEAD_DIM = 128
HGRN_WIDTH = HGRN_HEADS * HEAD_DIM
CONV_WIDTH = 512
CONV_K = 3
CHUNK = 64
N_EXPERTS = 32
TOP_K = 4
D_FF = 1024
SWIGLU_LIMIT = 7.0
SWIGLU_ALPHA = 1.702
EPS = 1e-6
IN_COLS = 4 * HGRN_WIDTH + 3 * CONV_WIDTH + 2 * D_MODEL
LANES = 128
ROW_BLOCK = 512
EXPERT_ROWS = 256
PACKED = D_MODEL // 2
MOE_GROUPS = 2
SC_CHUNK = 64
PAD_SLOTS = N_EXPERTS * ROW_BLOCK
VMEM_LIMIT = 56 * 1024 * 1024


def _sigmoid(x):
    return 1.0 / (1.0 + jnp.exp(-x))


def _rms(x, w):
    ms = jnp.mean(x * x, axis=-1, keepdims=True)
    return x * lax.rsqrt(ms + EPS) * w


def _pack_rows(x):
    w = x.shape[1] // 2
    lo = lax.bitcast_convert_type(x[:, :w].astype(BF16).astype(F32), jnp.uint32)
    hi = lax.bitcast_convert_type(x[:, w:].astype(BF16).astype(F32), jnp.uint32)
    return lax.bitcast_convert_type((lo >> 16) | (hi & jnp.uint32(0xFFFF0000)), jnp.int32)


def _unpack_rows(p):
    u = lax.bitcast_convert_type(p, jnp.uint32)
    lo = lax.bitcast_convert_type(u << 16, F32)
    hi = lax.bitcast_convert_type(u & jnp.uint32(0xFFFF0000), F32)
    return jnp.concatenate([lo, hi], axis=1)


def _nt_dot(a, b):
    return lax.dot_general(a, b, (((1,), (1,)), ((), ())), preferred_element_type=F32)


def _tn_dot(a, b):
    return lax.dot_general(a, b, (((0,), (0,)), ((), ())), preferred_element_type=F32)


def _ada_kernel(c_ref, w_ref, b_ref, o_ref):
    c = c_ref[...]
    sc = (c * _sigmoid(c)).astype(BF16)
    o_ref[...] = jnp.dot(sc, w_ref[...].astype(BF16), preferred_element_type=F32) + b_ref[...]


def _ada(c, w_ada, b_ada):
    bsz, d = c.shape
    n = w_ada.shape[1]
    return pl.pallas_call(
        _ada_kernel,
        out_shape=jax.ShapeDtypeStruct((bsz, n), F32),
        grid=(n // d,),
        in_specs=[pl.BlockSpec((bsz, d), lambda j: (0, 0)),
                  pl.BlockSpec((d, d), lambda j: (0, j)),
                  pl.BlockSpec((1, d), lambda j: (0, j))],
        out_specs=pl.BlockSpec((bsz, d), lambda j: (0, j)),
        name="ada",
    )(c, w_ada, b_ada.reshape(1, n))


def _inproj_kernel(x_ref, mod_ref, nw_ref, w_ref, o_ref, u_scr):
    y = _rms(x_ref[...], nw_ref[...])
    shift = mod_ref[0, 0:1, :]
    scale = mod_ref[0, 1:2, :]
    u_scr[...] = (y * (1.0 + scale) + shift).astype(BF16)
    step = 512
    for j in range(IN_COLS // step):
        o_ref[:, j * step:(j + 1) * step] = jnp.dot(
            u_scr[...], w_ref[:, j * step:(j + 1) * step], preferred_element_type=F32).astype(BF16)


def _inproj(x2, mod, norm_w, w_in_bf, seq, tm):
    n = x2.shape[0]
    per_b = seq // tm
    return pl.pallas_call(
        _inproj_kernel,
        out_shape=jax.ShapeDtypeStruct((n, IN_COLS), BF16),
        grid=(n // tm,),
        in_specs=[pl.BlockSpec((tm, D_MODEL), lambda i: (i, 0)),
                  pl.BlockSpec((1, 6, D_MODEL), lambda i: (i // per_b, 0, 0)),
                  pl.BlockSpec((1, D_MODEL), lambda i: (0, 0)),
                  pl.BlockSpec((D_MODEL, IN_COLS), lambda i: (0, 0))],
        out_specs=pl.BlockSpec((tm, IN_COLS), lambda i: (i, 0)),
        scratch_shapes=[pltpu.VMEM((tm, D_MODEL), BF16)],
        compiler_params=pltpu.CompilerParams(dimension_semantics=("arbitrary",),
                                             vmem_limit_bytes=VMEM_LIMIT),
        name="inproj",
    )(x2, mod, norm_w.reshape(1, D_MODEL), w_in_bf)


_HEADS = [slice(h * HEAD_DIM, (h + 1) * HEAD_DIM) for h in range(HGRN_HEADS)]


def _chunk_rows(c):
    if isinstance(c, int):
        return pl.ds(c * CHUNK, CHUNK)
    return pl.ds(pl.multiple_of(c * CHUNK, CHUNK), CHUNK)


def _hgrn_gates(c, lb, tri, f_ref, b_scr, k_scr, f_scr):
    fx = f_ref[_chunk_rows(c), :].astype(F32)
    f = lb + (1.0 - lb) * _sigmoid(fx)
    g = jnp.log(f)
    g1 = g.astype(BF16)
    r1 = g - g1.astype(F32)
    g2 = r1.astype(BF16)
    g3 = (r1 - g2.astype(F32)).astype(BF16)
    b_scr[...] = (jnp.dot(tri, g1, preferred_element_type=F32)
                  + jnp.dot(tri, g2, preferred_element_type=F32)
                  + jnp.dot(tri, g3, preferred_element_type=F32))
    f_scr[...] = f
    k_scr[...] = 1.0 - f


def _hgrn_scores(c, q_ref, b_scr, k_scr, f_scr):
    rows = _chunk_rows(c)
    qs = [q_ref[rows, hs].astype(F32) for hs in _HEADS]
    s_mats = [_level_scores(1, qs[h], _HEADS[h], b_scr, k_scr, f_scr) for h in range(HGRN_HEADS)]
    for lvl in range(2, 7):
        for h in range(HGRN_HEADS):
            s_mats[h] = s_mats[h] + _level_scores(lvl, qs[h], _HEADS[h], b_scr, k_scr, f_scr)
    return s_mats


def _hgrn_outputs(c, s_mats, q_ref, v_ref, st_scr, o_scr, b_scr, k_scr):
    rows = _chunk_rows(c)
    for h, hs in enumerate(_HEADS):
        q = q_ref[rows, hs].astype(F32)
        v_bf = v_ref[rows, hs]
        b = b_scr[:, hs]
        kk = k_scr[:, hs]
        b_last = b_scr[CHUNK - 1:CHUNK, hs]
        st = st_scr[h]
        qd = (q * jnp.exp(b)).astype(BF16)
        kdec = (kk * jnp.exp(b_last - b)).astype(BF16)
        diag = jnp.sum(q * kk, axis=-1, keepdims=True)
        o_scr[rows, hs] = (_nt_dot(qd, st.astype(BF16))
                           + jnp.dot(s_mats[h].astype(BF16), v_bf, preferred_element_type=F32)
                           + diag * v_bf.astype(F32))
        st_scr[h] = jnp.exp(b_last) * st + _tn_dot(v_bf, kdec)


def _level_scores(lvl, q, hs, b_scr, k_scr, f_scr):
    row = lax.broadcasted_iota(jnp.int32, (CHUNK, 1), 0)
    col = lax.broadcasted_iota(jnp.int32, (1, CHUNK), 1)
    b = b_scr[:, hs]
    kk = k_scr[:, hs]
    blk = 1 << lvl
    half = blk // 2
    if lvl == 1:
        odd = (row & 1) == 1
        qx = jnp.where(odd, q * f_scr[:, hs], 0.0).astype(BF16)
        kx = jnp.where(odd, 0.0, kk).astype(BF16)
        return jnp.where((row >> 1) == (col >> 1), _nt_dot(qx, kx), 0.0)
    if half < 8:
        groups = []
        sub = lax.broadcasted_iota(jnp.int32, (8, 1), 0)
        for j in range(CHUNK // 8):
            rj = None
            for k in reversed(range(8 // blk)):
                m = 8 * j + k * blk + half - 1
                bm = jnp.broadcast_to(b_scr[m:m + 1, hs], (8, HEAD_DIM))
                rj = bm if rj is None else jnp.where(sub < (k + 1) * blk, bm, rj)
            groups.append(rj)
        ref = jnp.concatenate(groups, axis=0)
        second = (row & (blk - 1)) >= half
        qx = (q * jnp.exp(jnp.where(second, b - ref, -jnp.inf))).astype(BF16)
        kx = (kk * jnp.exp(jnp.where(second, -jnp.inf, ref - b))).astype(BF16)
        return jnp.where((row >> lvl) == (col >> lvl), _nt_dot(qx, kx), 0.0)
    n_blk = CHUNK // blk
    qparts, kparts = [], []
    for j in range(n_blk):
        m = j * blk + half - 1
        bm = b_scr[m:m + 1, hs]
        tq = slice(j * blk + half, (j + 1) * blk)
        tk = slice(j * blk, j * blk + half)
        qparts.append(q[tq] * jnp.exp(b[tq] - bm))
        kparts.append(kk[tk] * jnp.exp(bm - b[tk]))
        kparts.append(jnp.zeros((half, HEAD_DIM), F32))
    qx = jnp.concatenate(qparts, axis=0).astype(BF16)
    kx = jnp.concatenate(kparts, axis=0).astype(BF16)
    sc = _nt_dot(qx, kx)
    if n_blk > 1:
        crow = lax.broadcasted_iota(jnp.int32, (CHUNK // 2, 1), 0)
        sc = jnp.where((crow // half) == (col >> lvl), sc, 0.0)
    pieces = []
    for j in range(n_blk):
        pieces.append(jnp.zeros((half, CHUNK), F32))
        pieces.append(sc[j * half:(j + 1) * half])
    return jnp.concatenate(pieces, axis=0)


def _mix_kernel(lbt_ref, q_ref, f_ref, i_ref, g_ref, cb_ref, cc_ref, ch_ref, ga0_ref, ga1_ref,
                gb0_ref, gb1_ref, x_ref, mod_ref, hnw_ref, cw_ref, wa_ref, wb_ref, wm_ref,
                o_ref, st_scr, o_scr, carry_scr, b_scr, k_scr, f_scr, *, layer, rows_per_step, epi_rows):
    @pl.when(pl.program_id(1) == 0)
    def _():
        st_scr[...] = jnp.zeros_like(st_scr)
        carry_scr[...] = jnp.zeros_like(carry_scr)

    tab = lbt_ref[...]
    tmax = jnp.max(tab, axis=0, keepdims=True)
    te = jnp.exp(tab - tmax)
    lb = jnp.sum(te[0:layer + 1], axis=0, keepdims=True) / jnp.sum(te, axis=0, keepdims=True)

    ri = lax.broadcasted_iota(jnp.int32, (CHUNK, CHUNK), 0)
    ci = lax.broadcasted_iota(jnp.int32, (CHUNK, CHUNK), 1)
    tri = jnp.where(ci <= ri, 1.0, 0.0).astype(BF16)

    n_chunks = rows_per_step // CHUNK

    def slot(s):
        return b_scr.at[s], k_scr.at[s], f_scr.at[s]

    def chunk(c, c_next, s):
        bs, ks, fs = slot(s)
        s_mats = _hgrn_scores(c, q_ref, bs, ks, fs)
        _hgrn_gates(c_next, lb, tri, f_ref, *slot(1 - s))
        _hgrn_outputs(c, s_mats, q_ref, i_ref, st_scr, o_scr, bs, ks)

    def pair_body(i, carry):
        chunk(2 * i, 2 * i + 1, 0)
        chunk(2 * i + 1, jnp.minimum(2 * i + 2, n_chunks - 1), 1)
        return carry

    _hgrn_gates(0, lb, tri, f_ref, *slot(0))
    lax.fori_loop(0, n_chunks // 2, pair_body, 0)

    gate_m = mod_ref[0, 2:3, :]
    hnw = hnw_ref[...]
    cw0 = cw_ref[0:1, :]
    cw1 = cw_ref[1:2, :]
    cw2 = cw_ref[2:3, :]
    for r in range(rows_per_step // epi_rows):
        rs = slice(r * epi_rows, (r + 1) * epi_rows)
        parts = []
        for h in range(HGRN_HEADS):
            hs = slice(h * HEAD_DIM, (h + 1) * HEAD_DIM)
            oh = _rms(o_scr[rs, hs], hnw)
            go = g_ref[rs, hs].astype(F32)
            parts.append((oh * (go * _sigmoid(go))).astype(BF16))
        ya = jnp.dot(jnp.concatenate(parts, axis=1), wa_ref[...], preferred_element_type=F32)
        uc = cc_ref[rs, :].astype(F32) * ch_ref[rs, :].astype(F32)
        prev = carry_scr[...]
        rowi = lax.broadcasted_iota(jnp.int32, (epi_rows, 1), 0)
        s1 = jnp.where(rowi == 0, prev[7:8, :], pltpu.roll(uc, 1, 0))
        s2 = pltpu.roll(uc, 2, 0)
        s2 = jnp.where(rowi == 0, prev[6:7, :], jnp.where(rowi == 1, prev[7:8, :], s2))
        carry_scr[...] = uc[epi_rows - 8:epi_rows, :]
        yc = cb_ref[rs, :].astype(F32) * (cw2 * uc + cw1 * s1 + cw0 * s2)
        yb = jnp.dot(yc.astype(BF16), wb_ref[...], preferred_element_type=F32)
        ga = jnp.concatenate([ga0_ref[rs, :], ga1_ref[rs, :]], axis=1).astype(F32)
        gb = jnp.concatenate([gb0_ref[rs, :], gb1_ref[rs, :]], axis=1).astype(F32)
        merged = (_sigmoid(ga) * ya + _sigmoid(gb) * yb).astype(BF16)
        o_ref[rs, :] = x_ref[rs, :] + gate_m * jnp.dot(merged, wm_ref[...], preferred_element_type=F32)


def _mix(proj, x2, mod, lb_table, hgrn_norm_w, conv_w, wa, wb, wm, layer, bsz, seq, tt):
    n = x2.shape[0]
    per_b = seq // tt

    def col(col_block):
        return pl.BlockSpec((tt, 512), lambda b, t: (b * per_b + t, col_block))

    const = lambda shape: pl.BlockSpec(shape, lambda b, t: (0,) * len(shape))
    in_specs = [
        const(lb_table.shape),
        col(0), col(1), col(2), col(3),
        col(4), col(5), col(6),
        col(7), col(8), col(9), col(10),
        pl.BlockSpec((tt, D_MODEL), lambda b, t: (b * per_b + t, 0)),
        pl.BlockSpec((1, 6, D_MODEL), lambda b, t: (b, 0, 0)),
        const((1, HEAD_DIM)), const((CONV_K, CONV_WIDTH)),
        const((HGRN_WIDTH, D_MODEL)), const((CONV_WIDTH, D_MODEL)), const((D_MODEL, D_MODEL)),
    ]
    kern = functools.partial(_mix_kernel, layer=layer, rows_per_step=tt, epi_rows=min(tt, 256))
    return pl.pallas_call(
        kern,
        out_shape=jax.ShapeDtypeStruct((n, D_MODEL), F32),
        grid=(bsz, per_b),
        in_specs=in_specs,
        out_specs=pl.BlockSpec((tt, D_MODEL), lambda b, t: (b * per_b + t, 0)),
        scratch_shapes=[pltpu.VMEM((HGRN_HEADS, HEAD_DIM, HEAD_DIM), F32),
                        pltpu.VMEM((tt, HGRN_WIDTH), F32),
                        pltpu.VMEM((8, CONV_WIDTH), F32),
                        pltpu.VMEM((2, CHUNK, HGRN_WIDTH), F32),
                        pltpu.VMEM((2, CHUNK, HGRN_WIDTH), F32),
                        pltpu.VMEM((2, CHUNK, HGRN_WIDTH), F32)],
        compiler_params=pltpu.CompilerParams(dimension_semantics=("arbitrary", "arbitrary"),
                                             vmem_limit_bytes=VMEM_LIMIT),
        name="mix",
    )(lb_table, *([proj] * 11),
      x2, mod, hgrn_norm_w.reshape(1, HEAD_DIM), conv_w, wa, wb, wm)


def _route_kernel(h_ref, mod_ref, nw_ref, wr_ref, br_ref, u_ref, oh4_ref, rk_ref, pw_ref, cnt_ref, *, tr):
    shift = mod_ref[0, 3:4, :]
    scale = mod_ref[0, 4:5, :]
    u = _rms(h_ref[...], nw_ref[...]) * (1.0 + scale) + shift
    u_ref[...] = _pack_rows(u)
    lane = lax.broadcasted_iota(jnp.int32, (tr, LANES), 1)
    logits = jnp.dot(u.astype(BF16), wr_ref[...], preferred_element_type=F32) + br_ref[...]
    logits = jnp.where(lane < N_EXPERTS, logits, -jnp.inf)
    idx, val = [], []
    cur = logits
    for _ in range(TOP_K):
        m = jnp.max(cur, axis=-1, keepdims=True)
        i = jnp.min(jnp.where(cur == m, lane, LANES), axis=-1, keepdims=True)
        idx.append(i)
        val.append(m)
        cur = jnp.where(lane == i, -jnp.inf, cur)
    ex = [jnp.exp(v - val[0]) for v in val]
    den = ex[0] + ex[1] + ex[2] + ex[3]
    onehot = jnp.zeros((tr, LANES), F32)
    for i in idx:
        onehot = onehot + jnp.where(lane == i, 1.0, 0.0)
    ri = lax.broadcasted_iota(jnp.int32, (tr, tr), 0)
    ci = lax.broadcasted_iota(jnp.int32, (tr, tr), 1)
    tri = jnp.where(ci < ri, 1.0, 0.0).astype(BF16)
    pref = jnp.dot(tri, onehot.astype(BF16), preferred_element_type=F32)
    rk = jnp.zeros((tr, LANES), jnp.int32)
    pw = jnp.zeros((tr, LANES), F32)
    oh4 = jnp.zeros((tr, LANES), F32)
    for j in range(TOP_K):
        rank = jnp.sum(jnp.where(lane == idx[j], pref, 0.0), axis=-1, keepdims=True).astype(jnp.int32)
        rk = jnp.where(lane == j, rank, rk)
        pw = jnp.where(lane == j, ex[j] / den, pw)
        oh4 = oh4 + jnp.where(lane == idx[j] + N_EXPERTS * j, 1.0, 0.0)
    oh4_ref[...] = oh4.astype(BF16)
    rk_ref[...] = rk
    pw_ref[...] = pw
    cnt_ref[0] = jnp.sum(onehot, axis=0, keepdims=True).astype(jnp.int32)


def _route(h1, mod, norm_w, wr_pad, br_pad, seq, tr, tile0, n):
    per_b = seq // tr
    nt = n // tr
    return pl.pallas_call(
        functools.partial(_route_kernel, tr=tr),
        out_shape=(jax.ShapeDtypeStruct((n, PACKED), jnp.int32),
                   jax.ShapeDtypeStruct((n, LANES), BF16),
                   jax.ShapeDtypeStruct((n, LANES), jnp.int32),
                   jax.ShapeDtypeStruct((n, LANES), F32),
                   jax.ShapeDtypeStruct((nt, 1, LANES), jnp.int32)),
        grid=(nt,),
        in_specs=[pl.BlockSpec((tr, D_MODEL), lambda i: (tile0 + i, 0)),
                  pl.BlockSpec((1, 6, D_MODEL), lambda i: ((tile0 + i) // per_b, 0, 0)),
                  pl.BlockSpec((1, D_MODEL), lambda i: (0, 0)),
                  pl.BlockSpec((D_MODEL, LANES), lambda i: (0, 0)),
                  pl.BlockSpec((1, LANES), lambda i: (0, 0))],
        out_specs=(pl.BlockSpec((tr, PACKED), lambda i: (i, 0)),
                   pl.BlockSpec((tr, LANES), lambda i: (i, 0)),
                   pl.BlockSpec((tr, LANES), lambda i: (i, 0)),
                   pl.BlockSpec((tr, LANES), lambda i: (i, 0)),
                   pl.BlockSpec((1, 1, LANES), lambda i: (i, 0, 0))),
        compiler_params=pltpu.CompilerParams(dimension_semantics=("arbitrary",),
                                             vmem_limit_bytes=VMEM_LIMIT),
        name="route",
    )(h1, mod, norm_w.reshape(1, D_MODEL), wr_pad, br_pad)


def _dest_kernel(oh4_ref, rk_ref, bt_ref, o_ref):
    oh = oh4_ref[...]
    start = (jnp.dot(oh, bt_ref[0, 0], preferred_element_type=F32)
             + 256.0 * jnp.dot(oh, bt_ref[0, 1], preferred_element_type=F32)
             + 65536.0 * jnp.dot(oh, bt_ref[0, 2], preferred_element_type=F32))
    o_ref[...] = start.astype(jnp.int32) + rk_ref[...]


def _dest(oh4, rk, bt, tr):
    n = oh4.shape[0]
    return pl.pallas_call(
        _dest_kernel,
        out_shape=jax.ShapeDtypeStruct((n, LANES), jnp.int32),
        grid=(n // tr,),
        in_specs=[pl.BlockSpec((tr, LANES), lambda i: (i, 0)),
                  pl.BlockSpec((tr, LANES), lambda i: (i, 0)),
                  pl.BlockSpec((1, 3, LANES, LANES), lambda i: (i, 0, 0, 0))],
        out_specs=pl.BlockSpec((tr, LANES), lambda i: (i, 0)),
        name="dest",
    )(oh4, rk, bt)


def _sc_workers():
    info = plsc.get_sparse_core_info()
    return info.num_cores, info.num_cores * info.num_subcores


def _sc_scatter_rows(rows, idx_slots, pad_idx, n_out):
    n_cores, n_workers = _sc_workers()
    n, w = rows.shape
    k = idx_slots.shape[0] // n
    per_worker = n // n_workers
    pad_per_worker = pad_idx.shape[0] // n_workers
    assert per_worker % SC_CHUNK == 0 and pad_per_worker % SC_CHUNK == 0
    mesh = plsc.VectorSubcoreMesh(core_axis_name="c", subcore_axis_name="s")
    zeros = jnp.zeros((SC_CHUNK, w), rows.dtype)

    @functools.partial(
        pl.kernel, mesh=mesh,
        out_type=jax.ShapeDtypeStruct((n_out, w), rows.dtype),
        scratch_types=[pltpu.VMEM((SC_CHUNK,), jnp.int32),
                       pltpu.VMEM((SC_CHUNK, w), rows.dtype),
                       pltpu.SemaphoreType.DMA],
        name="sc_scatter",
    )
    def scatter(rows_hbm, idx_hbm, pad_hbm, zeros_hbm, out_hbm, idx_v, rows_v, sem):
        wid = lax.axis_index("s") * n_cores + lax.axis_index("c")

        def body(i, carry):
            off = pl.multiple_of(wid * per_worker + i * SC_CHUNK, 8)
            pltpu.sync_copy(rows_hbm.at[pl.ds(off, SC_CHUNK)], rows_v)
            for j in range(k):
                pltpu.sync_copy(idx_hbm.at[pl.ds(pl.multiple_of(j * n + off, 8), SC_CHUNK)], idx_v)
                pltpu.async_copy(rows_v, out_hbm.at[idx_v], sem).wait()
            return carry

        lax.fori_loop(0, per_worker // SC_CHUNK, body, 0)
        pltpu.sync_copy(zeros_hbm, rows_v)

        def pad_body(i, carry):
            off = pl.multiple_of(wid * pad_per_worker + i * SC_CHUNK, 8)
            pltpu.sync_copy(pad_hbm.at[pl.ds(off, SC_CHUNK)], idx_v)
            pltpu.async_copy(rows_v, out_hbm.at[idx_v], sem).wait()
            return carry

        lax.fori_loop(0, pad_per_worker // SC_CHUNK, pad_body, 0)

    return scatter(rows, idx_slots, pad_idx, zeros)


def _experts_kernel(be_ref, nu_ref, first_ref, next_ref, slot_ref, x_ref, w1_hbm, b1_ref, w2_hbm, b2_ref,
                    o_ref, w1_f32, w2_f32, w1_bf, w2_bf, sems):
    i = pl.program_id(0)
    used = i < nu_ref[0]

    def weight_copies(e, s):
        return (pltpu.make_async_copy(w1_hbm.at[e], w1_f32.at[s], sems.at[s]),
                pltpu.make_async_copy(w2_hbm.at[e], w2_f32.at[s], sems.at[s]))

    @pl.when(used & (first_ref[i] == 1))
    def _():
        s = slot_ref[i]

        @pl.when(i == 0)
        def _():
            for cp in weight_copies(be_ref[i], s):
                cp.start()

        for cp in weight_copies(be_ref[i], s):
            cp.wait()
        cw = 256
        for c in range(2 * D_FF // cw):
            w1_bf[:, c * cw:(c + 1) * cw] = w1_f32[s, :, c * cw:(c + 1) * cw].astype(BF16)
        for c in range(D_MODEL // cw):
            w2_bf[:, c * cw:(c + 1) * cw] = w2_f32[s, :, c * cw:(c + 1) * cw].astype(BF16)

        @pl.when(next_ref[i] >= 0)
        def _():
            for cp in weight_copies(next_ref[i], 1 - s):
                cp.start()

    @pl.when(used)
    def _():
        step = 512
        for r in range(ROW_BLOCK // EXPERT_ROWS):
            rs = slice(r * EXPERT_ROWS, (r + 1) * EXPERT_ROWS)
            x = _unpack_rows(x_ref[rs, :]).astype(BF16)
            acc = None
            for j in range(D_FF // step):
                cs = slice(j * step, (j + 1) * step)
                ls = slice(D_FF + j * step, D_FF + (j + 1) * step)
                glu = jnp.dot(x, w1_bf[:, cs], preferred_element_type=F32) + b1_ref[0, :, cs]
                lin = jnp.dot(x, w1_bf[:, ls], preferred_element_type=F32) + b1_ref[0, :, ls]
                glu = jnp.minimum(glu, SWIGLU_LIMIT)
                lin = jnp.clip(lin, -SWIGLU_LIMIT, SWIGLU_LIMIT)
                act = (glu * _sigmoid(SWIGLU_ALPHA * glu) * (lin + 1.0)).astype(BF16)
                part = jnp.dot(act, w2_bf[cs, :], preferred_element_type=F32)
                acc = part if acc is None else acc + part
            o_ref[rs, :] = _pack_rows(acc + b2_ref[0])

    @pl.when(jnp.logical_not(used))
    def _():
        o_ref[...] = jnp.zeros_like(o_ref)


def _experts(plan, xs, w1, b1, w2, b2):
    n_rows = xs.shape[0]
    nb = n_rows // ROW_BLOCK
    grid_spec = pltpu.PrefetchScalarGridSpec(
        num_scalar_prefetch=5,
        grid=(nb,),
        in_specs=[pl.BlockSpec((ROW_BLOCK, PACKED), lambda i, be, nu, *_: (jnp.minimum(i, nu[0] - 1), 0)),
                  pl.BlockSpec(memory_space=pl.ANY),
                  pl.BlockSpec((1, 1, 2 * D_FF), lambda i, be, *_: (be[i], 0, 0)),
                  pl.BlockSpec(memory_space=pl.ANY),
                  pl.BlockSpec((1, 1, D_MODEL), lambda i, be, *_: (be[i], 0, 0))],
        out_specs=pl.BlockSpec((ROW_BLOCK, PACKED), lambda i, *_: (i, 0)),
        scratch_shapes=[pltpu.VMEM((2, D_MODEL, 2 * D_FF), F32), pltpu.VMEM((2, D_FF, D_MODEL), F32),
                        pltpu.VMEM((D_MODEL, 2 * D_FF), BF16), pltpu.VMEM((D_FF, D_MODEL), BF16),
                        pltpu.SemaphoreType.DMA((2,))],
    )
    return pl.pallas_call(
        _experts_kernel,
        out_shape=jax.ShapeDtypeStruct((n_rows, PACKED), jnp.int32),
        grid_spec=grid_spec,
        compiler_params=pltpu.CompilerParams(dimension_semantics=("arbitrary",),
                                             vmem_limit_bytes=VMEM_LIMIT),
        name="experts",
    )(*plan, xs, w1, b1.reshape(N_EXPERTS, 1, 2 * D_FF), w2, b2.reshape(N_EXPERTS, 1, D_MODEL))


def _sc_gather_rows(table, idx_flat):
    n_cores, n_workers = _sc_workers()
    n_idx = idx_flat.shape[0]
    w = table.shape[1]
    per_worker = n_idx // n_workers
    n_chunks = per_worker // SC_CHUNK
    assert per_worker * n_workers == n_idx and n_chunks * SC_CHUNK == per_worker and n_chunks % 2 == 0
    mesh = plsc.VectorSubcoreMesh(core_axis_name="c", subcore_axis_name="s")

    @functools.partial(
        pl.kernel, mesh=mesh,
        out_type=jax.ShapeDtypeStruct((n_idx, w), table.dtype),
        scratch_types=[pltpu.VMEM((SC_CHUNK,), jnp.int32), pltpu.VMEM((SC_CHUNK,), jnp.int32),
                       pltpu.VMEM((SC_CHUNK, w), table.dtype), pltpu.VMEM((SC_CHUNK, w), table.dtype),
                       pltpu.SemaphoreType.DMA, pltpu.SemaphoreType.DMA],
        name="sc_gather",
    )
    def gather(table_hbm, idx_hbm, out_hbm, idx0, idx1, rows0, rows1, sem0, sem1):
        wid = lax.axis_index("s") * n_cores + lax.axis_index("c")
        bufs = ((idx0, rows0, sem0), (idx1, rows1, sem1))

        def chunk_off(c):
            return pl.multiple_of(wid * per_worker + c * SC_CHUNK, 8)

        def start(c, b):
            idx_v, rows_v, sem = bufs[b]
            pltpu.sync_copy(idx_hbm.at[pl.ds(chunk_off(c), SC_CHUNK)], idx_v)
            pltpu.async_copy(table_hbm.at[idx_v], rows_v, sem)

        def finish(c, b):
            idx_v, rows_v, sem = bufs[b]
            pltpu.make_async_copy(table_hbm.at[idx_v], rows_v, sem).wait()
            pltpu.sync_copy(rows_v, out_hbm.at[pl.ds(chunk_off(c), SC_CHUNK)])

        start(0, 0)

        def body(i, carry):
            start(2 * i + 1, 1)
            finish(2 * i, 0)

            @pl.when(i < n_chunks // 2 - 1)
            def _():
                start(2 * i + 2, 0)

            finish(2 * i + 1, 1)
            return carry

        lax.fori_loop(0, n_chunks // 2, body, 0)

    return gather(table, idx_flat)


def _finish_kernel(h_ref, pw_ref, mod_ref, nw_ref, y0_ref, y1_ref, y2_ref, y3_ref, *rest):
    o_ref = rest[-1]
    pw = pw_ref[...]
    moe = pw[:, 0:1] * _unpack_rows(y0_ref[0])
    for j, y_ref in enumerate((y1_ref, y2_ref, y3_ref), start=1):
        moe = moe + pw[:, j:j + 1] * _unpack_rows(y_ref[0])
    gate_f = mod_ref[0, 5:6, :]
    o_ref[...] = _rms(h_ref[...] + gate_f * moe, nw_ref[...])


def _finish(h1, pw, mod, norm_w, y4, seq, tc, tile0, prev_out):
    n = pw.shape[0]
    per_b = seq // tc
    slot = lambda j: pl.BlockSpec((1, tc, PACKED), lambda i: (j, i, 0))
    in_specs = [pl.BlockSpec((tc, D_MODEL), lambda i: (tile0 + i, 0)),
                pl.BlockSpec((tc, LANES), lambda i: (i, 0)),
                pl.BlockSpec((1, 6, D_MODEL), lambda i: ((tile0 + i) // per_b, 0, 0)),
                pl.BlockSpec((1, D_MODEL), lambda i: (0, 0)),
                slot(0), slot(1), slot(2), slot(3)]
    args = [h1, pw, mod, norm_w.reshape(1, D_MODEL), y4, y4, y4, y4]
    aliases = {}
    if prev_out is not None:
        in_specs.append(pl.BlockSpec(memory_space=pl.ANY))
        args.append(prev_out)
        aliases = {len(args) - 1: 0}
    return pl.pallas_call(
        _finish_kernel,
        out_shape=jax.ShapeDtypeStruct(h1.shape, F32),
        grid=(n // tc,),
        in_specs=in_specs,
        out_specs=pl.BlockSpec((tc, D_MODEL), lambda i: (tile0 + i, 0)),
        input_output_aliases=aliases,
        compiler_params=pltpu.CompilerParams(dimension_semantics=("arbitrary",),
                                             vmem_limit_bytes=VMEM_LIMIT),
        name="finish",
    )(*args)


def _moe_plan(counts, n_assign):
    cnt = counts[:, 0, :N_EXPERTS]
    sizes = jnp.sum(cnt, axis=0)
    padded = (sizes + ROW_BLOCK - 1) // ROW_BLOCK * ROW_BLOCK
    pad_end = jnp.cumsum(padded)
    pad_start = pad_end - padded
    tile_base = pad_start[None, :] + jnp.cumsum(cnt, axis=0) - cnt
    digits = jnp.stack([tile_base % 256, (tile_base // 256) % 256, tile_base // 65536], axis=1)
    rows = jnp.tile(digits, (1, 1, TOP_K))
    slot_of_row = jnp.arange(LANES, dtype=jnp.int32) // N_EXPERTS
    col = jnp.arange(LANES, dtype=jnp.int32)
    bt = jnp.where(slot_of_row[:, None] == col[None, :], rows[..., None], 0).astype(BF16)
    nb = n_assign // ROW_BLOCK + N_EXPERTS
    block_start = jnp.arange(nb, dtype=jnp.int32) * ROW_BLOCK
    block_e = jnp.minimum(jnp.sum(pad_end[None, :] <= block_start[:, None], axis=1),
                          N_EXPERTS - 1).astype(jnp.int32)
    n_used = (pad_end[-1] // ROW_BLOCK).astype(jnp.int32).reshape(1)
    nonempty = padded > 0
    ids = jnp.arange(N_EXPERTS, dtype=jnp.int32)
    suffix_min = lax.cummin(jnp.where(nonempty, ids, N_EXPERTS), reverse=True)
    after = jnp.concatenate([suffix_min[1:], jnp.full((1,), N_EXPERTS, jnp.int32)])
    next_nonempty = jnp.where(after < N_EXPERTS, after, -1).astype(jnp.int32)
    slot_of = ((jnp.cumsum(nonempty) - 1) % 2).astype(jnp.int32)
    first = jnp.concatenate([jnp.ones((1,), jnp.int32),
                             (block_e[1:] != block_e[:-1]).astype(jnp.int32)])
    expert_plan = (block_e, n_used, first, next_nonempty[block_e], slot_of[block_e])
    r = jnp.arange(ROW_BLOCK, dtype=jnp.int32)[None, :]
    pad_idx = jnp.where(r < (padded - sizes)[:, None], (pad_start + sizes)[:, None] + r,
                        (nb - 1) * ROW_BLOCK + r).astype(jnp.int32).reshape(PAD_SLOTS)
    return bt, expert_plan, pad_idx, nb * ROW_BLOCK


def kernel(x, c, w_ada, b_ada, norm_mix_w, w_in, hgrn_lower_bounds, hgrn_norm_w, conv_w,
           w_hgrn_out, w_conv_out, w_mix_out, norm_ffn_w, w_router, b_router, w1, b1, w2, b2,
           norm_final_w):
    bsz, seq, d = x.shape
    assert d == D_MODEL and seq % CHUNK == 0
    n = bsz * seq
    depth = w_ada.shape[0]
    tile = min(512, seq)
    assert seq % tile == 0 and n % (MOE_GROUPS * tile) == 0 and (n // MOE_GROUPS * TOP_K) % ROW_BLOCK == 0
    h = x.reshape(n, d)
    wr_pad = jnp.zeros((depth, D_MODEL, LANES), BF16).at[:, :, :N_EXPERTS].set(w_router.astype(BF16))
    br_pad = jnp.zeros((depth, 1, LANES), F32).at[:, 0, :N_EXPERTS].set(b_router)
    for layer in range(depth):
        mod = _ada(c, w_ada[layer], b_ada[layer]).reshape(bsz, 6, d)
        proj = _inproj(h, mod, norm_mix_w[layer], w_in[layer].astype(BF16), seq, tile)
        h = _mix(proj, h, mod, hgrn_lower_bounds, hgrn_norm_w[layer], conv_w[layer],
                 w_hgrn_out[layer].astype(BF16), w_conv_out[layer].astype(BF16),
                 w_mix_out[layer].astype(BF16), layer, bsz, seq, tile)
        assert layer == depth - 1, "only the last layer applies the final norm"
        ng = n // MOE_GROUPS
        out = None
        for grp in range(MOE_GROUPS):
            tile0 = grp * (ng // tile)
            u2, oh4, rk, pw, counts = _route(h, mod, norm_ffn_w[layer], wr_pad[layer], br_pad[layer],
                                             seq, tile, tile0, ng)
            bt, expert_plan, pad_idx, n_rows = _moe_plan(counts, ng * TOP_K)
            dest_slots = _dest(oh4, rk, bt, tile)[:, :TOP_K].T.reshape(TOP_K * ng)
            xs = _sc_scatter_rows(u2, dest_slots, pad_idx, n_rows)
            ys = _experts(expert_plan, xs, w1[layer], b1[layer], w2[layer], b2[layer])
            y4 = _sc_gather_rows(ys, dest_slots).reshape(TOP_K, ng, PACKED)
            out = _finish(h, pw, mod, norm_final_w, y4, seq, tile, tile0, out)
        h = out
    return h.reshape(bsz, seq, d)
```

```python
import functools

import jax
import jax.numpy as jnp
from jax import lax
from jax.experimental import pallas as pl
from jax.experimental.pallas import tpu as pltpu
from jax.experimental.pallas import tpu_sc as plsc

F32 = jnp.float32
BF16 = jnp.bfloat16

D_MODEL = 1024
HGRN_HEADS = 4
HEAD_DIM = 128
HGRN_WIDTH = HGRN_HEADS * HEAD_DIM
CONV_WIDTH = 512
CONV_K = 3
CHUNK = 64
N_EXPERTS = 32
TOP_K = 4
D_FF = 1024
SWIGLU_LIMIT = 7.0
SWIGLU_ALPHA = 1.702
EPS = 1e-6
IN_COLS = 4 * HGRN_WIDTH + 3 * CONV_WIDTH + 2 * D_MODEL
LANES = 128
ROW_BLOCK = 512
EXPERT_ROWS = 512
PACKED = D_MODEL // 2
MOE_GROUPS = 2
SC_CHUNK = 64
PAD_SLOTS = N_EXPERTS * ROW_BLOCK
VMEM_LIMIT = 56 * 1024 * 1024


def _sigmoid(x):
    return 1.0 / (1.0 + jnp.exp(-x))


def _rms(x, w):
    ms = jnp.mean(x * x, axis=-1, keepdims=True)
    return x * lax.rsqrt(ms + EPS) * w


def _pack_rows(x):
    w = x.shape[1] // 2
    lo = lax.bitcast_convert_type(x[:, :w].astype(BF16).astype(F32), jnp.uint32)
    hi = lax.bitcast_convert_type(x[:, w:].astype(BF16).astype(F32), jnp.uint32)
    return lax.bitcast_convert_type((lo >> 16) | (hi & jnp.uint32(0xFFFF0000)), jnp.int32)


def _unpack_rows(p):
    u = lax.bitcast_convert_type(p, jnp.uint32)
    lo = lax.bitcast_convert_type(u << 16, F32)
    hi = lax.bitcast_convert_type(u & jnp.uint32(0xFFFF0000), F32)
    return jnp.concatenate([lo, hi], axis=1)


def _nt_dot(a, b):
    return lax.dot_general(a, b, (((1,), (1,)), ((), ())), preferred_element_type=F32)


def _tn_dot(a, b):
    return lax.dot_general(a, b, (((0,), (0,)), ((), ())), preferred_element_type=F32)


def _ada_kernel(c_ref, w_ref, b_ref, o_ref):
    c = c_ref[...]
    sc = (c * _sigmoid(c)).astype(BF16)
    o_ref[...] = jnp.dot(sc, w_ref[...].astype(BF16), preferred_element_type=F32) + b_ref[...]


def _ada(c, w_ada, b_ada):
    bsz, d = c.shape
    n = w_ada.shape[1]
    return pl.pallas_call(
        _ada_kernel,
        out_shape=jax.ShapeDtypeStruct((bsz, n), F32),
        grid=(n // d,),
        in_specs=[pl.BlockSpec((bsz, d), lambda j: (0, 0)),
                  pl.BlockSpec((d, d), lambda j: (0, j)),
                  pl.BlockSpec((1, d), lambda j: (0, j))],
        out_specs=pl.BlockSpec((bsz, d), lambda j: (0, j)),
        name="ada",
    )(c, w_ada, b_ada.reshape(1, n))


def _inproj_kernel(x_ref, mod_ref, nw_ref, w_ref, o_ref, u_scr):
    y = _rms(x_ref[...], nw_ref[...])
    shift = mod_ref[0, 0:1, :]
    scale = mod_ref[0, 1:2, :]
    u_scr[...] = (y * (1.0 + scale) + shift).astype(BF16)
    step = 512
    for j in range(IN_COLS // step):
        o_ref[:, j * step:(j + 1) * step] = jnp.dot(
            u_scr[...], w_ref[:, j * step:(j + 1) * step], preferred_element_type=F32).astype(BF16)


def _inproj(x2, mod, norm_w, w_in_bf, seq, tm):
    n = x2.shape[0]
    per_b = seq // tm
    return pl.pallas_call(
        _inproj_kernel,
        out_shape=jax.ShapeDtypeStruct((n, IN_COLS), BF16),
        grid=(n // tm,),
        in_specs=[pl.BlockSpec((tm, D_MODEL), lambda i: (i, 0)),
                  pl.BlockSpec((1, 6, D_MODEL), lambda i: (i // per_b, 0, 0)),
                  pl.BlockSpec((1, D_MODEL), lambda i: (0, 0)),
                  pl.BlockSpec((D_MODEL, IN_COLS), lambda i: (0, 0))],
        out_specs=pl.BlockSpec((tm, IN_COLS), lambda i: (i, 0)),
        scratch_shapes=[pltpu.VMEM((tm, D_MODEL), BF16)],
        compiler_params=pltpu.CompilerParams(dimension_semantics=("arbitrary",),
                                             vmem_limit_bytes=VMEM_LIMIT),
        name="inproj",
    )(x2, mod, norm_w.reshape(1, D_MODEL), w_in_bf)


_HEADS = [slice(h * HEAD_DIM, (h + 1) * HEAD_DIM) for h in range(HGRN_HEADS)]


def _chunk_rows(c):
    if isinstance(c, int):
        return pl.ds(c * CHUNK, CHUNK)
    return pl.ds(pl.multiple_of(c * CHUNK, CHUNK), CHUNK)


def _hgrn_gates(c, lb, tri, f_ref, b_scr, k_scr, f_scr):
    fx = f_ref[_chunk_rows(c), :].astype(F32)
    f = lb + (1.0 - lb) * _sigmoid(fx)
    g = jnp.log(f)
    g1 = g.astype(BF16)
    r1 = g - g1.astype(F32)
    g2 = r1.astype(BF16)
    g3 = (r1 - g2.astype(F32)).astype(BF16)
    b_scr[...] = (jnp.dot(tri, g1, preferred_element_type=F32)
                  + jnp.dot(tri, g2, preferred_element_type=F32)
                  + jnp.dot(tri, g3, preferred_element_type=F32))
    f_scr[...] = f
    k_scr[...] = 1.0 - f


def _hgrn_scores(c, q_ref, b_scr, k_scr, f_scr):
    rows = _chunk_rows(c)
    qs = [q_ref[rows, hs].astype(F32) for hs in _HEADS]
    s_mats = [_level_scores(1, qs[h], _HEADS[h], b_scr, k_scr, f_scr) for h in range(HGRN_HEADS)]
    for lvl in range(2, 7):
        for h in range(HGRN_HEADS):
            s_mats[h] = s_mats[h] + _level_scores(lvl, qs[h], _HEADS[h], b_scr, k_scr, f_scr)
    return s_mats


def _hgrn_outputs(c, s_mats, q_ref, v_ref, st_scr, o_scr, b_scr, k_scr):
    rows = _chunk_rows(c)
    for h, hs in enumerate(_HEADS):
        q = q_ref[rows, hs].astype(F32)
        v_bf = v_ref[rows, hs]
        b = b_scr[:, hs]
        kk = k_scr[:, hs]
        b_last = b_scr[CHUNK - 1:CHUNK, hs]
        st = st_scr[h]
        qd = (q * jnp.exp(b)).astype(BF16)
        kdec = (kk * jnp.exp(b_last - b)).astype(BF16)
        diag = jnp.sum(q * kk, axis=-1, keepdims=True)
        o_scr[rows, hs] = (_nt_dot(qd, st.astype(BF16))
                           + jnp.dot(s_mats[h].astype(BF16), v_bf, preferred_element_type=F32)
                           + diag * v_bf.astype(F32))
        st_scr[h] = jnp.exp(b_last) * st + _tn_dot(v_bf, kdec)


def _level_scores(lvl, q, hs, b_scr, k_scr, f_scr):
    row = lax.broadcasted_iota(jnp.int32, (CHUNK, 1), 0)
    col = lax.broadcasted_iota(jnp.int32, (1, CHUNK), 1)
    b = b_scr[:, hs]
    kk = k_scr[:, hs]
    blk = 1 << lvl
    half = blk // 2
    if lvl == 1:
        odd = (row & 1) == 1
        qx = jnp.where(odd, q * f_scr[:, hs], 0.0).astype(BF16)
        kx = jnp.where(odd, 0.0, kk).astype(BF16)
        return jnp.where((row >> 1) == (col >> 1), _nt_dot(qx, kx), 0.0)
    if half < 8:
        groups = []
        sub = lax.broadcasted_iota(jnp.int32, (8, 1), 0)
        for j in range(CHUNK // 8):
            rj = None
            for k in reversed(range(8 // blk)):
                m = 8 * j + k * blk + half - 1
                bm = jnp.broadcast_to(b_scr[m:m + 1, hs], (8, HEAD_DIM))
                rj = bm if rj is None else jnp.where(sub < (k + 1) * blk, bm, rj)
            groups.append(rj)
        ref = jnp.concatenate(groups, axis=0)
        second = (row & (blk - 1)) >= half
        qx = (q * jnp.exp(jnp.where(second, b - ref, -jnp.inf))).astype(BF16)
        kx = (kk * jnp.exp(jnp.where(second, -jnp.inf, ref - b))).astype(BF16)
        return jnp.where((row >> lvl) == (col >> lvl), _nt_dot(qx, kx), 0.0)
    n_blk = CHUNK // blk
    qparts, kparts = [], []
    for j in range(n_blk):
        m = j * blk + half - 1
        bm = b_scr[m:m + 1, hs]
        tq = slice(j * blk + half, (j + 1) * blk)
        tk = slice(j * blk, j * blk + half)
        qparts.append(q[tq] * jnp.exp(b[tq] - bm))
        kparts.append(kk[tk] * jnp.exp(bm - b[tk]))
        kparts.append(jnp.zeros((half, HEAD_DIM), F32))
    qx = jnp.concatenate(qparts, axis=0).astype(BF16)
    kx = jnp.concatenate(kparts, axis=0).astype(BF16)
    sc = _nt_dot(qx, kx)
    if n_blk > 1:
        crow = lax.broadcasted_iota(jnp.int32, (CHUNK // 2, 1), 0)
        sc = jnp.where((crow // half) == (col >> lvl), sc, 0.0)
    pieces = []
    for j in range(n_blk):
        pieces.append(jnp.zeros((half, CHUNK), F32))
        pieces.append(sc[j * half:(j + 1) * half])
    return jnp.concatenate(pieces, axis=0)


def _mix_kernel(lbt_ref, q_ref, f_ref, i_ref, g_ref, cb_ref, cc_ref, ch_ref, ga0_ref, ga1_ref,
                gb0_ref, gb1_ref, x_ref, mod_ref, hnw_ref, cw_ref, wa_ref, wb_ref, wm_ref,
                o_ref, st_scr, o_scr, carry_scr, b_scr, k_scr, f_scr, *, layer, rows_per_step, epi_rows):
    @pl.when(pl.program_id(1) == 0)
    def _():
        st_scr[...] = jnp.zeros_like(st_scr)
        carry_scr[...] = jnp.zeros_like(carry_scr)

    tab = lbt_ref[...]
    tmax = jnp.max(tab, axis=0, keepdims=True)
    te = jnp.exp(tab - tmax)
    lb = jnp.sum(te[0:layer + 1], axis=0, keepdims=True) / jnp.sum(te, axis=0, keepdims=True)

    ri = lax.broadcasted_iota(jnp.int32, (CHUNK, CHUNK), 0)
    ci = lax.broadcasted_iota(jnp.int32, (CHUNK, CHUNK), 1)
    tri = jnp.where(ci <= ri, 1.0, 0.0).astype(BF16)

    n_chunks = rows_per_step // CHUNK

    def slot(s):
        return b_scr.at[s], k_scr.at[s], f_scr.at[s]

    def chunk(c, c_next, s):
        bs, ks, fs = slot(s)
        s_mats = _hgrn_scores(c, q_ref, bs, ks, fs)
        _hgrn_gates(c_next, lb, tri, f_ref, *slot(1 - s))
        _hgrn_outputs(c, s_mats, q_ref, i_ref, st_scr, o_scr, bs, ks)

    def pair_body(i, carry):
        chunk(2 * i, 2 * i + 1, 0)
        chunk(2 * i + 1, jnp.minimum(2 * i + 2, n_chunks - 1), 1)
        return carry

    _hgrn_gates(0, lb, tri, f_ref, *slot(0))
    lax.fori_loop(0, n_chunks // 2, pair_body, 0)

    gate_m = mod_ref[0, 2:3, :]
    hnw = hnw_ref[...]
    cw0 = cw_ref[0:1, :]
    cw1 = cw_ref[1:2, :]
    cw2 = cw_ref[2:3, :]
    for r in range(rows_per_step // epi_rows):
        rs = slice(r * epi_rows, (r + 1) * epi_rows)
        parts = []
        for h in range(HGRN_HEADS):
            hs = slice(h * HEAD_DIM, (h + 1) * HEAD_DIM)
            oh = _rms(o_scr[rs, hs], hnw)
            go = g_ref[rs, hs].astype(F32)
            parts.append((oh * (go * _sigmoid(go))).astype(BF16))
        ya = jnp.dot(jnp.concatenate(parts, axis=1), wa_ref[...], preferred_element_type=F32)
        uc = cc_ref[rs, :].astype(F32) * ch_ref[rs, :].astype(F32)
        prev = carry_scr[...]
        rowi = lax.broadcasted_iota(jnp.int32, (epi_rows, 1), 0)
        s1 = jnp.where(rowi == 0, prev[7:8, :], pltpu.roll(uc, 1, 0))
        s2 = pltpu.roll(uc, 2, 0)
        s2 = jnp.where(rowi == 0, prev[6:7, :], jnp.where(rowi == 1, prev[7:8, :], s2))
        carry_scr[...] = uc[epi_rows - 8:epi_rows, :]
        yc = cb_ref[rs, :].astype(F32) * (cw2 * uc + cw1 * s1 + cw0 * s2)
        yb = jnp.dot(yc.astype(BF16), wb_ref[...], preferred_element_type=F32)
        ga = jnp.concatenate([ga0_ref[rs, :], ga1_ref[rs, :]], axis=1).astype(F32)
        gb = jnp.concatenate([gb0_ref[rs, :], gb1_ref[rs, :]], axis=1).astype(F32)
        merged = (_sigmoid(ga) * ya + _sigmoid(gb) * yb).astype(BF16)
        o_ref[rs, :] = x_ref[rs, :] + gate_m * jnp.dot(merged, wm_ref[...], preferred_element_type=F32)


def _mix(proj, x2, mod, lb_table, hgrn_norm_w, conv_w, wa, wb, wm, layer, bsz, seq, tt):
    n = x2.shape[0]
    per_b = seq // tt

    def col(col_block):
        return pl.BlockSpec((tt, 512), lambda b, t: (b * per_b + t, col_block))

    const = lambda shape: pl.BlockSpec(shape, lambda b, t: (0,) * len(shape))
    in_specs = [
        const(lb_table.shape),
        col(0), col(1), col(2), col(3),
        col(4), col(5), col(6),
        col(7), col(8), col(9), col(10),
        pl.BlockSpec((tt, D_MODEL), lambda b, t: (b * per_b + t, 0)),
        pl.BlockSpec((1, 6, D_MODEL), lambda b, t: (b, 0, 0)),
        const((1, HEAD_DIM)), const((CONV_K, CONV_WIDTH)),
        const((HGRN_WIDTH, D_MODEL)), const((CONV_WIDTH, D_MODEL)), const((D_MODEL, D_MODEL)),
    ]
    kern = functools.partial(_mix_kernel, layer=layer, rows_per_step=tt, epi_rows=min(tt, 256))
    return pl.pallas_call(
        kern,
        out_shape=jax.ShapeDtypeStruct((n, D_MODEL), F32),
        grid=(bsz, per_b),
        in_specs=in_specs,
        out_specs=pl.BlockSpec((tt, D_MODEL), lambda b, t: (b * per_b + t, 0)),
        scratch_shapes=[pltpu.VMEM((HGRN_HEADS, HEAD_DIM, HEAD_DIM), F32),
                        pltpu.VMEM((tt, HGRN_WIDTH), F32),
                        pltpu.VMEM((8, CONV_WIDTH), F32),
                        pltpu.VMEM((2, CHUNK, HGRN_WIDTH), F32),
                        pltpu.VMEM((2, CHUNK, HGRN_WIDTH), F32),
                        pltpu.VMEM((2, CHUNK, HGRN_WIDTH), F32)],
        compiler_params=pltpu.CompilerParams(dimension_semantics=("arbitrary", "arbitrary"),
                                             vmem_limit_bytes=VMEM_LIMIT),
        name="mix",
    )(lb_table, *([proj] * 11),
      x2, mod, hgrn_norm_w.reshape(1, HEAD_DIM), conv_w, wa, wb, wm)


def _route_kernel(h_ref, mod_ref, nw_ref, wr_ref, br_ref, u_ref, oh4_ref, rk_ref, pw_ref, cnt_ref, *, tr):
    shift = mod_ref[0, 3:4, :]
    scale = mod_ref[0, 4:5, :]
    u = _rms(h_ref[...], nw_ref[...]) * (1.0 + scale) + shift
    u_ref[...] = _pack_rows(u)
    lane = lax.broadcasted_iota(jnp.int32, (tr, LANES), 1)
    logits = jnp.dot(u.astype(BF16), wr_ref[...], preferred_element_type=F32) + br_ref[...]
    logits = jnp.where(lane < N_EXPERTS, logits, -jnp.inf)
    idx, val = [], []
    cur = logits
    for _ in range(TOP_K):
        m = jnp.max(cur, axis=-1, keepdims=True)
        i = jnp.min(jnp.where(cur == m, lane, LANES), axis=-1, keepdims=True)
        idx.append(i)
        val.append(m)
        cur = jnp.where(lane == i, -jnp.inf, cur)
    ex = [jnp.exp(v - val[0]) for v in val]
    den = ex[0] + ex[1] + ex[2] + ex[3]
    onehot = jnp.zeros((tr, LANES), F32)
    for i in idx:
        onehot = onehot + jnp.where(lane == i, 1.0, 0.0)
    ri = lax.broadcasted_iota(jnp.int32, (tr, tr), 0)
    ci = lax.broadcasted_iota(jnp.int32, (tr, tr), 1)
    tri = jnp.where(ci < ri, 1.0, 0.0).astype(BF16)
    pref = jnp.dot(tri, onehot.astype(BF16), preferred_element_type=F32)
    rk = jnp.zeros((tr, LANES), jnp.int32)
    pw = jnp.zeros((tr, LANES), F32)
    oh4 = jnp.zeros((tr, LANES), F32)
    for j in range(TOP_K):
        rank = jnp.sum(jnp.where(lane == idx[j], pref, 0.0), axis=-1, keepdims=True).astype(jnp.int32)
        rk = jnp.where(lane == j, rank, rk)
        pw = jnp.where(lane == j, ex[j] / den, pw)
        oh4 = oh4 + jnp.where(lane == idx[j] + N_EXPERTS * j, 1.0, 0.0)
    oh4_ref[...] = oh4.astype(BF16)
    rk_ref[...] = rk
    pw_ref[...] = pw
    cnt_ref[0] = jnp.sum(onehot, axis=0, keepdims=True).astype(jnp.int32)


def _route(h1, mod, norm_w, wr_pad, br_pad, seq, tr, tile0, n):
    per_b = seq // tr
    nt = n // tr
    return pl.pallas_call(
        functools.partial(_route_kernel, tr=tr),
        out_shape=(jax.ShapeDtypeStruct((n, PACKED), jnp.int32),
                   jax.ShapeDtypeStruct((n, LANES), BF16),
                   jax.ShapeDtypeStruct((n, LANES), jnp.int32),
                   jax.ShapeDtypeStruct((n, LANES), F32),
                   jax.ShapeDtypeStruct((nt, 1, LANES), jnp.int32)),
        grid=(nt,),
        in_specs=[pl.BlockSpec((tr, D_MODEL), lambda i: (tile0 + i, 0)),
                  pl.BlockSpec((1, 6, D_MODEL), lambda i: ((tile0 + i) // per_b, 0, 0)),
                  pl.BlockSpec((1, D_MODEL), lambda i: (0, 0)),
                  pl.BlockSpec((D_MODEL, LANES), lambda i: (0, 0)),
                  pl.BlockSpec((1, LANES), lambda i: (0, 0))],
        out_specs=(pl.BlockSpec((tr, PACKED), lambda i: (i, 0)),
                   pl.BlockSpec((tr, LANES), lambda i: (i, 0)),
                   pl.BlockSpec((tr, LANES), lambda i: (i, 0)),
                   pl.BlockSpec((tr, LANES), lambda i: (i, 0)),
                   pl.BlockSpec((1, 1, LANES), lambda i: (i, 0, 0))),
        compiler_params=pltpu.CompilerParams(dimension_semantics=("arbitrary",),
                                             vmem_limit_bytes=VMEM_LIMIT),
        name="route",
    )(h1, mod, norm_w.reshape(1, D_MODEL), wr_pad, br_pad)


def _dest_kernel(oh4_ref, rk_ref, bt_ref, o_ref):
    oh = oh4_ref[...]
    start = (jnp.dot(oh, bt_ref[0, 0], preferred_element_type=F32)
             + 256.0 * jnp.dot(oh, bt_ref[0, 1], preferred_element_type=F32)
             + 65536.0 * jnp.dot(oh, bt_ref[0, 2], preferred_element_type=F32))
    o_ref[...] = start.astype(jnp.int32) + rk_ref[...]


def _dest(oh4, rk, bt, tr):
    n = oh4.shape[0]
    return pl.pallas_call(
        _dest_kernel,
        out_shape=jax.ShapeDtypeStruct((n, LANES), jnp.int32),
        grid=(n // tr,),
        in_specs=[pl.BlockSpec((tr, LANES), lambda i: (i, 0)),
                  pl.BlockSpec((tr, LANES), lambda i: (i, 0)),
                  pl.BlockSpec((1, 3, LANES, LANES), lambda i: (i, 0, 0, 0))],
        out_specs=pl.BlockSpec((tr, LANES), lambda i: (i, 0)),
        name="dest",
    )(oh4, rk, bt)


def _sc_workers():
    info = plsc.get_sparse_core_info()
    return info.num_cores, info.num_cores * info.num_subcores


def _sc_scatter_rows(rows, idx_slots, pad_idx, n_out):
    n_cores, n_workers = _sc_workers()
    n, w = rows.shape
    k = idx_slots.shape[0] // n
    per_worker = n // n_workers
    pad_per_worker = pad_idx.shape[0] // n_workers
    assert per_worker % SC_CHUNK == 0 and pad_per_worker % SC_CHUNK == 0
    mesh = plsc.VectorSubcoreMesh(core_axis_name="c", subcore_axis_name="s")
    zeros = jnp.zeros((SC_CHUNK, w), rows.dtype)

    @functools.partial(
        pl.kernel, mesh=mesh,
        out_type=jax.ShapeDtypeStruct((n_out, w), rows.dtype),
        scratch_types=[pltpu.VMEM((SC_CHUNK,), jnp.int32),
                       pltpu.VMEM((SC_CHUNK, w), rows.dtype),
                       pltpu.SemaphoreType.DMA],
        name="sc_scatter",
    )
    def scatter(rows_hbm, idx_hbm, pad_hbm, zeros_hbm, out_hbm, idx_v, rows_v, sem):
        wid = lax.axis_index("s") * n_cores + lax.axis_index("c")

        def body(i, carry):
            off = pl.multiple_of(wid * per_worker + i * SC_CHUNK, 8)
            pltpu.sync_copy(rows_hbm.at[pl.ds(off, SC_CHUNK)], rows_v)
            for j in range(k):
                pltpu.sync_copy(idx_hbm.at[pl.ds(pl.multiple_of(j * n + off, 8), SC_CHUNK)], idx_v)
                pltpu.async_copy(rows_v, out_hbm.at[idx_v], sem).wait()
            return carry

        lax.fori_loop(0, per_worker // SC_CHUNK, body, 0)
        pltpu.sync_copy(zeros_hbm, rows_v)

        def pad_body(i, carry):
            off = pl.multiple_of(wid * pad_per_worker + i * SC_CHUNK, 8)
            pltpu.sync_copy(pad_hbm.at[pl.ds(off, SC_CHUNK)], idx_v)
            pltpu.async_copy(rows_v, out_hbm.at[idx_v], sem).wait()
            return carry

        lax.fori_loop(0, pad_per_worker // SC_CHUNK, pad_body, 0)

    return scatter(rows, idx_slots, pad_idx, zeros)


def _experts_kernel(be_ref, nu_ref, first_ref, next_ref, slot_ref, x_ref, w1_hbm, b1_ref, w2_hbm, b2_ref,
                    o_ref, w1_f32, w2_f32, w1_bf, w2_bf, sems):
    i = pl.program_id(0)
    used = i < nu_ref[0]

    def weight_copies(e, s):
        return (pltpu.make_async_copy(w1_hbm.at[e], w1_f32.at[s], sems.at[s]),
                pltpu.make_async_copy(w2_hbm.at[e], w2_f32.at[s], sems.at[s]))

    @pl.when(used & (first_ref[i] == 1))
    def _():
        s = slot_ref[i]

        @pl.when(i == 0)
        def _():
            for cp in weight_copies(be_ref[i], s):
                cp.start()

        for cp in weight_copies(be_ref[i], s):
            cp.wait()
        cw = 256
        for c in range(2 * D_FF // cw):
            w1_bf[:, c * cw:(c + 1) * cw] = w1_f32[s, :, c * cw:(c + 1) * cw].astype(BF16)
        for c in range(D_MODEL // cw):
            w2_bf[:, c * cw:(c + 1) * cw] = w2_f32[s, :, c * cw:(c + 1) * cw].astype(BF16)

        @pl.when(next_ref[i] >= 0)
        def _():
            for cp in weight_copies(next_ref[i], 1 - s):
                cp.start()

    @pl.when(used)
    def _():
        step = 512
        for r in range(ROW_BLOCK // EXPERT_ROWS):
            rs = slice(r * EXPERT_ROWS, (r + 1) * EXPERT_ROWS)
            x = _unpack_rows(x_ref[rs, :]).astype(BF16)
            acc = None
            for j in range(D_FF // step):
                cs = slice(j * step, (j + 1) * step)
                ls = slice(D_FF + j * step, D_FF + (j + 1) * step)
                glu = jnp.dot(x, w1_bf[:, cs], preferred_element_type=F32) + b1_ref[0, :, cs]
                lin = jnp.dot(x, w1_bf[:, ls], preferred_element_type=F32) + b1_ref[0, :, ls]
                glu = jnp.minimum(glu, SWIGLU_LIMIT)
                lin = jnp.clip(lin, -SWIGLU_LIMIT, SWIGLU_LIMIT)
                act = (glu * _sigmoid(SWIGLU_ALPHA * glu) * (lin + 1.0)).astype(BF16)
                part = jnp.dot(act, w2_bf[cs, :], preferred_element_type=F32)
                acc = part if acc is None else acc + part
            o_ref[rs, :] = _pack_rows(acc + b2_ref[0])

    @pl.when(jnp.logical_not(used))
    def _():
        o_ref[...] = jnp.zeros_like(o_ref)


def _experts(plan, xs, w1, b1, w2, b2):
    n_rows = xs.shape[0]
    nb = n_rows // ROW_BLOCK
    grid_spec = pltpu.PrefetchScalarGridSpec(
        num_scalar_prefetch=5,
        grid=(nb,),
        in_specs=[pl.BlockSpec((ROW_BLOCK, PACKED), lambda i, be, nu, *_: (jnp.minimum(i, nu[0] - 1), 0)),
                  pl.BlockSpec(memory_space=pl.ANY),
                  pl.BlockSpec((1, 1, 2 * D_FF), lambda i, be, *_: (be[i], 0, 0)),
                  pl.BlockSpec(memory_space=pl.ANY),
                  pl.BlockSpec((1, 1, D_MODEL), lambda i, be, *_: (be[i], 0, 0))],
        out_specs=pl.BlockSpec((ROW_BLOCK, PACKED), lambda i, *_: (i, 0)),
        scratch_shapes=[pltpu.VMEM((2, D_MODEL, 2 * D_FF), F32), pltpu.VMEM((2, D_FF, D_MODEL), F32),
                        pltpu.VMEM((D_MODEL, 2 * D_FF), BF16), pltpu.VMEM((D_FF, D_MODEL), BF16),
                        pltpu.SemaphoreType.DMA((2,))],
    )
    return pl.pallas_call(
        _experts_kernel,
        out_shape=jax.ShapeDtypeStruct((n_rows, PACKED), jnp.int32),
        grid_spec=grid_spec,
        compiler_params=pltpu.CompilerParams(dimension_semantics=("arbitrary",),
                                             vmem_limit_bytes=VMEM_LIMIT),
        name="experts",
    )(*plan, xs, w1, b1.reshape(N_EXPERTS, 1, 2 * D_FF), w2, b2.reshape(N_EXPERTS, 1, D_MODEL))


def _sc_gather_rows(table, idx_flat):
    n_cores, n_workers = _sc_workers()
    n_idx = idx_flat.shape[0]
    w = table.shape[1]
    per_worker = n_idx // n_workers
    n_chunks = per_worker // SC_CHUNK
    assert per_worker * n_workers == n_idx and n_chunks * SC_CHUNK == per_worker and n_chunks % 2 == 0
    mesh = plsc.VectorSubcoreMesh(core_axis_name="c", subcore_axis_name="s")

    @functools.partial(
        pl.kernel, mesh=mesh,
        out_type=jax.ShapeDtypeStruct((n_idx, w), table.dtype),
        scratch_types=[pltpu.VMEM((SC_CHUNK,), jnp.int32), pltpu.VMEM((SC_CHUNK,), jnp.int32),
                       pltpu.VMEM((SC_CHUNK, w), table.dtype), pltpu.VMEM((SC_CHUNK, w), table.dtype),
                       pltpu.SemaphoreType.DMA, pltpu.SemaphoreType.DMA],
        name="sc_gather",
    )
    def gather(table_hbm, idx_hbm, out_hbm, idx0, idx1, rows0, rows1, sem0, sem1):
        wid = lax.axis_index("s") * n_cores + lax.axis_index("c")
        bufs = ((idx0, rows0, sem0), (idx1, rows1, sem1))

        def chunk_off(c):
            return pl.multiple_of(wid * per_worker + c * SC_CHUNK, 8)

        def start(c, b):
            idx_v, rows_v, sem = bufs[b]
            pltpu.sync_copy(idx_hbm.at[pl.ds(chunk_off(c), SC_CHUNK)], idx_v)
            pltpu.async_copy(table_hbm.at[idx_v], rows_v, sem)

        def finish(c, b):
            idx_v, rows_v, sem = bufs[b]
            pltpu.make_async_copy(table_hbm.at[idx_v], rows_v, sem).wait()
            pltpu.sync_copy(rows_v, out_hbm.at[pl.ds(chunk_off(c), SC_CHUNK)])

        start(0, 0)

        def body(i, carry):
            start(2 * i + 1, 1)
            finish(2 * i, 0)

            @pl.when(i < n_chunks // 2 - 1)
            def _():
                start(2 * i + 2, 0)

            finish(2 * i + 1, 1)
            return carry

        lax.fori_loop(0, n_chunks // 2, body, 0)

    return gather(table, idx_flat)


def _finish_kernel(h_ref, pw_ref, mod_ref, nw_ref, y0_ref, y1_ref, y2_ref, y3_ref, *rest):
    o_ref = rest[-1]
    pw = pw_ref[...]
    moe = pw[:, 0:1] * _unpack_rows(y0_ref[0])
    for j, y_ref in enumerate((y1_ref, y2_ref, y3_ref), start=1):
        moe = moe + pw[:, j:j + 1] * _unpack_rows(y_ref[0])
    gate_f = mod_ref[0, 5:6, :]
    o_ref[...] = _rms(h_ref[...] + gate_f * moe, nw_ref[...])


def _finish(h1, pw, mod, norm_w, y4, seq, tc, tile0, prev_out):
    n = pw.shape[0]
    per_b = seq // tc
    slot = lambda j: pl.BlockSpec((1, tc, PACKED), lambda i: (j, i, 0))
    in_specs = [pl.BlockSpec((tc, D_MODEL), lambda i: (tile0 + i, 0)),
                pl.BlockSpec((tc, LANES), lambda i: (i, 0)),
                pl.BlockSpec((1, 6, D_MODEL), lambda i: ((tile0 + i) // per_b, 0, 0)),
                pl.BlockSpec((1, D_MODEL), lambda i: (0, 0)),
                slot(0), slot(1), slot(2), slot(3)]
    args = [h1, pw, mod, norm_w.reshape(1, D_MODEL), y4, y4, y4, y4]
    aliases = {}
    if prev_out is not None:
        in_specs.append(pl.BlockSpec(memory_space=pl.ANY))
        args.append(prev_out)
        aliases = {len(args) - 1: 0}
    return pl.pallas_call(
        _finish_kernel,
        out_shape=jax.ShapeDtypeStruct(h1.shape, F32),
        grid=(n // tc,),
        in_specs=in_specs,
        out_specs=pl.BlockSpec((tc, D_MODEL), lambda i: (tile0 + i, 0)),
        input_output_aliases=aliases,
        compiler_params=pltpu.CompilerParams(dimension_semantics=("arbitrary",),
                                             vmem_limit_bytes=VMEM_LIMIT),
        name="finish",
    )(*args)


def _moe_plan(counts, n_assign):
    cnt = counts[:, 0, :N_EXPERTS]
    sizes = jnp.sum(cnt, axis=0)
    padded = (sizes + ROW_BLOCK - 1) // ROW_BLOCK * ROW_BLOCK
    pad_end = jnp.cumsum(padded)
    pad_start = pad_end - padded
    tile_base = pad_start[None, :] + jnp.cumsum(cnt, axis=0) - cnt
    digits = jnp.stack([tile_base % 256, (tile_base // 256) % 256, tile_base // 65536], axis=1)
    rows = jnp.tile(digits, (1, 1, TOP_K))
    slot_of_row = jnp.arange(LANES, dtype=jnp.int32) // N_EXPERTS
    col = jnp.arange(LANES, dtype=jnp.int32)
    bt = jnp.where(slot_of_row[:, None] == col[None, :], rows[..., None], 0).astype(BF16)
    nb = n_assign // ROW_BLOCK + N_EXPERTS
    block_start = jnp.arange(nb, dtype=jnp.int32) * ROW_BLOCK
    block_e = jnp.minimum(jnp.sum(pad_end[None, :] <= block_start[:, None], axis=1),
                          N_EXPERTS - 1).astype(jnp.int32)
    n_used = (pad_end[-1] // ROW_BLOCK).astype(jnp.int32).reshape(1)
    nonempty = padded > 0
    ids = jnp.arange(N_EXPERTS, dtype=jnp.int32)
    suffix_min = lax.cummin(jnp.where(nonempty, ids, N_EXPERTS), reverse=True)
    after = jnp.concatenate([suffix_min[1:], jnp.full((1,), N_EXPERTS, jnp.int32)])
    next_nonempty = jnp.where(after < N_EXPERTS, after, -1).astype(jnp.int32)
    slot_of = ((jnp.cumsum(nonempty) - 1) % 2).astype(jnp.int32)
    first = jnp.concatenate([jnp.ones((1,), jnp.int32),
                             (block_e[1:] != block_e[:-1]).astype(jnp.int32)])
    expert_plan = (block_e, n_used, first, next_nonempty[block_e], slot_of[block_e])
    r = jnp.arange(ROW_BLOCK, dtype=jnp.int32)[None, :]
    pad_idx = jnp.where(r < (padded - sizes)[:, None], (pad_start + sizes)[:, None] + r,
                        (nb - 1) * ROW_BLOCK + r).astype(jnp.int32).reshape(PAD_SLOTS)
    return bt, expert_plan, pad_idx, nb * ROW_BLOCK


def kernel(x, c, w_ada, b_ada, norm_mix_w, w_in, hgrn_lower_bounds, hgrn_norm_w, conv_w,
           w_hgrn_out, w_conv_out, w_mix_out, norm_ffn_w, w_router, b_router, w1, b1, w2, b2,
           norm_final_w):
    bsz, seq, d = x.shape
    assert d == D_MODEL and seq % CHUNK == 0
    n = bsz * seq
    depth = w_ada.shape[0]
    tile = min(512, seq)
    assert seq % tile == 0 and n % (MOE_GROUPS * tile) == 0 and (n // MOE_GROUPS * TOP_K) % ROW_BLOCK == 0
    h = x.reshape(n, d)
    wr_pad = jnp.zeros((depth, D_MODEL, LANES), BF16).at[:, :, :N_EXPERTS].set(w_router.astype(BF16))
    br_pad = jnp.zeros((depth, 1, LANES), F32).at[:, 0, :N_EXPERTS].set(b_router)
    for layer in range(depth):
        mod = _ada(c, w_ada[layer], b_ada[layer]).reshape(bsz, 6, d)
        proj = _inproj(h, mod, norm_mix_w[layer], w_in[layer].astype(BF16), seq, tile)
        h = _mix(proj, h, mod, hgrn_lower_bounds, hgrn_norm_w[layer], conv_w[layer],
                 w_hgrn_out[layer].astype(BF16), w_conv_out[layer].astype(BF16),
                 w_mix_out[layer].astype(BF16), layer, bsz, seq, tile)
        assert layer == depth - 1, "only the last layer applies the final norm"
        ng = n // MOE_GROUPS
        out = None
        for grp in range(MOE_GROUPS):
            tile0 = grp * (ng // tile)
            u2, oh4, rk, pw, counts = _route(h, mod, norm_ffn_w[layer], wr_pad[layer], br_pad[layer],
                                             seq, tile, tile0, ng)
            bt, expert_plan, pad_idx, n_rows = _moe_plan(counts, ng * TOP_K)
            dest_slots = _dest(oh4, rk, bt, tile)[:, :TOP_K].T.reshape(TOP_K * ng)
            xs = _sc_scatter_rows(u2, dest_slots, pad_idx, n_rows)
            ys = _experts(expert_plan, xs, w1[layer], b1[layer], w2[layer], b2[layer])
            y4 = _sc_gather_rows(ys, dest_slots).reshape(TOP_K, ng, PACKED)
            out = _finish(h, pw, mod, norm_final_w, y4, seq, tile, tile0, out)
        h = out
    return h.reshape(bsz, seq, d)
```

```python
import functools

import jax
import jax.numpy as jnp
from jax import lax
from jax.experimental import pallas as pl
from jax.experimental.pallas import tpu as pltpu
from jax.experimental.pallas import tpu_sc as plsc

F32 = jnp.float32
BF16 = jnp.bfloat16

D_MODEL = 1024
HGRN_HEADS = 4
HEAD_DIM = 128
HGRN_WIDTH = HGRN_HEADS * HEAD_DIM
CONV_WIDTH = 512
CONV_K = 3
CHUNK = 64
N_EXPERTS = 32
TOP_K = 4
D_FF = 1024
SWIGLU_LIMIT = 7.0
SWIGLU_ALPHA = 1.702
EPS = 1e-6
IN_COLS = 4 * HGRN_WIDTH + 3 * CONV_WIDTH + 2 * D_MODEL
LANES = 128
ROW_BLOCK = 512
EXPERT_ROWS = 512
PACKED = D_MODEL // 2
MOE_GROUPS = 2
SC_CHUNK = 64
PAD_SLOTS = N_EXPERTS * ROW_BLOCK
VMEM_LIMIT = 56 * 1024 * 1024


def _sigmoid(x):
    return 1.0 / (1.0 + jnp.exp(-x))


def _rms(x, w):
    ms = jnp.mean(x * x, axis=-1, keepdims=True)
    return x * lax.rsqrt(ms + EPS) * w


def _pack_rows(x):
    w = x.shape[1] // 2
    lo = lax.bitcast_convert_type(x[:, :w].astype(BF16).astype(F32), jnp.uint32)
    hi = lax.bitcast_convert_type(x[:, w:].astype(BF16).astype(F32), jnp.uint32)
    return lax.bitcast_convert_type((lo >> 16) | (hi & jnp.uint32(0xFFFF0000)), jnp.int32)


def _unpack_rows(p):
    u = lax.bitcast_convert_type(p, jnp.uint32)
    lo = lax.bitcast_convert_type(u << 16, F32)
    hi = lax.bitcast_convert_type(u & jnp.uint32(0xFFFF0000), F32)
    return jnp.concatenate([lo, hi], axis=1)


def _nt_dot(a, b):
    return lax.dot_general(a, b, (((1,), (1,)), ((), ())), preferred_element_type=F32)


def _tn_dot(a, b):
    return lax.dot_general(a, b, (((0,), (0,)), ((), ())), preferred_element_type=F32)


def _ada_kernel(c_ref, w_ref, b_ref, o_ref):
    c = c_ref[...]
    sc = (c * _sigmoid(c)).astype(BF16)
    o_ref[...] = jnp.dot(sc, w_ref[...].astype(BF16), preferred_element_type=F32) + b_ref[...]


def _ada(c, w_ada, b_ada):
    bsz, d = c.shape
    n = w_ada.shape[1]
    return pl.pallas_call(
        _ada_kernel,
        out_shape=jax.ShapeDtypeStruct((bsz, n), F32),
        grid=(n // d,),
        in_specs=[pl.BlockSpec((bsz, d), lambda j: (0, 0)),
                  pl.BlockSpec((d, d), lambda j: (0, j)),
                  pl.BlockSpec((1, d), lambda j: (0, j))],
        out_specs=pl.BlockSpec((bsz, d), lambda j: (0, j)),
        name="ada",
    )(c, w_ada, b_ada.reshape(1, n))


def _inproj_kernel(x_ref, mod_ref, nw_ref, w_ref, o_ref, u_scr):
    y = _rms(x_ref[...], nw_ref[...])
    shift = mod_ref[0, 0:1, :]
    scale = mod_ref[0, 1:2, :]
    u_scr[...] = (y * (1.0 + scale) + shift).astype(BF16)
    step = 512
    for j in range(IN_COLS // step):
        o_ref[:, j * step:(j + 1) * step] = jnp.dot(
            u_scr[...], w_ref[:, j * step:(j + 1) * step], preferred_element_type=F32).astype(BF16)


def _inproj(x2, mod, norm_w, w_in_bf, seq, tm):
    n = x2.shape[0]
    per_b = seq // tm
    return pl.pallas_call(
        _inproj_kernel,
        out_shape=jax.ShapeDtypeStruct((n, IN_COLS), BF16),
        grid=(n // tm,),
        in_specs=[pl.BlockSpec((tm, D_MODEL), lambda i: (i, 0)),
                  pl.BlockSpec((1, 6, D_MODEL), lambda i: (i // per_b, 0, 0)),
                  pl.BlockSpec((1, D_MODEL), lambda i: (0, 0)),
                  pl.BlockSpec((D_MODEL, IN_COLS), lambda i: (0, 0))],
        out_specs=pl.BlockSpec((tm, IN_COLS), lambda i: (i, 0)),
        scratch_shapes=[pltpu.VMEM((tm, D_MODEL), BF16)],
        compiler_params=pltpu.CompilerParams(dimension_semantics=("arbitrary",),
                                             vmem_limit_bytes=VMEM_LIMIT),
        name="inproj",
    )(x2, mod, norm_w.reshape(1, D_MODEL), w_in_bf)


_HEADS = [slice(h * HEAD_DIM, (h + 1) * HEAD_DIM) for h in range(HGRN_HEADS)]


def _chunk_rows(c):
    if isinstance(c, int):
        return pl.ds(c * CHUNK, CHUNK)
    return pl.ds(pl.multiple_of(c * CHUNK, CHUNK), CHUNK)


def _hgrn_gates(c, lb, tri, f_ref, b_scr, k_scr, f_scr):
    fx = f_ref[_chunk_rows(c), :].astype(F32)
    f = lb + (1.0 - lb) * _sigmoid(fx)
    g = jnp.log(f)
    g1 = g.astype(BF16)
    r1 = g - g1.astype(F32)
    g2 = r1.astype(BF16)
    g3 = (r1 - g2.astype(F32)).astype(BF16)
    b_scr[...] = (jnp.dot(tri, g1, preferred_element_type=F32)
                  + jnp.dot(tri, g2, preferred_element_type=F32)
                  + jnp.dot(tri, g3, preferred_element_type=F32))
    f_scr[...] = f
    k_scr[...] = 1.0 - f


def _hgrn_scores(c, q_ref, b_scr, k_scr, f_scr):
    rows = _chunk_rows(c)
    qs = [q_ref[rows, hs].astype(F32) for hs in _HEADS]
    s_mats = [_level_scores(1, qs[h], _HEADS[h], b_scr, k_scr, f_scr) for h in range(HGRN_HEADS)]
    for lvl in range(2, 7):
        for h in range(HGRN_HEADS):
            s_mats[h] = s_mats[h] + _level_scores(lvl, qs[h], _HEADS[h], b_scr, k_scr, f_scr)
    return s_mats


def _hgrn_outputs(c, s_mats, q_ref, v_ref, st_scr, o_scr, b_scr, k_scr):
    rows = _chunk_rows(c)
    for h, hs in enumerate(_HEADS):
        q = q_ref[rows, hs].astype(F32)
        v_bf = v_ref[rows, hs]
        b = b_scr[:, hs]
        kk = k_scr[:, hs]
        b_last = b_scr[CHUNK - 1:CHUNK, hs]
        st = st_scr[h]
        qd = (q * jnp.exp(b)).astype(BF16)
        kdec = (kk * jnp.exp(b_last - b)).astype(BF16)
        diag = jnp.sum(q * kk, axis=-1, keepdims=True)
        o_scr[rows, hs] = (_nt_dot(qd, st.astype(BF16))
                           + jnp.dot(s_mats[h].astype(BF16), v_bf, preferred_element_type=F32)
                           + diag * v_bf.astype(F32))
        st_scr[h] = jnp.exp(b_last) * st + _tn_dot(v_bf, kdec)


def _level_scores(lvl, q, hs, b_scr, k_scr, f_scr):
    row = lax.broadcasted_iota(jnp.int32, (CHUNK, 1), 0)
    col = lax.broadcasted_iota(jnp.int32, (1, CHUNK), 1)
    b = b_scr[:, hs]
    kk = k_scr[:, hs]
    blk = 1 << lvl
    half = blk // 2
    if lvl == 1:
        odd = (row & 1) == 1
        qx = jnp.where(odd, q * f_scr[:, hs], 0.0).astype(BF16)
        kx = jnp.where(odd, 0.0, kk).astype(BF16)
        return jnp.where((row >> 1) == (col >> 1), _nt_dot(qx, kx), 0.0)
    if half < 8:
        groups = []
        sub = lax.broadcasted_iota(jnp.int32, (8, 1), 0)
        for j in range(CHUNK // 8):
            rj = None
            for k in reversed(range(8 // blk)):
                m = 8 * j + k * blk + half - 1
                bm = jnp.broadcast_to(b_scr[m:m + 1, hs], (8, HEAD_DIM))
                rj = bm if rj is None else jnp.where(sub < (k + 1) * blk, bm, rj)
            groups.append(rj)
        ref = jnp.concatenate(groups, axis=0)
        second = (row & (blk - 1)) >= half
        qx = (q * jnp.exp(jnp.where(second, b - ref, -jnp.inf))).astype(BF16)
        kx = (kk * jnp.exp(jnp.where(second, -jnp.inf, ref - b))).astype(BF16)
        return jnp.where((row >> lvl) == (col >> lvl), _nt_dot(qx, kx), 0.0)
    n_blk = CHUNK // blk
    qparts, kparts = [], []
    for j in range(n_blk):
        m = j * blk + half - 1
        bm = b_scr[m:m + 1, hs]
        tq = slice(j * blk + half, (j + 1) * blk)
        tk = slice(j * blk, j * blk + half)
        qparts.append(q[tq] * jnp.exp(b[tq] - bm))
        kparts.append(kk[tk] * jnp.exp(bm - b[tk]))
        kparts.append(jnp.zeros((half, HEAD_DIM), F32))
    qx = jnp.concatenate(qparts, axis=0).astype(BF16)
    kx = jnp.concatenate(kparts, axis=0).astype(BF16)
    sc = _nt_dot(qx, kx)
    if n_blk > 1:
        crow = lax.broadcasted_iota(jnp.int32, (CHUNK // 2, 1), 0)
        sc = jnp.where((crow // half) == (col >> lvl), sc, 0.0)
    pieces = []
    for j in range(n_blk):
        pieces.append(jnp.zeros((half, CHUNK), F32))
        pieces.append(sc[j * half:(j + 1) * half])
    return jnp.concatenate(pieces, axis=0)


def _mix_kernel(lbt_ref, q_ref, f_ref, i_ref, g_ref, cb_ref, cc_ref, ch_ref, ga0_ref, ga1_ref,
                gb0_ref, gb1_ref, x_ref, mod_ref, hnw_ref, cw_ref, wa_ref, wb_ref, wm_ref,
                o_ref, st_scr, o_scr, carry_scr, b_scr, k_scr, f_scr, *, layer, rows_per_step, epi_rows):
    @pl.when(pl.program_id(1) == 0)
    def _():
        st_scr[...] = jnp.zeros_like(st_scr)
        carry_scr[...] = jnp.zeros_like(carry_scr)

    tab = lbt_ref[...]
    tmax = jnp.max(tab, axis=0, keepdims=True)
    te = jnp.exp(tab - tmax)
    lb = jnp.sum(te[0:layer + 1], axis=0, keepdims=True) / jnp.sum(te, axis=0, keepdims=True)

    ri = lax.broadcasted_iota(jnp.int32, (CHUNK, CHUNK), 0)
    ci = lax.broadcasted_iota(jnp.int32, (CHUNK, CHUNK), 1)
    tri = jnp.where(ci <= ri, 1.0, 0.0).astype(BF16)

    n_chunks = rows_per_step // CHUNK

    def slot(s):
        return b_scr.at[s], k_scr.at[s], f_scr.at[s]

    def chunk(c, c_next, s):
        bs, ks, fs = slot(s)
        s_mats = _hgrn_scores(c, q_ref, bs, ks, fs)
        _hgrn_gates(c_next, lb, tri, f_ref, *slot(1 - s))
        _hgrn_outputs(c, s_mats, q_ref, i_ref, st_scr, o_scr, bs, ks)

    def pair_body(i, carry):
        chunk(2 * i, 2 * i + 1, 0)
        chunk(2 * i + 1, jnp.minimum(2 * i + 2, n_chunks - 1), 1)
        return carry

    _hgrn_gates(0, lb, tri, f_ref, *slot(0))
    lax.fori_loop(0, n_chunks // 2, pair_body, 0)

    gate_m = mod_ref[0, 2:3, :]
    hnw = hnw_ref[...]
    cw0 = cw_ref[0:1, :]
    cw1 = cw_ref[1:2, :]
    cw2 = cw_ref[2:3, :]
    for r in range(rows_per_step // epi_rows):
        rs = slice(r * epi_rows, (r + 1) * epi_rows)
        parts = []
        for h in range(HGRN_HEADS):
            hs = slice(h * HEAD_DIM, (h + 1) * HEAD_DIM)
            oh = _rms(o_scr[rs, hs], hnw)
            go = g_ref[rs, hs].astype(F32)
            parts.append((oh * (go * _sigmoid(go))).astype(BF16))
        ya = jnp.dot(jnp.concatenate(parts, axis=1), wa_ref[...], preferred_element_type=F32)
        uc = cc_ref[rs, :].astype(F32) * ch_ref[rs, :].astype(F32)
        prev = carry_scr[...]
        rowi = lax.broadcasted_iota(jnp.int32, (epi_rows, 1), 0)
        s1 = jnp.where(rowi == 0, prev[7:8, :], pltpu.roll(uc, 1, 0))
        s2 = pltpu.roll(uc, 2, 0)
        s2 = jnp.where(rowi == 0, prev[6:7, :], jnp.where(rowi == 1, prev[7:8, :], s2))
        carry_scr[...] = uc[epi_rows - 8:epi_rows, :]
        yc = cb_ref[rs, :].astype(F32) * (cw2 * uc + cw1 * s1 + cw0 * s2)
        yb = jnp.dot(yc.astype(BF16), wb_ref[...], preferred_element_type=F32)
        ga = jnp.concatenate([ga0_ref[rs, :], ga1_ref[rs, :]], axis=1).astype(F32)
        gb = jnp.concatenate([gb0_ref[rs, :], gb1_ref[rs, :]], axis=1).astype(F32)
        merged = (_sigmoid(ga) * ya + _sigmoid(gb) * yb).astype(BF16)
        o_ref[rs, :] = x_ref[rs, :] + gate_m * jnp.dot(merged, wm_ref[...], preferred_element_type=F32)


def _mix(proj, x2, mod, lb_table, hgrn_norm_w, conv_w, wa, wb, wm, layer, bsz, seq, tt):
    n = x2.shape[0]
    per_b = seq // tt

    def col(col_block):
        return pl.BlockSpec((tt, 512), lambda b, t: (b * per_b + t, col_block))

    const = lambda shape: pl.BlockSpec(shape, lambda b, t: (0,) * len(shape))
    in_specs = [
        const(lb_table.shape),
        col(0), col(1), col(2), col(3),
        col(4), col(5), col(6),
        col(7), col(8), col(9), col(10),
        pl.BlockSpec((tt, D_MODEL), lambda b, t: (b * per_b + t, 0)),
        pl.BlockSpec((1, 6, D_MODEL), lambda b, t: (b, 0, 0)),
        const((1, HEAD_DIM)), const((CONV_K, CONV_WIDTH)),
        const((HGRN_WIDTH, D_MODEL)), const((CONV_WIDTH, D_MODEL)), const((D_MODEL, D_MODEL)),
    ]
    kern = functools.partial(_mix_kernel, layer=layer, rows_per_step=tt, epi_rows=tt)
    return pl.pallas_call(
        kern,
        out_shape=jax.ShapeDtypeStruct((n, D_MODEL), F32),
        grid=(bsz, per_b),
        in_specs=in_specs,
        out_specs=pl.BlockSpec((tt, D_MODEL), lambda b, t: (b * per_b + t, 0)),
        scratch_shapes=[pltpu.VMEM((HGRN_HEADS, HEAD_DIM, HEAD_DIM), F32),
                        pltpu.VMEM((tt, HGRN_WIDTH), F32),
                        pltpu.VMEM((8, CONV_WIDTH), F32),
                        pltpu.VMEM((2, CHUNK, HGRN_WIDTH), F32),
                        pltpu.VMEM((2, CHUNK, HGRN_WIDTH), F32),
                        pltpu.VMEM((2, CHUNK, HGRN_WIDTH), F32)],
        compiler_params=pltpu.CompilerParams(dimension_semantics=("arbitrary", "arbitrary"),
                                             vmem_limit_bytes=VMEM_LIMIT),
        name="mix",
    )(lb_table, *([proj] * 11),
      x2, mod, hgrn_norm_w.reshape(1, HEAD_DIM), conv_w, wa, wb, wm)


def _route_kernel(h_ref, mod_ref, nw_ref, wr_ref, br_ref, u_ref, oh4_ref, rk_ref, pw_ref, cnt_ref, *, tr):
    shift = mod_ref[0, 3:4, :]
    scale = mod_ref[0, 4:5, :]
    u = _rms(h_ref[...], nw_ref[...]) * (1.0 + scale) + shift
    u_ref[...] = _pack_rows(u)
    lane = lax.broadcasted_iota(jnp.int32, (tr, LANES), 1)
    logits = jnp.dot(u.astype(BF16), wr_ref[...], preferred_element_type=F32) + br_ref[...]
    logits = jnp.where(lane < N_EXPERTS, logits, -jnp.inf)
    idx, val = [], []
    cur = logits
    for _ in range(TOP_K):
        m = jnp.max(cur, axis=-1, keepdims=True)
        i = jnp.min(jnp.where(cur == m, lane, LANES), axis=-1, keepdims=True)
        idx.append(i)
        val.append(m)
        cur = jnp.where(lane == i, -jnp.inf, cur)
    ex = [jnp.exp(v - val[0]) for v in val]
    den = ex[0] + ex[1] + ex[2] + ex[3]
    onehot = jnp.zeros((tr, LANES), F32)
    for i in idx:
        onehot = onehot + jnp.where(lane == i, 1.0, 0.0)
    ri = lax.broadcasted_iota(jnp.int32, (tr, tr), 0)
    ci = lax.broadcasted_iota(jnp.int32, (tr, tr), 1)
    tri = jnp.where(ci < ri, 1.0, 0.0).astype(BF16)
    pref = jnp.dot(tri, onehot.astype(BF16), preferred_element_type=F32)
    rk = jnp.zeros((tr, LANES), jnp.int32)
    pw = jnp.zeros((tr, LANES), F32)
    oh4 = jnp.zeros((tr, LANES), F32)
    for j in range(TOP_K):
        rank = jnp.sum(jnp.where(lane == idx[j], pref, 0.0), axis=-1, keepdims=True).astype(jnp.int32)
        rk = jnp.where(lane == j, rank, rk)
        pw = jnp.where(lane == j, ex[j] / den, pw)
        oh4 = oh4 + jnp.where(lane == idx[j] + N_EXPERTS * j, 1.0, 0.0)
    oh4_ref[...] = oh4.astype(BF16)
    rk_ref[...] = rk
    pw_ref[...] = pw
    cnt_ref[0] = jnp.sum(onehot, axis=0, keepdims=True).astype(jnp.int32)


def _route(h1, mod, norm_w, wr_pad, br_pad, seq, tr, tile0, n):
    per_b = seq // tr
    nt = n // tr
    return pl.pallas_call(
        functools.partial(_route_kernel, tr=tr),
        out_shape=(jax.ShapeDtypeStruct((n, PACKED), jnp.int32),
                   jax.ShapeDtypeStruct((n, LANES), BF16),
                   jax.ShapeDtypeStruct((n, LANES), jnp.int32),
                   jax.ShapeDtypeStruct((n, LANES), F32),
                   jax.ShapeDtypeStruct((nt, 1, LANES), jnp.int32)),
        grid=(nt,),
        in_specs=[pl.BlockSpec((tr, D_MODEL), lambda i: (tile0 + i, 0)),
                  pl.BlockSpec((1, 6, D_MODEL), lambda i: ((tile0 + i) // per_b, 0, 0)),
                  pl.BlockSpec((1, D_MODEL), lambda i: (0, 0)),
                  pl.BlockSpec((D_MODEL, LANES), lambda i: (0, 0)),
                  pl.BlockSpec((1, LANES), lambda i: (0, 0))],
        out_specs=(pl.BlockSpec((tr, PACKED), lambda i: (i, 0)),
                   pl.BlockSpec((tr, LANES), lambda i: (i, 0)),
                   pl.BlockSpec((tr, LANES), lambda i: (i, 0)),
                   pl.BlockSpec((tr, LANES), lambda i: (i, 0)),
                   pl.BlockSpec((1, 1, LANES), lambda i: (i, 0, 0))),
        compiler_params=pltpu.CompilerParams(dimension_semantics=("arbitrary",),
                                             vmem_limit_bytes=VMEM_LIMIT),
        name="route",
    )(h1, mod, norm_w.reshape(1, D_MODEL), wr_pad, br_pad)


def _dest_kernel(oh4_ref, rk_ref, bt_ref, o_ref):
    oh = oh4_ref[...]
    start = (jnp.dot(oh, bt_ref[0, 0], preferred_element_type=F32)
             + 256.0 * jnp.dot(oh, bt_ref[0, 1], preferred_element_type=F32)
             + 65536.0 * jnp.dot(oh, bt_ref[0, 2], preferred_element_type=F32))
    o_ref[...] = start.astype(jnp.int32) + rk_ref[...]


def _dest(oh4, rk, bt, tr):
    n = oh4.shape[0]
    return pl.pallas_call(
        _dest_kernel,
        out_shape=jax.ShapeDtypeStruct((n, LANES), jnp.int32),
        grid=(n // tr,),
        in_specs=[pl.BlockSpec((tr, LANES), lambda i: (i, 0)),
                  pl.BlockSpec((tr, LANES), lambda i: (i, 0)),
                  pl.BlockSpec((1, 3, LANES, LANES), lambda i: (i, 0, 0, 0))],
        out_specs=pl.BlockSpec((tr, LANES), lambda i: (i, 0)),
        name="dest",
    )(oh4, rk, bt)


def _sc_workers():
    info = plsc.get_sparse_core_info()
    return info.num_cores, info.num_cores * info.num_subcores


def _sc_scatter_rows(rows, idx_slots, pad_idx, n_out):
    n_cores, n_workers = _sc_workers()
    n, w = rows.shape
    k = idx_slots.shape[0] // n
    per_worker = n // n_workers
    pad_per_worker = pad_idx.shape[0] // n_workers
    assert per_worker % SC_CHUNK == 0 and pad_per_worker % SC_CHUNK == 0
    mesh = plsc.VectorSubcoreMesh(core_axis_name="c", subcore_axis_name="s")
    zeros = jnp.zeros((SC_CHUNK, w), rows.dtype)

    @functools.partial(
        pl.kernel, mesh=mesh,
        out_type=jax.ShapeDtypeStruct((n_out, w), rows.dtype),
        scratch_types=[pltpu.VMEM((SC_CHUNK,), jnp.int32),
                       pltpu.VMEM((SC_CHUNK, w), rows.dtype),
                       pltpu.SemaphoreType.DMA],
        name="sc_scatter",
    )
    def scatter(rows_hbm, idx_hbm, pad_hbm, zeros_hbm, out_hbm, idx_v, rows_v, sem):
        wid = lax.axis_index("s") * n_cores + lax.axis_index("c")

        def body(i, carry):
            off = pl.multiple_of(wid * per_worker + i * SC_CHUNK, 8)
            pltpu.sync_copy(rows_hbm.at[pl.ds(off, SC_CHUNK)], rows_v)
            for j in range(k):
                pltpu.sync_copy(idx_hbm.at[pl.ds(pl.multiple_of(j * n + off, 8), SC_CHUNK)], idx_v)
                pltpu.async_copy(rows_v, out_hbm.at[idx_v], sem).wait()
            return carry

        lax.fori_loop(0, per_worker // SC_CHUNK, body, 0)
        pltpu.sync_copy(zeros_hbm, rows_v)

        def pad_body(i, carry):
            off = pl.multiple_of(wid * pad_per_worker + i * SC_CHUNK, 8)
            pltpu.sync_copy(pad_hbm.at[pl.ds(off, SC_CHUNK)], idx_v)
            pltpu.async_copy(rows_v, out_hbm.at[idx_v], sem).wait()
            return carry

        lax.fori_loop(0, pad_per_worker // SC_CHUNK, pad_body, 0)

    return scatter(rows, idx_slots, pad_idx, zeros)


def _experts_kernel(be_ref, nu_ref, first_ref, next_ref, slot_ref, x_ref, w1_hbm, b1_ref, w2_hbm, b2_ref,
                    o_ref, w1_f32, w2_f32, w1_bf, w2_bf, sems):
    i = pl.program_id(0)
    used = i < nu_ref[0]

    def weight_copies(e, s):
        return (pltpu.make_async_copy(w1_hbm.at[e], w1_f32.at[s], sems.at[s]),
                pltpu.make_async_copy(w2_hbm.at[e], w2_f32.at[s], sems.at[s]))

    @pl.when(used & (first_ref[i] == 1))
    def _():
        s = slot_ref[i]

        @pl.when(i == 0)
        def _():
            for cp in weight_copies(be_ref[i], s):
                cp.start()

        for cp in weight_copies(be_ref[i], s):
            cp.wait()
        cw = 256
        for c in range(2 * D_FF // cw):
            w1_bf[:, c * cw:(c + 1) * cw] = w1_f32[s, :, c * cw:(c + 1) * cw].astype(BF16)
        for c in range(D_MODEL // cw):
            w2_bf[:, c * cw:(c + 1) * cw] = w2_f32[s, :, c * cw:(c + 1) * cw].astype(BF16)

        @pl.when(next_ref[i] >= 0)
        def _():
            for cp in weight_copies(next_ref[i], 1 - s):
                cp.start()

    @pl.when(used)
    def _():
        step = 512
        for r in range(ROW_BLOCK // EXPERT_ROWS):
            rs = slice(r * EXPERT_ROWS, (r + 1) * EXPERT_ROWS)
            x = _unpack_rows(x_ref[rs, :]).astype(BF16)
            acc = None
            for j in range(D_FF // step):
                cs = slice(j * step, (j + 1) * step)
                ls = slice(D_FF + j * step, D_FF + (j + 1) * step)
                glu = jnp.dot(x, w1_bf[:, cs], preferred_element_type=F32) + b1_ref[0, :, cs]
                lin = jnp.dot(x, w1_bf[:, ls], preferred_element_type=F32) + b1_ref[0, :, ls]
                glu = jnp.minimum(glu, SWIGLU_LIMIT)
                lin = jnp.clip(lin, -SWIGLU_LIMIT, SWIGLU_LIMIT)
                act = (glu * _sigmoid(SWIGLU_ALPHA * glu) * (lin + 1.0)).astype(BF16)
                part = jnp.dot(act, w2_bf[cs, :], preferred_element_type=F32)
                acc = part if acc is None else acc + part
            o_ref[rs, :] = _pack_rows(acc + b2_ref[0])

    @pl.when(jnp.logical_not(used))
    def _():
        o_ref[...] = jnp.zeros_like(o_ref)


def _experts(plan, xs, w1, b1, w2, b2):
    n_rows = xs.shape[0]
    nb = n_rows // ROW_BLOCK
    grid_spec = pltpu.PrefetchScalarGridSpec(
        num_scalar_prefetch=5,
        grid=(nb,),
        in_specs=[pl.BlockSpec((ROW_BLOCK, PACKED), lambda i, be, nu, *_: (jnp.minimum(i, nu[0] - 1), 0)),
                  pl.BlockSpec(memory_space=pl.ANY),
                  pl.BlockSpec((1, 1, 2 * D_FF), lambda i, be, *_: (be[i], 0, 0)),
                  pl.BlockSpec(memory_space=pl.ANY),
                  pl.BlockSpec((1, 1, D_MODEL), lambda i, be, *_: (be[i], 0, 0))],
        out_specs=pl.BlockSpec((ROW_BLOCK, PACKED), lambda i, *_: (i, 0)),
        scratch_shapes=[pltpu.VMEM((2, D_MODEL, 2 * D_FF), F32), pltpu.VMEM((2, D_FF, D_MODEL), F32),
                        pltpu.VMEM((D_MODEL, 2 * D_FF), BF16), pltpu.VMEM((D_FF, D_MODEL), BF16),
                        pltpu.SemaphoreType.DMA((2,))],
    )
    return pl.pallas_call(
        _experts_kernel,
        out_shape=jax.ShapeDtypeStruct((n_rows, PACKED), jnp.int32),
        grid_spec=grid_spec,
        compiler_params=pltpu.CompilerParams(dimension_semantics=("arbitrary",),
                                             vmem_limit_bytes=VMEM_LIMIT),
        name="experts",
    )(*plan, xs, w1, b1.reshape(N_EXPERTS, 1, 2 * D_FF), w2, b2.reshape(N_EXPERTS, 1, D_MODEL))


def _sc_gather_rows(table, idx_flat):
    n_cores, n_workers = _sc_workers()
    n_idx = idx_flat.shape[0]
    w = table.shape[1]
    per_worker = n_idx // n_workers
    n_chunks = per_worker // SC_CHUNK
    assert per_worker * n_workers == n_idx and n_chunks * SC_CHUNK == per_worker and n_chunks % 2 == 0
    mesh = plsc.VectorSubcoreMesh(core_axis_name="c", subcore_axis_name="s")

    @functools.partial(
        pl.kernel, mesh=mesh,
        out_type=jax.ShapeDtypeStruct((n_idx, w), table.dtype),
        scratch_types=[pltpu.VMEM((SC_CHUNK,), jnp.int32), pltpu.VMEM((SC_CHUNK,), jnp.int32),
                       pltpu.VMEM((SC_CHUNK, w), table.dtype), pltpu.VMEM((SC_CHUNK, w), table.dtype),
                       pltpu.SemaphoreType.DMA, pltpu.SemaphoreType.DMA],
        name="sc_gather",
    )
    def gather(table_hbm, idx_hbm, out_hbm, idx0, idx1, rows0, rows1, sem0, sem1):
        wid = lax.axis_index("s") * n_cores + lax.axis_index("c")
        bufs = ((idx0, rows0, sem0), (idx1, rows1, sem1))

        def chunk_off(c):
            return pl.multiple_of(wid * per_worker + c * SC_CHUNK, 8)

        def start(c, b):
            idx_v, rows_v, sem = bufs[b]
            pltpu.sync_copy(idx_hbm.at[pl.ds(chunk_off(c), SC_CHUNK)], idx_v)
            pltpu.async_copy(table_hbm.at[idx_v], rows_v, sem)

        def finish(c, b):
            idx_v, rows_v, sem = bufs[b]
            pltpu.make_async_copy(table_hbm.at[idx_v], rows_v, sem).wait()
            pltpu.sync_copy(rows_v, out_hbm.at[pl.ds(chunk_off(c), SC_CHUNK)])

        start(0, 0)

        def body(i, carry):
            start(2 * i + 1, 1)
            finish(2 * i, 0)

            @pl.when(i < n_chunks // 2 - 1)
            def _():
                start(2 * i + 2, 0)

            finish(2 * i + 1, 1)
            return carry

        lax.fori_loop(0, n_chunks // 2, body, 0)

    return gather(table, idx_flat)


def _finish_kernel(h_ref, pw_ref, mod_ref, nw_ref, y0_ref, y1_ref, y2_ref, y3_ref, *rest):
    o_ref = rest[-1]
    pw = pw_ref[...]
    moe = pw[:, 0:1] * _unpack_rows(y0_ref[0])
    for j, y_ref in enumerate((y1_ref, y2_ref, y3_ref), start=1):
        moe = moe + pw[:, j:j + 1] * _unpack_rows(y_ref[0])
    gate_f = mod_ref[0, 5:6, :]
    o_ref[...] = _rms(h_ref[...] + gate_f * moe, nw_ref[...])


def _finish(h1, pw, mod, norm_w, y4, seq, tc, tile0, prev_out):
    n = pw.shape[0]
    per_b = seq // tc
    slot = lambda j: pl.BlockSpec((1, tc, PACKED), lambda i: (j, i, 0))
    in_specs = [pl.BlockSpec((tc, D_MODEL), lambda i: (tile0 + i, 0)),
                pl.BlockSpec((tc, LANES), lambda i: (i, 0)),
                pl.BlockSpec((1, 6, D_MODEL), lambda i: ((tile0 + i) // per_b, 0, 0)),
                pl.BlockSpec((1, D_MODEL), lambda i: (0, 0)),
                slot(0), slot(1), slot(2), slot(3)]
    args = [h1, pw, mod, norm_w.reshape(1, D_MODEL), y4, y4, y4, y4]
    aliases = {}
    if prev_out is not None:
        in_specs.append(pl.BlockSpec(memory_space=pl.ANY))
        args.append(prev_out)
        aliases = {len(args) - 1: 0}
    return pl.pallas_call(
        _finish_kernel,
        out_shape=jax.ShapeDtypeStruct(h1.shape, F32),
        grid=(n // tc,),
        in_specs=in_specs,
        out_specs=pl.BlockSpec((tc, D_MODEL), lambda i: (tile0 + i, 0)),
        input_output_aliases=aliases,
        compiler_params=pltpu.CompilerParams(dimension_semantics=("arbitrary",),
                                             vmem_limit_bytes=VMEM_LIMIT),
        name="finish",
    )(*args)


def _moe_plan(counts, n_assign):
    cnt = counts[:, 0, :N_EXPERTS]
    sizes = jnp.sum(cnt, axis=0)
    padded = (sizes + ROW_BLOCK - 1) // ROW_BLOCK * ROW_BLOCK
    pad_end = jnp.cumsum(padded)
    pad_start = pad_end - padded
    tile_base = pad_start[None, :] + jnp.cumsum(cnt, axis=0) - cnt
    digits = jnp.stack([tile_base % 256, (tile_base // 256) % 256, tile_base // 65536], axis=1)
    rows = jnp.tile(digits, (1, 1, TOP_K))
    slot_of_row = jnp.arange(LANES, dtype=jnp.int32) // N_EXPERTS
    col = jnp.arange(LANES, dtype=jnp.int32)
    bt = jnp.where(slot_of_row[:, None] == col[None, :], rows[..., None], 0).astype(BF16)
    nb = n_assign // ROW_BLOCK + N_EXPERTS
    block_start = jnp.arange(nb, dtype=jnp.int32) * ROW_BLOCK
    block_e = jnp.minimum(jnp.sum(pad_end[None, :] <= block_start[:, None], axis=1),
                          N_EXPERTS - 1).astype(jnp.int32)
    n_used = (pad_end[-1] // ROW_BLOCK).astype(jnp.int32).reshape(1)
    nonempty = padded > 0
    ids = jnp.arange(N_EXPERTS, dtype=jnp.int32)
    suffix_min = lax.cummin(jnp.where(nonempty, ids, N_EXPERTS), reverse=True)
    after = jnp.concatenate([suffix_min[1:], jnp.full((1,), N_EXPERTS, jnp.int32)])
    next_nonempty = jnp.where(after < N_EXPERTS, after, -1).astype(jnp.int32)
    slot_of = ((jnp.cumsum(nonempty) - 1) % 2).astype(jnp.int32)
    first = jnp.concatenate([jnp.ones((1,), jnp.int32),
                             (block_e[1:] != block_e[:-1]).astype(jnp.int32)])
    expert_plan = (block_e, n_used, first, next_nonempty[block_e], slot_of[block_e])
    r = jnp.arange(ROW_BLOCK, dtype=jnp.int32)[None, :]
    pad_idx = jnp.where(r < (padded - sizes)[:, None], (pad_start + sizes)[:, None] + r,
                        (nb - 1) * ROW_BLOCK + r).astype(jnp.int32).reshape(PAD_SLOTS)
    return bt, expert_plan, pad_idx


def kernel(x, c, w_ada, b_ada, norm_mix_w, w_in, hgrn_lower_bounds, hgrn_norm_w, conv_w,
           w_hgrn_out, w_conv_out, w_mix_out, norm_ffn_w, w_router, b_router, w1, b1, w2, b2,
           norm_final_w):
    bsz, seq, d = x.shape
    assert d == D_MODEL and seq % CHUNK == 0
    n = bsz * seq
    depth = w_ada.shape[0]
    tile = min(512, seq)
    assert seq % tile == 0 and n % (MOE_GROUPS * tile) == 0 and (n // MOE_GROUPS * TOP_K) % ROW_BLOCK == 0
    h = x.reshape(n, d)
    wr_pad = jnp.zeros((depth, D_MODEL, LANES), BF16).at[:, :, :N_EXPERTS].set(w_router.astype(BF16))
    br_pad = jnp.zeros((depth, 1, LANES), F32).at[:, 0, :N_EXPERTS].set(b_router)
    for layer in range(depth):
        mod = _ada(c, w_ada[layer], b_ada[layer]).reshape(bsz, 6, d)
        proj = _inproj(h, mod, norm_mix_w[layer], w_in[layer].astype(BF16), seq, tile)
        h = _mix(proj, h, mod, hgrn_lower_bounds, hgrn_norm_w[layer], conv_w[layer],
                 w_hgrn_out[layer].astype(BF16), w_conv_out[layer].astype(BF16),
                 w_mix_out[layer].astype(BF16), layer, bsz, seq, tile)
        assert layer == depth - 1, "only the last layer applies the final norm"
        ng = n // MOE_GROUPS
        n_rows = (ng * TOP_K // ROW_BLOCK + N_EXPERTS) * ROW_BLOCK
        routed = [_route(h, mod, norm_ffn_w[layer], wr_pad[layer], br_pad[layer], seq, tile,
                         grp * (ng // tile), ng) for grp in range(MOE_GROUPS)]
        plans = jax.vmap(lambda cnt: _moe_plan(cnt, ng * TOP_K))(jnp.stack([r[4] for r in routed]))
        out = None
        for grp in range(MOE_GROUPS):
            tile0 = grp * (ng // tile)
            u2, oh4, rk, pw, _ = routed[grp]
            bt, expert_plan, pad_idx = jax.tree.map(lambda a: a[grp], plans)
            dest_slots = _dest(oh4, rk, bt, tile)[:, :TOP_K].T.reshape(TOP_K * ng)
            xs = _sc_scatter_rows(u2, dest_slots, pad_idx, n_rows)
            ys = _experts(expert_plan, xs, w1[layer], b1[layer], w2[layer], b2[layer])
            y4 = _sc_gather_rows(ys, dest_slots).reshape(TOP_K, ng, PACKED)
            out = _finish(h, pw, mod, norm_final_w, y4, seq, tile, tile0, out)
        h = out
    return h.reshape(bsz, seq, d)
```

```python
import functools

import jax
import jax.numpy as jnp
from jax import lax
from jax.experimental import pallas as pl
from jax.experimental.pallas import tpu as pltpu
from jax.experimental.pallas import tpu_sc as plsc

F32 = jnp.float32
BF16 = jnp.bfloat16

D_MODEL = 1024
HGRN_HEADS = 4
HEAD_DIM = 128
HGRN_WIDTH = HGRN_HEADS * HEAD_DIM
CONV_WIDTH = 512
CONV_K = 3
CHUNK = 64
N_EXPERTS = 32
TOP_K = 4
D_FF = 1024
SWIGLU_LIMIT = 7.0
SWIGLU_ALPHA = 1.702
EPS = 1e-6
LOG2_E = 1.4426950408889634
IN_COLS = 4 * HGRN_WIDTH + 3 * CONV_WIDTH + 2 * D_MODEL
LANES = 128
ROW_BLOCK = 512
EXPERT_ROWS = 512
PACKED = D_MODEL // 2
MOE_GROUPS = 2
SC_CHUNK = 64
PAD_SLOTS = N_EXPERTS * ROW_BLOCK
VMEM_LIMIT = 56 * 1024 * 1024


def _sigmoid(x):
    return 1.0 / (1.0 + jnp.exp(-x))


def _rms(x, w):
    ms = jnp.mean(x * x, axis=-1, keepdims=True)
    return x * lax.rsqrt(ms + EPS) * w


def _pack_rows(x):
    w = x.shape[1] // 2
    lo = lax.bitcast_convert_type(x[:, :w].astype(BF16).astype(F32), jnp.uint32)
    hi = lax.bitcast_convert_type(x[:, w:].astype(BF16).astype(F32), jnp.uint32)
    return lax.bitcast_convert_type((lo >> 16) | (hi & jnp.uint32(0xFFFF0000)), jnp.int32)


def _unpack_rows(p):
    u = lax.bitcast_convert_type(p, jnp.uint32)
    lo = lax.bitcast_convert_type(u << 16, F32)
    hi = lax.bitcast_convert_type(u & jnp.uint32(0xFFFF0000), F32)
    return jnp.concatenate([lo, hi], axis=1)


def _nt_dot(a, b):
    return lax.dot_general(a, b, (((1,), (1,)), ((), ())), preferred_element_type=F32)


def _tn_dot(a, b):
    return lax.dot_general(a, b, (((0,), (0,)), ((), ())), preferred_element_type=F32)


def _ada_kernel(c_ref, w_ref, b_ref, o_ref):
    c = c_ref[...]
    sc = (c * _sigmoid(c)).astype(BF16)
    o_ref[...] = jnp.dot(sc, w_ref[...].astype(BF16), preferred_element_type=F32) + b_ref[...]


def _ada(c, w_ada, b_ada):
    bsz, d = c.shape
    n = w_ada.shape[1]
    return pl.pallas_call(
        _ada_kernel,
        out_shape=jax.ShapeDtypeStruct((bsz, n), F32),
        grid=(n // d,),
        in_specs=[pl.BlockSpec((bsz, d), lambda j: (0, 0)),
                  pl.BlockSpec((d, d), lambda j: (0, j)),
                  pl.BlockSpec((1, d), lambda j: (0, j))],
        out_specs=pl.BlockSpec((bsz, d), lambda j: (0, j)),
        name="ada",
    )(c, w_ada, b_ada.reshape(1, n))


def _inproj_kernel(x_ref, mod_ref, nw_ref, w_ref, o_ref, u_scr):
    y = _rms(x_ref[...], nw_ref[...])
    shift = mod_ref[0, 0:1, :]
    scale = mod_ref[0, 1:2, :]
    u_scr[...] = (y * (1.0 + scale) + shift).astype(BF16)
    step = 512
    for j in range(IN_COLS // step):
        o_ref[:, j * step:(j + 1) * step] = jnp.dot(
            u_scr[...], w_ref[:, j * step:(j + 1) * step], preferred_element_type=F32).astype(BF16)


def _inproj(x2, mod, norm_w, w_in_bf, seq, tm):
    n = x2.shape[0]
    per_b = seq // tm
    return pl.pallas_call(
        _inproj_kernel,
        out_shape=jax.ShapeDtypeStruct((n, IN_COLS), BF16),
        grid=(n // tm,),
        in_specs=[pl.BlockSpec((tm, D_MODEL), lambda i: (i, 0)),
                  pl.BlockSpec((1, 6, D_MODEL), lambda i: (i // per_b, 0, 0)),
                  pl.BlockSpec((1, D_MODEL), lambda i: (0, 0)),
                  pl.BlockSpec((D_MODEL, IN_COLS), lambda i: (0, 0))],
        out_specs=pl.BlockSpec((tm, IN_COLS), lambda i: (i, 0)),
        scratch_shapes=[pltpu.VMEM((tm, D_MODEL), BF16)],
        compiler_params=pltpu.CompilerParams(dimension_semantics=("arbitrary",),
                                             vmem_limit_bytes=VMEM_LIMIT),
        name="inproj",
    )(x2, mod, norm_w.reshape(1, D_MODEL), w_in_bf)


_HEADS = [slice(h * HEAD_DIM, (h + 1) * HEAD_DIM) for h in range(HGRN_HEADS)]


def _chunk_rows(c):
    if isinstance(c, int):
        return pl.ds(c * CHUNK, CHUNK)
    return pl.ds(pl.multiple_of(c * CHUNK, CHUNK), CHUNK)


def _hgrn_gates(c, lb, tri, f_ref, b_scr, k_scr, f_scr):
    fx = f_ref[_chunk_rows(c), :].astype(F32)
    f = lb + (1.0 - lb) * _sigmoid(fx)
    g = jnp.maximum(jnp.log(f), -128.0) * LOG2_E
    g1 = g.astype(BF16)
    r1 = g - g1.astype(F32)
    g2 = r1.astype(BF16)
    g3 = (r1 - g2.astype(F32)).astype(BF16)
    b_scr[...] = (jnp.dot(tri, g1, preferred_element_type=F32)
                  + jnp.dot(tri, g2, preferred_element_type=F32)
                  + jnp.dot(tri, g3, preferred_element_type=F32))
    f_scr[...] = f
    k_scr[...] = 1.0 - f


def _hgrn_scores(c, q_ref, b_scr, k_scr, f_scr):
    rows = _chunk_rows(c)
    qs = [q_ref[rows, hs].astype(F32) for hs in _HEADS]
    s_mats = [_level_scores(1, qs[h], _HEADS[h], b_scr, k_scr, f_scr) for h in range(HGRN_HEADS)]
    for lvl in range(2, 7):
        for h in range(HGRN_HEADS):
            s_mats[h] = s_mats[h] + _level_scores(lvl, qs[h], _HEADS[h], b_scr, k_scr, f_scr)
    return s_mats


def _hgrn_outputs(c, s_mats, q_ref, v_ref, st_scr, o_scr, b_scr, k_scr):
    rows = _chunk_rows(c)
    for h, hs in enumerate(_HEADS):
        q = q_ref[rows, hs].astype(F32)
        v_bf = v_ref[rows, hs]
        b = b_scr[:, hs]
        kk = k_scr[:, hs]
        b_last = b_scr[CHUNK - 1:CHUNK, hs]
        st = st_scr[h]
        qd = (q * jnp.exp2(b)).astype(BF16)
        kdec = (kk * jnp.exp2(b_last - b)).astype(BF16)
        diag = jnp.sum(q * kk, axis=-1, keepdims=True)
        o_scr[rows, hs] = (_nt_dot(qd, st.astype(BF16))
                           + jnp.dot(s_mats[h].astype(BF16), v_bf, preferred_element_type=F32)
                           + diag * v_bf.astype(F32))
        st_scr[h] = jnp.exp2(b_last) * st + _tn_dot(v_bf, kdec)


def _level_scores(lvl, q, hs, b_scr, k_scr, f_scr):
    row = lax.broadcasted_iota(jnp.int32, (CHUNK, 1), 0)
    col = lax.broadcasted_iota(jnp.int32, (1, CHUNK), 1)
    b = b_scr[:, hs]
    kk = k_scr[:, hs]
    blk = 1 << lvl
    half = blk // 2
    if lvl == 1:
        odd = (row & 1) == 1
        qx = jnp.where(odd, q * f_scr[:, hs], 0.0).astype(BF16)
        kx = jnp.where(odd, 0.0, kk).astype(BF16)
        return jnp.where((row >> 1) == (col >> 1), _nt_dot(qx, kx), 0.0)
    if half < 8:
        groups = []
        sub = lax.broadcasted_iota(jnp.int32, (8, 1), 0)
        for j in range(CHUNK // 8):
            rj = None
            for k in reversed(range(8 // blk)):
                m = 8 * j + k * blk + half - 1
                bm = jnp.broadcast_to(b_scr[m:m + 1, hs], (8, HEAD_DIM))
                rj = bm if rj is None else jnp.where(sub < (k + 1) * blk, bm, rj)
            groups.append(rj)
        ref = jnp.concatenate(groups, axis=0)
        second = (row & (blk - 1)) >= half
        qx = (q * jnp.exp2(jnp.where(second, b - ref, -jnp.inf))).astype(BF16)
        kx = (kk * jnp.exp2(jnp.where(second, -jnp.inf, ref - b))).astype(BF16)
        return jnp.where((row >> lvl) == (col >> lvl), _nt_dot(qx, kx), 0.0)
    n_blk = CHUNK // blk
    qparts, kparts = [], []
    for j in range(n_blk):
        m = j * blk + half - 1
        bm = b_scr[m:m + 1, hs]
        tq = slice(j * blk + half, (j + 1) * blk)
        tk = slice(j * blk, j * blk + half)
        qparts.append(q[tq] * jnp.exp2(b[tq] - bm))
        kparts.append(kk[tk] * jnp.exp2(bm - b[tk]))
        kparts.append(jnp.zeros((half, HEAD_DIM), F32))
    qx = jnp.concatenate(qparts, axis=0).astype(BF16)
    kx = jnp.concatenate(kparts, axis=0).astype(BF16)
    sc = _nt_dot(qx, kx)
    if n_blk > 1:
        crow = lax.broadcasted_iota(jnp.int32, (CHUNK // 2, 1), 0)
        sc = jnp.where((crow // half) == (col >> lvl), sc, 0.0)
    pieces = []
    for j in range(n_blk):
        pieces.append(jnp.zeros((half, CHUNK), F32))
        pieces.append(sc[j * half:(j + 1) * half])
    return jnp.concatenate(pieces, axis=0)


def _mix_kernel(lbt_ref, q_ref, f_ref, i_ref, g_ref, cb_ref, cc_ref, ch_ref, ga0_ref, ga1_ref,
                gb0_ref, gb1_ref, x_ref, mod_ref, hnw_ref, cw_ref, wa_ref, wb_ref, wm_ref,
                o_ref, st_scr, o_scr, carry_scr, b_scr, k_scr, f_scr, *, layer, rows_per_step, epi_rows):
    @pl.when(pl.program_id(1) == 0)
    def _():
        st_scr[...] = jnp.zeros_like(st_scr)
        carry_scr[...] = jnp.zeros_like(carry_scr)

    tab = lbt_ref[...]
    tmax = jnp.max(tab, axis=0, keepdims=True)
    te = jnp.exp(tab - tmax)
    lb = jnp.sum(te[0:layer + 1], axis=0, keepdims=True) / jnp.sum(te, axis=0, keepdims=True)

    ri = lax.broadcasted_iota(jnp.int32, (CHUNK, CHUNK), 0)
    ci = lax.broadcasted_iota(jnp.int32, (CHUNK, CHUNK), 1)
    tri = jnp.where(ci <= ri, 1.0, 0.0).astype(BF16)

    n_chunks = rows_per_step // CHUNK

    def slot(s):
        return b_scr.at[s], k_scr.at[s], f_scr.at[s]

    def chunk(c, c_next, s):
        bs, ks, fs = slot(s)
        s_mats = _hgrn_scores(c, q_ref, bs, ks, fs)
        _hgrn_gates(c_next, lb, tri, f_ref, *slot(1 - s))
        _hgrn_outputs(c, s_mats, q_ref, i_ref, st_scr, o_scr, bs, ks)

    def pair_body(i, carry):
        chunk(2 * i, 2 * i + 1, 0)
        chunk(2 * i + 1, jnp.minimum(2 * i + 2, n_chunks - 1), 1)
        return carry

    _hgrn_gates(0, lb, tri, f_ref, *slot(0))
    lax.fori_loop(0, n_chunks // 2, pair_body, 0)

    gate_m = mod_ref[0, 2:3, :]
    hnw = hnw_ref[...]
    cw0 = cw_ref[0:1, :]
    cw1 = cw_ref[1:2, :]
    cw2 = cw_ref[2:3, :]
    for r in range(rows_per_step // epi_rows):
        rs = slice(r * epi_rows, (r + 1) * epi_rows)
        parts = []
        for h in range(HGRN_HEADS):
            hs = slice(h * HEAD_DIM, (h + 1) * HEAD_DIM)
            oh = _rms(o_scr[rs, hs], hnw)
            go = g_ref[rs, hs].astype(F32)
            parts.append((oh * (go * _sigmoid(go))).astype(BF16))
        ya = jnp.dot(jnp.concatenate(parts, axis=1), wa_ref[...], preferred_element_type=F32)
        uc = cc_ref[rs, :].astype(F32) * ch_ref[rs, :].astype(F32)
        prev = carry_scr[...]
        rowi = lax.broadcasted_iota(jnp.int32, (epi_rows, 1), 0)
        s1 = jnp.where(rowi == 0, prev[7:8, :], pltpu.roll(uc, 1, 0))
        s2 = pltpu.roll(uc, 2, 0)
        s2 = jnp.where(rowi == 0, prev[6:7, :], jnp.where(rowi == 1, prev[7:8, :], s2))
        carry_scr[...] = uc[epi_rows - 8:epi_rows, :]
        yc = cb_ref[rs, :].astype(F32) * (cw2 * uc + cw1 * s1 + cw0 * s2)
        yb = jnp.dot(yc.astype(BF16), wb_ref[...], preferred_element_type=F32)
        ga = jnp.concatenate([ga0_ref[rs, :], ga1_ref[rs, :]], axis=1).astype(F32)
        gb = jnp.concatenate([gb0_ref[rs, :], gb1_ref[rs, :]], axis=1).astype(F32)
        merged = (_sigmoid(ga) * ya + _sigmoid(gb) * yb).astype(BF16)
        o_ref[rs, :] = x_ref[rs, :] + gate_m * jnp.dot(merged, wm_ref[...], preferred_element_type=F32)


def _mix(proj, x2, mod, lb_table, hgrn_norm_w, conv_w, wa, wb, wm, layer, bsz, seq, tt):
    n = x2.shape[0]
    per_b = seq // tt

    def col(col_block):
        return pl.BlockSpec((tt, 512), lambda b, t: (b * per_b + t, col_block))

    const = lambda shape: pl.BlockSpec(shape, lambda b, t: (0,) * len(shape))
    in_specs = [
        const(lb_table.shape),
        col(0), col(1), col(2), col(3),
        col(4), col(5), col(6),
        col(7), col(8), col(9), col(10),
        pl.BlockSpec((tt, D_MODEL), lambda b, t: (b * per_b + t, 0)),
        pl.BlockSpec((1, 6, D_MODEL), lambda b, t: (b, 0, 0)),
        const((1, HEAD_DIM)), const((CONV_K, CONV_WIDTH)),
        const((HGRN_WIDTH, D_MODEL)), const((CONV_WIDTH, D_MODEL)), const((D_MODEL, D_MODEL)),
    ]
    kern = functools.partial(_mix_kernel, layer=layer, rows_per_step=tt, epi_rows=tt)
    return pl.pallas_call(
        kern,
        out_shape=jax.ShapeDtypeStruct((n, D_MODEL), F32),
        grid=(bsz, per_b),
        in_specs=in_specs,
        out_specs=pl.BlockSpec((tt, D_MODEL), lambda b, t: (b * per_b + t, 0)),
        scratch_shapes=[pltpu.VMEM((HGRN_HEADS, HEAD_DIM, HEAD_DIM), F32),
                        pltpu.VMEM((tt, HGRN_WIDTH), F32),
                        pltpu.VMEM((8, CONV_WIDTH), F32),
                        pltpu.VMEM((2, CHUNK, HGRN_WIDTH), F32),
                        pltpu.VMEM((2, CHUNK, HGRN_WIDTH), F32),
                        pltpu.VMEM((2, CHUNK, HGRN_WIDTH), F32)],
        compiler_params=pltpu.CompilerParams(dimension_semantics=("arbitrary", "arbitrary"),
                                             vmem_limit_bytes=VMEM_LIMIT),
        name="mix",
    )(lb_table, *([proj] * 11),
      x2, mod, hgrn_norm_w.reshape(1, HEAD_DIM), conv_w, wa, wb, wm)


def _route_kernel(h_ref, mod_ref, nw_ref, wr_ref, br_ref, u_ref, oh4_ref, rk_ref, pw_ref, cnt_ref, *, tr):
    shift = mod_ref[0, 3:4, :]
    scale = mod_ref[0, 4:5, :]
    u = _rms(h_ref[...], nw_ref[...]) * (1.0 + scale) + shift
    u_ref[...] = _pack_rows(u)
    lane = lax.broadcasted_iota(jnp.int32, (tr, LANES), 1).astype(F32)
    logits = jnp.dot(u.astype(BF16), wr_ref[...], preferred_element_type=F32) + br_ref[...]
    logits = jnp.where(lane < N_EXPERTS, logits, -jnp.inf)
    idx, val = [], []
    cur = logits
    for _ in range(TOP_K):
        m = jnp.max(cur, axis=-1, keepdims=True)
        i = jnp.min(jnp.where(cur == m, lane, float(LANES)), axis=-1, keepdims=True)
        idx.append(i)
        val.append(m)
        cur = jnp.where(lane == i, -jnp.inf, cur)
    ex = [jnp.exp(v - val[0]) for v in val]
    den = ex[0] + ex[1] + ex[2] + ex[3]
    onehot = jnp.zeros((tr, LANES), F32)
    for i in idx:
        onehot = onehot + jnp.where(lane == i, 1.0, 0.0)
    ri = lax.broadcasted_iota(jnp.int32, (tr, tr), 0)
    ci = lax.broadcasted_iota(jnp.int32, (tr, tr), 1)
    tri = jnp.where(ci < ri, 1.0, 0.0).astype(BF16)
    pref = jnp.dot(tri, onehot.astype(BF16), preferred_element_type=F32)
    rk = jnp.zeros((tr, LANES), jnp.int32)
    pw = jnp.zeros((tr, LANES), F32)
    oh4 = jnp.zeros((tr, LANES), F32)
    for j in range(TOP_K):
        rank = jnp.sum(jnp.where(lane == idx[j], pref, 0.0), axis=-1, keepdims=True).astype(jnp.int32)
        rk = jnp.where(lane == j, rank, rk)
        pw = jnp.where(lane == j, ex[j] / den, pw)
        oh4 = oh4 + jnp.where(lane == idx[j] + N_EXPERTS * j, 1.0, 0.0)
    oh4_ref[...] = oh4.astype(BF16)
    rk_ref[...] = rk
    pw_ref[...] = pw
    cnt_ref[0] = jnp.sum(onehot, axis=0, keepdims=True).astype(jnp.int32)


def _route(h1, mod, norm_w, wr_pad, br_pad, seq, tr, tile0, n):
    per_b = seq // tr
    nt = n // tr
    return pl.pallas_call(
        functools.partial(_route_kernel, tr=tr),
        out_shape=(jax.ShapeDtypeStruct((n, PACKED), jnp.int32),
                   jax.ShapeDtypeStruct((n, LANES), BF16),
                   jax.ShapeDtypeStruct((n, LANES), jnp.int32),
                   jax.ShapeDtypeStruct((n, LANES), F32),
                   jax.ShapeDtypeStruct((nt, 1, LANES), jnp.int32)),
        grid=(nt,),
        in_specs=[pl.BlockSpec((tr, D_MODEL), lambda i: (tile0 + i, 0)),
                  pl.BlockSpec((1, 6, D_MODEL), lambda i: ((tile0 + i) // per_b, 0, 0)),
                  pl.BlockSpec((1, D_MODEL), lambda i: (0, 0)),
                  pl.BlockSpec((D_MODEL, LANES), lambda i: (0, 0)),
                  pl.BlockSpec((1, LANES), lambda i: (0, 0))],
        out_specs=(pl.BlockSpec((tr, PACKED), lambda i: (i, 0)),
                   pl.BlockSpec((tr, LANES), lambda i: (i, 0)),
                   pl.BlockSpec((tr, LANES), lambda i: (i, 0)),
                   pl.BlockSpec((tr, LANES), lambda i: (i, 0)),
                   pl.BlockSpec((1, 1, LANES), lambda i: (i, 0, 0))),
        compiler_params=pltpu.CompilerParams(dimension_semantics=("arbitrary",),
                                             vmem_limit_bytes=VMEM_LIMIT),
        name="route",
    )(h1, mod, norm_w.reshape(1, D_MODEL), wr_pad, br_pad)


def _dest_kernel(oh4_ref, rk_ref, bt_ref, o_ref):
    oh = oh4_ref[...]
    start = (jnp.dot(oh, bt_ref[0, 0], preferred_element_type=F32)
             + 256.0 * jnp.dot(oh, bt_ref[0, 1], preferred_element_type=F32)
             + 65536.0 * jnp.dot(oh, bt_ref[0, 2], preferred_element_type=F32))
    o_ref[...] = (start + rk_ref[...].astype(F32)).T[:8, :].astype(jnp.int32)


def _dest(oh4, rk, bt, tr):
    n = oh4.shape[0]
    return pl.pallas_call(
        _dest_kernel,
        out_shape=jax.ShapeDtypeStruct((8, n), jnp.int32),
        grid=(n // tr,),
        in_specs=[pl.BlockSpec((tr, LANES), lambda i: (i, 0)),
                  pl.BlockSpec((tr, LANES), lambda i: (i, 0)),
                  pl.BlockSpec((1, 3, LANES, LANES), lambda i: (i, 0, 0, 0))],
        out_specs=pl.BlockSpec((8, tr), lambda i: (0, i)),
        name="dest",
    )(oh4, rk, bt)


def _sc_workers():
    info = plsc.get_sparse_core_info()
    return info.num_cores, info.num_cores * info.num_subcores


def _sc_scatter_rows(rows, idx_slots, pad_idx, n_out):
    n_cores, n_workers = _sc_workers()
    n, w = rows.shape
    k = idx_slots.shape[0] // n
    per_worker = n // n_workers
    pad_per_worker = pad_idx.shape[0] // n_workers
    assert per_worker % SC_CHUNK == 0 and pad_per_worker % SC_CHUNK == 0
    mesh = plsc.VectorSubcoreMesh(core_axis_name="c", subcore_axis_name="s")
    zeros = jnp.zeros((SC_CHUNK, w), rows.dtype)

    @functools.partial(
        pl.kernel, mesh=mesh,
        out_type=jax.ShapeDtypeStruct((n_out, w), rows.dtype),
        scratch_types=[pltpu.VMEM((SC_CHUNK,), jnp.int32),
                       pltpu.VMEM((SC_CHUNK, w), rows.dtype),
                       pltpu.SemaphoreType.DMA],
        name="sc_scatter",
    )
    def scatter(rows_hbm, idx_hbm, pad_hbm, zeros_hbm, out_hbm, idx_v, rows_v, sem):
        wid = lax.axis_index("s") * n_cores + lax.axis_index("c")

        def body(i, carry):
            off = pl.multiple_of(wid * per_worker + i * SC_CHUNK, 8)
            pltpu.sync_copy(rows_hbm.at[pl.ds(off, SC_CHUNK)], rows_v)
            for j in range(k):
                pltpu.sync_copy(idx_hbm.at[pl.ds(pl.multiple_of(j * n + off, 8), SC_CHUNK)], idx_v)
                pltpu.async_copy(rows_v, out_hbm.at[idx_v], sem).wait()
            return carry

        lax.fori_loop(0, per_worker // SC_CHUNK, body, 0)
        pltpu.sync_copy(zeros_hbm, rows_v)

        def pad_body(i, carry):
            off = pl.multiple_of(wid * pad_per_worker + i * SC_CHUNK, 8)
            pltpu.sync_copy(pad_hbm.at[pl.ds(off, SC_CHUNK)], idx_v)
            pltpu.async_copy(rows_v, out_hbm.at[idx_v], sem).wait()
            return carry

        lax.fori_loop(0, pad_per_worker // SC_CHUNK, pad_body, 0)

    return scatter(rows, idx_slots, pad_idx, zeros)


def _experts_kernel(be_ref, nu_ref, first_ref, next_ref, slot_ref, x_ref, w1_hbm, b1_ref, w2_hbm, b2_ref,
                    o_ref, w1_f32, w2_f32, w1_bf, w2_bf, sems):
    i = pl.program_id(0)
    used = i < nu_ref[0]

    def weight_copies(e, s):
        return (pltpu.make_async_copy(w1_hbm.at[e], w1_f32.at[s], sems.at[s]),
                pltpu.make_async_copy(w2_hbm.at[e], w2_f32.at[s], sems.at[s]))

    @pl.when(used & (first_ref[i] == 1))
    def _():
        s = slot_ref[i]

        @pl.when(i == 0)
        def _():
            for cp in weight_copies(be_ref[i], s):
                cp.start()

        for cp in weight_copies(be_ref[i], s):
            cp.wait()
        cw = 256
        for c in range(2 * D_FF // cw):
            w1_bf[:, c * cw:(c + 1) * cw] = w1_f32[s, :, c * cw:(c + 1) * cw].astype(BF16)
        for c in range(D_MODEL // cw):
            w2_bf[:, c * cw:(c + 1) * cw] = w2_f32[s, :, c * cw:(c + 1) * cw].astype(BF16)

        @pl.when(next_ref[i] >= 0)
        def _():
            for cp in weight_copies(next_ref[i], 1 - s):
                cp.start()

    @pl.when(used)
    def _():
        step = 512
        for r in range(ROW_BLOCK // EXPERT_ROWS):
            rs = slice(r * EXPERT_ROWS, (r + 1) * EXPERT_ROWS)
            x = _unpack_rows(x_ref[rs, :]).astype(BF16)
            acc = None
            for j in range(D_FF // step):
                cs = slice(j * step, (j + 1) * step)
                ls = slice(D_FF + j * step, D_FF + (j + 1) * step)
                glu = jnp.dot(x, w1_bf[:, cs], preferred_element_type=F32) + b1_ref[0, :, cs]
                lin = jnp.dot(x, w1_bf[:, ls], preferred_element_type=F32) + b1_ref[0, :, ls]
                glu = jnp.minimum(glu, SWIGLU_LIMIT)
                lin = jnp.clip(lin, -SWIGLU_LIMIT, SWIGLU_LIMIT)
                act = (glu * _sigmoid(SWIGLU_ALPHA * glu) * (lin + 1.0)).astype(BF16)
                part = jnp.dot(act, w2_bf[cs, :], preferred_element_type=F32)
                acc = part if acc is None else acc + part
            o_ref[rs, :] = _pack_rows(acc + b2_ref[0])

    @pl.when(jnp.logical_not(used))
    def _():
        o_ref[...] = jnp.zeros_like(o_ref)


def _experts(plan, xs, w1, b1, w2, b2):
    n_rows = xs.shape[0]
    nb = n_rows // ROW_BLOCK
    grid_spec = pltpu.PrefetchScalarGridSpec(
        num_scalar_prefetch=5,
        grid=(nb,),
        in_specs=[pl.BlockSpec((ROW_BLOCK, PACKED), lambda i, be, nu, *_: (jnp.minimum(i, nu[0] - 1), 0)),
                  pl.BlockSpec(memory_space=pl.ANY),
                  pl.BlockSpec((1, 1, 2 * D_FF), lambda i, be, *_: (be[i], 0, 0)),
                  pl.BlockSpec(memory_space=pl.ANY),
                  pl.BlockSpec((1, 1, D_MODEL), lambda i, be, *_: (be[i], 0, 0))],
        out_specs=pl.BlockSpec((ROW_BLOCK, PACKED), lambda i, *_: (i, 0)),
        scratch_shapes=[pltpu.VMEM((2, D_MODEL, 2 * D_FF), F32), pltpu.VMEM((2, D_FF, D_MODEL), F32),
                        pltpu.VMEM((D_MODEL, 2 * D_FF), BF16), pltpu.VMEM((D_FF, D_MODEL), BF16),
                        pltpu.SemaphoreType.DMA((2,))],
    )
    return pl.pallas_call(
        _experts_kernel,
        out_shape=jax.ShapeDtypeStruct((n_rows, PACKED), jnp.int32),
        grid_spec=grid_spec,
        compiler_params=pltpu.CompilerParams(dimension_semantics=("arbitrary",),
                                             vmem_limit_bytes=VMEM_LIMIT),
        name="experts",
    )(*plan, xs, w1, b1.reshape(N_EXPERTS, 1, 2 * D_FF), w2, b2.reshape(N_EXPERTS, 1, D_MODEL))


def _sc_gather_rows(table, idx_flat):
    n_cores, n_workers = _sc_workers()
    n_idx = idx_flat.shape[0]
    w = table.shape[1]
    per_worker = n_idx // n_workers
    n_chunks = per_worker // SC_CHUNK
    assert per_worker * n_workers == n_idx and n_chunks * SC_CHUNK == per_worker and n_chunks % 2 == 0
    mesh = plsc.VectorSubcoreMesh(core_axis_name="c", subcore_axis_name="s")

    @functools.partial(
        pl.kernel, mesh=mesh,
        out_type=jax.ShapeDtypeStruct((n_idx, w), table.dtype),
        scratch_types=[pltpu.VMEM((SC_CHUNK,), jnp.int32), pltpu.VMEM((SC_CHUNK,), jnp.int32),
                       pltpu.VMEM((SC_CHUNK, w), table.dtype), pltpu.VMEM((SC_CHUNK, w), table.dtype),
                       pltpu.SemaphoreType.DMA, pltpu.SemaphoreType.DMA],
        name="sc_gather",
    )
    def gather(table_hbm, idx_hbm, out_hbm, idx0, idx1, rows0, rows1, sem0, sem1):
        wid = lax.axis_index("s") * n_cores + lax.axis_index("c")
        bufs = ((idx0, rows0, sem0), (idx1, rows1, sem1))

        def chunk_off(c):
            return pl.multiple_of(wid * per_worker + c * SC_CHUNK, 8)

        def start(c, b):
            idx_v, rows_v, sem = bufs[b]
            pltpu.sync_copy(idx_hbm.at[pl.ds(chunk_off(c), SC_CHUNK)], idx_v)
            pltpu.async_copy(table_hbm.at[idx_v], rows_v, sem)

        def finish(c, b):
            idx_v, rows_v, sem = bufs[b]
            pltpu.make_async_copy(table_hbm.at[idx_v], rows_v, sem).wait()
            pltpu.sync_copy(rows_v, out_hbm.at[pl.ds(chunk_off(c), SC_CHUNK)])

        start(0, 0)

        def body(i, carry):
            start(2 * i + 1, 1)
            finish(2 * i, 0)

            @pl.when(i < n_chunks // 2 - 1)
            def _():
                start(2 * i + 2, 0)

            finish(2 * i + 1, 1)
            return carry

        lax.fori_loop(0, n_chunks // 2, body, 0)

    return gather(table, idx_flat)


def _finish_kernel(h_ref, pw_ref, mod_ref, nw_ref, y0_ref, y1_ref, y2_ref, y3_ref, *rest):
    o_ref = rest[-1]
    pw = pw_ref[...]
    moe = pw[:, 0:1] * _unpack_rows(y0_ref[0])
    for j, y_ref in enumerate((y1_ref, y2_ref, y3_ref), start=1):
        moe = moe + pw[:, j:j + 1] * _unpack_rows(y_ref[0])
    gate_f = mod_ref[0, 5:6, :]
    o_ref[...] = _rms(h_ref[...] + gate_f * moe, nw_ref[...])


def _finish(h1, pw, mod, norm_w, y4, seq, tc, tile0, prev_out):
    n = pw.shape[0]
    per_b = seq // tc
    slot = lambda j: pl.BlockSpec((1, tc, PACKED), lambda i: (j, i, 0))
    in_specs = [pl.BlockSpec((tc, D_MODEL), lambda i: (tile0 + i, 0)),
                pl.BlockSpec((tc, LANES), lambda i: (i, 0)),
                pl.BlockSpec((1, 6, D_MODEL), lambda i: ((tile0 + i) // per_b, 0, 0)),
                pl.BlockSpec((1, D_MODEL), lambda i: (0, 0)),
                slot(0), slot(1), slot(2), slot(3)]
    args = [h1, pw, mod, norm_w.reshape(1, D_MODEL), y4, y4, y4, y4]
    aliases = {}
    if prev_out is not None:
        in_specs.append(pl.BlockSpec(memory_space=pl.ANY))
        args.append(prev_out)
        aliases = {len(args) - 1: 0}
    return pl.pallas_call(
        _finish_kernel,
        out_shape=jax.ShapeDtypeStruct(h1.shape, F32),
        grid=(n // tc,),
        in_specs=in_specs,
        out_specs=pl.BlockSpec((tc, D_MODEL), lambda i: (tile0 + i, 0)),
        input_output_aliases=aliases,
        compiler_params=pltpu.CompilerParams(dimension_semantics=("arbitrary",),
                                             vmem_limit_bytes=VMEM_LIMIT),
        name="finish",
    )(*args)


def _moe_plan(counts, n_assign):
    cnt = counts[:, 0, :N_EXPERTS]
    sizes = jnp.sum(cnt, axis=0)
    padded = (sizes + ROW_BLOCK - 1) // ROW_BLOCK * ROW_BLOCK
    pad_end = jnp.cumsum(padded)
    pad_start = pad_end - padded
    tile_base = pad_start[None, :] + jnp.cumsum(cnt, axis=0) - cnt
    digits = jnp.stack([tile_base % 256, (tile_base // 256) % 256, tile_base // 65536], axis=1)
    rows = jnp.tile(digits, (1, 1, TOP_K))
    slot_of_row = jnp.arange(LANES, dtype=jnp.int32) // N_EXPERTS
    col = jnp.arange(LANES, dtype=jnp.int32)
    bt = jnp.where(slot_of_row[:, None] == col[None, :], rows[..., None], 0).astype(BF16)
    nb = n_assign // ROW_BLOCK + N_EXPERTS
    block_start = jnp.arange(nb, dtype=jnp.int32) * ROW_BLOCK
    block_e = jnp.minimum(jnp.sum(pad_end[None, :] <= block_start[:, None], axis=1),
                          N_EXPERTS - 1).astype(jnp.int32)
    n_used = (pad_end[-1] // ROW_BLOCK).astype(jnp.int32).reshape(1)
    nonempty = padded > 0
    ids = jnp.arange(N_EXPERTS, dtype=jnp.int32)
    suffix_min = lax.cummin(jnp.where(nonempty, ids, N_EXPERTS), reverse=True)
    after = jnp.concatenate([suffix_min[1:], jnp.full((1,), N_EXPERTS, jnp.int32)])
    next_nonempty = jnp.where(after < N_EXPERTS, after, -1).astype(jnp.int32)
    slot_of = ((jnp.cumsum(nonempty) - 1) % 2).astype(jnp.int32)
    first = jnp.concatenate([jnp.ones((1,), jnp.int32),
                             (block_e[1:] != block_e[:-1]).astype(jnp.int32)])
    expert_plan = (block_e, n_used, first, next_nonempty[block_e], slot_of[block_e])
    r = jnp.arange(ROW_BLOCK, dtype=jnp.int32)[None, :]
    pad_idx = jnp.where(r < (padded - sizes)[:, None], (pad_start + sizes)[:, None] + r,
                        (nb - 1) * ROW_BLOCK + r).astype(jnp.int32).reshape(PAD_SLOTS)
    return bt, expert_plan, pad_idx


def kernel(x, c, w_ada, b_ada, norm_mix_w, w_in, hgrn_lower_bounds, hgrn_norm_w, conv_w,
           w_hgrn_out, w_conv_out, w_mix_out, norm_ffn_w, w_router, b_router, w1, b1, w2, b2,
           norm_final_w):
    bsz, seq, d = x.shape
    assert d == D_MODEL and seq % CHUNK == 0
    n = bsz * seq
    depth = w_ada.shape[0]
    tile = min(512, seq)
    assert seq % tile == 0 and n % (MOE_GROUPS * tile) == 0 and (n // MOE_GROUPS * TOP_K) % ROW_BLOCK == 0
    h = x.reshape(n, d)
    wr_pad = jnp.zeros((depth, D_MODEL, LANES), BF16).at[:, :, :N_EXPERTS].set(w_router.astype(BF16))
    br_pad = jnp.zeros((depth, 1, LANES), F32).at[:, 0, :N_EXPERTS].set(b_router)
    for layer in range(depth):
        mod = _ada(c, w_ada[layer], b_ada[layer]).reshape(bsz, 6, d)
        proj = _inproj(h, mod, norm_mix_w[layer], w_in[layer].astype(BF16), seq, tile)
        h = _mix(proj, h, mod, hgrn_lower_bounds, hgrn_norm_w[layer], conv_w[layer],
                 w_hgrn_out[layer].astype(BF16), w_conv_out[layer].astype(BF16),
                 w_mix_out[layer].astype(BF16), layer, bsz, seq, tile)
        assert layer == depth - 1, "only the last layer applies the final norm"
        ng = n // MOE_GROUPS
        n_rows = (ng * TOP_K // ROW_BLOCK + N_EXPERTS) * ROW_BLOCK
        routed = [_route(h, mod, norm_ffn_w[layer], wr_pad[layer], br_pad[layer], seq, tile,
                         grp * (ng // tile), ng) for grp in range(MOE_GROUPS)]
        plans = jax.vmap(lambda cnt: _moe_plan(cnt, ng * TOP_K))(jnp.stack([r[4] for r in routed]))
        out = None
        for grp in range(MOE_GROUPS):
            tile0 = grp * (ng // tile)
            u2, oh4, rk, pw, _ = routed[grp]
            bt, expert_plan, pad_idx = jax.tree.map(lambda a: a[grp], plans)
            dest_slots = _dest(oh4, rk, bt, tile)[:TOP_K].reshape(TOP_K * ng)
            xs = _sc_scatter_rows(u2, dest_slots, pad_idx, n_rows)
            ys = _experts(expert_plan, xs, w1[layer], b1[layer], w2[layer], b2[layer])
            y4 = _sc_gather_rows(ys, dest_slots).reshape(TOP_K, ng, PACKED)
            out = _finish(h, pw, mod, norm_final_w, y4, seq, tile, tile0, out)
        h = out
    return h.reshape(bsz, seq, d)
```

```python
import functools

import jax
import jax.numpy as jnp
from jax import lax
from jax.experimental import pallas as pl
from jax.experimental.pallas import tpu as pltpu
from jax.experimental.pallas import tpu_sc as plsc

F32 = jnp.float32
BF16 = jnp.bfloat16

D_MODEL = 1024
HGRN_HEADS = 4
HEAD_DIM = 128
HGRN_WIDTH = HGRN_HEADS * HEAD_DIM
CONV_WIDTH = 512
CONV_K = 3
CHUNK = 64
N_EXPERTS = 32
TOP_K = 4
D_FF = 1024
SWIGLU_LIMIT = 7.0
SWIGLU_ALPHA = 1.702
EPS = 1e-6
LOG2_E = 1.4426950408889634
IN_COLS = 4 * HGRN_WIDTH + 3 * CONV_WIDTH + 2 * D_MODEL
LANES = 128
ROW_BLOCK = 512
EXPERT_ROWS = 512
PACKED = D_MODEL // 2
MOE_GROUPS = 2
SC_CHUNK = 64
PAD_SLOTS = N_EXPERTS * ROW_BLOCK
VMEM_LIMIT = 56 * 1024 * 1024


def _sigmoid(x, scale=1.0):
    return 1.0 / (1.0 + jnp.exp2(x * (-scale * LOG2_E)))


def _rms(x, w):
    ms = jnp.mean(x * x, axis=-1, keepdims=True)
    return x * lax.rsqrt(ms + EPS) * w


def _pack_rows(x):
    w = x.shape[1] // 2
    lo = lax.bitcast_convert_type(x[:, :w].astype(BF16).astype(F32), jnp.uint32)
    hi = lax.bitcast_convert_type(x[:, w:].astype(BF16).astype(F32), jnp.uint32)
    return lax.bitcast_convert_type((lo >> 16) | (hi & jnp.uint32(0xFFFF0000)), jnp.int32)


def _unpack_rows(p):
    u = lax.bitcast_convert_type(p, jnp.uint32)
    lo = lax.bitcast_convert_type(u << 16, F32)
    hi = lax.bitcast_convert_type(u & jnp.uint32(0xFFFF0000), F32)
    return jnp.concatenate([lo, hi], axis=1)


def _nt_dot(a, b):
    return lax.dot_general(a, b, (((1,), (1,)), ((), ())), preferred_element_type=F32)


def _tn_dot(a, b):
    return lax.dot_general(a, b, (((0,), (0,)), ((), ())), preferred_element_type=F32)


def _ada_kernel(c_ref, w_ref, b_ref, o_ref):
    c = c_ref[...]
    sc = (c * _sigmoid(c)).astype(BF16)
    o_ref[...] = jnp.dot(sc, w_ref[...].astype(BF16), preferred_element_type=F32) + b_ref[...]


def _ada(c, w_ada, b_ada):
    bsz, d = c.shape
    n = w_ada.shape[1]
    return pl.pallas_call(
        _ada_kernel,
        out_shape=jax.ShapeDtypeStruct((bsz, n), F32),
        grid=(n // d,),
        in_specs=[pl.BlockSpec((bsz, d), lambda j: (0, 0)),
                  pl.BlockSpec((d, d), lambda j: (0, j)),
                  pl.BlockSpec((1, d), lambda j: (0, j))],
        out_specs=pl.BlockSpec((bsz, d), lambda j: (0, j)),
        name="ada",
    )(c, w_ada, b_ada.reshape(1, n))


def _inproj_kernel(x_ref, mod_ref, nw_ref, w_ref, o_ref, u_scr):
    y = _rms(x_ref[...], nw_ref[...])
    shift = mod_ref[0, 0:1, :]
    scale = mod_ref[0, 1:2, :]
    u_scr[...] = (y * (1.0 + scale) + shift).astype(BF16)
    step = 512
    for j in range(IN_COLS // step):
        o_ref[:, j * step:(j + 1) * step] = jnp.dot(
            u_scr[...], w_ref[:, j * step:(j + 1) * step], preferred_element_type=F32).astype(BF16)


def _inproj(x2, mod, norm_w, w_in_bf, seq, tm):
    n = x2.shape[0]
    per_b = seq // tm
    return pl.pallas_call(
        _inproj_kernel,
        out_shape=jax.ShapeDtypeStruct((n, IN_COLS), BF16),
        grid=(n // tm,),
        in_specs=[pl.BlockSpec((tm, D_MODEL), lambda i: (i, 0)),
                  pl.BlockSpec((1, 6, D_MODEL), lambda i: (i // per_b, 0, 0)),
                  pl.BlockSpec((1, D_MODEL), lambda i: (0, 0)),
                  pl.BlockSpec((D_MODEL, IN_COLS), lambda i: (0, 0))],
        out_specs=pl.BlockSpec((tm, IN_COLS), lambda i: (i, 0)),
        scratch_shapes=[pltpu.VMEM((tm, D_MODEL), BF16)],
        compiler_params=pltpu.CompilerParams(dimension_semantics=("arbitrary",),
                                             vmem_limit_bytes=VMEM_LIMIT),
        name="inproj",
    )(x2, mod, norm_w.reshape(1, D_MODEL), w_in_bf)


_HEADS = [slice(h * HEAD_DIM, (h + 1) * HEAD_DIM) for h in range(HGRN_HEADS)]


def _chunk_rows(c):
    if isinstance(c, int):
        return pl.ds(c * CHUNK, CHUNK)
    return pl.ds(pl.multiple_of(c * CHUNK, CHUNK), CHUNK)


def _hgrn_gates(c, lb, tri, f_ref, b_scr, k_scr, f_scr):
    fx = f_ref[_chunk_rows(c), :].astype(F32)
    f = lb + (1.0 - lb) * _sigmoid(fx)
    g = jnp.maximum(jnp.log(f), -128.0) * LOG2_E
    g1 = g.astype(BF16)
    r1 = g - g1.astype(F32)
    g2 = r1.astype(BF16)
    g3 = (r1 - g2.astype(F32)).astype(BF16)
    b_scr[...] = (jnp.dot(tri, g1, preferred_element_type=F32)
                  + jnp.dot(tri, g2, preferred_element_type=F32)
                  + jnp.dot(tri, g3, preferred_element_type=F32))
    f_scr[...] = f
    k_scr[...] = 1.0 - f


def _hgrn_scores(c, q_ref, b_scr, k_scr, f_scr):
    rows = _chunk_rows(c)
    qs = [q_ref[rows, hs].astype(F32) for hs in _HEADS]
    s_mats = [_level_scores(1, qs[h], _HEADS[h], b_scr, k_scr, f_scr) for h in range(HGRN_HEADS)]
    for lvl in range(2, 7):
        for h in range(HGRN_HEADS):
            s_mats[h] = s_mats[h] + _level_scores(lvl, qs[h], _HEADS[h], b_scr, k_scr, f_scr)
    return s_mats


def _hgrn_outputs(c, s_mats, q_ref, v_ref, st_scr, o_scr, b_scr, k_scr):
    rows = _chunk_rows(c)
    for h, hs in enumerate(_HEADS):
        q = q_ref[rows, hs].astype(F32)
        v_bf = v_ref[rows, hs]
        b = b_scr[:, hs]
        kk = k_scr[:, hs]
        b_last = b_scr[CHUNK - 1:CHUNK, hs]
        st = st_scr[h]
        qd = (q * jnp.exp2(b)).astype(BF16)
        kdec = (kk * jnp.exp2(b_last - b)).astype(BF16)
        diag = jnp.sum(q * kk, axis=-1, keepdims=True)
        o_scr[rows, hs] = (_nt_dot(qd, st.astype(BF16))
                           + jnp.dot(s_mats[h].astype(BF16), v_bf, preferred_element_type=F32)
                           + diag * v_bf.astype(F32))
        st_scr[h] = jnp.exp2(b_last) * st + _tn_dot(v_bf, kdec)


def _level_scores(lvl, q, hs, b_scr, k_scr, f_scr):
    row = lax.broadcasted_iota(jnp.int32, (CHUNK, 1), 0)
    col = lax.broadcasted_iota(jnp.int32, (1, CHUNK), 1)
    b = b_scr[:, hs]
    kk = k_scr[:, hs]
    blk = 1 << lvl
    half = blk // 2
    if lvl == 1:
        odd = (row & 1) == 1
        qx = jnp.where(odd, q * f_scr[:, hs], 0.0).astype(BF16)
        kx = jnp.where(odd, 0.0, kk).astype(BF16)
        return jnp.where((row >> 1) == (col >> 1), _nt_dot(qx, kx), 0.0)
    if half < 8:
        groups = []
        sub = lax.broadcasted_iota(jnp.int32, (8, 1), 0)
        for j in range(CHUNK // 8):
            rj = None
            for k in reversed(range(8 // blk)):
                m = 8 * j + k * blk + half - 1
                bm = jnp.broadcast_to(b_scr[m:m + 1, hs], (8, HEAD_DIM))
                rj = bm if rj is None else jnp.where(sub < (k + 1) * blk, bm, rj)
            groups.append(rj)
        ref = jnp.concatenate(groups, axis=0)
        second = (row & (blk - 1)) >= half
        qx = (q * jnp.exp2(jnp.where(second, b - ref, -jnp.inf))).astype(BF16)
        kx = (kk * jnp.exp2(jnp.where(second, -jnp.inf, ref - b))).astype(BF16)
        return jnp.where((row >> lvl) == (col >> lvl), _nt_dot(qx, kx), 0.0)
    n_blk = CHUNK // blk
    qparts, kparts = [], []
    for j in range(n_blk):
        m = j * blk + half - 1
        bm = b_scr[m:m + 1, hs]
        tq = slice(j * blk + half, (j + 1) * blk)
        tk = slice(j * blk, j * blk + half)
        qparts.append(q[tq] * jnp.exp2(b[tq] - bm))
        kparts.append(kk[tk] * jnp.exp2(bm - b[tk]))
        kparts.append(jnp.zeros((half, HEAD_DIM), F32))
    qx = jnp.concatenate(qparts, axis=0).astype(BF16)
    kx = jnp.concatenate(kparts, axis=0).astype(BF16)
    sc = _nt_dot(qx, kx)
    if n_blk > 1:
        crow = lax.broadcasted_iota(jnp.int32, (CHUNK // 2, 1), 0)
        sc = jnp.where((crow // half) == (col >> lvl), sc, 0.0)
    pieces = []
    for j in range(n_blk):
        pieces.append(jnp.zeros((half, CHUNK), F32))
        pieces.append(sc[j * half:(j + 1) * half])
    return jnp.concatenate(pieces, axis=0)


def _mix_kernel(lbt_ref, q_ref, f_ref, i_ref, g_ref, cb_ref, cc_ref, ch_ref, ga0_ref, ga1_ref,
                gb0_ref, gb1_ref, x_ref, mod_ref, hnw_ref, cw_ref, wa_ref, wb_ref, wm_ref,
                o_ref, st_scr, o_scr, carry_scr, b_scr, k_scr, f_scr, *, layer, rows_per_step, epi_rows):
    @pl.when(pl.program_id(1) == 0)
    def _():
        st_scr[...] = jnp.zeros_like(st_scr)
        carry_scr[0:8, :] = jnp.zeros((8, CONV_WIDTH), F32)

    tab = lbt_ref[...]
    tmax = jnp.max(tab, axis=0, keepdims=True)
    te = jnp.exp(tab - tmax)
    lb = jnp.sum(te[0:layer + 1], axis=0, keepdims=True) / jnp.sum(te, axis=0, keepdims=True)

    ri = lax.broadcasted_iota(jnp.int32, (CHUNK, CHUNK), 0)
    ci = lax.broadcasted_iota(jnp.int32, (CHUNK, CHUNK), 1)
    tri = jnp.where(ci <= ri, 1.0, 0.0).astype(BF16)

    n_chunks = rows_per_step // CHUNK

    def slot(s):
        return b_scr.at[s], k_scr.at[s], f_scr.at[s]

    def chunk(c, c_next, s):
        bs, ks, fs = slot(s)
        s_mats = _hgrn_scores(c, q_ref, bs, ks, fs)
        _hgrn_gates(c_next, lb, tri, f_ref, *slot(1 - s))
        _hgrn_outputs(c, s_mats, q_ref, i_ref, st_scr, o_scr, bs, ks)

    def pair_body(i, carry):
        chunk(2 * i, 2 * i + 1, 0)
        chunk(2 * i + 1, jnp.minimum(2 * i + 2, n_chunks - 1), 1)
        return carry

    _hgrn_gates(0, lb, tri, f_ref, *slot(0))
    lax.fori_loop(0, n_chunks // 2, pair_body, 0)

    gate_m = mod_ref[0, 2:3, :]
    hnw = hnw_ref[...]
    cw0 = cw_ref[0:1, :]
    cw1 = cw_ref[1:2, :]
    cw2 = cw_ref[2:3, :]
    for r in range(rows_per_step // epi_rows):
        rs = slice(r * epi_rows, (r + 1) * epi_rows)
        parts = []
        for h in range(HGRN_HEADS):
            hs = slice(h * HEAD_DIM, (h + 1) * HEAD_DIM)
            oh = _rms(o_scr[rs, hs], hnw)
            go = g_ref[rs, hs].astype(F32)
            parts.append((oh * (go * _sigmoid(go))).astype(BF16))
        ya = jnp.dot(jnp.concatenate(parts, axis=1), wa_ref[...], preferred_element_type=F32)
        uc = cc_ref[rs, :].astype(F32) * ch_ref[rs, :].astype(F32)
        carry_scr[8:8 + epi_rows, :] = uc
        s1 = carry_scr[7:7 + epi_rows, :]
        s2 = carry_scr[6:6 + epi_rows, :]
        carry_scr[0:8, :] = uc[epi_rows - 8:epi_rows, :]
        yc = cb_ref[rs, :].astype(F32) * (cw2 * uc + cw1 * s1 + cw0 * s2)
        yb = jnp.dot(yc.astype(BF16), wb_ref[...], preferred_element_type=F32)
        ga = jnp.concatenate([ga0_ref[rs, :], ga1_ref[rs, :]], axis=1).astype(F32)
        gb = jnp.concatenate([gb0_ref[rs, :], gb1_ref[rs, :]], axis=1).astype(F32)
        merged = (_sigmoid(ga) * ya + _sigmoid(gb) * yb).astype(BF16)
        o_ref[rs, :] = x_ref[rs, :] + gate_m * jnp.dot(merged, wm_ref[...], preferred_element_type=F32)


def _mix(proj, x2, mod, lb_table, hgrn_norm_w, conv_w, wa, wb, wm, layer, bsz, seq, tt):
    n = x2.shape[0]
    per_b = seq // tt

    def col(col_block):
        return pl.BlockSpec((tt, 512), lambda b, t: (b * per_b + t, col_block))

    const = lambda shape: pl.BlockSpec(shape, lambda b, t: (0,) * len(shape))
    in_specs = [
        const(lb_table.shape),
        col(0), col(1), col(2), col(3),
        col(4), col(5), col(6),
        col(7), col(8), col(9), col(10),
        pl.BlockSpec((tt, D_MODEL), lambda b, t: (b * per_b + t, 0)),
        pl.BlockSpec((1, 6, D_MODEL), lambda b, t: (b, 0, 0)),
        const((1, HEAD_DIM)), const((CONV_K, CONV_WIDTH)),
        const((HGRN_WIDTH, D_MODEL)), const((CONV_WIDTH, D_MODEL)), const((D_MODEL, D_MODEL)),
    ]
    kern = functools.partial(_mix_kernel, layer=layer, rows_per_step=tt, epi_rows=tt)
    return pl.pallas_call(
        kern,
        out_shape=jax.ShapeDtypeStruct((n, D_MODEL), F32),
        grid=(bsz, per_b),
        in_specs=in_specs,
        out_specs=pl.BlockSpec((tt, D_MODEL), lambda b, t: (b * per_b + t, 0)),
        scratch_shapes=[pltpu.VMEM((HGRN_HEADS, HEAD_DIM, HEAD_DIM), F32),
                        pltpu.VMEM((tt, HGRN_WIDTH), F32),
                        pltpu.VMEM((8 + tt, CONV_WIDTH), F32),
                        pltpu.VMEM((2, CHUNK, HGRN_WIDTH), F32),
                        pltpu.VMEM((2, CHUNK, HGRN_WIDTH), F32),
                        pltpu.VMEM((2, CHUNK, HGRN_WIDTH), F32)],
        compiler_params=pltpu.CompilerParams(dimension_semantics=("arbitrary", "arbitrary"),
                                             vmem_limit_bytes=VMEM_LIMIT),
        name="mix",
    )(lb_table, *([proj] * 11),
      x2, mod, hgrn_norm_w.reshape(1, HEAD_DIM), conv_w, wa, wb, wm)


def _route_kernel(h_ref, mod_ref, nw_ref, wr_ref, br_ref, u_ref, oh4_ref, rk_ref, pw_ref, cnt_ref, *, tr):
    shift = mod_ref[0, 3:4, :]
    scale = mod_ref[0, 4:5, :]
    u = _rms(h_ref[...], nw_ref[...]) * (1.0 + scale) + shift
    u_ref[...] = _pack_rows(u)
    lane = lax.broadcasted_iota(jnp.int32, (tr, LANES), 1).astype(F32)
    logits = jnp.dot(u.astype(BF16), wr_ref[...], preferred_element_type=F32) + br_ref[...]
    logits = jnp.where(lane < N_EXPERTS, logits, -jnp.inf)
    idx, val = [], []
    cur = logits
    for _ in range(TOP_K):
        m = jnp.max(cur, axis=-1, keepdims=True)
        i = jnp.min(jnp.where(cur == m, lane, float(LANES)), axis=-1, keepdims=True)
        idx.append(i)
        val.append(m)
        cur = jnp.where(lane == i, -jnp.inf, cur)
    ex = [jnp.exp(v - val[0]) for v in val]
    den = ex[0] + ex[1] + ex[2] + ex[3]
    onehot = jnp.zeros((tr, LANES), F32)
    for i in idx:
        onehot = onehot + jnp.where(lane == i, 1.0, 0.0)
    ri = lax.broadcasted_iota(jnp.int32, (tr, tr), 0)
    ci = lax.broadcasted_iota(jnp.int32, (tr, tr), 1)
    tri = jnp.where(ci < ri, 1.0, 0.0).astype(BF16)
    pref = jnp.dot(tri, onehot.astype(BF16), preferred_element_type=F32)
    rk = jnp.zeros((tr, LANES), jnp.int32)
    pw = jnp.zeros((tr, LANES), F32)
    oh4 = jnp.zeros((tr, LANES), F32)
    for j in range(TOP_K):
        rank = jnp.sum(jnp.where(lane == idx[j], pref, 0.0), axis=-1, keepdims=True).astype(jnp.int32)
        rk = jnp.where(lane == j, rank, rk)
        pw = jnp.where(lane == j, ex[j] / den, pw)
        oh4 = oh4 + jnp.where(lane == idx[j] + N_EXPERTS * j, 1.0, 0.0)
    oh4_ref[...] = oh4.astype(BF16)
    rk_ref[...] = rk
    pw_ref[...] = pw
    cnt_ref[0] = jnp.sum(onehot, axis=0, keepdims=True).astype(jnp.int32)


def _route(h1, mod, norm_w, wr_pad, br_pad, seq, tr, tile0, n):
    per_b = seq // tr
    nt = n // tr
    return pl.pallas_call(
        functools.partial(_route_kernel, tr=tr),
        out_shape=(jax.ShapeDtypeStruct((n, PACKED), jnp.int32),
                   jax.ShapeDtypeStruct((n, LANES), BF16),
                   jax.ShapeDtypeStruct((n, LANES), jnp.int32),
                   jax.ShapeDtypeStruct((n, LANES), F32),
                   jax.ShapeDtypeStruct((nt, 1, LANES), jnp.int32)),
        grid=(nt,),
        in_specs=[pl.BlockSpec((tr, D_MODEL), lambda i: (tile0 + i, 0)),
                  pl.BlockSpec((1, 6, D_MODEL), lambda i: ((tile0 + i) // per_b, 0, 0)),
                  pl.BlockSpec((1, D_MODEL), lambda i: (0, 0)),
                  pl.BlockSpec((D_MODEL, LANES), lambda i: (0, 0)),
                  pl.BlockSpec((1, LANES), lambda i: (0, 0))],
        out_specs=(pl.BlockSpec((tr, PACKED), lambda i: (i, 0)),
                   pl.BlockSpec((tr, LANES), lambda i: (i, 0)),
                   pl.BlockSpec((tr, LANES), lambda i: (i, 0)),
                   pl.BlockSpec((tr, LANES), lambda i: (i, 0)),
                   pl.BlockSpec((1, 1, LANES), lambda i: (i, 0, 0))),
        compiler_params=pltpu.CompilerParams(dimension_semantics=("arbitrary",),
                                             vmem_limit_bytes=VMEM_LIMIT),
        name="route",
    )(h1, mod, norm_w.reshape(1, D_MODEL), wr_pad, br_pad)


def _dest_kernel(oh4_ref, rk_ref, bt_ref, o_ref):
    oh = oh4_ref[...]
    start = (jnp.dot(oh, bt_ref[0, 0], preferred_element_type=F32)
             + 256.0 * jnp.dot(oh, bt_ref[0, 1], preferred_element_type=F32)
             + 65536.0 * jnp.dot(oh, bt_ref[0, 2], preferred_element_type=F32))
    o_ref[...] = (start + rk_ref[...].astype(F32)).T[:8, :].astype(jnp.int32)


def _dest(oh4, rk, bt, tr):
    n = oh4.shape[0]
    return pl.pallas_call(
        _dest_kernel,
        out_shape=jax.ShapeDtypeStruct((8, n), jnp.int32),
        grid=(n // tr,),
        in_specs=[pl.BlockSpec((tr, LANES), lambda i: (i, 0)),
                  pl.BlockSpec((tr, LANES), lambda i: (i, 0)),
                  pl.BlockSpec((1, 3, LANES, LANES), lambda i: (i, 0, 0, 0))],
        out_specs=pl.BlockSpec((8, tr), lambda i: (0, i)),
        name="dest",
    )(oh4, rk, bt)


def _sc_workers():
    info = plsc.get_sparse_core_info()
    return info.num_cores, info.num_cores * info.num_subcores


def _sc_scatter_rows(rows, idx_slots, pad_idx, n_out):
    n_cores, n_workers = _sc_workers()
    n, w = rows.shape
    k = idx_slots.shape[0] // n
    per_worker = n // n_workers
    pad_per_worker = pad_idx.shape[0] // n_workers
    assert per_worker % SC_CHUNK == 0 and pad_per_worker % SC_CHUNK == 0
    mesh = plsc.VectorSubcoreMesh(core_axis_name="c", subcore_axis_name="s")
    zeros = jnp.zeros((SC_CHUNK, w), rows.dtype)

    n_chunks = per_worker // SC_CHUNK
    assert n_chunks % 2 == 0

    @functools.partial(
        pl.kernel, mesh=mesh,
        out_type=jax.ShapeDtypeStruct((n_out, w), rows.dtype),
        scratch_types=[pltpu.VMEM((SC_CHUNK,), jnp.int32)] * k
        + [pltpu.VMEM((SC_CHUNK, w), rows.dtype)] * 2
        + [pltpu.SemaphoreType.DMA] * 2,
        name="sc_scatter",
    )
    def scatter(rows_hbm, idx_hbm, pad_hbm, zeros_hbm, out_hbm, *scratch):
        idx_bufs = scratch[:k]
        row_bufs = scratch[k:k + 2]
        sem_rows, sem_out = scratch[k + 2:]
        wid = lax.axis_index("s") * n_cores + lax.axis_index("c")

        def chunk_off(c):
            return pl.multiple_of(wid * per_worker + c * SC_CHUNK, 8)

        def load(c, b):
            pltpu.async_copy(rows_hbm.at[pl.ds(chunk_off(c), SC_CHUNK)], row_bufs[b], sem_rows)

        def wait_load(b):
            pltpu.make_async_copy(rows_hbm.at[pl.ds(0, SC_CHUNK)], row_bufs[b], sem_rows).wait()

        def scatter_chunk(c, b):
            for j in range(k):
                pltpu.sync_copy(idx_hbm.at[pl.ds(pl.multiple_of(j * n + chunk_off(c), 8), SC_CHUNK)],
                                idx_bufs[j])
            copies = [pltpu.async_copy(row_bufs[b], out_hbm.at[idx_bufs[j]], sem_out) for j in range(k)]
            for cp in copies:
                cp.wait()

        load(0, 0)

        def body(i, carry):
            wait_load(0)
            load(2 * i + 1, 1)
            scatter_chunk(2 * i, 0)
            wait_load(1)

            @pl.when(i < n_chunks // 2 - 1)
            def _():
                load(2 * i + 2, 0)

            scatter_chunk(2 * i + 1, 1)
            return carry

        lax.fori_loop(0, n_chunks // 2, body, 0)
        pltpu.sync_copy(zeros_hbm, row_bufs[0])

        def pad_body(i, carry):
            off = pl.multiple_of(wid * pad_per_worker + i * SC_CHUNK, 8)
            pltpu.sync_copy(pad_hbm.at[pl.ds(off, SC_CHUNK)], idx_bufs[0])
            pltpu.async_copy(row_bufs[0], out_hbm.at[idx_bufs[0]], sem_out).wait()
            return carry

        lax.fori_loop(0, pad_per_worker // SC_CHUNK, pad_body, 0)

    return scatter(rows, idx_slots, pad_idx, zeros)


def _experts_kernel(be_ref, nu_ref, first_ref, next_ref, slot_ref, x_ref, w1_hbm, b1_ref, w2_hbm, b2_ref,
                    o_ref, w1_f32, w2_f32, w1_bf, w2_bf, sems):
    i = pl.program_id(0)
    used = i < nu_ref[0]

    def weight_copies(e, s):
        return (pltpu.make_async_copy(w1_hbm.at[e], w1_f32.at[s], sems.at[s]),
                pltpu.make_async_copy(w2_hbm.at[e], w2_f32.at[s], sems.at[s]))

    @pl.when(used & (first_ref[i] == 1))
    def _():
        s = slot_ref[i]

        @pl.when(i == 0)
        def _():
            for cp in weight_copies(be_ref[i], s):
                cp.start()

        for cp in weight_copies(be_ref[i], s):
            cp.wait()
        cw = 256
        for c in range(2 * D_FF // cw):
            w1_bf[:, c * cw:(c + 1) * cw] = w1_f32[s, :, c * cw:(c + 1) * cw].astype(BF16)
        for c in range(D_MODEL // cw):
            w2_bf[:, c * cw:(c + 1) * cw] = w2_f32[s, :, c * cw:(c + 1) * cw].astype(BF16)

        @pl.when(next_ref[i] >= 0)
        def _():
            for cp in weight_copies(next_ref[i], 1 - s):
                cp.start()

    @pl.when(used)
    def _():
        step = 512
        for r in range(ROW_BLOCK // EXPERT_ROWS):
            rs = slice(r * EXPERT_ROWS, (r + 1) * EXPERT_ROWS)
            x = _unpack_rows(x_ref[rs, :]).astype(BF16)
            acc = None
            for j in range(D_FF // step):
                cs = slice(j * step, (j + 1) * step)
                ls = slice(D_FF + j * step, D_FF + (j + 1) * step)
                glu = jnp.dot(x, w1_bf[:, cs], preferred_element_type=F32) + b1_ref[0, :, cs]
                lin = jnp.dot(x, w1_bf[:, ls], preferred_element_type=F32) + b1_ref[0, :, ls]
                glu = jnp.minimum(glu, SWIGLU_LIMIT)
                lin = jnp.clip(lin, -SWIGLU_LIMIT, SWIGLU_LIMIT)
                act = (glu * _sigmoid(glu, SWIGLU_ALPHA) * (lin + 1.0)).astype(BF16)
                part = jnp.dot(act, w2_bf[cs, :], preferred_element_type=F32)
                acc = part if acc is None else acc + part
            o_ref[rs, :] = _pack_rows(acc + b2_ref[0])

    @pl.when(jnp.logical_not(used))
    def _():
        o_ref[...] = jnp.zeros_like(o_ref)


def _experts(plan, xs, w1, b1, w2, b2):
    n_rows = xs.shape[0]
    nb = n_rows // ROW_BLOCK
    grid_spec = pltpu.PrefetchScalarGridSpec(
        num_scalar_prefetch=5,
        grid=(nb,),
        in_specs=[pl.BlockSpec((ROW_BLOCK, PACKED), lambda i, be, nu, *_: (jnp.minimum(i, nu[0] - 1), 0)),
                  pl.BlockSpec(memory_space=pl.ANY),
                  pl.BlockSpec((1, 1, 2 * D_FF), lambda i, be, *_: (be[i], 0, 0)),
                  pl.BlockSpec(memory_space=pl.ANY),
                  pl.BlockSpec((1, 1, D_MODEL), lambda i, be, *_: (be[i], 0, 0))],
        out_specs=pl.BlockSpec((ROW_BLOCK, PACKED), lambda i, *_: (i, 0)),
        scratch_shapes=[pltpu.VMEM((2, D_MODEL, 2 * D_FF), F32), pltpu.VMEM((2, D_FF, D_MODEL), F32),
                        pltpu.VMEM((D_MODEL, 2 * D_FF), BF16), pltpu.VMEM((D_FF, D_MODEL), BF16),
                        pltpu.SemaphoreType.DMA((2,))],
    )
    return pl.pallas_call(
        _experts_kernel,
        out_shape=jax.ShapeDtypeStruct((n_rows, PACKED), jnp.int32),
        grid_spec=grid_spec,
        compiler_params=pltpu.CompilerParams(dimension_semantics=("arbitrary",),
                                             vmem_limit_bytes=VMEM_LIMIT),
        name="experts",
    )(*plan, xs, w1, b1.reshape(N_EXPERTS, 1, 2 * D_FF), w2, b2.reshape(N_EXPERTS, 1, D_MODEL))


def _sc_gather_rows(table, idx_flat):
    n_cores, n_workers = _sc_workers()
    n_idx = idx_flat.shape[0]
    w = table.shape[1]
    per_worker = n_idx // n_workers
    n_chunks = per_worker // SC_CHUNK
    assert per_worker * n_workers == n_idx and n_chunks * SC_CHUNK == per_worker and n_chunks % 2 == 0
    mesh = plsc.VectorSubcoreMesh(core_axis_name="c", subcore_axis_name="s")

    @functools.partial(
        pl.kernel, mesh=mesh,
        out_type=jax.ShapeDtypeStruct((n_idx, w), table.dtype),
        scratch_types=[pltpu.VMEM((SC_CHUNK,), jnp.int32), pltpu.VMEM((SC_CHUNK,), jnp.int32),
                       pltpu.VMEM((SC_CHUNK, w), table.dtype), pltpu.VMEM((SC_CHUNK, w), table.dtype),
                       pltpu.SemaphoreType.DMA, pltpu.SemaphoreType.DMA],
        name="sc_gather",
    )
    def gather(table_hbm, idx_hbm, out_hbm, idx0, idx1, rows0, rows1, sem0, sem1):
        wid = lax.axis_index("s") * n_cores + lax.axis_index("c")
        bufs = ((idx0, rows0, sem0), (idx1, rows1, sem1))

        def chunk_off(c):
            return pl.multiple_of(wid * per_worker + c * SC_CHUNK, 8)

        def start(c, b):
            idx_v, rows_v, sem = bufs[b]
            pltpu.sync_copy(idx_hbm.at[pl.ds(chunk_off(c), SC_CHUNK)], idx_v)
            pltpu.async_copy(table_hbm.at[idx_v], rows_v, sem)

        def finish(c, b):
            idx_v, rows_v, sem = bufs[b]
            pltpu.make_async_copy(table_hbm.at[idx_v], rows_v, sem).wait()
            pltpu.sync_copy(rows_v, out_hbm.at[pl.ds(chunk_off(c), SC_CHUNK)])

        start(0, 0)

        def body(i, carry):
            start(2 * i + 1, 1)
            finish(2 * i, 0)

            @pl.when(i < n_chunks // 2 - 1)
            def _():
                start(2 * i + 2, 0)

            finish(2 * i + 1, 1)
            return carry

        lax.fori_loop(0, n_chunks // 2, body, 0)

    return gather(table, idx_flat)


def _finish_kernel(h_ref, pw_ref, mod_ref, nw_ref, y0_ref, y1_ref, y2_ref, y3_ref, *rest):
    o_ref = rest[-1]
    pw = pw_ref[...]
    moe = pw[:, 0:1] * _unpack_rows(y0_ref[0])
    for j, y_ref in enumerate((y1_ref, y2_ref, y3_ref), start=1):
        moe = moe + pw[:, j:j + 1] * _unpack_rows(y_ref[0])
    gate_f = mod_ref[0, 5:6, :]
    o_ref[...] = _rms(h_ref[...] + gate_f * moe, nw_ref[...])


def _finish(h1, pw, mod, norm_w, y4, seq, tc, tile0, prev_out):
    n = pw.shape[0]
    per_b = seq // tc
    slot = lambda j: pl.BlockSpec((1, tc, PACKED), lambda i: (j, i, 0))
    in_specs = [pl.BlockSpec((tc, D_MODEL), lambda i: (tile0 + i, 0)),
                pl.BlockSpec((tc, LANES), lambda i: (i, 0)),
                pl.BlockSpec((1, 6, D_MODEL), lambda i: ((tile0 + i) // per_b, 0, 0)),
                pl.BlockSpec((1, D_MODEL), lambda i: (0, 0)),
                slot(0), slot(1), slot(2), slot(3)]
    args = [h1, pw, mod, norm_w.reshape(1, D_MODEL), y4, y4, y4, y4]
    aliases = {}
    if prev_out is not None:
        in_specs.append(pl.BlockSpec(memory_space=pl.ANY))
        args.append(prev_out)
        aliases = {len(args) - 1: 0}
    return pl.pallas_call(
        _finish_kernel,
        out_shape=jax.ShapeDtypeStruct(h1.shape, F32),
        grid=(n // tc,),
        in_specs=in_specs,
        out_specs=pl.BlockSpec((tc, D_MODEL), lambda i: (tile0 + i, 0)),
        input_output_aliases=aliases,
        compiler_params=pltpu.CompilerParams(dimension_semantics=("arbitrary",),
                                             vmem_limit_bytes=VMEM_LIMIT),
        name="finish",
    )(*args)


def _moe_plan(counts, n_assign):
    cnt = counts[:, 0, :N_EXPERTS]
    sizes = jnp.sum(cnt, axis=0)
    padded = (sizes + ROW_BLOCK - 1) // ROW_BLOCK * ROW_BLOCK
    pad_end = jnp.cumsum(padded)
    pad_start = pad_end - padded
    tile_base = pad_start[None, :] + jnp.cumsum(cnt, axis=0) - cnt
    digits = jnp.stack([tile_base % 256, (tile_base // 256) % 256, tile_base // 65536], axis=1)
    rows = jnp.tile(digits, (1, 1, TOP_K))
    slot_of_row = jnp.arange(LANES, dtype=jnp.int32) // N_EXPERTS
    col = jnp.arange(LANES, dtype=jnp.int32)
    bt = jnp.where(slot_of_row[:, None] == col[None, :], rows[..., None], 0).astype(BF16)
    nb = n_assign // ROW_BLOCK + N_EXPERTS
    block_start = jnp.arange(nb, dtype=jnp.int32) * ROW_BLOCK
    block_e = jnp.minimum(jnp.sum(pad_end[None, :] <= block_start[:, None], axis=1),
                          N_EXPERTS - 1).astype(jnp.int32)
    n_used = (pad_end[-1] // ROW_BLOCK).astype(jnp.int32).reshape(1)
    nonempty = padded > 0
    ids = jnp.arange(N_EXPERTS, dtype=jnp.int32)
    suffix_min = lax.cummin(jnp.where(nonempty, ids, N_EXPERTS), reverse=True)
    after = jnp.concatenate([suffix_min[1:], jnp.full((1,), N_EXPERTS, jnp.int32)])
    next_nonempty = jnp.where(after < N_EXPERTS, after, -1).astype(jnp.int32)
    slot_of = ((jnp.cumsum(nonempty) - 1) % 2).astype(jnp.int32)
    first = jnp.concatenate([jnp.ones((1,), jnp.int32),
                             (block_e[1:] != block_e[:-1]).astype(jnp.int32)])
    expert_plan = (block_e, n_used, first, next_nonempty[block_e], slot_of[block_e])
    r = jnp.arange(ROW_BLOCK, dtype=jnp.int32)[None, :]
    pad_idx = jnp.where(r < (padded - sizes)[:, None], (pad_start + sizes)[:, None] + r,
                        (nb - 1) * ROW_BLOCK + r).astype(jnp.int32).reshape(PAD_SLOTS)
    return bt, expert_plan, pad_idx


def kernel(x, c, w_ada, b_ada, norm_mix_w, w_in, hgrn_lower_bounds, hgrn_norm_w, conv_w,
           w_hgrn_out, w_conv_out, w_mix_out, norm_ffn_w, w_router, b_router, w1, b1, w2, b2,
           norm_final_w):
    bsz, seq, d = x.shape
    assert d == D_MODEL and seq % CHUNK == 0
    n = bsz * seq
    depth = w_ada.shape[0]
    tile = min(512, seq)
    assert seq % tile == 0 and n % (MOE_GROUPS * tile) == 0 and (n // MOE_GROUPS * TOP_K) % ROW_BLOCK == 0
    h = x.reshape(n, d)
    wr_pad = jnp.zeros((depth, D_MODEL, LANES), BF16).at[:, :, :N_EXPERTS].set(w_router.astype(BF16))
    br_pad = jnp.zeros((depth, 1, LANES), F32).at[:, 0, :N_EXPERTS].set(b_router)
    for layer in range(depth):
        mod = _ada(c, w_ada[layer], b_ada[layer]).reshape(bsz, 6, d)
        proj = _inproj(h, mod, norm_mix_w[layer], w_in[layer].astype(BF16), seq, tile)
        h = _mix(proj, h, mod, hgrn_lower_bounds, hgrn_norm_w[layer], conv_w[layer],
                 w_hgrn_out[layer].astype(BF16), w_conv_out[layer].astype(BF16),
                 w_mix_out[layer].astype(BF16), layer, bsz, seq, tile)
        assert layer == depth - 1, "only the last layer applies the final norm"
        ng = n // MOE_GROUPS
        n_rows = (ng * TOP_K // ROW_BLOCK + N_EXPERTS) * ROW_BLOCK
        routed = [_route(h, mod, norm_ffn_w[layer], wr_pad[layer], br_pad[layer], seq, tile,
                         grp * (ng // tile), ng) for grp in range(MOE_GROUPS)]
        plans = jax.vmap(lambda cnt: _moe_plan(cnt, ng * TOP_K))(jnp.stack([r[4] for r in routed]))
        out = None
        for grp in range(MOE_GROUPS):
            tile0 = grp * (ng // tile)
            u2, oh4, rk, pw, _ = routed[grp]
            bt, expert_plan, pad_idx = jax.tree.map(lambda a: a[grp], plans)
            dest_slots = _dest(oh4, rk, bt, tile)[:TOP_K].reshape(TOP_K * ng)
            xs = _sc_scatter_rows(u2, dest_slots, pad_idx, n_rows)
            ys = _experts(expert_plan, xs, w1[layer], b1[layer], w2[layer], b2[layer])
            y4 = _sc_gather_rows(ys, dest_slots).reshape(TOP_K, ng, PACKED)
            out = _finish(h, pw, mod, norm_final_w, y4, seq, tile, tile0, out)
        h = out
    return h.reshape(bsz, seq, d)
```

```python
import functools

import jax
import jax.numpy as jnp
from jax import lax
from jax.experimental import pallas as pl
from jax.experimental.pallas import tpu as pltpu
from jax.experimental.pallas import tpu_sc as plsc

F32 = jnp.float32
BF16 = jnp.bfloat16

D_MODEL = 1024
HGRN_HEADS = 4
HEAD_DIM = 128
HGRN_WIDTH = HGRN_HEADS * HEAD_DIM
CONV_WIDTH = 512
CONV_K = 3
CHUNK = 64
N_EXPERTS = 32
TOP_K = 4
D_FF = 1024
SWIGLU_LIMIT = 7.0
SWIGLU_ALPHA = 1.702
EPS = 1e-6
LOG2_E = 1.4426950408889634
IN_COLS = 4 * HGRN_WIDTH + 3 * CONV_WIDTH + 2 * D_MODEL
LANES = 128
ROW_BLOCK = 512
PACKED = D_MODEL // 2
MOE_GROUPS = 2
SC_CHUNK = 64
PAD_SLOTS = N_EXPERTS * ROW_BLOCK
VMEM_LIMIT = 56 * 1024 * 1024


def _sigmoid(x, scale=1.0):
    return 1.0 / (1.0 + jnp.exp2(x * (-scale * LOG2_E)))


def _rms(x, w):
    ms = jnp.mean(x * x, axis=-1, keepdims=True)
    return x * lax.rsqrt(ms + EPS) * w


def _pack_rows(x):
    w = x.shape[1] // 2
    lo = lax.bitcast_convert_type(x[:, :w].astype(BF16).astype(F32), jnp.uint32)
    hi = lax.bitcast_convert_type(x[:, w:].astype(BF16).astype(F32), jnp.uint32)
    return lax.bitcast_convert_type((lo >> 16) | (hi & jnp.uint32(0xFFFF0000)), jnp.int32)


def _unpack_rows(p):
    u = lax.bitcast_convert_type(p, jnp.uint32)
    lo = lax.bitcast_convert_type(u << 16, F32)
    hi = lax.bitcast_convert_type(u & jnp.uint32(0xFFFF0000), F32)
    return jnp.concatenate([lo, hi], axis=1)


def _nt_dot(a, b):
    return lax.dot_general(a, b, (((1,), (1,)), ((), ())), preferred_element_type=F32)


def _tn_dot(a, b):
    return lax.dot_general(a, b, (((0,), (0,)), ((), ())), preferred_element_type=F32)


def _ada_kernel(c_ref, w_ref, b_ref, o_ref):
    c = c_ref[...]
    sc = (c * _sigmoid(c)).astype(BF16)
    o_ref[...] = jnp.dot(sc, w_ref[...].astype(BF16), preferred_element_type=F32) + b_ref[...]


def _ada(c, w_ada, b_ada):
    bsz, d = c.shape
    n = w_ada.shape[1]
    return pl.pallas_call(
        _ada_kernel,
        out_shape=jax.ShapeDtypeStruct((bsz, n), F32),
        grid=(n // d,),
        in_specs=[pl.BlockSpec((bsz, d), lambda j: (0, 0)),
                  pl.BlockSpec((d, d), lambda j: (0, j)),
                  pl.BlockSpec((1, d), lambda j: (0, j))],
        out_specs=pl.BlockSpec((bsz, d), lambda j: (0, j)),
        name="ada",
    )(c, w_ada, b_ada.reshape(1, n))


def _inproj_kernel(x_ref, mod_ref, nw_ref, w_ref, o_ref, u_scr):
    y = _rms(x_ref[...], nw_ref[...])
    shift = mod_ref[0, 0:1, :]
    scale = mod_ref[0, 1:2, :]
    u_scr[...] = (y * (1.0 + scale) + shift).astype(BF16)

    step = 512
    for j in range(IN_COLS // step):
        o_ref[:, j * step:(j + 1) * step] = jnp.dot(
            u_scr[...], w_ref[:, j * step:(j + 1) * step], preferred_element_type=F32).astype(BF16)


def _inproj(x2, mod, norm_w, w_in_bf, seq, tm):
    n = x2.shape[0]
    per_b = seq // tm
    return pl.pallas_call(
        _inproj_kernel,
        out_shape=jax.ShapeDtypeStruct((n, IN_COLS), BF16),
        grid=(n // tm,),
        in_specs=[pl.BlockSpec((tm, D_MODEL), lambda i: (i, 0)),
                  pl.BlockSpec((1, 6, D_MODEL), lambda i: (i // per_b, 0, 0)),
                  pl.BlockSpec((1, D_MODEL), lambda i: (0, 0)),
                  pl.BlockSpec((D_MODEL, IN_COLS), lambda i: (0, 0))],
        out_specs=pl.BlockSpec((tm, IN_COLS), lambda i: (i, 0)),
        scratch_shapes=[pltpu.VMEM((tm, D_MODEL), BF16)],
        compiler_params=pltpu.CompilerParams(dimension_semantics=("arbitrary",),
                                             vmem_limit_bytes=VMEM_LIMIT),
        name="inproj",
    )(x2, mod, norm_w.reshape(1, D_MODEL), w_in_bf)


_HEADS = [slice(h * HEAD_DIM, (h + 1) * HEAD_DIM) for h in range(HGRN_HEADS)]


def _chunk_rows(c):
    if isinstance(c, int):
        return pl.ds(c * CHUNK, CHUNK)
    return pl.ds(pl.multiple_of(c * CHUNK, CHUNK), CHUNK)


def _hgrn_gates(c, lb, tri, f_ref, b_scr, k_scr, f_scr):
    fx = f_ref[_chunk_rows(c), :].astype(F32)
    f = lb + (1.0 - lb) * _sigmoid(fx)
    g = jnp.maximum(jnp.log(f), -128.0) * LOG2_E
    g1 = g.astype(BF16)
    r1 = g - g1.astype(F32)
    g2 = r1.astype(BF16)
    g3 = (r1 - g2.astype(F32)).astype(BF16)
    b_scr[...] = (jnp.dot(tri, g1, preferred_element_type=F32)
                  + jnp.dot(tri, g2, preferred_element_type=F32)
                  + jnp.dot(tri, g3, preferred_element_type=F32))
    f_scr[...] = f
    k_scr[...] = 1.0 - f


def _hgrn_scores(c, q_ref, b_scr, k_scr, f_scr):
    rows = _chunk_rows(c)
    qs = [q_ref[rows, hs].astype(F32) for hs in _HEADS]
    s_mats = [_level_scores(1, qs[h], _HEADS[h], b_scr, k_scr, f_scr) for h in range(HGRN_HEADS)]
    for lvl in range(2, 7):
        for h in range(HGRN_HEADS):
            s_mats[h] = s_mats[h] + _level_scores(lvl, qs[h], _HEADS[h], b_scr, k_scr, f_scr)
    return s_mats


def _hgrn_outputs(c, s_mats, q_ref, v_ref, st_scr, o_scr, b_scr, k_scr):
    rows = _chunk_rows(c)
    for h, hs in enumerate(_HEADS):
        q = q_ref[rows, hs].astype(F32)
        v_bf = v_ref[rows, hs]
        b = b_scr[:, hs]
        kk = k_scr[:, hs]
        b_last = b_scr[CHUNK - 1:CHUNK, hs]
        st = st_scr[h]
        qd = (q * jnp.exp2(b)).astype(BF16)
        kdec = (kk * jnp.exp2(b_last - b)).astype(BF16)
        diag = jnp.sum(q * kk, axis=-1, keepdims=True)
        o_scr[rows, hs] = (_nt_dot(qd, st.astype(BF16))
                           + jnp.dot(s_mats[h].astype(BF16), v_bf, preferred_element_type=F32)
                           + diag * v_bf.astype(F32))
        st_scr[h] = jnp.exp2(b_last) * st + _tn_dot(v_bf, kdec)


def _level_scores(lvl, q, hs, b_scr, k_scr, f_scr):
    row = lax.broadcasted_iota(jnp.int32, (CHUNK, 1), 0)
    col = lax.broadcasted_iota(jnp.int32, (1, CHUNK), 1)
    b = b_scr[:, hs]
    kk = k_scr[:, hs]
    blk = 1 << lvl
    half = blk // 2
    if lvl == 1:
        odd = (row & 1) == 1
        qx = jnp.where(odd, q * f_scr[:, hs], 0.0).astype(BF16)
        kx = jnp.where(odd, 0.0, kk).astype(BF16)
        return jnp.where((row >> 1) == (col >> 1), _nt_dot(qx, kx), 0.0)
    if half < 8:
        groups = []
        sub = lax.broadcasted_iota(jnp.int32, (8, 1), 0)
        for j in range(CHUNK // 8):
            rj = None
            for k in reversed(range(8 // blk)):
                m = 8 * j + k * blk + half - 1
                bm = jnp.broadcast_to(b_scr[m:m + 1, hs], (8, HEAD_DIM))
                rj = bm if rj is None else jnp.where(sub < (k + 1) * blk, bm, rj)
            groups.append(rj)
        ref = jnp.concatenate(groups, axis=0)
        second = (row & (blk - 1)) >= half
        qx = (q * jnp.exp2(jnp.where(second, b - ref, -jnp.inf))).astype(BF16)
        kx = (kk * jnp.exp2(jnp.where(second, -jnp.inf, ref - b))).astype(BF16)
        return jnp.where((row >> lvl) == (col >> lvl), _nt_dot(qx, kx), 0.0)
    n_blk = CHUNK // blk
    qparts, kparts = [], []
    for j in range(n_blk):
        m = j * blk + half - 1
        bm = b_scr[m:m + 1, hs]
        tq = slice(j * blk + half, (j + 1) * blk)
        tk = slice(j * blk, j * blk + half)
        qparts.append(q[tq] * jnp.exp2(b[tq] - bm))
        kparts.append(kk[tk] * jnp.exp2(bm - b[tk]))
        kparts.append(jnp.zeros((half, HEAD_DIM), F32))
    qx = jnp.concatenate(qparts, axis=0).astype(BF16)
    kx = jnp.concatenate(kparts, axis=0).astype(BF16)
    sc = _nt_dot(qx, kx)
    if n_blk > 1:
        crow = lax.broadcasted_iota(jnp.int32, (CHUNK // 2, 1), 0)
        sc = jnp.where((crow // half) == (col >> lvl), sc, 0.0)
    pieces = []
    for j in range(n_blk):
        pieces.append(jnp.zeros((half, CHUNK), F32))
        pieces.append(sc[j * half:(j + 1) * half])
    return jnp.concatenate(pieces, axis=0)


def _mix_kernel(lbt_ref, q_ref, f_ref, i_ref, g_ref, cb_ref, cc_ref, ch_ref, ga0_ref, ga1_ref,
                gb0_ref, gb1_ref, x_ref, mod_ref, hnw_ref, cw_ref, wa_ref, wb_ref, wm_ref,
                o_ref, st_scr, o_scr, carry_scr, b_scr, k_scr, f_scr, *, layer, rows_per_step, epi_rows):
    @pl.when(pl.program_id(1) == 0)
    def _():
        st_scr[...] = jnp.zeros_like(st_scr)
        carry_scr[0:8, :] = jnp.zeros((8, CONV_WIDTH), F32)

    tab = lbt_ref[...]
    tmax = jnp.max(tab, axis=0, keepdims=True)
    te = jnp.exp(tab - tmax)
    lb = jnp.sum(te[0:layer + 1], axis=0, keepdims=True) / jnp.sum(te, axis=0, keepdims=True)

    ri = lax.broadcasted_iota(jnp.int32, (CHUNK, CHUNK), 0)
    ci = lax.broadcasted_iota(jnp.int32, (CHUNK, CHUNK), 1)
    tri = jnp.where(ci <= ri, 1.0, 0.0).astype(BF16)

    n_chunks = rows_per_step // CHUNK

    def slot(s):
        return b_scr.at[s], k_scr.at[s], f_scr.at[s]

    def chunk(c, c_next, s):
        bs, ks, fs = slot(s)
        s_mats = _hgrn_scores(c, q_ref, bs, ks, fs)
        _hgrn_gates(c_next, lb, tri, f_ref, *slot(1 - s))
        _hgrn_outputs(c, s_mats, q_ref, i_ref, st_scr, o_scr, bs, ks)

    def pair_body(i, carry):
        chunk(2 * i, 2 * i + 1, 0)
        chunk(2 * i + 1, jnp.minimum(2 * i + 2, n_chunks - 1), 1)
        return carry

    _hgrn_gates(0, lb, tri, f_ref, *slot(0))
    lax.fori_loop(0, n_chunks // 2, pair_body, 0)

    gate_m = mod_ref[0, 2:3, :]
    hnw = hnw_ref[...]
    cw0 = cw_ref[0:1, :]
    cw1 = cw_ref[1:2, :]
    cw2 = cw_ref[2:3, :]
    for r in range(rows_per_step // epi_rows):
        rs = slice(r * epi_rows, (r + 1) * epi_rows)
        parts = []
        for h in range(HGRN_HEADS):
            hs = slice(h * HEAD_DIM, (h + 1) * HEAD_DIM)
            oh = _rms(o_scr[rs, hs], hnw)
            go = g_ref[rs, hs].astype(F32)
            parts.append((oh * (go * _sigmoid(go))).astype(BF16))
        ya = jnp.dot(jnp.concatenate(parts, axis=1), wa_ref[...], preferred_element_type=F32)
        uc = cc_ref[rs, :].astype(F32) * ch_ref[rs, :].astype(F32)
        carry_scr[8:8 + epi_rows, :] = uc
        s1 = carry_scr[7:7 + epi_rows, :]
        s2 = carry_scr[6:6 + epi_rows, :]
        carry_scr[0:8, :] = uc[epi_rows - 8:epi_rows, :]
        yc = cb_ref[rs, :].astype(F32) * (cw2 * uc + cw1 * s1 + cw0 * s2)
        yb = jnp.dot(yc.astype(BF16), wb_ref[...], preferred_element_type=F32)
        ga = jnp.concatenate([ga0_ref[rs, :], ga1_ref[rs, :]], axis=1).astype(F32)
        gb = jnp.concatenate([gb0_ref[rs, :], gb1_ref[rs, :]], axis=1).astype(F32)
        merged = (_sigmoid(ga) * ya + _sigmoid(gb) * yb).astype(BF16)
        o_ref[rs, :] = x_ref[rs, :] + gate_m * jnp.dot(merged, wm_ref[...], preferred_element_type=F32)


def _mix(proj, x2, mod, lb_table, hgrn_norm_w, conv_w, wa, wb, wm, layer, bsz, seq, tt):
    n = x2.shape[0]
    per_b = seq // tt

    def col(col_block):
        return pl.BlockSpec((tt, 512), lambda b, t: (b * per_b + t, col_block))

    const = lambda shape: pl.BlockSpec(shape, lambda b, t: (0,) * len(shape))
    in_specs = [
        const(lb_table.shape),
        col(0), col(1), col(2), col(3),
        col(4), col(5), col(6),
        col(7), col(8), col(9), col(10),
        pl.BlockSpec((tt, D_MODEL), lambda b, t: (b * per_b + t, 0)),
        pl.BlockSpec((1, 6, D_MODEL), lambda b, t: (b, 0, 0)),
        const((1, HEAD_DIM)), const((CONV_K, CONV_WIDTH)),
        const((HGRN_WIDTH, D_MODEL)), const((CONV_WIDTH, D_MODEL)), const((D_MODEL, D_MODEL)),
    ]
    kern = functools.partial(_mix_kernel, layer=layer, rows_per_step=tt, epi_rows=tt)
    return pl.pallas_call(
        kern,
        out_shape=jax.ShapeDtypeStruct((n, D_MODEL), F32),
        grid=(bsz, per_b),
        in_specs=in_specs,
        out_specs=pl.BlockSpec((tt, D_MODEL), lambda b, t: (b * per_b + t, 0)),
        scratch_shapes=[pltpu.VMEM((HGRN_HEADS, HEAD_DIM, HEAD_DIM), F32),
                        pltpu.VMEM((tt, HGRN_WIDTH), F32),
                        pltpu.VMEM((8 + tt, CONV_WIDTH), F32),
                        pltpu.VMEM((2, CHUNK, HGRN_WIDTH), F32),
                        pltpu.VMEM((2, CHUNK, HGRN_WIDTH), F32),
                        pltpu.VMEM((2, CHUNK, HGRN_WIDTH), F32)],
        compiler_params=pltpu.CompilerParams(dimension_semantics=("arbitrary", "arbitrary"),
                                             vmem_limit_bytes=VMEM_LIMIT),
        name="mix",
    )(lb_table, *([proj] * 11),
      x2, mod, hgrn_norm_w.reshape(1, HEAD_DIM), conv_w, wa, wb, wm)


def _route_kernel(h_ref, mod_ref, nw_ref, wr_ref, br_ref, u_ref, oh4_ref, rk_ref, pw_ref, cnt_ref, *, tr):
    shift = mod_ref[0, 3:4, :]
    scale = mod_ref[0, 4:5, :]
    u = _rms(h_ref[...], nw_ref[...]) * (1.0 + scale) + shift
    u_ref[...] = _pack_rows(u)
    lane = lax.broadcasted_iota(jnp.int32, (tr, LANES), 1).astype(F32)
    logits = jnp.dot(u.astype(BF16), wr_ref[...], preferred_element_type=F32) + br_ref[...]
    logits = jnp.where(lane < N_EXPERTS, logits, -jnp.inf)
    idx, val = [], []
    cur = logits
    for _ in range(TOP_K):
        m = jnp.max(cur, axis=-1, keepdims=True)
        i = jnp.min(jnp.where(cur == m, lane, float(LANES)), axis=-1, keepdims=True)
        idx.append(i)
        val.append(m)
        cur = jnp.where(lane == i, -jnp.inf, cur)
    ex = [jnp.exp(v - val[0]) for v in val]
    den = ex[0] + ex[1] + ex[2] + ex[3]
    onehot = jnp.zeros((tr, LANES), F32)
    for i in idx:
        onehot = onehot + jnp.where(lane == i, 1.0, 0.0)
    ri = lax.broadcasted_iota(jnp.int32, (tr, tr), 0)
    ci = lax.broadcasted_iota(jnp.int32, (tr, tr), 1)
    tri = jnp.where(ci < ri, 1.0, 0.0).astype(BF16)
    pref = jnp.dot(tri, onehot.astype(BF16), preferred_element_type=F32)
    rk = jnp.zeros((tr, LANES), jnp.int32)
    pw = jnp.zeros((tr, LANES), F32)
    oh4 = jnp.zeros((tr, LANES), F32)
    for j in range(TOP_K):
        rank = jnp.sum(jnp.where(lane == idx[j], pref, 0.0), axis=-1, keepdims=True).astype(jnp.int32)
        rk = jnp.where(lane == j, rank, rk)
        pw = jnp.where(lane == j, ex[j] / den, pw)
        oh4 = oh4 + jnp.where(lane == idx[j] + N_EXPERTS * j, 1.0, 0.0)
    oh4_ref[...] = oh4.astype(BF16)
    rk_ref[...] = rk
    pw_ref[...] = pw
    cnt_ref[0] = jnp.sum(onehot, axis=0, keepdims=True).astype(jnp.int32)


def _route(h1, mod, norm_w, wr_pad, br_pad, seq, tr, tile0, n):
    per_b = seq // tr
    nt = n // tr
    return pl.pallas_call(
        functools.partial(_route_kernel, tr=tr),
        out_shape=(jax.ShapeDtypeStruct((n, PACKED), jnp.int32),
                   jax.ShapeDtypeStruct((n, LANES), BF16),
                   jax.ShapeDtypeStruct((n, LANES), jnp.int32),
                   jax.ShapeDtypeStruct((n, LANES), F32),
                   jax.ShapeDtypeStruct((nt, 1, LANES), jnp.int32)),
        grid=(nt,),
        in_specs=[pl.BlockSpec((tr, D_MODEL), lambda i: (tile0 + i, 0)),
                  pl.BlockSpec((1, 6, D_MODEL), lambda i: ((tile0 + i) // per_b, 0, 0)),
                  pl.BlockSpec((1, D_MODEL), lambda i: (0, 0)),
                  pl.BlockSpec((D_MODEL, LANES), lambda i: (0, 0)),
                  pl.BlockSpec((1, LANES), lambda i: (0, 0))],
        out_specs=(pl.BlockSpec((tr, PACKED), lambda i: (i, 0)),
                   pl.BlockSpec((tr, LANES), lambda i: (i, 0)),
                   pl.BlockSpec((tr, LANES), lambda i: (i, 0)),
                   pl.BlockSpec((tr, LANES), lambda i: (i, 0)),
                   pl.BlockSpec((1, 1, LANES), lambda i: (i, 0, 0))),
        compiler_params=pltpu.CompilerParams(dimension_semantics=("arbitrary",),
                                             vmem_limit_bytes=VMEM_LIMIT),
        name="route",
    )(h1, mod, norm_w.reshape(1, D_MODEL), wr_pad, br_pad)


def _dest_kernel(oh4_ref, rk_ref, bt_ref, o_ref):
    oh = oh4_ref[...]
    start = (jnp.dot(oh, bt_ref[0, 0], preferred_element_type=F32)
             + 256.0 * jnp.dot(oh, bt_ref[0, 1], preferred_element_type=F32)
             + 65536.0 * jnp.dot(oh, bt_ref[0, 2], preferred_element_type=F32))
    o_ref[...] = (start + rk_ref[...].astype(F32)).T[:8, :].astype(jnp.int32)


def _dest(oh4, rk, bt, tr):
    n = oh4.shape[0]
    return pl.pallas_call(
        _dest_kernel,
        out_shape=jax.ShapeDtypeStruct((8, n), jnp.int32),
        grid=(n // tr,),
        in_specs=[pl.BlockSpec((tr, LANES), lambda i: (i, 0)),
                  pl.BlockSpec((tr, LANES), lambda i: (i, 0)),
                  pl.BlockSpec((1, 3, LANES, LANES), lambda i: (i, 0, 0, 0))],
        out_specs=pl.BlockSpec((8, tr), lambda i: (0, i)),
        name="dest",
    )(oh4, rk, bt)


def _sc_workers():
    info = plsc.get_sparse_core_info()
    return info.num_cores, info.num_cores * info.num_subcores


def _sc_scatter_rows(rows, idx_slots, pad_idx, n_out):
    n_cores, n_workers = _sc_workers()
    n, w = rows.shape
    k = idx_slots.shape[0] // n
    per_worker = n // n_workers
    pad_per_worker = pad_idx.shape[0] // n_workers
    assert per_worker % SC_CHUNK == 0 and pad_per_worker % SC_CHUNK == 0
    mesh = plsc.VectorSubcoreMesh(core_axis_name="c", subcore_axis_name="s")
    zeros = jnp.zeros((SC_CHUNK, w), rows.dtype)

    n_chunks = per_worker // SC_CHUNK
    assert n_chunks % 2 == 0

    @functools.partial(
        pl.kernel, mesh=mesh,
        out_type=jax.ShapeDtypeStruct((n_out, w), rows.dtype),
        scratch_types=[pltpu.VMEM((SC_CHUNK,), jnp.int32)] * k
        + [pltpu.VMEM((SC_CHUNK, w), rows.dtype)] * 2
        + [pltpu.SemaphoreType.DMA] * 2,
        name="sc_scatter",
    )
    def scatter(rows_hbm, idx_hbm, pad_hbm, zeros_hbm, out_hbm, *scratch):
        idx_bufs = scratch[:k]
        row_bufs = scratch[k:k + 2]
        sem_rows, sem_out = scratch[k + 2:]
        wid = lax.axis_index("s") * n_cores + lax.axis_index("c")

        def chunk_off(c):
            return pl.multiple_of(wid * per_worker + c * SC_CHUNK, 8)

        def load(c, b):
            pltpu.async_copy(rows_hbm.at[pl.ds(chunk_off(c), SC_CHUNK)], row_bufs[b], sem_rows)

        def wait_load(b):
            pltpu.make_async_copy(rows_hbm.at[pl.ds(0, SC_CHUNK)], row_bufs[b], sem_rows).wait()

        def scatter_chunk(c, b):
            for j in range(k):
                pltpu.sync_copy(idx_hbm.at[pl.ds(pl.multiple_of(j * n + chunk_off(c), 8), SC_CHUNK)],
                                idx_bufs[j])
            copies = [pltpu.async_copy(row_bufs[b], out_hbm.at[idx_bufs[j]], sem_out) for j in range(k)]
            for cp in copies:
                cp.wait()

        load(0, 0)

        def body(i, carry):
            wait_load(0)
            load(2 * i + 1, 1)
            scatter_chunk(2 * i, 0)
            wait_load(1)

            @pl.when(i < n_chunks // 2 - 1)
            def _():
                load(2 * i + 2, 0)

            scatter_chunk(2 * i + 1, 1)
            return carry

        lax.fori_loop(0, n_chunks // 2, body, 0)
        pltpu.sync_copy(zeros_hbm, row_bufs[0])

        def pad_body(i, carry):
            off = pl.multiple_of(wid * pad_per_worker + i * SC_CHUNK, 8)
            pltpu.sync_copy(pad_hbm.at[pl.ds(off, SC_CHUNK)], idx_bufs[0])
            pltpu.async_copy(row_bufs[0], out_hbm.at[idx_bufs[0]], sem_out).wait()
            return carry

        lax.fori_loop(0, pad_per_worker // SC_CHUNK, pad_body, 0)

    return scatter(rows, idx_slots, pad_idx, zeros)


def _experts_kernel(be_ref, nu_ref, first_ref, next_ref, slot_ref, half_ref, x_ref, w1_hbm, b1_ref, w2_hbm,
                    b2_ref, o_ref, w1_f32, w2_f32, w1_bf, w2_bf, sems):
    i = pl.program_id(0)
    used = i < nu_ref[0]

    def weight_copies(e, s):
        return (pltpu.make_async_copy(w1_hbm.at[e], w1_f32.at[s], sems.at[s]),
                pltpu.make_async_copy(w2_hbm.at[e], w2_f32.at[s], sems.at[s]))

    @pl.when(used & (first_ref[i] == 1))
    def _():
        s = slot_ref[i]

        @pl.when(i == 0)
        def _():
            for cp in weight_copies(be_ref[i], s):
                cp.start()

        for cp in weight_copies(be_ref[i], s):
            cp.wait()
        cw = 256
        for c in range(2 * D_FF // cw):
            w1_bf[:, c * cw:(c + 1) * cw] = w1_f32[s, :, c * cw:(c + 1) * cw].astype(BF16)
        for c in range(D_MODEL // cw):
            w2_bf[:, c * cw:(c + 1) * cw] = w2_f32[s, :, c * cw:(c + 1) * cw].astype(BF16)

        @pl.when(next_ref[i] >= 0)
        def _():
            for cp in weight_copies(next_ref[i], 1 - s):
                cp.start()

    def mlp(n_rows):
        step = 512
        x = _unpack_rows(x_ref[0:n_rows, :]).astype(BF16)
        acc = None
        for j in range(D_FF // step):
            cs = slice(j * step, (j + 1) * step)
            ls = slice(D_FF + j * step, D_FF + (j + 1) * step)
            glu = jnp.dot(x, w1_bf[:, cs], preferred_element_type=F32) + b1_ref[0, :, cs]
            lin = jnp.dot(x, w1_bf[:, ls], preferred_element_type=F32) + b1_ref[0, :, ls]
            glu = jnp.minimum(glu, SWIGLU_LIMIT)
            lin = jnp.clip(lin, -SWIGLU_LIMIT, SWIGLU_LIMIT)
            act = (glu * _sigmoid(glu, SWIGLU_ALPHA) * (lin + 1.0)).astype(BF16)
            part = jnp.dot(act, w2_bf[cs, :], preferred_element_type=F32)
            acc = part if acc is None else acc + part
        o_ref[0:n_rows, :] = _pack_rows(acc + b2_ref[0])
        if n_rows < ROW_BLOCK:
            o_ref[n_rows:ROW_BLOCK, :] = jnp.zeros((ROW_BLOCK - n_rows, PACKED), jnp.int32)

    @pl.when(used & (half_ref[i] == 0))
    def _():
        mlp(ROW_BLOCK)

    @pl.when(used & (half_ref[i] == 1))
    def _():
        mlp(ROW_BLOCK // 2)

    @pl.when(jnp.logical_not(used))
    def _():
        o_ref[...] = jnp.zeros_like(o_ref)


def _experts(plan, xs, w1, b1, w2, b2):
    n_rows = xs.shape[0]
    nb = n_rows // ROW_BLOCK
    grid_spec = pltpu.PrefetchScalarGridSpec(
        num_scalar_prefetch=6,
        grid=(nb,),
        in_specs=[pl.BlockSpec((ROW_BLOCK, PACKED), lambda i, be, nu, *_: (jnp.minimum(i, nu[0] - 1), 0)),
                  pl.BlockSpec(memory_space=pl.ANY),
                  pl.BlockSpec((1, 1, 2 * D_FF), lambda i, be, *_: (be[i], 0, 0)),
                  pl.BlockSpec(memory_space=pl.ANY),
                  pl.BlockSpec((1, 1, D_MODEL), lambda i, be, *_: (be[i], 0, 0))],
        out_specs=pl.BlockSpec((ROW_BLOCK, PACKED), lambda i, *_: (i, 0)),
        scratch_shapes=[pltpu.VMEM((2, D_MODEL, 2 * D_FF), F32), pltpu.VMEM((2, D_FF, D_MODEL), F32),
                        pltpu.VMEM((D_MODEL, 2 * D_FF), BF16), pltpu.VMEM((D_FF, D_MODEL), BF16),
                        pltpu.SemaphoreType.DMA((2,))],
    )
    return pl.pallas_call(
        _experts_kernel,
        out_shape=jax.ShapeDtypeStruct((n_rows, PACKED), jnp.int32),
        grid_spec=grid_spec,
        compiler_params=pltpu.CompilerParams(dimension_semantics=("arbitrary",),
                                             vmem_limit_bytes=VMEM_LIMIT),
        name="experts",
    )(*plan, xs, w1, b1.reshape(N_EXPERTS, 1, 2 * D_FF), w2, b2.reshape(N_EXPERTS, 1, D_MODEL))


def _sc_gather_rows(table, idx_flat):
    n_cores, n_workers = _sc_workers()
    n_idx = idx_flat.shape[0]
    w = table.shape[1]
    per_worker = n_idx // n_workers
    n_chunks = per_worker // SC_CHUNK
    assert per_worker * n_workers == n_idx and n_chunks * SC_CHUNK == per_worker and n_chunks % 2 == 0
    mesh = plsc.VectorSubcoreMesh(core_axis_name="c", subcore_axis_name="s")

    @functools.partial(
        pl.kernel, mesh=mesh,
        out_type=jax.ShapeDtypeStruct((n_idx, w), table.dtype),
        scratch_types=[pltpu.VMEM((SC_CHUNK,), jnp.int32), pltpu.VMEM((SC_CHUNK,), jnp.int32),
                       pltpu.VMEM((SC_CHUNK, w), table.dtype), pltpu.VMEM((SC_CHUNK, w), table.dtype),
                       pltpu.SemaphoreType.DMA, pltpu.SemaphoreType.DMA],
        name="sc_gather",
    )
    def gather(table_hbm, idx_hbm, out_hbm, idx0, idx1, rows0, rows1, sem0, sem1):
        wid = lax.axis_index("s") * n_cores + lax.axis_index("c")
        bufs = ((idx0, rows0, sem0), (idx1, rows1, sem1))

        def chunk_off(c):
            return pl.multiple_of(wid * per_worker + c * SC_CHUNK, 8)

        def start(c, b):
            idx_v, rows_v, sem = bufs[b]
            pltpu.sync_copy(idx_hbm.at[pl.ds(chunk_off(c), SC_CHUNK)], idx_v)
            pltpu.async_copy(table_hbm.at[idx_v], rows_v, sem)

        def finish(c, b):
            idx_v, rows_v, sem = bufs[b]
            pltpu.make_async_copy(table_hbm.at[idx_v], rows_v, sem).wait()
            pltpu.sync_copy(rows_v, out_hbm.at[pl.ds(chunk_off(c), SC_CHUNK)])

        start(0, 0)

        def body(i, carry):
            start(2 * i + 1, 1)
            finish(2 * i, 0)

            @pl.when(i < n_chunks // 2 - 1)
            def _():
                start(2 * i + 2, 0)

            finish(2 * i + 1, 1)
            return carry

        lax.fori_loop(0, n_chunks // 2, body, 0)

    return gather(table, idx_flat)


def _finish_kernel(h_ref, pw_ref, mod_ref, nw_ref, y0_ref, y1_ref, y2_ref, y3_ref, *rest):
    o_ref = rest[-1]
    pw = pw_ref[...]
    moe = pw[:, 0:1] * _unpack_rows(y0_ref[0])
    for j, y_ref in enumerate((y1_ref, y2_ref, y3_ref), start=1):
        moe = moe + pw[:, j:j + 1] * _unpack_rows(y_ref[0])
    gate_f = mod_ref[0, 5:6, :]
    o_ref[...] = _rms(h_ref[...] + gate_f * moe, nw_ref[...])


def _finish(h1, pw, mod, norm_w, y4, seq, tc, tile0, prev_out):
    n = pw.shape[0]
    per_b = seq // tc
    slot = lambda j: pl.BlockSpec((1, tc, PACKED), lambda i: (j, i, 0))
    in_specs = [pl.BlockSpec((tc, D_MODEL), lambda i: (tile0 + i, 0)),
                pl.BlockSpec((tc, LANES), lambda i: (i, 0)),
                pl.BlockSpec((1, 6, D_MODEL), lambda i: ((tile0 + i) // per_b, 0, 0)),
                pl.BlockSpec((1, D_MODEL), lambda i: (0, 0)),
                slot(0), slot(1), slot(2), slot(3)]
    args = [h1, pw, mod, norm_w.reshape(1, D_MODEL), y4, y4, y4, y4]
    aliases = {}
    if prev_out is not None:
        in_specs.append(pl.BlockSpec(memory_space=pl.ANY))
        args.append(prev_out)
        aliases = {len(args) - 1: 0}
    return pl.pallas_call(
        _finish_kernel,
        out_shape=jax.ShapeDtypeStruct(h1.shape, F32),
        grid=(n // tc,),
        in_specs=in_specs,
        out_specs=pl.BlockSpec((tc, D_MODEL), lambda i: (tile0 + i, 0)),
        input_output_aliases=aliases,
        compiler_params=pltpu.CompilerParams(dimension_semantics=("arbitrary",),
                                             vmem_limit_bytes=VMEM_LIMIT),
        name="finish",
    )(*args)


def _moe_plan(counts, n_assign):
    cnt = counts[:, 0, :N_EXPERTS]
    sizes = jnp.sum(cnt, axis=0)
    padded = (sizes + ROW_BLOCK - 1) // ROW_BLOCK * ROW_BLOCK
    pad_end = jnp.cumsum(padded)
    pad_start = pad_end - padded
    tile_base = pad_start[None, :] + jnp.cumsum(cnt, axis=0) - cnt
    digits = jnp.stack([tile_base % 256, (tile_base // 256) % 256, tile_base // 65536], axis=1)
    rows = jnp.tile(digits, (1, 1, TOP_K))
    slot_of_row = jnp.arange(LANES, dtype=jnp.int32) // N_EXPERTS
    col = jnp.arange(LANES, dtype=jnp.int32)
    bt = jnp.where(slot_of_row[:, None] == col[None, :], rows[..., None], 0).astype(BF16)
    nb = n_assign // ROW_BLOCK + N_EXPERTS
    block_start = jnp.arange(nb, dtype=jnp.int32) * ROW_BLOCK
    block_e = jnp.minimum(jnp.sum(pad_end[None, :] <= block_start[:, None], axis=1),
                          N_EXPERTS - 1).astype(jnp.int32)
    n_used = (pad_end[-1] // ROW_BLOCK).astype(jnp.int32).reshape(1)
    nonempty = padded > 0
    ids = jnp.arange(N_EXPERTS, dtype=jnp.int32)
    suffix_min = lax.cummin(jnp.where(nonempty, ids, N_EXPERTS), reverse=True)
    after = jnp.concatenate([suffix_min[1:], jnp.full((1,), N_EXPERTS, jnp.int32)])
    next_nonempty = jnp.where(after < N_EXPERTS, after, -1).astype(jnp.int32)
    slot_of = ((jnp.cumsum(nonempty) - 1) % 2).astype(jnp.int32)
    first = jnp.concatenate([jnp.ones((1,), jnp.int32),
                             (block_e[1:] != block_e[:-1]).astype(jnp.int32)])
    valid_rows = (pad_start + sizes)[block_e] - block_start
    half_full = (valid_rows <= ROW_BLOCK // 2).astype(jnp.int32)
    expert_plan = (block_e, n_used, first, next_nonempty[block_e], slot_of[block_e], half_full)
    r = jnp.arange(ROW_BLOCK, dtype=jnp.int32)[None, :]
    pad_idx = jnp.where(r < (padded - sizes)[:, None], (pad_start + sizes)[:, None] + r,
                        (nb - 1) * ROW_BLOCK + r).astype(jnp.int32).reshape(PAD_SLOTS)
    return bt, expert_plan, pad_idx


def kernel(x, c, w_ada, b_ada, norm_mix_w, w_in, hgrn_lower_bounds, hgrn_norm_w, conv_w,
           w_hgrn_out, w_conv_out, w_mix_out, norm_ffn_w, w_router, b_router, w1, b1, w2, b2,
           norm_final_w):
    bsz, seq, d = x.shape
    assert d == D_MODEL and seq % CHUNK == 0
    n = bsz * seq
    depth = w_ada.shape[0]
    tile = min(512, seq)
    assert seq % tile == 0 and n % (MOE_GROUPS * tile) == 0 and (n // MOE_GROUPS * TOP_K) % ROW_BLOCK == 0
    h = x.reshape(n, d)
    wr_pad = jnp.zeros((depth, D_MODEL, LANES), BF16).at[:, :, :N_EXPERTS].set(w_router.astype(BF16))
    br_pad = jnp.zeros((depth, 1, LANES), F32).at[:, 0, :N_EXPERTS].set(b_router)
    for layer in range(depth):
        mod = _ada(c, w_ada[layer], b_ada[layer]).reshape(bsz, 6, d)
        proj = _inproj(h, mod, norm_mix_w[layer], w_in[layer].astype(BF16), seq, tile)
        h = _mix(proj, h, mod, hgrn_lower_bounds, hgrn_norm_w[layer], conv_w[layer],
                 w_hgrn_out[layer].astype(BF16), w_conv_out[layer].astype(BF16),
                 w_mix_out[layer].astype(BF16), layer, bsz, seq, tile)
        assert layer == depth - 1, "only the last layer applies the final norm"
        ng = n // MOE_GROUPS
        n_rows = (ng * TOP_K // ROW_BLOCK + N_EXPERTS) * ROW_BLOCK
        routed = [_route(h, mod, norm_ffn_w[layer], wr_pad[layer], br_pad[layer], seq, tile,
                         grp * (ng // tile), ng) for grp in range(MOE_GROUPS)]
        plans = jax.vmap(lambda cnt: _moe_plan(cnt, ng * TOP_K))(jnp.stack([r[4] for r in routed]))
        out = None
        for grp in range(MOE_GROUPS):
            tile0 = grp * (ng // tile)
            u2, oh4, rk, pw, _ = routed[grp]
            bt, expert_plan, pad_idx = jax.tree.map(lambda a: a[grp], plans)
            dest_slots = _dest(oh4, rk, bt, tile)[:TOP_K].reshape(TOP_K * ng)
            xs = _sc_scatter_rows(u2, dest_slots, pad_idx, n_rows)
            ys = _experts(expert_plan, xs, w1[layer], b1[layer], w2[layer], b2[layer])
            y4 = _sc_gather_rows(ys, dest_slots).reshape(TOP_K, ng, PACKED)
            out = _finish(h, pw, mod, norm_final_w, y4, seq, tile, tile0, out)
        h = out
    return h.reshape(bsz, seq, d)
```

```python
import functools

import jax
import jax.numpy as jnp
from jax import lax
from jax.experimental import pallas as pl
from jax.experimental.pallas import tpu as pltpu
from jax.experimental.pallas import tpu_sc as plsc

F32 = jnp.float32
BF16 = jnp.bfloat16

D_MODEL = 1024
HGRN_HEADS = 4
HEAD_DIM = 128
HGRN_WIDTH = HGRN_HEADS * HEAD_DIM
CONV_WIDTH = 512
CONV_K = 3
CHUNK = 64
N_EXPERTS = 32
TOP_K = 4
D_FF = 1024
SWIGLU_LIMIT = 7.0
SWIGLU_ALPHA = 1.702
EPS = 1e-6
LOG2_E = 1.4426950408889634
IN_COLS = 4 * HGRN_WIDTH + 3 * CONV_WIDTH + 2 * D_MODEL
LANES = 128
ROW_BLOCK = 512
PACKED = D_MODEL // 2
MOE_GROUPS = 2
SC_CHUNK = 64
PAD_SLOTS = N_EXPERTS * ROW_BLOCK
VMEM_LIMIT = 56 * 1024 * 1024


def _sigmoid(x, scale=1.0):
    return 1.0 / (1.0 + jnp.exp2(x * (-scale * LOG2_E)))


def _rms(x, w):
    ms = jnp.mean(x * x, axis=-1, keepdims=True)
    return x * lax.rsqrt(ms + EPS) * w


def _pack_rows(x):
    w = x.shape[1] // 2
    lo = lax.bitcast_convert_type(x[:, :w].astype(BF16).astype(F32), jnp.uint32)
    hi = lax.bitcast_convert_type(x[:, w:].astype(BF16).astype(F32), jnp.uint32)
    return lax.bitcast_convert_type((lo >> 16) | (hi & jnp.uint32(0xFFFF0000)), jnp.int32)


def _unpack_rows(p):
    u = lax.bitcast_convert_type(p, jnp.uint32)
    lo = lax.bitcast_convert_type(u << 16, F32)
    hi = lax.bitcast_convert_type(u & jnp.uint32(0xFFFF0000), F32)
    return jnp.concatenate([lo, hi], axis=1)


def _nt_dot(a, b):
    return lax.dot_general(a, b, (((1,), (1,)), ((), ())), preferred_element_type=F32)


def _tn_dot(a, b):
    return lax.dot_general(a, b, (((0,), (0,)), ((), ())), preferred_element_type=F32)


def _ada_kernel(c_ref, w_ref, b_ref, o_ref):
    c = c_ref[...]
    sc = (c * _sigmoid(c)).astype(BF16)
    o_ref[...] = jnp.dot(sc, w_ref[...].astype(BF16), preferred_element_type=F32) + b_ref[...]


def _ada(c, w_ada, b_ada):
    bsz, d = c.shape
    n = w_ada.shape[1]
    return pl.pallas_call(
        _ada_kernel,
        out_shape=jax.ShapeDtypeStruct((bsz, n), F32),
        grid=(n // d,),
        in_specs=[pl.BlockSpec((bsz, d), lambda j: (0, 0)),
                  pl.BlockSpec((d, d), lambda j: (0, j)),
                  pl.BlockSpec((1, d), lambda j: (0, j))],
        out_specs=pl.BlockSpec((bsz, d), lambda j: (0, j)),
        name="ada",
    )(c, w_ada, b_ada.reshape(1, n))


def _inproj_kernel(x_ref, mod_ref, nw_ref, w_ref, o_ref, u_scr):
    y = _rms(x_ref[...], nw_ref[...])
    shift = mod_ref[0, 0:1, :]
    scale = mod_ref[0, 1:2, :]
    u_scr[...] = (y * (1.0 + scale) + shift).astype(BF16)

    step = 512
    for j in range(IN_COLS // step):
        o_ref[:, j * step:(j + 1) * step] = jnp.dot(
            u_scr[...], w_ref[:, j * step:(j + 1) * step], preferred_element_type=F32).astype(BF16)


def _inproj(x2, mod, norm_w, w_in_bf, seq, tm):
    n = x2.shape[0]
    per_b = seq // tm
    return pl.pallas_call(
        _inproj_kernel,
        out_shape=jax.ShapeDtypeStruct((n, IN_COLS), BF16),
        grid=(n // tm,),
        in_specs=[pl.BlockSpec((tm, D_MODEL), lambda i: (i, 0)),
                  pl.BlockSpec((1, 6, D_MODEL), lambda i: (i // per_b, 0, 0)),
                  pl.BlockSpec((1, D_MODEL), lambda i: (0, 0)),
                  pl.BlockSpec((D_MODEL, IN_COLS), lambda i: (0, 0))],
        out_specs=pl.BlockSpec((tm, IN_COLS), lambda i: (i, 0)),
        scratch_shapes=[pltpu.VMEM((tm, D_MODEL), BF16)],
        compiler_params=pltpu.CompilerParams(dimension_semantics=("arbitrary",),
                                             vmem_limit_bytes=VMEM_LIMIT),
        name="inproj",
    )(x2, mod, norm_w.reshape(1, D_MODEL), w_in_bf)


_HEADS = [slice(h * HEAD_DIM, (h + 1) * HEAD_DIM) for h in range(HGRN_HEADS)]


def _chunk_rows(c):
    if isinstance(c, int):
        return pl.ds(c * CHUNK, CHUNK)
    return pl.ds(pl.multiple_of(c * CHUNK, CHUNK), CHUNK)


def _hgrn_gates(c, lb, tri, f_ref, b_scr, k_scr, f_scr):
    fx = f_ref[_chunk_rows(c), :].astype(F32)
    f = lb + (1.0 - lb) * _sigmoid(fx)
    g = jnp.maximum(jnp.log(f), -128.0) * LOG2_E
    g1 = g.astype(BF16)
    r1 = g - g1.astype(F32)
    g2 = r1.astype(BF16)
    g3 = (r1 - g2.astype(F32)).astype(BF16)
    b_scr[...] = (jnp.dot(tri, g1, preferred_element_type=F32)
                  + jnp.dot(tri, g2, preferred_element_type=F32)
                  + jnp.dot(tri, g3, preferred_element_type=F32))
    f_scr[...] = f
    k_scr[...] = 1.0 - f


def _hgrn_scores(c, q_ref, b_scr, k_scr, f_scr):
    rows = _chunk_rows(c)
    qs = [q_ref[rows, hs].astype(F32) for hs in _HEADS]
    s_mats = [None] * HGRN_HEADS
    for lvl in range(1, 7):
        for h in range(HGRN_HEADS):
            s_mats[h] = _level_scores(lvl, qs[h], _HEADS[h], b_scr, k_scr, f_scr, s_mats[h])
    return s_mats


def _hgrn_outputs(c, s_mats, q_ref, v_ref, st_scr, o_scr, b_scr, k_scr):
    rows = _chunk_rows(c)
    for h, hs in enumerate(_HEADS):
        q = q_ref[rows, hs].astype(F32)
        v_bf = v_ref[rows, hs]
        b = b_scr[:, hs]
        kk = k_scr[:, hs]
        b_last = b_scr[CHUNK - 1:CHUNK, hs]
        st = st_scr[h]
        qd = (q * jnp.exp2(b)).astype(BF16)
        kdec = (kk * jnp.exp2(b_last - b)).astype(BF16)
        diag = jnp.sum(q * kk, axis=-1, keepdims=True)
        o_scr[rows, hs] = (_nt_dot(qd, st.astype(BF16))
                           + jnp.dot(s_mats[h].astype(BF16), v_bf, preferred_element_type=F32)
                           + diag * v_bf.astype(F32))
        st_scr[h] = jnp.exp2(b_last) * st + _tn_dot(v_bf, kdec)


def _level_scores(lvl, q, hs, b_scr, k_scr, f_scr, acc):
    row = lax.broadcasted_iota(jnp.int32, (CHUNK, 1), 0)
    col = lax.broadcasted_iota(jnp.int32, (1, CHUNK), 1)
    b = b_scr[:, hs]
    kk = k_scr[:, hs]
    blk = 1 << lvl
    half = blk // 2
    first_half_col = (col & (blk - 1)) < half
    if lvl == 1:
        odd = (row & 1) == 1
        qx = jnp.where(odd, q * f_scr[:, hs], 0.0).astype(BF16)
        kx = jnp.where(odd, 0.0, kk).astype(BF16)
        return jnp.where((row >> 1) == (col >> 1), _nt_dot(qx, kx), 0.0)
    if half < 8:
        groups = []
        sub = lax.broadcasted_iota(jnp.int32, (8, 1), 0)
        for j in range(CHUNK // 8):
            rj = None
            for k in reversed(range(8 // blk)):
                m = 8 * j + k * blk + half - 1
                bm = jnp.broadcast_to(b_scr[m:m + 1, hs], (8, HEAD_DIM))
                rj = bm if rj is None else jnp.where(sub < (k + 1) * blk, bm, rj)
            groups.append(rj)
        ref = jnp.concatenate(groups, axis=0)
        second = (row & (blk - 1)) >= half
        qx = (q * jnp.exp2(jnp.where(second, b - ref, -jnp.inf))).astype(BF16)
        kx = (kk * jnp.exp2(jnp.where(second, -jnp.inf, ref - b))).astype(BF16)
        mine = ((row >> lvl) == (col >> lvl)) & second & first_half_col
        return jnp.where(mine, _nt_dot(qx, kx), acc)
    n_blk = CHUNK // blk
    qparts, kparts = [], []
    for j in range(n_blk):
        m = j * blk + half - 1
        bm = b_scr[m:m + 1, hs]
        tq = slice(j * blk + half, (j + 1) * blk)
        tk = slice(j * blk, j * blk + half)
        qparts.append(q[tq] * jnp.exp2(b[tq] - bm))
        kparts.append(kk[tk] * jnp.exp2(bm - b[tk]))
        kparts.append(jnp.zeros((half, HEAD_DIM), F32))
    qx = jnp.concatenate(qparts, axis=0).astype(BF16)
    kx = jnp.concatenate(kparts, axis=0).astype(BF16)
    sc = _nt_dot(qx, kx)
    pieces = []
    for j in range(n_blk):
        mine = ((col >> lvl) == j) & first_half_col
        pieces.append(acc[j * blk:j * blk + half])
        pieces.append(jnp.where(mine, sc[j * half:(j + 1) * half], acc[j * blk + half:(j + 1) * blk]))
    return jnp.concatenate(pieces, axis=0)


def _mix_kernel(lbt_ref, q_ref, f_ref, i_ref, g_ref, cb_ref, cc_ref, ch_ref, ga0_ref, ga1_ref,
                gb0_ref, gb1_ref, x_ref, mod_ref, hnw_ref, cw_ref, wa_ref, wb_ref, wm_ref,
                o_ref, st_scr, o_scr, carry_scr, b_scr, k_scr, f_scr, *, layer, rows_per_step, epi_rows):
    @pl.when(pl.program_id(1) == 0)
    def _():
        st_scr[...] = jnp.zeros_like(st_scr)
        carry_scr[0:8, :] = jnp.zeros((8, CONV_WIDTH), F32)

    tab = lbt_ref[...]
    tmax = jnp.max(tab, axis=0, keepdims=True)
    te = jnp.exp(tab - tmax)
    lb = jnp.sum(te[0:layer + 1], axis=0, keepdims=True) / jnp.sum(te, axis=0, keepdims=True)

    ri = lax.broadcasted_iota(jnp.int32, (CHUNK, CHUNK), 0)
    ci = lax.broadcasted_iota(jnp.int32, (CHUNK, CHUNK), 1)
    tri = jnp.where(ci <= ri, 1.0, 0.0).astype(BF16)

    n_chunks = rows_per_step // CHUNK

    def slot(s):
        return b_scr.at[s], k_scr.at[s], f_scr.at[s]

    def chunk(c, c_next, s):
        bs, ks, fs = slot(s)
        s_mats = _hgrn_scores(c, q_ref, bs, ks, fs)
        _hgrn_gates(c_next, lb, tri, f_ref, *slot(1 - s))
        _hgrn_outputs(c, s_mats, q_ref, i_ref, st_scr, o_scr, bs, ks)

    def pair_body(i, carry):
        chunk(2 * i, 2 * i + 1, 0)
        chunk(2 * i + 1, jnp.minimum(2 * i + 2, n_chunks - 1), 1)
        return carry

    _hgrn_gates(0, lb, tri, f_ref, *slot(0))
    lax.fori_loop(0, n_chunks // 2, pair_body, 0)

    gate_m = mod_ref[0, 2:3, :]
    hnw = hnw_ref[...]
    cw0 = cw_ref[0:1, :]
    cw1 = cw_ref[1:2, :]
    cw2 = cw_ref[2:3, :]
    for r in range(rows_per_step // epi_rows):
        rs = slice(r * epi_rows, (r + 1) * epi_rows)
        parts = []
        for h in range(HGRN_HEADS):
            hs = slice(h * HEAD_DIM, (h + 1) * HEAD_DIM)
            oh = _rms(o_scr[rs, hs], hnw)
            go = g_ref[rs, hs].astype(F32)
            parts.append((oh * (go * _sigmoid(go))).astype(BF16))
        ya = jnp.dot(jnp.concatenate(parts, axis=1), wa_ref[...], preferred_element_type=F32)
        uc = cc_ref[rs, :].astype(F32) * ch_ref[rs, :].astype(F32)
        carry_scr[8:8 + epi_rows, :] = uc
        s1 = carry_scr[7:7 + epi_rows, :]
        s2 = carry_scr[6:6 + epi_rows, :]
        carry_scr[0:8, :] = uc[epi_rows - 8:epi_rows, :]
        yc = cb_ref[rs, :].astype(F32) * (cw2 * uc + cw1 * s1 + cw0 * s2)
        yb = jnp.dot(yc.astype(BF16), wb_ref[...], preferred_element_type=F32)
        ga = jnp.concatenate([ga0_ref[rs, :], ga1_ref[rs, :]], axis=1).astype(F32)
        gb = jnp.concatenate([gb0_ref[rs, :], gb1_ref[rs, :]], axis=1).astype(F32)
        merged = (_sigmoid(ga) * ya + _sigmoid(gb) * yb).astype(BF16)
        o_ref[rs, :] = x_ref[rs, :] + gate_m * jnp.dot(merged, wm_ref[...], preferred_element_type=F32)


def _mix(proj, x2, mod, lb_table, hgrn_norm_w, conv_w, wa, wb, wm, layer, bsz, seq, tt):
    n = x2.shape[0]
    per_b = seq // tt

    def col(col_block):
        return pl.BlockSpec((tt, 512), lambda b, t: (b * per_b + t, col_block))

    const = lambda shape: pl.BlockSpec(shape, lambda b, t: (0,) * len(shape))
    in_specs = [
        const(lb_table.shape),
        col(0), col(1), col(2), col(3),
        col(4), col(5), col(6),
        col(7), col(8), col(9), col(10),
        pl.BlockSpec((tt, D_MODEL), lambda b, t: (b * per_b + t, 0)),
        pl.BlockSpec((1, 6, D_MODEL), lambda b, t: (b, 0, 0)),
        const((1, HEAD_DIM)), const((CONV_K, CONV_WIDTH)),
        const((HGRN_WIDTH, D_MODEL)), const((CONV_WIDTH, D_MODEL)), const((D_MODEL, D_MODEL)),
    ]
    kern = functools.partial(_mix_kernel, layer=layer, rows_per_step=tt, epi_rows=tt)
    return pl.pallas_call(
        kern,
        out_shape=jax.ShapeDtypeStruct((n, D_MODEL), F32),
        grid=(bsz, per_b),
        in_specs=in_specs,
        out_specs=pl.BlockSpec((tt, D_MODEL), lambda b, t: (b * per_b + t, 0)),
        scratch_shapes=[pltpu.VMEM((HGRN_HEADS, HEAD_DIM, HEAD_DIM), F32),
                        pltpu.VMEM((tt, HGRN_WIDTH), F32),
                        pltpu.VMEM((8 + tt, CONV_WIDTH), F32),
                        pltpu.VMEM((2, CHUNK, HGRN_WIDTH), F32),
                        pltpu.VMEM((2, CHUNK, HGRN_WIDTH), F32),
                        pltpu.VMEM((2, CHUNK, HGRN_WIDTH), F32)],
        compiler_params=pltpu.CompilerParams(dimension_semantics=("arbitrary", "arbitrary"),
                                             vmem_limit_bytes=VMEM_LIMIT),
        name="mix",
    )(lb_table, *([proj] * 11),
      x2, mod, hgrn_norm_w.reshape(1, HEAD_DIM), conv_w, wa, wb, wm)


def _route_kernel(h_ref, mod_ref, nw_ref, wr_ref, br_ref, u_ref, oh4_ref, rk_ref, pw_ref, cnt_ref, *, tr):
    shift = mod_ref[0, 3:4, :]
    scale = mod_ref[0, 4:5, :]
    u = _rms(h_ref[...], nw_ref[...]) * (1.0 + scale) + shift
    u_ref[...] = _pack_rows(u)
    lane = lax.broadcasted_iota(jnp.int32, (tr, LANES), 1).astype(F32)
    logits = jnp.dot(u.astype(BF16), wr_ref[...], preferred_element_type=F32) + br_ref[...]
    logits = jnp.where(lane < N_EXPERTS, logits, -jnp.inf)
    idx, val = [], []
    cur = logits
    for _ in range(TOP_K):
        m = jnp.max(cur, axis=-1, keepdims=True)
        i = jnp.min(jnp.where(cur == m, lane, float(LANES)), axis=-1, keepdims=True)
        idx.append(i)
        val.append(m)
        cur = jnp.where(lane == i, -jnp.inf, cur)
    ex = [jnp.exp(v - val[0]) for v in val]
    den = ex[0] + ex[1] + ex[2] + ex[3]
    onehot = jnp.zeros((tr, LANES), F32)
    for i in idx:
        onehot = onehot + jnp.where(lane == i, 1.0, 0.0)
    ri = lax.broadcasted_iota(jnp.int32, (tr, tr), 0)
    ci = lax.broadcasted_iota(jnp.int32, (tr, tr), 1)
    tri = jnp.where(ci < ri, 1.0, 0.0).astype(BF16)
    pref = jnp.dot(tri, onehot.astype(BF16), preferred_element_type=F32)
    rk = jnp.zeros((tr, LANES), jnp.int32)
    pw = jnp.zeros((tr, LANES), F32)
    oh4 = jnp.zeros((tr, LANES), F32)
    for j in range(TOP_K):
        rank = jnp.sum(jnp.where(lane == idx[j], pref, 0.0), axis=-1, keepdims=True).astype(jnp.int32)
        rk = jnp.where(lane == j, rank, rk)
        pw = jnp.where(lane == j, ex[j] / den, pw)
        oh4 = oh4 + jnp.where(lane == idx[j] + N_EXPERTS * j, 1.0, 0.0)
    oh4_ref[...] = oh4.astype(BF16)
    rk_ref[...] = rk
    pw_ref[...] = pw
    cnt_ref[0] = jnp.sum(onehot, axis=0, keepdims=True).astype(jnp.int32)


def _route(h1, mod, norm_w, wr_pad, br_pad, seq, tr, tile0, n):
    per_b = seq // tr
    nt = n // tr
    return pl.pallas_call(
        functools.partial(_route_kernel, tr=tr),
        out_shape=(jax.ShapeDtypeStruct((n, PACKED), jnp.int32),
                   jax.ShapeDtypeStruct((n, LANES), BF16),
                   jax.ShapeDtypeStruct((n, LANES), jnp.int32),
                   jax.ShapeDtypeStruct((n, LANES), F32),
                   jax.ShapeDtypeStruct((nt, 1, LANES), jnp.int32)),
        grid=(nt,),
        in_specs=[pl.BlockSpec((tr, D_MODEL), lambda i: (tile0 + i, 0)),
                  pl.BlockSpec((1, 6, D_MODEL), lambda i: ((tile0 + i) // per_b, 0, 0)),
                  pl.BlockSpec((1, D_MODEL), lambda i: (0, 0)),
                  pl.BlockSpec((D_MODEL, LANES), lambda i: (0, 0)),
                  pl.BlockSpec((1, LANES), lambda i: (0, 0))],
        out_specs=(pl.BlockSpec((tr, PACKED), lambda i: (i, 0)),
                   pl.BlockSpec((tr, LANES), lambda i: (i, 0)),
                   pl.BlockSpec((tr, LANES), lambda i: (i, 0)),
                   pl.BlockSpec((tr, LANES), lambda i: (i, 0)),
                   pl.BlockSpec((1, 1, LANES), lambda i: (i, 0, 0))),
        compiler_params=pltpu.CompilerParams(dimension_semantics=("arbitrary",),
                                             vmem_limit_bytes=VMEM_LIMIT),
        name="route",
    )(h1, mod, norm_w.reshape(1, D_MODEL), wr_pad, br_pad)


def _dest_kernel(oh4_ref, rk_ref, bt_ref, o_ref):
    oh = oh4_ref[...]
    start = (jnp.dot(oh, bt_ref[0, 0], preferred_element_type=F32)
             + 256.0 * jnp.dot(oh, bt_ref[0, 1], preferred_element_type=F32)
             + 65536.0 * jnp.dot(oh, bt_ref[0, 2], preferred_element_type=F32))
    o_ref[...] = (start + rk_ref[...].astype(F32)).T[:8, :].astype(jnp.int32)


def _dest(oh4, rk, bt, tr):
    n = oh4.shape[0]
    return pl.pallas_call(
        _dest_kernel,
        out_shape=jax.ShapeDtypeStruct((8, n), jnp.int32),
        grid=(n // tr,),
        in_specs=[pl.BlockSpec((tr, LANES), lambda i: (i, 0)),
                  pl.BlockSpec((tr, LANES), lambda i: (i, 0)),
                  pl.BlockSpec((1, 3, LANES, LANES), lambda i: (i, 0, 0, 0))],
        out_specs=pl.BlockSpec((8, tr), lambda i: (0, i)),
        name="dest",
    )(oh4, rk, bt)


def _sc_workers():
    info = plsc.get_sparse_core_info()
    return info.num_cores, info.num_cores * info.num_subcores


def _sc_scatter_rows(rows, idx_slots, pad_idx, n_out):
    n_cores, n_workers = _sc_workers()
    n, w = rows.shape
    k = idx_slots.shape[0] // n
    per_worker = n // n_workers
    pad_per_worker = pad_idx.shape[0] // n_workers
    assert per_worker % SC_CHUNK == 0 and pad_per_worker % SC_CHUNK == 0
    mesh = plsc.VectorSubcoreMesh(core_axis_name="c", subcore_axis_name="s")
    zeros = jnp.zeros((SC_CHUNK, w), rows.dtype)

    n_chunks = per_worker // SC_CHUNK
    assert n_chunks % 2 == 0

    @functools.partial(
        pl.kernel, mesh=mesh,
        out_type=jax.ShapeDtypeStruct((n_out, w), rows.dtype),
        scratch_types=[pltpu.VMEM((SC_CHUNK,), jnp.int32)] * k
        + [pltpu.VMEM((SC_CHUNK, w), rows.dtype)] * 2
        + [pltpu.SemaphoreType.DMA] * 2,
        name="sc_scatter",
    )
    def scatter(rows_hbm, idx_hbm, pad_hbm, zeros_hbm, out_hbm, *scratch):
        idx_bufs = scratch[:k]
        row_bufs = scratch[k:k + 2]
        sem_rows, sem_out = scratch[k + 2:]
        wid = lax.axis_index("s") * n_cores + lax.axis_index("c")

        def chunk_off(c):
            return pl.multiple_of(wid * per_worker + c * SC_CHUNK, 8)

        def load(c, b):
            pltpu.async_copy(rows_hbm.at[pl.ds(chunk_off(c), SC_CHUNK)], row_bufs[b], sem_rows)

        def wait_load(b):
            pltpu.make_async_copy(rows_hbm.at[pl.ds(0, SC_CHUNK)], row_bufs[b], sem_rows).wait()

        def scatter_chunk(c, b):
            for j in range(k):
                pltpu.sync_copy(idx_hbm.at[pl.ds(pl.multiple_of(j * n + chunk_off(c), 8), SC_CHUNK)],
                                idx_bufs[j])
            copies = [pltpu.async_copy(row_bufs[b], out_hbm.at[idx_bufs[j]], sem_out) for j in range(k)]
            for cp in copies:
                cp.wait()

        load(0, 0)

        def body(i, carry):
            wait_load(0)
            load(2 * i + 1, 1)
            scatter_chunk(2 * i, 0)
            wait_load(1)

            @pl.when(i < n_chunks // 2 - 1)
            def _():
                load(2 * i + 2, 0)

            scatter_chunk(2 * i + 1, 1)
            return carry

        lax.fori_loop(0, n_chunks // 2, body, 0)
        pltpu.sync_copy(zeros_hbm, row_bufs[0])

        def pad_body(i, carry):
            off = pl.multiple_of(wid * pad_per_worker + i * SC_CHUNK, 8)
            pltpu.sync_copy(pad_hbm.at[pl.ds(off, SC_CHUNK)], idx_bufs[0])
            pltpu.async_copy(row_bufs[0], out_hbm.at[idx_bufs[0]], sem_out).wait()
            return carry

        lax.fori_loop(0, pad_per_worker // SC_CHUNK, pad_body, 0)

    return scatter(rows, idx_slots, pad_idx, zeros)


def _experts_kernel(be_ref, nu_ref, first_ref, next_ref, slot_ref, quarters_ref, x_ref, w1_hbm, b1_ref, w2_hbm,
                    b2_ref, o_ref, w1_f32, w2_f32, w1_bf, w2_bf, sems):
    i = pl.program_id(0)
    used = i < nu_ref[0]

    def weight_copies(e, s):
        return (pltpu.make_async_copy(w1_hbm.at[e], w1_f32.at[s], sems.at[s]),
                pltpu.make_async_copy(w2_hbm.at[e], w2_f32.at[s], sems.at[s]))

    @pl.when(used & (first_ref[i] == 1))
    def _():
        s = slot_ref[i]

        @pl.when(i == 0)
        def _():
            for cp in weight_copies(be_ref[i], s):
                cp.start()

        for cp in weight_copies(be_ref[i], s):
            cp.wait()
        cw = 256
        for c in range(2 * D_FF // cw):
            w1_bf[:, c * cw:(c + 1) * cw] = w1_f32[s, :, c * cw:(c + 1) * cw].astype(BF16)
        for c in range(D_MODEL // cw):
            w2_bf[:, c * cw:(c + 1) * cw] = w2_f32[s, :, c * cw:(c + 1) * cw].astype(BF16)

        @pl.when(next_ref[i] >= 0)
        def _():
            for cp in weight_copies(next_ref[i], 1 - s):
                cp.start()

    def mlp(n_rows):
        step = 512
        x = _unpack_rows(x_ref[0:n_rows, :]).astype(BF16)
        acc = None
        for j in range(D_FF // step):
            cs = slice(j * step, (j + 1) * step)
            ls = slice(D_FF + j * step, D_FF + (j + 1) * step)
            glu = jnp.dot(x, w1_bf[:, cs], preferred_element_type=F32) + b1_ref[0, :, cs]
            lin = jnp.dot(x, w1_bf[:, ls], preferred_element_type=F32) + b1_ref[0, :, ls]
            glu = jnp.minimum(glu, SWIGLU_LIMIT)
            lin = jnp.clip(lin, -SWIGLU_LIMIT, SWIGLU_LIMIT)
            act = (glu * _sigmoid(glu, SWIGLU_ALPHA) * (lin + 1.0)).astype(BF16)
            part = jnp.dot(act, w2_bf[cs, :], preferred_element_type=F32)
            acc = part if acc is None else acc + part
        o_ref[0:n_rows, :] = _pack_rows(acc + b2_ref[0])
        if n_rows < ROW_BLOCK:
            o_ref[n_rows:ROW_BLOCK, :] = jnp.zeros((ROW_BLOCK - n_rows, PACKED), jnp.int32)

    for quarters in range(1, 5):
        @pl.when(used & (quarters_ref[i] == quarters))
        def _():
            mlp(quarters * (ROW_BLOCK // 4))

    @pl.when(jnp.logical_not(used))
    def _():
        o_ref[...] = jnp.zeros_like(o_ref)


def _experts(plan, xs, w1, b1, w2, b2):
    n_rows = xs.shape[0]
    nb = n_rows // ROW_BLOCK
    grid_spec = pltpu.PrefetchScalarGridSpec(
        num_scalar_prefetch=6,
        grid=(nb,),
        in_specs=[pl.BlockSpec((ROW_BLOCK, PACKED), lambda i, be, nu, *_: (jnp.minimum(i, nu[0] - 1), 0)),
                  pl.BlockSpec(memory_space=pl.ANY),
                  pl.BlockSpec((1, 1, 2 * D_FF), lambda i, be, *_: (be[i], 0, 0)),
                  pl.BlockSpec(memory_space=pl.ANY),
                  pl.BlockSpec((1, 1, D_MODEL), lambda i, be, *_: (be[i], 0, 0))],
        out_specs=pl.BlockSpec((ROW_BLOCK, PACKED), lambda i, *_: (i, 0)),
        scratch_shapes=[pltpu.VMEM((2, D_MODEL, 2 * D_FF), F32), pltpu.VMEM((2, D_FF, D_MODEL), F32),
                        pltpu.VMEM((D_MODEL, 2 * D_FF), BF16), pltpu.VMEM((D_FF, D_MODEL), BF16),
                        pltpu.SemaphoreType.DMA((2,))],
    )
    return pl.pallas_call(
        _experts_kernel,
        out_shape=jax.ShapeDtypeStruct((n_rows, PACKED), jnp.int32),
        grid_spec=grid_spec,
        compiler_params=pltpu.CompilerParams(dimension_semantics=("arbitrary",),
                                             vmem_limit_bytes=VMEM_LIMIT),
        name="experts",
    )(*plan, xs, w1, b1.reshape(N_EXPERTS, 1, 2 * D_FF), w2, b2.reshape(N_EXPERTS, 1, D_MODEL))


def _sc_gather_rows(table, idx_flat):
    n_cores, n_workers = _sc_workers()
    n_idx = idx_flat.shape[0]
    w = table.shape[1]
    per_worker = n_idx // n_workers
    n_chunks = per_worker // SC_CHUNK
    assert per_worker * n_workers == n_idx and n_chunks * SC_CHUNK == per_worker and n_chunks % 2 == 0
    mesh = plsc.VectorSubcoreMesh(core_axis_name="c", subcore_axis_name="s")

    @functools.partial(
        pl.kernel, mesh=mesh,
        out_type=jax.ShapeDtypeStruct((n_idx, w), table.dtype),
        scratch_types=[pltpu.VMEM((SC_CHUNK,), jnp.int32), pltpu.VMEM((SC_CHUNK,), jnp.int32),
                       pltpu.VMEM((SC_CHUNK, w), table.dtype), pltpu.VMEM((SC_CHUNK, w), table.dtype),
                       pltpu.SemaphoreType.DMA, pltpu.SemaphoreType.DMA],
        name="sc_gather",
    )
    def gather(table_hbm, idx_hbm, out_hbm, idx0, idx1, rows0, rows1, sem0, sem1):
        wid = lax.axis_index("s") * n_cores + lax.axis_index("c")
        bufs = ((idx0, rows0, sem0), (idx1, rows1, sem1))

        def chunk_off(c):
            return pl.multiple_of(wid * per_worker + c * SC_CHUNK, 8)

        def start(c, b):
            idx_v, rows_v, sem = bufs[b]
            pltpu.sync_copy(idx_hbm.at[pl.ds(chunk_off(c), SC_CHUNK)], idx_v)
            pltpu.async_copy(table_hbm.at[idx_v], rows_v, sem)

        def finish(c, b):
            idx_v, rows_v, sem = bufs[b]
            pltpu.make_async_copy(table_hbm.at[idx_v], rows_v, sem).wait()
            pltpu.sync_copy(rows_v, out_hbm.at[pl.ds(chunk_off(c), SC_CHUNK)])

        start(0, 0)

        def body(i, carry):
            start(2 * i + 1, 1)
            finish(2 * i, 0)

            @pl.when(i < n_chunks // 2 - 1)
            def _():
                start(2 * i + 2, 0)

            finish(2 * i + 1, 1)
            return carry

        lax.fori_loop(0, n_chunks // 2, body, 0)

    return gather(table, idx_flat)


def _finish_kernel(h_ref, pw_ref, mod_ref, nw_ref, y0_ref, y1_ref, y2_ref, y3_ref, *rest):
    o_ref = rest[-1]
    pw = pw_ref[...]
    moe = pw[:, 0:1] * _unpack_rows(y0_ref[0])
    for j, y_ref in enumerate((y1_ref, y2_ref, y3_ref), start=1):
        moe = moe + pw[:, j:j + 1] * _unpack_rows(y_ref[0])
    gate_f = mod_ref[0, 5:6, :]
    o_ref[...] = _rms(h_ref[...] + gate_f * moe, nw_ref[...])


def _finish(h1, pw, mod, norm_w, y4, seq, tc, tile0, prev_out):
    n = pw.shape[0]
    per_b = seq // tc
    slot = lambda j: pl.BlockSpec((1, tc, PACKED), lambda i: (j, i, 0))
    in_specs = [pl.BlockSpec((tc, D_MODEL), lambda i: (tile0 + i, 0)),
                pl.BlockSpec((tc, LANES), lambda i: (i, 0)),
                pl.BlockSpec((1, 6, D_MODEL), lambda i: ((tile0 + i) // per_b, 0, 0)),
                pl.BlockSpec((1, D_MODEL), lambda i: (0, 0)),
                slot(0), slot(1), slot(2), slot(3)]
    args = [h1, pw, mod, norm_w.reshape(1, D_MODEL), y4, y4, y4, y4]
    aliases = {}
    if prev_out is not None:
        in_specs.append(pl.BlockSpec(memory_space=pl.ANY))
        args.append(prev_out)
        aliases = {len(args) - 1: 0}
    return pl.pallas_call(
        _finish_kernel,
        out_shape=jax.ShapeDtypeStruct(h1.shape, F32),
        grid=(n // tc,),
        in_specs=in_specs,
        out_specs=pl.BlockSpec((tc, D_MODEL), lambda i: (tile0 + i, 0)),
        input_output_aliases=aliases,
        compiler_params=pltpu.CompilerParams(dimension_semantics=("arbitrary",),
                                             vmem_limit_bytes=VMEM_LIMIT),
        name="finish",
    )(*args)


def _moe_plan(counts, n_assign):
    cnt = counts[:, 0, :N_EXPERTS]
    sizes = jnp.sum(cnt, axis=0)
    padded = (sizes + ROW_BLOCK - 1) // ROW_BLOCK * ROW_BLOCK
    pad_end = jnp.cumsum(padded)
    pad_start = pad_end - padded
    tile_base = pad_start[None, :] + jnp.cumsum(cnt, axis=0) - cnt
    digits = jnp.stack([tile_base % 256, (tile_base // 256) % 256, tile_base // 65536], axis=1)
    rows = jnp.tile(digits, (1, 1, TOP_K))
    slot_of_row = jnp.arange(LANES, dtype=jnp.int32) // N_EXPERTS
    col = jnp.arange(LANES, dtype=jnp.int32)
    bt = jnp.where(slot_of_row[:, None] == col[None, :], rows[..., None], 0).astype(BF16)
    nb = n_assign // ROW_BLOCK + N_EXPERTS
    block_start = jnp.arange(nb, dtype=jnp.int32) * ROW_BLOCK
    block_e = jnp.minimum(jnp.sum(pad_end[None, :] <= block_start[:, None], axis=1),
                          N_EXPERTS - 1).astype(jnp.int32)
    n_used = (pad_end[-1] // ROW_BLOCK).astype(jnp.int32).reshape(1)
    nonempty = padded > 0
    ids = jnp.arange(N_EXPERTS, dtype=jnp.int32)
    suffix_min = lax.cummin(jnp.where(nonempty, ids, N_EXPERTS), reverse=True)
    after = jnp.concatenate([suffix_min[1:], jnp.full((1,), N_EXPERTS, jnp.int32)])
    next_nonempty = jnp.where(after < N_EXPERTS, after, -1).astype(jnp.int32)
    slot_of = ((jnp.cumsum(nonempty) - 1) % 2).astype(jnp.int32)
    first = jnp.concatenate([jnp.ones((1,), jnp.int32),
                             (block_e[1:] != block_e[:-1]).astype(jnp.int32)])
    valid_rows = (pad_start + sizes)[block_e] - block_start
    quarter = ROW_BLOCK // 4
    quarters = jnp.clip((valid_rows + quarter - 1) // quarter, 1, 4).astype(jnp.int32)
    expert_plan = (block_e, n_used, first, next_nonempty[block_e], slot_of[block_e], quarters)
    r = jnp.arange(ROW_BLOCK, dtype=jnp.int32)[None, :]
    pad_idx = jnp.where(r < (padded - sizes)[:, None], (pad_start + sizes)[:, None] + r,
                        (nb - 1) * ROW_BLOCK + r).astype(jnp.int32).reshape(PAD_SLOTS)
    return bt, expert_plan, pad_idx


def kernel(x, c, w_ada, b_ada, norm_mix_w, w_in, hgrn_lower_bounds, hgrn_norm_w, conv_w,
           w_hgrn_out, w_conv_out, w_mix_out, norm_ffn_w, w_router, b_router, w1, b1, w2, b2,
           norm_final_w):
    bsz, seq, d = x.shape
    assert d == D_MODEL and seq % CHUNK == 0
    n = bsz * seq
    depth = w_ada.shape[0]
    tile = min(512, seq)
    assert seq % tile == 0 and n % (MOE_GROUPS * tile) == 0 and (n // MOE_GROUPS * TOP_K) % ROW_BLOCK == 0
    h = x.reshape(n, d)
    wr_pad = jnp.zeros((depth, D_MODEL, LANES), BF16).at[:, :, :N_EXPERTS].set(w_router.astype(BF16))
    br_pad = jnp.zeros((depth, 1, LANES), F32).at[:, 0, :N_EXPERTS].set(b_router)
    for layer in range(depth):
        mod = _ada(c, w_ada[layer], b_ada[layer]).reshape(bsz, 6, d)
        proj = _inproj(h, mod, norm_mix_w[layer], w_in[layer].astype(BF16), seq, tile)
        h = _mix(proj, h, mod, hgrn_lower_bounds, hgrn_norm_w[layer], conv_w[layer],
                 w_hgrn_out[layer].astype(BF16), w_conv_out[layer].astype(BF16),
                 w_mix_out[layer].astype(BF16), layer, bsz, seq, tile)
        assert layer == depth - 1, "only the last layer applies the final norm"
        ng = n // MOE_GROUPS
        n_rows = (ng * TOP_K // ROW_BLOCK + N_EXPERTS) * ROW_BLOCK
        routed = [_route(h, mod, norm_ffn_w[layer], wr_pad[layer], br_pad[layer], seq, tile,
                         grp * (ng // tile), ng) for grp in range(MOE_GROUPS)]
        plans = jax.vmap(lambda cnt: _moe_plan(cnt, ng * TOP_K))(jnp.stack([r[4] for r in routed]))
        out = None
        for grp in range(MOE_GROUPS):
            tile0 = grp * (ng // tile)
            u2, oh4, rk, pw, _ = routed[grp]
            bt, expert_plan, pad_idx = jax.tree.map(lambda a: a[grp], plans)
            dest_slots = _dest(oh4, rk, bt, tile)[:TOP_K].reshape(TOP_K * ng)
            xs = _sc_scatter_rows(u2, dest_slots, pad_idx, n_rows)
            ys = _experts(expert_plan, xs, w1[layer], b1[layer], w2[layer], b2[layer])
            y4 = _sc_gather_rows(ys, dest_slots).reshape(TOP_K, ng, PACKED)
            out = _finish(h, pw, mod, norm_final_w, y4, seq, tile, tile0, out)
        h = out
    return h.reshape(bsz, seq, d)
```

```python
import functools

import jax
import jax.numpy as jnp
from jax import lax
from jax.experimental import pallas as pl
from jax.experimental.pallas import tpu as pltpu
from jax.experimental.pallas import tpu_sc as plsc

F32 = jnp.float32
BF16 = jnp.bfloat16

D_MODEL = 1024
HGRN_HEADS = 4
HEAD_DIM = 128
HGRN_WIDTH = HGRN_HEADS * HEAD_DIM
CONV_WIDTH = 512
CONV_K = 3
CHUNK = 64
N_EXPERTS = 32
TOP_K = 4
D_FF = 1024
SWIGLU_LIMIT = 7.0
SWIGLU_ALPHA = 1.702
EPS = 1e-6
LOG2_E = 1.4426950408889634
IN_COLS = 4 * HGRN_WIDTH + 3 * CONV_WIDTH + 2 * D_MODEL
LANES = 128
ROW_BLOCK = 1024
PACKED = D_MODEL // 2
MOE_GROUPS = 2
SC_CHUNK = 64
PAD_SLOTS = N_EXPERTS * ROW_BLOCK
VMEM_LIMIT = 56 * 1024 * 1024


def _sigmoid(x, scale=1.0):
    return 1.0 / (1.0 + jnp.exp2(x * (-scale * LOG2_E)))


def _rms(x, w):
    ms = jnp.mean(x * x, axis=-1, keepdims=True)
    return x * lax.rsqrt(ms + EPS) * w


def _pack_rows(x):
    w = x.shape[1] // 2
    lo = lax.bitcast_convert_type(x[:, :w].astype(BF16).astype(F32), jnp.uint32)
    hi = lax.bitcast_convert_type(x[:, w:].astype(BF16).astype(F32), jnp.uint32)
    return lax.bitcast_convert_type((lo >> 16) | (hi & jnp.uint32(0xFFFF0000)), jnp.int32)


def _unpack_rows(p):
    u = lax.bitcast_convert_type(p, jnp.uint32)
    lo = lax.bitcast_convert_type(u << 16, F32)
    hi = lax.bitcast_convert_type(u & jnp.uint32(0xFFFF0000), F32)
    return jnp.concatenate([lo, hi], axis=1)


def _nt_dot(a, b):
    return lax.dot_general(a, b, (((1,), (1,)), ((), ())), preferred_element_type=F32)


def _tn_dot(a, b):
    return lax.dot_general(a, b, (((0,), (0,)), ((), ())), preferred_element_type=F32)


def _ada_kernel(c_ref, w_ref, b_ref, o_ref):
    c = c_ref[...]
    sc = (c * _sigmoid(c)).astype(BF16)
    o_ref[...] = jnp.dot(sc, w_ref[...].astype(BF16), preferred_element_type=F32) + b_ref[...]


def _ada(c, w_ada, b_ada):
    bsz, d = c.shape
    n = w_ada.shape[1]
    return pl.pallas_call(
        _ada_kernel,
        out_shape=jax.ShapeDtypeStruct((bsz, n), F32),
        grid=(n // d,),
        in_specs=[pl.BlockSpec((bsz, d), lambda j: (0, 0)),
                  pl.BlockSpec((d, d), lambda j: (0, j)),
                  pl.BlockSpec((1, d), lambda j: (0, j))],
        out_specs=pl.BlockSpec((bsz, d), lambda j: (0, j)),
        name="ada",
    )(c, w_ada, b_ada.reshape(1, n))


def _inproj_kernel(x_ref, mod_ref, nw_ref, w_ref, o_ref, u_scr):
    y = _rms(x_ref[...], nw_ref[...])
    shift = mod_ref[0, 0:1, :]
    scale = mod_ref[0, 1:2, :]
    u_scr[...] = (y * (1.0 + scale) + shift).astype(BF16)

    step = 512
    for j in range(IN_COLS // step):
        o_ref[:, j * step:(j + 1) * step] = jnp.dot(
            u_scr[...], w_ref[:, j * step:(j + 1) * step], preferred_element_type=F32).astype(BF16)


def _inproj(x2, mod, norm_w, w_in_bf, seq, tm):
    n = x2.shape[0]
    per_b = seq // tm
    return pl.pallas_call(
        _inproj_kernel,
        out_shape=jax.ShapeDtypeStruct((n, IN_COLS), BF16),
        grid=(n // tm,),
        in_specs=[pl.BlockSpec((tm, D_MODEL), lambda i: (i, 0)),
                  pl.BlockSpec((1, 6, D_MODEL), lambda i: (i // per_b, 0, 0)),
                  pl.BlockSpec((1, D_MODEL), lambda i: (0, 0)),
                  pl.BlockSpec((D_MODEL, IN_COLS), lambda i: (0, 0))],
        out_specs=pl.BlockSpec((tm, IN_COLS), lambda i: (i, 0)),
        scratch_shapes=[pltpu.VMEM((tm, D_MODEL), BF16)],
        compiler_params=pltpu.CompilerParams(dimension_semantics=("arbitrary",),
                                             vmem_limit_bytes=VMEM_LIMIT),
        name="inproj",
    )(x2, mod, norm_w.reshape(1, D_MODEL), w_in_bf)


_HEADS = [slice(h * HEAD_DIM, (h + 1) * HEAD_DIM) for h in range(HGRN_HEADS)]


def _chunk_rows(c):
    if isinstance(c, int):
        return pl.ds(c * CHUNK, CHUNK)
    return pl.ds(pl.multiple_of(c * CHUNK, CHUNK), CHUNK)


def _hgrn_gates(c, lb, tri, f_ref, b_scr, k_scr, f_scr):
    fx = f_ref[_chunk_rows(c), :].astype(F32)
    f = lb + (1.0 - lb) * _sigmoid(fx)
    g = jnp.maximum(jnp.log(f), -128.0) * LOG2_E
    g1 = g.astype(BF16)
    r1 = g - g1.astype(F32)
    g2 = r1.astype(BF16)
    g3 = (r1 - g2.astype(F32)).astype(BF16)
    b_scr[...] = (jnp.dot(tri, g1, preferred_element_type=F32)
                  + jnp.dot(tri, g2, preferred_element_type=F32)
                  + jnp.dot(tri, g3, preferred_element_type=F32))
    f_scr[...] = f
    k_scr[...] = 1.0 - f


def _hgrn_scores(c, q_ref, b_scr, k_scr, f_scr):
    rows = _chunk_rows(c)
    qs = [q_ref[rows, hs].astype(F32) for hs in _HEADS]
    s_mats = [_level_scores(1, qs[h], _HEADS[h], b_scr, k_scr, f_scr) for h in range(HGRN_HEADS)]
    for lvl in range(2, 7):
        for h in range(HGRN_HEADS):
            s_mats[h] = s_mats[h] + _level_scores(lvl, qs[h], _HEADS[h], b_scr, k_scr, f_scr)
    return s_mats


def _hgrn_outputs(c, s_mats, q_ref, v_ref, st_scr, o_scr, b_scr, k_scr):
    rows = _chunk_rows(c)
    for h, hs in enumerate(_HEADS):
        q = q_ref[rows, hs].astype(F32)
        v_bf = v_ref[rows, hs]
        b = b_scr[:, hs]
        kk = k_scr[:, hs]
        b_last = b_scr[CHUNK - 1:CHUNK, hs]
        st = st_scr[h]
        qd = (q * jnp.exp2(b)).astype(BF16)
        kdec = (kk * jnp.exp2(b_last - b)).astype(BF16)
        diag = jnp.sum(q * kk, axis=-1, keepdims=True)
        o_scr[rows, hs] = (_nt_dot(qd, st.astype(BF16))
                           + jnp.dot(s_mats[h].astype(BF16), v_bf, preferred_element_type=F32)
                           + diag * v_bf.astype(F32))
        st_scr[h] = jnp.exp2(b_last) * st + _tn_dot(v_bf, kdec)


def _level_scores(lvl, q, hs, b_scr, k_scr, f_scr):
    row = lax.broadcasted_iota(jnp.int32, (CHUNK, 1), 0)
    col = lax.broadcasted_iota(jnp.int32, (1, CHUNK), 1)
    b = b_scr[:, hs]
    kk = k_scr[:, hs]
    blk = 1 << lvl
    half = blk // 2
    if lvl == 1:
        odd = (row & 1) == 1
        qx = jnp.where(odd, q * f_scr[:, hs], 0.0).astype(BF16)
        kx = jnp.where(odd, 0.0, kk).astype(BF16)
        return jnp.where((row >> 1) == (col >> 1), _nt_dot(qx, kx), 0.0)
    if half < 8:
        groups = []
        sub = lax.broadcasted_iota(jnp.int32, (8, 1), 0)
        for j in range(CHUNK // 8):
            rj = None
            for k in reversed(range(8 // blk)):
                m = 8 * j + k * blk + half - 1
                bm = jnp.broadcast_to(b_scr[m:m + 1, hs], (8, HEAD_DIM))
                rj = bm if rj is None else jnp.where(sub < (k + 1) * blk, bm, rj)
            groups.append(rj)
        ref = jnp.concatenate(groups, axis=0)
        second = (row & (blk - 1)) >= half
        qx = (q * jnp.exp2(jnp.where(second, b - ref, -jnp.inf))).astype(BF16)
        kx = (kk * jnp.exp2(jnp.where(second, -jnp.inf, ref - b))).astype(BF16)
        return jnp.where((row >> lvl) == (col >> lvl), _nt_dot(qx, kx), 0.0)
    n_blk = CHUNK // blk
    qparts, kparts = [], []
    for j in range(n_blk):
        m = j * blk + half - 1
        bm = b_scr[m:m + 1, hs]
        tq = slice(j * blk + half, (j + 1) * blk)
        tk = slice(j * blk, j * blk + half)
        qparts.append(q[tq] * jnp.exp2(b[tq] - bm))
        kparts.append(kk[tk] * jnp.exp2(bm - b[tk]))
        kparts.append(jnp.zeros((half, HEAD_DIM), F32))
    qx = jnp.concatenate(qparts, axis=0).astype(BF16)
    kx = jnp.concatenate(kparts, axis=0).astype(BF16)
    sc = _nt_dot(qx, kx)
    if n_blk > 1:
        crow = lax.broadcasted_iota(jnp.int32, (CHUNK // 2, 1), 0)
        sc = jnp.where((crow // half) == (col >> lvl), sc, 0.0)
    pieces = []
    for j in range(n_blk):
        pieces.append(jnp.zeros((half, CHUNK), F32))
        pieces.append(sc[j * half:(j + 1) * half])
    return jnp.concatenate(pieces, axis=0)


def _mix_kernel(lbt_ref, q_ref, f_ref, i_ref, g_ref, cb_ref, cc_ref, ch_ref, ga0_ref, ga1_ref,
                gb0_ref, gb1_ref, x_ref, mod_ref, hnw_ref, cw_ref, wa_ref, wb_ref, wm_ref,
                o_ref, st_scr, o_scr, carry_scr, b_scr, k_scr, f_scr, *, layer, rows_per_step, epi_rows):
    @pl.when(pl.program_id(1) == 0)
    def _():
        st_scr[...] = jnp.zeros_like(st_scr)
        carry_scr[0:8, :] = jnp.zeros((8, CONV_WIDTH), F32)

    tab = lbt_ref[...]
    tmax = jnp.max(tab, axis=0, keepdims=True)
    te = jnp.exp(tab - tmax)
    lb = jnp.sum(te[0:layer + 1], axis=0, keepdims=True) / jnp.sum(te, axis=0, keepdims=True)

    ri = lax.broadcasted_iota(jnp.int32, (CHUNK, CHUNK), 0)
    ci = lax.broadcasted_iota(jnp.int32, (CHUNK, CHUNK), 1)
    tri = jnp.where(ci <= ri, 1.0, 0.0).astype(BF16)

    n_chunks = rows_per_step // CHUNK

    def slot(s):
        return b_scr.at[s], k_scr.at[s], f_scr.at[s]

    def chunk(c, c_next, s):
        bs, ks, fs = slot(s)
        s_mats = _hgrn_scores(c, q_ref, bs, ks, fs)
        _hgrn_gates(c_next, lb, tri, f_ref, *slot(1 - s))
        _hgrn_outputs(c, s_mats, q_ref, i_ref, st_scr, o_scr, bs, ks)

    def pair_body(i, carry):
        chunk(2 * i, 2 * i + 1, 0)
        chunk(2 * i + 1, jnp.minimum(2 * i + 2, n_chunks - 1), 1)
        return carry

    _hgrn_gates(0, lb, tri, f_ref, *slot(0))
    lax.fori_loop(0, n_chunks // 2, pair_body, 0)

    gate_m = mod_ref[0, 2:3, :]
    hnw = hnw_ref[...]
    cw0 = cw_ref[0:1, :]
    cw1 = cw_ref[1:2, :]
    cw2 = cw_ref[2:3, :]
    for r in range(rows_per_step // epi_rows):
        rs = slice(r * epi_rows, (r + 1) * epi_rows)
        parts = []
        for h in range(HGRN_HEADS):
            hs = slice(h * HEAD_DIM, (h + 1) * HEAD_DIM)
            oh = _rms(o_scr[rs, hs], hnw)
            go = g_ref[rs, hs].astype(F32)
            parts.append((oh * (go * _sigmoid(go))).astype(BF16))
        ya = jnp.dot(jnp.concatenate(parts, axis=1), wa_ref[...], preferred_element_type=F32)
        uc = cc_ref[rs, :].astype(F32) * ch_ref[rs, :].astype(F32)
        carry_scr[8:8 + epi_rows, :] = uc
        s1 = carry_scr[7:7 + epi_rows, :]
        s2 = carry_scr[6:6 + epi_rows, :]
        carry_scr[0:8, :] = uc[epi_rows - 8:epi_rows, :]
        yc = cb_ref[rs, :].astype(F32) * (cw2 * uc + cw1 * s1 + cw0 * s2)
        yb = jnp.dot(yc.astype(BF16), wb_ref[...], preferred_element_type=F32)
        ga = jnp.concatenate([ga0_ref[rs, :], ga1_ref[rs, :]], axis=1).astype(F32)
        gb = jnp.concatenate([gb0_ref[rs, :], gb1_ref[rs, :]], axis=1).astype(F32)
        merged = (_sigmoid(ga) * ya + _sigmoid(gb) * yb).astype(BF16)
        o_ref[rs, :] = x_ref[rs, :] + gate_m * jnp.dot(merged, wm_ref[...], preferred_element_type=F32)


def _mix(proj, x2, mod, lb_table, hgrn_norm_w, conv_w, wa, wb, wm, layer, bsz, seq, tt):
    n = x2.shape[0]
    per_b = seq // tt

    def col(col_block):
        return pl.BlockSpec((tt, 512), lambda b, t: (b * per_b + t, col_block))

    const = lambda shape: pl.BlockSpec(shape, lambda b, t: (0,) * len(shape))
    in_specs = [
        const(lb_table.shape),
        col(0), col(1), col(2), col(3),
        col(4), col(5), col(6),
        col(7), col(8), col(9), col(10),
        pl.BlockSpec((tt, D_MODEL), lambda b, t: (b * per_b + t, 0)),
        pl.BlockSpec((1, 6, D_MODEL), lambda b, t: (b, 0, 0)),
        const((1, HEAD_DIM)), const((CONV_K, CONV_WIDTH)),
        const((HGRN_WIDTH, D_MODEL)), const((CONV_WIDTH, D_MODEL)), const((D_MODEL, D_MODEL)),
    ]
    kern = functools.partial(_mix_kernel, layer=layer, rows_per_step=tt, epi_rows=tt)
    return pl.pallas_call(
        kern,
        out_shape=jax.ShapeDtypeStruct((n, D_MODEL), F32),
        grid=(bsz, per_b),
        in_specs=in_specs,
        out_specs=pl.BlockSpec((tt, D_MODEL), lambda b, t: (b * per_b + t, 0)),
        scratch_shapes=[pltpu.VMEM((HGRN_HEADS, HEAD_DIM, HEAD_DIM), F32),
                        pltpu.VMEM((tt, HGRN_WIDTH), F32),
                        pltpu.VMEM((8 + tt, CONV_WIDTH), F32),
                        pltpu.VMEM((2, CHUNK, HGRN_WIDTH), F32),
                        pltpu.VMEM((2, CHUNK, HGRN_WIDTH), F32),
                        pltpu.VMEM((2, CHUNK, HGRN_WIDTH), F32)],
        compiler_params=pltpu.CompilerParams(dimension_semantics=("arbitrary", "arbitrary"),
                                             vmem_limit_bytes=VMEM_LIMIT),
        name="mix",
    )(lb_table, *([proj] * 11),
      x2, mod, hgrn_norm_w.reshape(1, HEAD_DIM), conv_w, wa, wb, wm)


def _route_kernel(h_ref, mod_ref, nw_ref, wr_ref, br_ref, u_ref, oh4_ref, rk_ref, pw_ref, cnt_ref, *, tr):
    shift = mod_ref[0, 3:4, :]
    scale = mod_ref[0, 4:5, :]
    u = _rms(h_ref[...], nw_ref[...]) * (1.0 + scale) + shift
    u_ref[...] = _pack_rows(u)
    lane = lax.broadcasted_iota(jnp.int32, (tr, LANES), 1).astype(F32)
    logits = jnp.dot(u.astype(BF16), wr_ref[...], preferred_element_type=F32) + br_ref[...]
    logits = jnp.where(lane < N_EXPERTS, logits, -jnp.inf)
    idx, val = [], []
    cur = logits
    for _ in range(TOP_K):
        m = jnp.max(cur, axis=-1, keepdims=True)
        i = jnp.min(jnp.where(cur == m, lane, float(LANES)), axis=-1, keepdims=True)
        idx.append(i)
        val.append(m)
        cur = jnp.where(lane == i, -jnp.inf, cur)
    ex = [jnp.exp(v - val[0]) for v in val]
    den = ex[0] + ex[1] + ex[2] + ex[3]
    onehot = jnp.zeros((tr, LANES), F32)
    for i in idx:
        onehot = onehot + jnp.where(lane == i, 1.0, 0.0)
    ri = lax.broadcasted_iota(jnp.int32, (tr, tr), 0)
    ci = lax.broadcasted_iota(jnp.int32, (tr, tr), 1)
    tri = jnp.where(ci < ri, 1.0, 0.0).astype(BF16)
    pref = jnp.dot(tri, onehot.astype(BF16), preferred_element_type=F32)
    rk = jnp.zeros((tr, LANES), jnp.int32)
    pw = jnp.zeros((tr, LANES), F32)
    oh4 = jnp.zeros((tr, LANES), F32)
    for j in range(TOP_K):
        rank = jnp.sum(jnp.where(lane == idx[j], pref, 0.0), axis=-1, keepdims=True).astype(jnp.int32)
        rk = jnp.where(lane == j, rank, rk)
        pw = jnp.where(lane == j, ex[j] / den, pw)
        oh4 = oh4 + jnp.where(lane == idx[j] + N_EXPERTS * j, 1.0, 0.0)
    oh4_ref[...] = oh4.astype(BF16)
    rk_ref[...] = rk
    pw_ref[...] = pw
    cnt_ref[0] = jnp.sum(onehot, axis=0, keepdims=True).astype(jnp.int32)


def _route(h1, mod, norm_w, wr_pad, br_pad, seq, tr, tile0, n):
    per_b = seq // tr
    nt = n // tr
    return pl.pallas_call(
        functools.partial(_route_kernel, tr=tr),
        out_shape=(jax.ShapeDtypeStruct((n, PACKED), jnp.int32),
                   jax.ShapeDtypeStruct((n, LANES), BF16),
                   jax.ShapeDtypeStruct((n, LANES), jnp.int32),
                   jax.ShapeDtypeStruct((n, LANES), F32),
                   jax.ShapeDtypeStruct((nt, 1, LANES), jnp.int32)),
        grid=(nt,),
        in_specs=[pl.BlockSpec((tr, D_MODEL), lambda i: (tile0 + i, 0)),
                  pl.BlockSpec((1, 6, D_MODEL), lambda i: ((tile0 + i) // per_b, 0, 0)),
                  pl.BlockSpec((1, D_MODEL), lambda i: (0, 0)),
                  pl.BlockSpec((D_MODEL, LANES), lambda i: (0, 0)),
                  pl.BlockSpec((1, LANES), lambda i: (0, 0))],
        out_specs=(pl.BlockSpec((tr, PACKED), lambda i: (i, 0)),
                   pl.BlockSpec((tr, LANES), lambda i: (i, 0)),
                   pl.BlockSpec((tr, LANES), lambda i: (i, 0)),
                   pl.BlockSpec((tr, LANES), lambda i: (i, 0)),
                   pl.BlockSpec((1, 1, LANES), lambda i: (i, 0, 0))),
        compiler_params=pltpu.CompilerParams(dimension_semantics=("arbitrary",),
                                             vmem_limit_bytes=VMEM_LIMIT),
        name="route",
    )(h1, mod, norm_w.reshape(1, D_MODEL), wr_pad, br_pad)


def _dest_kernel(oh4_ref, rk_ref, bt_ref, o_ref):
    oh = oh4_ref[...]
    start = (jnp.dot(oh, bt_ref[0, 0], preferred_element_type=F32)
             + 256.0 * jnp.dot(oh, bt_ref[0, 1], preferred_element_type=F32)
             + 65536.0 * jnp.dot(oh, bt_ref[0, 2], preferred_element_type=F32))
    o_ref[...] = (start + rk_ref[...].astype(F32)).T[:8, :].astype(jnp.int32)


def _dest(oh4, rk, bt, tr):
    n = oh4.shape[0]
    return pl.pallas_call(
        _dest_kernel,
        out_shape=jax.ShapeDtypeStruct((8, n), jnp.int32),
        grid=(n // tr,),
        in_specs=[pl.BlockSpec((tr, LANES), lambda i: (i, 0)),
                  pl.BlockSpec((tr, LANES), lambda i: (i, 0)),
                  pl.BlockSpec((1, 3, LANES, LANES), lambda i: (i, 0, 0, 0))],
        out_specs=pl.BlockSpec((8, tr), lambda i: (0, i)),
        name="dest",
    )(oh4, rk, bt)


def _sc_workers():
    info = plsc.get_sparse_core_info()
    return info.num_cores, info.num_cores * info.num_subcores


def _sc_scatter_rows(rows, idx_slots, pad_idx, n_out):
    n_cores, n_workers = _sc_workers()
    n, w = rows.shape
    k = idx_slots.shape[0] // n
    per_worker = n // n_workers
    pad_per_worker = pad_idx.shape[0] // n_workers
    assert per_worker % SC_CHUNK == 0 and pad_per_worker % SC_CHUNK == 0
    mesh = plsc.VectorSubcoreMesh(core_axis_name="c", subcore_axis_name="s")
    zeros = jnp.zeros((SC_CHUNK, w), rows.dtype)

    n_chunks = per_worker // SC_CHUNK
    assert n_chunks % 2 == 0

    @functools.partial(
        pl.kernel, mesh=mesh,
        out_type=jax.ShapeDtypeStruct((n_out, w), rows.dtype),
        scratch_types=[pltpu.VMEM((SC_CHUNK,), jnp.int32)] * k
        + [pltpu.VMEM((SC_CHUNK, w), rows.dtype)] * 2
        + [pltpu.SemaphoreType.DMA] * 2,
        name="sc_scatter",
    )
    def scatter(rows_hbm, idx_hbm, pad_hbm, zeros_hbm, out_hbm, *scratch):
        idx_bufs = scratch[:k]
        row_bufs = scratch[k:k + 2]
        sem_rows, sem_out = scratch[k + 2:]
        wid = lax.axis_index("s") * n_cores + lax.axis_index("c")

        def chunk_off(c):
            return pl.multiple_of(wid * per_worker + c * SC_CHUNK, 8)

        def load(c, b):
            pltpu.async_copy(rows_hbm.at[pl.ds(chunk_off(c), SC_CHUNK)], row_bufs[b], sem_rows)

        def wait_load(b):
            pltpu.make_async_copy(rows_hbm.at[pl.ds(0, SC_CHUNK)], row_bufs[b], sem_rows).wait()

        def scatter_chunk(c, b):
            for j in range(k):
                pltpu.sync_copy(idx_hbm.at[pl.ds(pl.multiple_of(j * n + chunk_off(c), 8), SC_CHUNK)],
                                idx_bufs[j])
            copies = [pltpu.async_copy(row_bufs[b], out_hbm.at[idx_bufs[j]], sem_out) for j in range(k)]
            for cp in copies:
                cp.wait()

        load(0, 0)

        def body(i, carry):
            wait_load(0)
            load(2 * i + 1, 1)
            scatter_chunk(2 * i, 0)
            wait_load(1)

            @pl.when(i < n_chunks // 2 - 1)
            def _():
                load(2 * i + 2, 0)

            scatter_chunk(2 * i + 1, 1)
            return carry

        lax.fori_loop(0, n_chunks // 2, body, 0)
        pltpu.sync_copy(zeros_hbm, row_bufs[0])

        def pad_body(i, carry):
            off = pl.multiple_of(wid * pad_per_worker + i * SC_CHUNK, 8)
            pltpu.sync_copy(pad_hbm.at[pl.ds(off, SC_CHUNK)], idx_bufs[0])
            pltpu.async_copy(row_bufs[0], out_hbm.at[idx_bufs[0]], sem_out).wait()
            return carry

        lax.fori_loop(0, pad_per_worker // SC_CHUNK, pad_body, 0)

    return scatter(rows, idx_slots, pad_idx, zeros)


def _experts_kernel(be_ref, nu_ref, first_ref, next_ref, slot_ref, quarters_ref, x_ref, w1_hbm, b1_ref, w2_hbm,
                    b2_ref, o_ref, w1_f32, w2_f32, w1_bf, w2_bf, sems):
    i = pl.program_id(0)
    used = i < nu_ref[0]

    def weight_copies(e, s):
        return (pltpu.make_async_copy(w1_hbm.at[e], w1_f32.at[s], sems.at[s]),
                pltpu.make_async_copy(w2_hbm.at[e], w2_f32.at[s], sems.at[s]))

    @pl.when(used & (first_ref[i] == 1))
    def _():
        s = slot_ref[i]

        @pl.when(i == 0)
        def _():
            for cp in weight_copies(be_ref[i], s):
                cp.start()

        for cp in weight_copies(be_ref[i], s):
            cp.wait()
        cw = 256
        for c in range(2 * D_FF // cw):
            w1_bf[:, c * cw:(c + 1) * cw] = w1_f32[s, :, c * cw:(c + 1) * cw].astype(BF16)
        for c in range(D_MODEL // cw):
            w2_bf[:, c * cw:(c + 1) * cw] = w2_f32[s, :, c * cw:(c + 1) * cw].astype(BF16)

        @pl.when(next_ref[i] >= 0)
        def _():
            for cp in weight_copies(next_ref[i], 1 - s):
                cp.start()

    def mlp(n_rows):
        step = 512
        x = _unpack_rows(x_ref[0:n_rows, :]).astype(BF16)
        acc = None
        for j in range(D_FF // step):
            cs = slice(j * step, (j + 1) * step)
            ls = slice(D_FF + j * step, D_FF + (j + 1) * step)
            glu = jnp.dot(x, w1_bf[:, cs], preferred_element_type=F32) + b1_ref[0, :, cs]
            lin = jnp.dot(x, w1_bf[:, ls], preferred_element_type=F32) + b1_ref[0, :, ls]
            glu = jnp.minimum(glu, SWIGLU_LIMIT)
            lin = jnp.clip(lin, -SWIGLU_LIMIT, SWIGLU_LIMIT)
            act = (glu * _sigmoid(glu, SWIGLU_ALPHA) * (lin + 1.0)).astype(BF16)
            part = jnp.dot(act, w2_bf[cs, :], preferred_element_type=F32)
            acc = part if acc is None else acc + part
        o_ref[0:n_rows, :] = _pack_rows(acc + b2_ref[0])
        if n_rows < ROW_BLOCK:
            o_ref[n_rows:ROW_BLOCK, :] = jnp.zeros((ROW_BLOCK - n_rows, PACKED), jnp.int32)

    for quarters in range(1, 5):
        @pl.when(used & (quarters_ref[i] == quarters))
        def _():
            mlp(quarters * (ROW_BLOCK // 4))

    @pl.when(jnp.logical_not(used))
    def _():
        o_ref[...] = jnp.zeros_like(o_ref)


def _experts(plan, xs, w1, b1, w2, b2):
    n_rows = xs.shape[0]
    nb = n_rows // ROW_BLOCK
    grid_spec = pltpu.PrefetchScalarGridSpec(
        num_scalar_prefetch=6,
        grid=(nb,),
        in_specs=[pl.BlockSpec((ROW_BLOCK, PACKED), lambda i, be, nu, *_: (jnp.minimum(i, nu[0] - 1), 0)),
                  pl.BlockSpec(memory_space=pl.ANY),
                  pl.BlockSpec((1, 1, 2 * D_FF), lambda i, be, *_: (be[i], 0, 0)),
                  pl.BlockSpec(memory_space=pl.ANY),
                  pl.BlockSpec((1, 1, D_MODEL), lambda i, be, *_: (be[i], 0, 0))],
        out_specs=pl.BlockSpec((ROW_BLOCK, PACKED), lambda i, *_: (i, 0)),
        scratch_shapes=[pltpu.VMEM((2, D_MODEL, 2 * D_FF), F32), pltpu.VMEM((2, D_FF, D_MODEL), F32),
                        pltpu.VMEM((D_MODEL, 2 * D_FF), BF16), pltpu.VMEM((D_FF, D_MODEL), BF16),
                        pltpu.SemaphoreType.DMA((2,))],
    )
    return pl.pallas_call(
        _experts_kernel,
        out_shape=jax.ShapeDtypeStruct((n_rows, PACKED), jnp.int32),
        grid_spec=grid_spec,
        compiler_params=pltpu.CompilerParams(dimension_semantics=("arbitrary",),
                                             vmem_limit_bytes=VMEM_LIMIT),
        name="experts",
    )(*plan, xs, w1, b1.reshape(N_EXPERTS, 1, 2 * D_FF), w2, b2.reshape(N_EXPERTS, 1, D_MODEL))


def _sc_gather_rows(table, idx_flat):
    n_cores, n_workers = _sc_workers()
    n_idx = idx_flat.shape[0]
    w = table.shape[1]
    per_worker = n_idx // n_workers
    n_chunks = per_worker // SC_CHUNK
    assert per_worker * n_workers == n_idx and n_chunks * SC_CHUNK == per_worker and n_chunks % 2 == 0
    mesh = plsc.VectorSubcoreMesh(core_axis_name="c", subcore_axis_name="s")

    @functools.partial(
        pl.kernel, mesh=mesh,
        out_type=jax.ShapeDtypeStruct((n_idx, w), table.dtype),
        scratch_types=[pltpu.VMEM((SC_CHUNK,), jnp.int32), pltpu.VMEM((SC_CHUNK,), jnp.int32),
                       pltpu.VMEM((SC_CHUNK, w), table.dtype), pltpu.VMEM((SC_CHUNK, w), table.dtype),
                       pltpu.SemaphoreType.DMA, pltpu.SemaphoreType.DMA],
        name="sc_gather",
    )
    def gather(table_hbm, idx_hbm, out_hbm, idx0, idx1, rows0, rows1, sem0, sem1):
        wid = lax.axis_index("s") * n_cores + lax.axis_index("c")
        bufs = ((idx0, rows0, sem0), (idx1, rows1, sem1))

        def chunk_off(c):
            return pl.multiple_of(wid * per_worker + c * SC_CHUNK, 8)

        def start(c, b):
            idx_v, rows_v, sem = bufs[b]
            pltpu.sync_copy(idx_hbm.at[pl.ds(chunk_off(c), SC_CHUNK)], idx_v)
            pltpu.async_copy(table_hbm.at[idx_v], rows_v, sem)

        def finish(c, b):
            idx_v, rows_v, sem = bufs[b]
            pltpu.make_async_copy(table_hbm.at[idx_v], rows_v, sem).wait()
            pltpu.sync_copy(rows_v, out_hbm.at[pl.ds(chunk_off(c), SC_CHUNK)])

        start(0, 0)

        def body(i, carry):
            start(2 * i + 1, 1)
            finish(2 * i, 0)

            @pl.when(i < n_chunks // 2 - 1)
            def _():
                start(2 * i + 2, 0)

            finish(2 * i + 1, 1)
            return carry

        lax.fori_loop(0, n_chunks // 2, body, 0)

    return gather(table, idx_flat)


def _finish_kernel(h_ref, pw_ref, mod_ref, nw_ref, y0_ref, y1_ref, y2_ref, y3_ref, *rest):
    o_ref = rest[-1]
    pw = pw_ref[...]
    moe = pw[:, 0:1] * _unpack_rows(y0_ref[0])
    for j, y_ref in enumerate((y1_ref, y2_ref, y3_ref), start=1):
        moe = moe + pw[:, j:j + 1] * _unpack_rows(y_ref[0])
    gate_f = mod_ref[0, 5:6, :]
    o_ref[...] = _rms(h_ref[...] + gate_f * moe, nw_ref[...])


def _finish(h1, pw, mod, norm_w, y4, seq, tc, tile0, prev_out):
    n = pw.shape[0]
    per_b = seq // tc
    slot = lambda j: pl.BlockSpec((1, tc, PACKED), lambda i: (j, i, 0))
    in_specs = [pl.BlockSpec((tc, D_MODEL), lambda i: (tile0 + i, 0)),
                pl.BlockSpec((tc, LANES), lambda i: (i, 0)),
                pl.BlockSpec((1, 6, D_MODEL), lambda i: ((tile0 + i) // per_b, 0, 0)),
                pl.BlockSpec((1, D_MODEL), lambda i: (0, 0)),
                slot(0), slot(1), slot(2), slot(3)]
    args = [h1, pw, mod, norm_w.reshape(1, D_MODEL), y4, y4, y4, y4]
    aliases = {}
    if prev_out is not None:
        in_specs.append(pl.BlockSpec(memory_space=pl.ANY))
        args.append(prev_out)
        aliases = {len(args) - 1: 0}
    return pl.pallas_call(
        _finish_kernel,
        out_shape=jax.ShapeDtypeStruct(h1.shape, F32),
        grid=(n // tc,),
        in_specs=in_specs,
        out_specs=pl.BlockSpec((tc, D_MODEL), lambda i: (tile0 + i, 0)),
        input_output_aliases=aliases,
        compiler_params=pltpu.CompilerParams(dimension_semantics=("arbitrary",),
                                             vmem_limit_bytes=VMEM_LIMIT),
        name="finish",
    )(*args)


def _moe_plan(counts, n_assign):
    cnt = counts[:, 0, :N_EXPERTS]
    sizes = jnp.sum(cnt, axis=0)
    padded = (sizes + ROW_BLOCK - 1) // ROW_BLOCK * ROW_BLOCK
    pad_end = jnp.cumsum(padded)
    pad_start = pad_end - padded
    tile_base = pad_start[None, :] + jnp.cumsum(cnt, axis=0) - cnt
    digits = jnp.stack([tile_base % 256, (tile_base // 256) % 256, tile_base // 65536], axis=1)
    rows = jnp.tile(digits, (1, 1, TOP_K))
    slot_of_row = jnp.arange(LANES, dtype=jnp.int32) // N_EXPERTS
    col = jnp.arange(LANES, dtype=jnp.int32)
    bt = jnp.where(slot_of_row[:, None] == col[None, :], rows[..., None], 0).astype(BF16)
    nb = n_assign // ROW_BLOCK + N_EXPERTS
    block_start = jnp.arange(nb, dtype=jnp.int32) * ROW_BLOCK
    block_e = jnp.minimum(jnp.sum(pad_end[None, :] <= block_start[:, None], axis=1),
                          N_EXPERTS - 1).astype(jnp.int32)
    n_used = (pad_end[-1] // ROW_BLOCK).astype(jnp.int32).reshape(1)
    nonempty = padded > 0
    ids = jnp.arange(N_EXPERTS, dtype=jnp.int32)
    suffix_min = lax.cummin(jnp.where(nonempty, ids, N_EXPERTS), reverse=True)
    after = jnp.concatenate([suffix_min[1:], jnp.full((1,), N_EXPERTS, jnp.int32)])
    next_nonempty = jnp.where(after < N_EXPERTS, after, -1).astype(jnp.int32)
    slot_of = ((jnp.cumsum(nonempty) - 1) % 2).astype(jnp.int32)
    first = jnp.concatenate([jnp.ones((1,), jnp.int32),
                             (block_e[1:] != block_e[:-1]).astype(jnp.int32)])
    valid_rows = (pad_start + sizes)[block_e] - block_start
    quarter = ROW_BLOCK // 4
    quarters = jnp.clip((valid_rows + quarter - 1) // quarter, 1, 4).astype(jnp.int32)
    expert_plan = (block_e, n_used, first, next_nonempty[block_e], slot_of[block_e], quarters)
    r = jnp.arange(ROW_BLOCK, dtype=jnp.int32)[None, :]
    pad_idx = jnp.where(r < (padded - sizes)[:, None], (pad_start + sizes)[:, None] + r,
                        (nb - 1) * ROW_BLOCK + r).astype(jnp.int32).reshape(PAD_SLOTS)
    return bt, expert_plan, pad_idx


def kernel(x, c, w_ada, b_ada, norm_mix_w, w_in, hgrn_lower_bounds, hgrn_norm_w, conv_w,
           w_hgrn_out, w_conv_out, w_mix_out, norm_ffn_w, w_router, b_router, w1, b1, w2, b2,
           norm_final_w):
    bsz, seq, d = x.shape
    assert d == D_MODEL and seq % CHUNK == 0
    n = bsz * seq
    depth = w_ada.shape[0]
    tile = min(512, seq)
    assert seq % tile == 0 and n % (MOE_GROUPS * tile) == 0 and (n // MOE_GROUPS * TOP_K) % ROW_BLOCK == 0
    h = x.reshape(n, d)
    wr_pad = jnp.zeros((depth, D_MODEL, LANES), BF16).at[:, :, :N_EXPERTS].set(w_router.astype(BF16))
    br_pad = jnp.zeros((depth, 1, LANES), F32).at[:, 0, :N_EXPERTS].set(b_router)
    for layer in range(depth):
        mod = _ada(c, w_ada[layer], b_ada[layer]).reshape(bsz, 6, d)
        proj = _inproj(h, mod, norm_mix_w[layer], w_in[layer].astype(BF16), seq, tile)
        h = _mix(proj, h, mod, hgrn_lower_bounds, hgrn_norm_w[layer], conv_w[layer],
                 w_hgrn_out[layer].astype(BF16), w_conv_out[layer].astype(BF16),
                 w_mix_out[layer].astype(BF16), layer, bsz, seq, tile)
        assert layer == depth - 1, "only the last layer applies the final norm"
        ng = n // MOE_GROUPS
        n_rows = (ng * TOP_K // ROW_BLOCK + N_EXPERTS) * ROW_BLOCK
        routed = [_route(h, mod, norm_ffn_w[layer], wr_pad[layer], br_pad[layer], seq, tile,
                         grp * (ng // tile), ng) for grp in range(MOE_GROUPS)]
        plans = jax.vmap(lambda cnt: _moe_plan(cnt, ng * TOP_K))(jnp.stack([r[4] for r in routed]))
        out = None
        for grp in range(MOE_GROUPS):
            tile0 = grp * (ng // tile)
            u2, oh4, rk, pw, _ = routed[grp]
            bt, expert_plan, pad_idx = jax.tree.map(lambda a: a[grp], plans)
            dest_slots = _dest(oh4, rk, bt, tile)[:TOP_K].reshape(TOP_K * ng)
            xs = _sc_scatter_rows(u2, dest_slots, pad_idx, n_rows)
            ys = _experts(expert_plan, xs, w1[layer], b1[layer], w2[layer], b2[layer])
            y4 = _sc_gather_rows(ys, dest_slots).reshape(TOP_K, ng, PACKED)
            out = _finish(h, pw, mod, norm_final_w, y4, seq, tile, tile0, out)
        h = out
    return h.reshape(bsz, seq, d)
```

```python
import functools

import jax
import jax.numpy as jnp
from jax import lax
from jax.experimental import pallas as pl
from jax.experimental.pallas import tpu as pltpu
from jax.experimental.pallas import tpu_sc as plsc

F32 = jnp.float32
BF16 = jnp.bfloat16

D_MODEL = 1024
HGRN_HEADS = 4
HEAD_DIM = 128
HGRN_WIDTH = HGRN_HEADS * HEAD_DIM
CONV_WIDTH = 512
CONV_K = 3
CHUNK = 64
N_EXPERTS = 32
TOP_K = 4
D_FF = 1024
SWIGLU_LIMIT = 7.0
SWIGLU_ALPHA = 1.702
EPS = 1e-6
LOG2_E = 1.4426950408889634
IN_COLS = 4 * HGRN_WIDTH + 3 * CONV_WIDTH + 2 * D_MODEL
LANES = 128
MIX_ROWS = 1024
ROW_BLOCK = 1024
PACKED = D_MODEL // 2
MOE_GROUPS = 2
SC_CHUNK = 64
PAD_SLOTS = N_EXPERTS * ROW_BLOCK
VMEM_LIMIT = 56 * 1024 * 1024


def _sigmoid(x, scale=1.0):
    return 1.0 / (1.0 + jnp.exp2(x * (-scale * LOG2_E)))


def _rms(x, w):
    ms = jnp.mean(x * x, axis=-1, keepdims=True)
    return x * lax.rsqrt(ms + EPS) * w


def _pack_rows(x):
    w = x.shape[1] // 2
    lo = lax.bitcast_convert_type(x[:, :w].astype(BF16).astype(F32), jnp.uint32)
    hi = lax.bitcast_convert_type(x[:, w:].astype(BF16).astype(F32), jnp.uint32)
    return lax.bitcast_convert_type((lo >> 16) | (hi & jnp.uint32(0xFFFF0000)), jnp.int32)


def _unpack_rows(p):
    u = lax.bitcast_convert_type(p, jnp.uint32)
    lo = lax.bitcast_convert_type(u << 16, F32)
    hi = lax.bitcast_convert_type(u & jnp.uint32(0xFFFF0000), F32)
    return jnp.concatenate([lo, hi], axis=1)


def _nt_dot(a, b):
    return lax.dot_general(a, b, (((1,), (1,)), ((), ())), preferred_element_type=F32)


def _tn_dot(a, b):
    return lax.dot_general(a, b, (((0,), (0,)), ((), ())), preferred_element_type=F32)


def _ada_kernel(c_ref, w_ref, b_ref, o_ref):
    c = c_ref[...]
    sc = (c * _sigmoid(c)).astype(BF16)
    o_ref[...] = jnp.dot(sc, w_ref[...].astype(BF16), preferred_element_type=F32) + b_ref[...]


def _ada(c, w_ada, b_ada):
    bsz, d = c.shape
    n = w_ada.shape[1]
    return pl.pallas_call(
        _ada_kernel,
        out_shape=jax.ShapeDtypeStruct((bsz, n), F32),
        grid=(n // d,),
        in_specs=[pl.BlockSpec((bsz, d), lambda j: (0, 0)),
                  pl.BlockSpec((d, d), lambda j: (0, j)),
                  pl.BlockSpec((1, d), lambda j: (0, j))],
        out_specs=pl.BlockSpec((bsz, d), lambda j: (0, j)),
        name="ada",
    )(c, w_ada, b_ada.reshape(1, n))


def _inproj_kernel(x_ref, mod_ref, nw_ref, w_ref, o_ref, u_scr):
    y = _rms(x_ref[...], nw_ref[...])
    shift = mod_ref[0, 0:1, :]
    scale = mod_ref[0, 1:2, :]
    u_scr[...] = (y * (1.0 + scale) + shift).astype(BF16)

    step = 512
    for j in range(IN_COLS // step):
        o_ref[:, j * step:(j + 1) * step] = jnp.dot(
            u_scr[...], w_ref[:, j * step:(j + 1) * step], preferred_element_type=F32).astype(BF16)


def _inproj(x2, mod, norm_w, w_in_bf, seq, tm):
    n = x2.shape[0]
    per_b = seq // tm
    return pl.pallas_call(
        _inproj_kernel,
        out_shape=jax.ShapeDtypeStruct((n, IN_COLS), BF16),
        grid=(n // tm,),
        in_specs=[pl.BlockSpec((tm, D_MODEL), lambda i: (i, 0)),
                  pl.BlockSpec((1, 6, D_MODEL), lambda i: (i // per_b, 0, 0)),
                  pl.BlockSpec((1, D_MODEL), lambda i: (0, 0)),
                  pl.BlockSpec((D_MODEL, IN_COLS), lambda i: (0, 0))],
        out_specs=pl.BlockSpec((tm, IN_COLS), lambda i: (i, 0)),
        scratch_shapes=[pltpu.VMEM((tm, D_MODEL), BF16)],
        compiler_params=pltpu.CompilerParams(dimension_semantics=("arbitrary",),
                                             vmem_limit_bytes=VMEM_LIMIT),
        name="inproj",
    )(x2, mod, norm_w.reshape(1, D_MODEL), w_in_bf)


_HEADS = [slice(h * HEAD_DIM, (h + 1) * HEAD_DIM) for h in range(HGRN_HEADS)]


def _chunk_rows(c):
    if isinstance(c, int):
        return pl.ds(c * CHUNK, CHUNK)
    return pl.ds(pl.multiple_of(c * CHUNK, CHUNK), CHUNK)


def _hgrn_gates(c, lb, tri, f_ref, b_scr, k_scr, f_scr):
    fx = f_ref[_chunk_rows(c), :].astype(F32)
    f = lb + (1.0 - lb) * _sigmoid(fx)
    g = jnp.maximum(jnp.log(f), -128.0) * LOG2_E
    g1 = g.astype(BF16)
    r1 = g - g1.astype(F32)
    g2 = r1.astype(BF16)
    g3 = (r1 - g2.astype(F32)).astype(BF16)
    b_scr[...] = (jnp.dot(tri, g1, preferred_element_type=F32)
                  + jnp.dot(tri, g2, preferred_element_type=F32)
                  + jnp.dot(tri, g3, preferred_element_type=F32))
    f_scr[...] = f
    k_scr[...] = 1.0 - f


def _hgrn_scores(c, q_ref, b_scr, k_scr, f_scr):
    rows = _chunk_rows(c)
    qs = [q_ref[rows, hs].astype(F32) for hs in _HEADS]
    s_mats = [_level_scores(1, qs[h], _HEADS[h], b_scr, k_scr, f_scr) for h in range(HGRN_HEADS)]
    for lvl in range(2, 7):
        for h in range(HGRN_HEADS):
            s_mats[h] = s_mats[h] + _level_scores(lvl, qs[h], _HEADS[h], b_scr, k_scr, f_scr)
    return s_mats


def _hgrn_outputs(c, s_mats, q_ref, v_ref, st_scr, o_scr, b_scr, k_scr):
    rows = _chunk_rows(c)
    for h, hs in enumerate(_HEADS):
        q = q_ref[rows, hs].astype(F32)
        v_bf = v_ref[rows, hs]
        b = b_scr[:, hs]
        kk = k_scr[:, hs]
        b_last = b_scr[CHUNK - 1:CHUNK, hs]
        st = st_scr[h]
        qd = (q * jnp.exp2(b)).astype(BF16)
        kdec = (kk * jnp.exp2(b_last - b)).astype(BF16)
        diag = jnp.sum(q * kk, axis=-1, keepdims=True)
        o_scr[rows, hs] = (_nt_dot(qd, st.astype(BF16))
                           + jnp.dot(s_mats[h].astype(BF16), v_bf, preferred_element_type=F32)
                           + diag * v_bf.astype(F32))
        st_scr[h] = jnp.exp2(b_last) * st + _tn_dot(v_bf, kdec)


def _level_scores(lvl, q, hs, b_scr, k_scr, f_scr):
    row = lax.broadcasted_iota(jnp.int32, (CHUNK, 1), 0)
    col = lax.broadcasted_iota(jnp.int32, (1, CHUNK), 1)
    b = b_scr[:, hs]
    kk = k_scr[:, hs]
    blk = 1 << lvl
    half = blk // 2
    if lvl == 1:
        odd = (row & 1) == 1
        qx = jnp.where(odd, q * f_scr[:, hs], 0.0).astype(BF16)
        kx = jnp.where(odd, 0.0, kk).astype(BF16)
        return jnp.where((row >> 1) == (col >> 1), _nt_dot(qx, kx), 0.0)
    if half < 8:
        groups = []
        sub = lax.broadcasted_iota(jnp.int32, (8, 1), 0)
        for j in range(CHUNK // 8):
            rj = None
            for k in reversed(range(8 // blk)):
                m = 8 * j + k * blk + half - 1
                bm = jnp.broadcast_to(b_scr[m:m + 1, hs], (8, HEAD_DIM))
                rj = bm if rj is None else jnp.where(sub < (k + 1) * blk, bm, rj)
            groups.append(rj)
        ref = jnp.concatenate(groups, axis=0)
        second = (row & (blk - 1)) >= half
        qx = (q * jnp.exp2(jnp.where(second, b - ref, -jnp.inf))).astype(BF16)
        kx = (kk * jnp.exp2(jnp.where(second, -jnp.inf, ref - b))).astype(BF16)
        return jnp.where((row >> lvl) == (col >> lvl), _nt_dot(qx, kx), 0.0)
    n_blk = CHUNK // blk
    qparts, kparts = [], []
    for j in range(n_blk):
        m = j * blk + half - 1
        bm = b_scr[m:m + 1, hs]
        tq = slice(j * blk + half, (j + 1) * blk)
        tk = slice(j * blk, j * blk + half)
        qparts.append(q[tq] * jnp.exp2(b[tq] - bm))
        kparts.append(kk[tk] * jnp.exp2(bm - b[tk]))
        kparts.append(jnp.zeros((half, HEAD_DIM), F32))
    qx = jnp.concatenate(qparts, axis=0).astype(BF16)
    kx = jnp.concatenate(kparts, axis=0).astype(BF16)
    sc = _nt_dot(qx, kx)
    if n_blk > 1:
        crow = lax.broadcasted_iota(jnp.int32, (CHUNK // 2, 1), 0)
        sc = jnp.where((crow // half) == (col >> lvl), sc, 0.0)
    pieces = []
    for j in range(n_blk):
        pieces.append(jnp.zeros((half, CHUNK), F32))
        pieces.append(sc[j * half:(j + 1) * half])
    return jnp.concatenate(pieces, axis=0)


def _mix_kernel(lbt_ref, q_ref, f_ref, i_ref, g_ref, cb_ref, cc_ref, ch_ref, ga0_ref, ga1_ref,
                gb0_ref, gb1_ref, x_ref, mod_ref, hnw_ref, cw_ref, wa_ref, wb_ref, wm_ref,
                o_ref, st_scr, o_scr, carry_scr, b_scr, k_scr, f_scr, *, layer, rows_per_step, epi_rows):
    @pl.when(pl.program_id(1) == 0)
    def _():
        st_scr[...] = jnp.zeros_like(st_scr)
        carry_scr[0:8, :] = jnp.zeros((8, CONV_WIDTH), F32)

    tab = lbt_ref[...]
    tmax = jnp.max(tab, axis=0, keepdims=True)
    te = jnp.exp(tab - tmax)
    lb = jnp.sum(te[0:layer + 1], axis=0, keepdims=True) / jnp.sum(te, axis=0, keepdims=True)

    ri = lax.broadcasted_iota(jnp.int32, (CHUNK, CHUNK), 0)
    ci = lax.broadcasted_iota(jnp.int32, (CHUNK, CHUNK), 1)
    tri = jnp.where(ci <= ri, 1.0, 0.0).astype(BF16)

    n_chunks = rows_per_step // CHUNK

    def slot(s):
        return b_scr.at[s], k_scr.at[s], f_scr.at[s]

    def chunk(c, c_next, s):
        bs, ks, fs = slot(s)
        s_mats = _hgrn_scores(c, q_ref, bs, ks, fs)
        _hgrn_gates(c_next, lb, tri, f_ref, *slot(1 - s))
        _hgrn_outputs(c, s_mats, q_ref, i_ref, st_scr, o_scr, bs, ks)

    def pair_body(i, carry):
        chunk(2 * i, 2 * i + 1, 0)
        chunk(2 * i + 1, jnp.minimum(2 * i + 2, n_chunks - 1), 1)
        return carry

    _hgrn_gates(0, lb, tri, f_ref, *slot(0))
    lax.fori_loop(0, n_chunks // 2, pair_body, 0)

    gate_m = mod_ref[0, 2:3, :]
    hnw = hnw_ref[...]
    cw0 = cw_ref[0:1, :]
    cw1 = cw_ref[1:2, :]
    cw2 = cw_ref[2:3, :]
    for r in range(rows_per_step // epi_rows):
        rs = slice(r * epi_rows, (r + 1) * epi_rows)
        parts = []
        for h in range(HGRN_HEADS):
            hs = slice(h * HEAD_DIM, (h + 1) * HEAD_DIM)
            oh = _rms(o_scr[rs, hs], hnw)
            go = g_ref[rs, hs].astype(F32)
            parts.append((oh * (go * _sigmoid(go))).astype(BF16))
        ya = jnp.dot(jnp.concatenate(parts, axis=1), wa_ref[...], preferred_element_type=F32)
        uc = cc_ref[rs, :].astype(F32) * ch_ref[rs, :].astype(F32)
        carry_scr[8:8 + epi_rows, :] = uc
        s1 = carry_scr[7:7 + epi_rows, :]
        s2 = carry_scr[6:6 + epi_rows, :]
        carry_scr[0:8, :] = uc[epi_rows - 8:epi_rows, :]
        yc = cb_ref[rs, :].astype(F32) * (cw2 * uc + cw1 * s1 + cw0 * s2)
        yb = jnp.dot(yc.astype(BF16), wb_ref[...], preferred_element_type=F32)
        ga = jnp.concatenate([ga0_ref[rs, :], ga1_ref[rs, :]], axis=1).astype(F32)
        gb = jnp.concatenate([gb0_ref[rs, :], gb1_ref[rs, :]], axis=1).astype(F32)
        merged = (_sigmoid(ga) * ya + _sigmoid(gb) * yb).astype(BF16)
        o_ref[rs, :] = x_ref[rs, :] + gate_m * jnp.dot(merged, wm_ref[...], preferred_element_type=F32)


def _mix(proj, x2, mod, lb_table, hgrn_norm_w, conv_w, wa, wb, wm, layer, bsz, seq, tt):
    n = x2.shape[0]
    per_b = seq // tt

    def col(col_block):
        return pl.BlockSpec((tt, 512), lambda b, t: (b * per_b + t, col_block))

    const = lambda shape: pl.BlockSpec(shape, lambda b, t: (0,) * len(shape))
    in_specs = [
        const(lb_table.shape),
        col(0), col(1), col(2), col(3),
        col(4), col(5), col(6),
        col(7), col(8), col(9), col(10),
        pl.BlockSpec((tt, D_MODEL), lambda b, t: (b * per_b + t, 0)),
        pl.BlockSpec((1, 6, D_MODEL), lambda b, t: (b, 0, 0)),
        const((1, HEAD_DIM)), const((CONV_K, CONV_WIDTH)),
        const((HGRN_WIDTH, D_MODEL)), const((CONV_WIDTH, D_MODEL)), const((D_MODEL, D_MODEL)),
    ]
    kern = functools.partial(_mix_kernel, layer=layer, rows_per_step=tt, epi_rows=tt)
    return pl.pallas_call(
        kern,
        out_shape=jax.ShapeDtypeStruct((n, D_MODEL), F32),
        grid=(bsz, per_b),
        in_specs=in_specs,
        out_specs=pl.BlockSpec((tt, D_MODEL), lambda b, t: (b * per_b + t, 0)),
        scratch_shapes=[pltpu.VMEM((HGRN_HEADS, HEAD_DIM, HEAD_DIM), F32),
                        pltpu.VMEM((tt, HGRN_WIDTH), F32),
                        pltpu.VMEM((8 + tt, CONV_WIDTH), F32),
                        pltpu.VMEM((2, CHUNK, HGRN_WIDTH), F32),
                        pltpu.VMEM((2, CHUNK, HGRN_WIDTH), F32),
                        pltpu.VMEM((2, CHUNK, HGRN_WIDTH), F32)],
        compiler_params=pltpu.CompilerParams(dimension_semantics=("arbitrary", "arbitrary"),
                                             vmem_limit_bytes=VMEM_LIMIT),
        name="mix",
    )(lb_table, *([proj] * 11),
      x2, mod, hgrn_norm_w.reshape(1, HEAD_DIM), conv_w, wa, wb, wm)


def _route_kernel(h_ref, mod_ref, nw_ref, wr_ref, br_ref, u_ref, oh4_ref, rk_ref, pw_ref, cnt_ref, *, tr):
    shift = mod_ref[0, 3:4, :]
    scale = mod_ref[0, 4:5, :]
    u = _rms(h_ref[...], nw_ref[...]) * (1.0 + scale) + shift
    u_ref[...] = _pack_rows(u)
    lane = lax.broadcasted_iota(jnp.int32, (tr, LANES), 1).astype(F32)
    logits = jnp.dot(u.astype(BF16), wr_ref[...], preferred_element_type=F32) + br_ref[...]
    logits = jnp.where(lane < N_EXPERTS, logits, -jnp.inf)
    idx, val = [], []
    cur = logits
    for _ in range(TOP_K):
        m = jnp.max(cur, axis=-1, keepdims=True)
        i = jnp.min(jnp.where(cur == m, lane, float(LANES)), axis=-1, keepdims=True)
        idx.append(i)
        val.append(m)
        cur = jnp.where(lane == i, -jnp.inf, cur)
    ex = [jnp.exp(v - val[0]) for v in val]
    den = ex[0] + ex[1] + ex[2] + ex[3]
    onehot = jnp.zeros((tr, LANES), F32)
    for i in idx:
        onehot = onehot + jnp.where(lane == i, 1.0, 0.0)
    ri = lax.broadcasted_iota(jnp.int32, (tr, tr), 0)
    ci = lax.broadcasted_iota(jnp.int32, (tr, tr), 1)
    tri = jnp.where(ci < ri, 1.0, 0.0).astype(BF16)
    pref = jnp.dot(tri, onehot.astype(BF16), preferred_element_type=F32)
    rk = jnp.zeros((tr, LANES), jnp.int32)
    pw = jnp.zeros((tr, LANES), F32)
    oh4 = jnp.zeros((tr, LANES), F32)
    for j in range(TOP_K):
        rank = jnp.sum(jnp.where(lane == idx[j], pref, 0.0), axis=-1, keepdims=True).astype(jnp.int32)
        rk = jnp.where(lane == j, rank, rk)
        pw = jnp.where(lane == j, ex[j] / den, pw)
        oh4 = oh4 + jnp.where(lane == idx[j] + N_EXPERTS * j, 1.0, 0.0)
    oh4_ref[...] = oh4.astype(BF16)
    rk_ref[...] = rk
    pw_ref[...] = pw
    cnt_ref[0] = jnp.sum(onehot, axis=0, keepdims=True).astype(jnp.int32)


def _route(h1, mod, norm_w, wr_pad, br_pad, seq, tr, tile0, n):
    per_b = seq // tr
    nt = n // tr
    return pl.pallas_call(
        functools.partial(_route_kernel, tr=tr),
        out_shape=(jax.ShapeDtypeStruct((n, PACKED), jnp.int32),
                   jax.ShapeDtypeStruct((n, LANES), BF16),
                   jax.ShapeDtypeStruct((n, LANES), jnp.int32),
                   jax.ShapeDtypeStruct((n, LANES), F32),
                   jax.ShapeDtypeStruct((nt, 1, LANES), jnp.int32)),
        grid=(nt,),
        in_specs=[pl.BlockSpec((tr, D_MODEL), lambda i: (tile0 + i, 0)),
                  pl.BlockSpec((1, 6, D_MODEL), lambda i: ((tile0 + i) // per_b, 0, 0)),
                  pl.BlockSpec((1, D_MODEL), lambda i: (0, 0)),
                  pl.BlockSpec((D_MODEL, LANES), lambda i: (0, 0)),
                  pl.BlockSpec((1, LANES), lambda i: (0, 0))],
        out_specs=(pl.BlockSpec((tr, PACKED), lambda i: (i, 0)),
                   pl.BlockSpec((tr, LANES), lambda i: (i, 0)),
                   pl.BlockSpec((tr, LANES), lambda i: (i, 0)),
                   pl.BlockSpec((tr, LANES), lambda i: (i, 0)),
                   pl.BlockSpec((1, 1, LANES), lambda i: (i, 0, 0))),
        compiler_params=pltpu.CompilerParams(dimension_semantics=("arbitrary",),
                                             vmem_limit_bytes=VMEM_LIMIT),
        name="route",
    )(h1, mod, norm_w.reshape(1, D_MODEL), wr_pad, br_pad)


def _dest_kernel(oh4_ref, rk_ref, bt_ref, o_ref):
    oh = oh4_ref[...]
    start = (jnp.dot(oh, bt_ref[0, 0], preferred_element_type=F32)
             + 256.0 * jnp.dot(oh, bt_ref[0, 1], preferred_element_type=F32)
             + 65536.0 * jnp.dot(oh, bt_ref[0, 2], preferred_element_type=F32))
    o_ref[...] = (start + rk_ref[...].astype(F32)).T[:8, :].astype(jnp.int32)


def _dest(oh4, rk, bt, tr):
    n = oh4.shape[0]
    return pl.pallas_call(
        _dest_kernel,
        out_shape=jax.ShapeDtypeStruct((8, n), jnp.int32),
        grid=(n // tr,),
        in_specs=[pl.BlockSpec((tr, LANES), lambda i: (i, 0)),
                  pl.BlockSpec((tr, LANES), lambda i: (i, 0)),
                  pl.BlockSpec((1, 3, LANES, LANES), lambda i: (i, 0, 0, 0))],
        out_specs=pl.BlockSpec((8, tr), lambda i: (0, i)),
        name="dest",
    )(oh4, rk, bt)


def _sc_workers():
    info = plsc.get_sparse_core_info()
    return info.num_cores, info.num_cores * info.num_subcores


def _sc_scatter_rows(rows, idx_slots, pad_idx, n_out):
    n_cores, n_workers = _sc_workers()
    n, w = rows.shape
    k = idx_slots.shape[0] // n
    per_worker = n // n_workers
    pad_per_worker = pad_idx.shape[0] // n_workers
    assert per_worker % SC_CHUNK == 0 and pad_per_worker % SC_CHUNK == 0
    mesh = plsc.VectorSubcoreMesh(core_axis_name="c", subcore_axis_name="s")
    zeros = jnp.zeros((SC_CHUNK, w), rows.dtype)

    n_chunks = per_worker // SC_CHUNK
    assert n_chunks % 2 == 0

    @functools.partial(
        pl.kernel, mesh=mesh,
        out_type=jax.ShapeDtypeStruct((n_out, w), rows.dtype),
        scratch_types=[pltpu.VMEM((SC_CHUNK,), jnp.int32)] * k
        + [pltpu.VMEM((SC_CHUNK, w), rows.dtype)] * 2
        + [pltpu.SemaphoreType.DMA] * 2,
        name="sc_scatter",
    )
    def scatter(rows_hbm, idx_hbm, pad_hbm, zeros_hbm, out_hbm, *scratch):
        idx_bufs = scratch[:k]
        row_bufs = scratch[k:k + 2]
        sem_rows, sem_out = scratch[k + 2:]
        wid = lax.axis_index("s") * n_cores + lax.axis_index("c")

        def chunk_off(c):
            return pl.multiple_of(wid * per_worker + c * SC_CHUNK, 8)

        def load(c, b):
            pltpu.async_copy(rows_hbm.at[pl.ds(chunk_off(c), SC_CHUNK)], row_bufs[b], sem_rows)

        def wait_load(b):
            pltpu.make_async_copy(rows_hbm.at[pl.ds(0, SC_CHUNK)], row_bufs[b], sem_rows).wait()

        def scatter_chunk(c, b):
            for j in range(k):
                pltpu.sync_copy(idx_hbm.at[pl.ds(pl.multiple_of(j * n + chunk_off(c), 8), SC_CHUNK)],
                                idx_bufs[j])
            copies = [pltpu.async_copy(row_bufs[b], out_hbm.at[idx_bufs[j]], sem_out) for j in range(k)]
            for cp in copies:
                cp.wait()

        load(0, 0)

        def body(i, carry):
            wait_load(0)
            load(2 * i + 1, 1)
            scatter_chunk(2 * i, 0)
            wait_load(1)

            @pl.when(i < n_chunks // 2 - 1)
            def _():
                load(2 * i + 2, 0)

            scatter_chunk(2 * i + 1, 1)
            return carry

        lax.fori_loop(0, n_chunks // 2, body, 0)
        pltpu.sync_copy(zeros_hbm, row_bufs[0])

        def pad_body(i, carry):
            off = pl.multiple_of(wid * pad_per_worker + i * SC_CHUNK, 8)
            pltpu.sync_copy(pad_hbm.at[pl.ds(off, SC_CHUNK)], idx_bufs[0])
            pltpu.async_copy(row_bufs[0], out_hbm.at[idx_bufs[0]], sem_out).wait()
            return carry

        lax.fori_loop(0, pad_per_worker // SC_CHUNK, pad_body, 0)

    return scatter(rows, idx_slots, pad_idx, zeros)


def _experts_kernel(be_ref, nu_ref, first_ref, next_ref, slot_ref, quarters_ref, x_ref, w1_hbm, b1_ref, w2_hbm,
                    b2_ref, o_ref, w1_f32, w2_f32, w1_bf, w2_bf, sems):
    i = pl.program_id(0)
    used = i < nu_ref[0]

    def weight_copies(e, s):
        return (pltpu.make_async_copy(w1_hbm.at[e], w1_f32.at[s], sems.at[s]),
                pltpu.make_async_copy(w2_hbm.at[e], w2_f32.at[s], sems.at[s]))

    @pl.when(used & (first_ref[i] == 1))
    def _():
        s = slot_ref[i]

        @pl.when(i == 0)
        def _():
            for cp in weight_copies(be_ref[i], s):
                cp.start()

        for cp in weight_copies(be_ref[i], s):
            cp.wait()
        cw = 256
        for c in range(2 * D_FF // cw):
            w1_bf[:, c * cw:(c + 1) * cw] = w1_f32[s, :, c * cw:(c + 1) * cw].astype(BF16)
        for c in range(D_MODEL // cw):
            w2_bf[:, c * cw:(c + 1) * cw] = w2_f32[s, :, c * cw:(c + 1) * cw].astype(BF16)

        @pl.when(next_ref[i] >= 0)
        def _():
            for cp in weight_copies(next_ref[i], 1 - s):
                cp.start()

    def mlp(n_rows):
        step = 512
        x = _unpack_rows(x_ref[0:n_rows, :]).astype(BF16)
        acc = None
        for j in range(D_FF // step):
            cs = slice(j * step, (j + 1) * step)
            ls = slice(D_FF + j * step, D_FF + (j + 1) * step)
            glu = jnp.dot(x, w1_bf[:, cs], preferred_element_type=F32) + b1_ref[0, :, cs]
            lin = jnp.dot(x, w1_bf[:, ls], preferred_element_type=F32) + b1_ref[0, :, ls]
            glu = jnp.minimum(glu, SWIGLU_LIMIT)
            lin = jnp.clip(lin, -SWIGLU_LIMIT, SWIGLU_LIMIT)
            act = (glu * _sigmoid(glu, SWIGLU_ALPHA) * (lin + 1.0)).astype(BF16)
            part = jnp.dot(act, w2_bf[cs, :], preferred_element_type=F32)
            acc = part if acc is None else acc + part
        o_ref[0:n_rows, :] = _pack_rows(acc + b2_ref[0])
        if n_rows < ROW_BLOCK:
            o_ref[n_rows:ROW_BLOCK, :] = jnp.zeros((ROW_BLOCK - n_rows, PACKED), jnp.int32)

    for quarters in range(1, 5):
        @pl.when(used & (quarters_ref[i] == quarters))
        def _():
            mlp(quarters * (ROW_BLOCK // 4))

    @pl.when(jnp.logical_not(used))
    def _():
        o_ref[...] = jnp.zeros_like(o_ref)


def _experts(plan, xs, w1, b1, w2, b2):
    n_rows = xs.shape[0]
    nb = n_rows // ROW_BLOCK
    grid_spec = pltpu.PrefetchScalarGridSpec(
        num_scalar_prefetch=6,
        grid=(nb,),
        in_specs=[pl.BlockSpec((ROW_BLOCK, PACKED), lambda i, be, nu, *_: (jnp.minimum(i, nu[0] - 1), 0)),
                  pl.BlockSpec(memory_space=pl.ANY),
                  pl.BlockSpec((1, 1, 2 * D_FF), lambda i, be, *_: (be[i], 0, 0)),
                  pl.BlockSpec(memory_space=pl.ANY),
                  pl.BlockSpec((1, 1, D_MODEL), lambda i, be, *_: (be[i], 0, 0))],
        out_specs=pl.BlockSpec((ROW_BLOCK, PACKED), lambda i, *_: (i, 0)),
        scratch_shapes=[pltpu.VMEM((2, D_MODEL, 2 * D_FF), F32), pltpu.VMEM((2, D_FF, D_MODEL), F32),
                        pltpu.VMEM((D_MODEL, 2 * D_FF), BF16), pltpu.VMEM((D_FF, D_MODEL), BF16),
                        pltpu.SemaphoreType.DMA((2,))],
    )
    return pl.pallas_call(
        _experts_kernel,
        out_shape=jax.ShapeDtypeStruct((n_rows, PACKED), jnp.int32),
        grid_spec=grid_spec,
        compiler_params=pltpu.CompilerParams(dimension_semantics=("arbitrary",),
                                             vmem_limit_bytes=VMEM_LIMIT),
        name="experts",
    )(*plan, xs, w1, b1.reshape(N_EXPERTS, 1, 2 * D_FF), w2, b2.reshape(N_EXPERTS, 1, D_MODEL))


def _sc_gather_rows(table, idx_flat):
    n_cores, n_workers = _sc_workers()
    n_idx = idx_flat.shape[0]
    w = table.shape[1]
    per_worker = n_idx // n_workers
    n_chunks = per_worker // SC_CHUNK
    assert per_worker * n_workers == n_idx and n_chunks * SC_CHUNK == per_worker and n_chunks % 2 == 0
    mesh = plsc.VectorSubcoreMesh(core_axis_name="c", subcore_axis_name="s")

    @functools.partial(
        pl.kernel, mesh=mesh,
        out_type=jax.ShapeDtypeStruct((n_idx, w), table.dtype),
        scratch_types=[pltpu.VMEM((SC_CHUNK,), jnp.int32), pltpu.VMEM((SC_CHUNK,), jnp.int32),
                       pltpu.VMEM((SC_CHUNK, w), table.dtype), pltpu.VMEM((SC_CHUNK, w), table.dtype),
                       pltpu.SemaphoreType.DMA, pltpu.SemaphoreType.DMA],
        name="sc_gather",
    )
    def gather(table_hbm, idx_hbm, out_hbm, idx0, idx1, rows0, rows1, sem0, sem1):
        wid = lax.axis_index("s") * n_cores + lax.axis_index("c")
        bufs = ((idx0, rows0, sem0), (idx1, rows1, sem1))

        def chunk_off(c):
            return pl.multiple_of(wid * per_worker + c * SC_CHUNK, 8)

        def start(c, b):
            idx_v, rows_v, sem = bufs[b]
            pltpu.sync_copy(idx_hbm.at[pl.ds(chunk_off(c), SC_CHUNK)], idx_v)
            pltpu.async_copy(table_hbm.at[idx_v], rows_v, sem)

        def finish(c, b):
            idx_v, rows_v, sem = bufs[b]
            pltpu.make_async_copy(table_hbm.at[idx_v], rows_v, sem).wait()
            pltpu.sync_copy(rows_v, out_hbm.at[pl.ds(chunk_off(c), SC_CHUNK)])

        start(0, 0)

        def body(i, carry):
            start(2 * i + 1, 1)
            finish(2 * i, 0)

            @pl.when(i < n_chunks // 2 - 1)
            def _():
                start(2 * i + 2, 0)

            finish(2 * i + 1, 1)
            return carry

        lax.fori_loop(0, n_chunks // 2, body, 0)

    return gather(table, idx_flat)


def _finish_kernel(h_ref, pw_ref, mod_ref, nw_ref, y0_ref, y1_ref, y2_ref, y3_ref, *rest):
    o_ref = rest[-1]
    pw = pw_ref[...]
    moe = pw[:, 0:1] * _unpack_rows(y0_ref[0])
    for j, y_ref in enumerate((y1_ref, y2_ref, y3_ref), start=1):
        moe = moe + pw[:, j:j + 1] * _unpack_rows(y_ref[0])
    gate_f = mod_ref[0, 5:6, :]
    o_ref[...] = _rms(h_ref[...] + gate_f * moe, nw_ref[...])


def _finish(h1, pw, mod, norm_w, y4, seq, tc, tile0, prev_out):
    n = pw.shape[0]
    per_b = seq // tc
    slot = lambda j: pl.BlockSpec((1, tc, PACKED), lambda i: (j, i, 0))
    in_specs = [pl.BlockSpec((tc, D_MODEL), lambda i: (tile0 + i, 0)),
                pl.BlockSpec((tc, LANES), lambda i: (i, 0)),
                pl.BlockSpec((1, 6, D_MODEL), lambda i: ((tile0 + i) // per_b, 0, 0)),
                pl.BlockSpec((1, D_MODEL), lambda i: (0, 0)),
                slot(0), slot(1), slot(2), slot(3)]
    args = [h1, pw, mod, norm_w.reshape(1, D_MODEL), y4, y4, y4, y4]
    aliases = {}
    if prev_out is not None:
        in_specs.append(pl.BlockSpec(memory_space=pl.ANY))
        args.append(prev_out)
        aliases = {len(args) - 1: 0}
    return pl.pallas_call(
        _finish_kernel,
        out_shape=jax.ShapeDtypeStruct(h1.shape, F32),
        grid=(n // tc,),
        in_specs=in_specs,
        out_specs=pl.BlockSpec((tc, D_MODEL), lambda i: (tile0 + i, 0)),
        input_output_aliases=aliases,
        compiler_params=pltpu.CompilerParams(dimension_semantics=("arbitrary",),
                                             vmem_limit_bytes=VMEM_LIMIT),
        name="finish",
    )(*args)


def _moe_plan(counts, n_assign):
    cnt = counts[:, 0, :N_EXPERTS]
    sizes = jnp.sum(cnt, axis=0)
    padded = (sizes + ROW_BLOCK - 1) // ROW_BLOCK * ROW_BLOCK
    pad_end = jnp.cumsum(padded)
    pad_start = pad_end - padded
    tile_base = pad_start[None, :] + jnp.cumsum(cnt, axis=0) - cnt
    digits = jnp.stack([tile_base % 256, (tile_base // 256) % 256, tile_base // 65536], axis=1)
    rows = jnp.tile(digits, (1, 1, TOP_K))
    slot_of_row = jnp.arange(LANES, dtype=jnp.int32) // N_EXPERTS
    col = jnp.arange(LANES, dtype=jnp.int32)
    bt = jnp.where(slot_of_row[:, None] == col[None, :], rows[..., None], 0).astype(BF16)
    nb = n_assign // ROW_BLOCK + N_EXPERTS
    block_start = jnp.arange(nb, dtype=jnp.int32) * ROW_BLOCK
    block_e = jnp.minimum(jnp.sum(pad_end[None, :] <= block_start[:, None], axis=1),
                          N_EXPERTS - 1).astype(jnp.int32)
    n_used = (pad_end[-1] // ROW_BLOCK).astype(jnp.int32).reshape(1)
    nonempty = padded > 0
    ids = jnp.arange(N_EXPERTS, dtype=jnp.int32)
    suffix_min = lax.cummin(jnp.where(nonempty, ids, N_EXPERTS), reverse=True)
    after = jnp.concatenate([suffix_min[1:], jnp.full((1,), N_EXPERTS, jnp.int32)])
    next_nonempty = jnp.where(after < N_EXPERTS, after, -1).astype(jnp.int32)
    slot_of = ((jnp.cumsum(nonempty) - 1) % 2).astype(jnp.int32)
    first = jnp.concatenate([jnp.ones((1,), jnp.int32),
                             (block_e[1:] != block_e[:-1]).astype(jnp.int32)])
    valid_rows = (pad_start + sizes)[block_e] - block_start
    quarter = ROW_BLOCK // 4
    quarters = jnp.clip((valid_rows + quarter - 1) // quarter, 1, 4).astype(jnp.int32)
    expert_plan = (block_e, n_used, first, next_nonempty[block_e], slot_of[block_e], quarters)
    r = jnp.arange(ROW_BLOCK, dtype=jnp.int32)[None, :]
    pad_idx = jnp.where(r < (padded - sizes)[:, None], (pad_start + sizes)[:, None] + r,
                        (nb - 1) * ROW_BLOCK + r).astype(jnp.int32).reshape(PAD_SLOTS)
    return bt, expert_plan, pad_idx


def kernel(x, c, w_ada, b_ada, norm_mix_w, w_in, hgrn_lower_bounds, hgrn_norm_w, conv_w,
           w_hgrn_out, w_conv_out, w_mix_out, norm_ffn_w, w_router, b_router, w1, b1, w2, b2,
           norm_final_w):
    bsz, seq, d = x.shape
    assert d == D_MODEL and seq % CHUNK == 0
    n = bsz * seq
    depth = w_ada.shape[0]
    tile = min(512, seq)
    assert seq % tile == 0 and n % (MOE_GROUPS * tile) == 0 and (n // MOE_GROUPS * TOP_K) % ROW_BLOCK == 0
    h = x.reshape(n, d)
    wr_pad = jnp.zeros((depth, D_MODEL, LANES), BF16).at[:, :, :N_EXPERTS].set(w_router.astype(BF16))
    br_pad = jnp.zeros((depth, 1, LANES), F32).at[:, 0, :N_EXPERTS].set(b_router)
    for layer in range(depth):
        mod = _ada(c, w_ada[layer], b_ada[layer]).reshape(bsz, 6, d)
        proj = _inproj(h, mod, norm_mix_w[layer], w_in[layer].astype(BF16), seq, tile)
        h = _mix(proj, h, mod, hgrn_lower_bounds, hgrn_norm_w[layer], conv_w[layer],
                 w_hgrn_out[layer].astype(BF16), w_conv_out[layer].astype(BF16),
                 w_mix_out[layer].astype(BF16), layer, bsz, seq, min(MIX_ROWS, seq))
        assert layer == depth - 1, "only the last layer applies the final norm"
        ng = n // MOE_GROUPS
        n_rows = (ng * TOP_K // ROW_BLOCK + N_EXPERTS) * ROW_BLOCK
        routed = [_route(h, mod, norm_ffn_w[layer], wr_pad[layer], br_pad[layer], seq, tile,
                         grp * (ng // tile), ng) for grp in range(MOE_GROUPS)]
        plans = jax.vmap(lambda cnt: _moe_plan(cnt, ng * TOP_K))(jnp.stack([r[4] for r in routed]))
        out = None
        for grp in range(MOE_GROUPS):
            tile0 = grp * (ng // tile)
            u2, oh4, rk, pw, _ = routed[grp]
            bt, expert_plan, pad_idx = jax.tree.map(lambda a: a[grp], plans)
            dest_slots = _dest(oh4, rk, bt, tile)[:TOP_K].reshape(TOP_K * ng)
            xs = _sc_scatter_rows(u2, dest_slots, pad_idx, n_rows)
            ys = _experts(expert_plan, xs, w1[layer], b1[layer], w2[layer], b2[layer])
            y4 = _sc_gather_rows(ys, dest_slots).reshape(TOP_K, ng, PACKED)
            out = _finish(h, pw, mod, norm_final_w, y4, seq, tile, tile0, out)
        h = out
    return h.reshape(bsz, seq, d)
```

```python
import functools

import jax
import jax.numpy as jnp
from jax import lax
from jax.experimental import pallas as pl
from jax.experimental.pallas import tpu as pltpu
from jax.experimental.pallas import tpu_sc as plsc

F32 = jnp.float32
BF16 = jnp.bfloat16

D_MODEL = 1024
HGRN_HEADS = 4
HEAD_DIM = 128
HGRN_WIDTH = HGRN_HEADS * HEAD_DIM
CONV_WIDTH = 512
CONV_K = 3
CHUNK = 64
N_EXPERTS = 32
TOP_K = 4
D_FF = 1024
SWIGLU_LIMIT = 7.0
SWIGLU_ALPHA = 1.702
EPS = 1e-6
LOG2_E = 1.4426950408889634
IN_COLS = 4 * HGRN_WIDTH + 3 * CONV_WIDTH + 2 * D_MODEL
LANES = 128
DENSE_ROWS = 1024
ROW_BLOCK = 1024
PACKED = D_MODEL // 2
MOE_GROUPS = 2
SC_CHUNK = 64
PAD_SLOTS = N_EXPERTS * ROW_BLOCK
VMEM_LIMIT = 56 * 1024 * 1024


def _sigmoid(x, scale=1.0):
    return 1.0 / (1.0 + jnp.exp2(x * (-scale * LOG2_E)))


def _rms(x, w):
    ms = jnp.mean(x * x, axis=-1, keepdims=True)
    return x * lax.rsqrt(ms + EPS) * w


def _pack_rows(x):
    w = x.shape[1] // 2
    lo = lax.bitcast_convert_type(x[:, :w].astype(BF16).astype(F32), jnp.uint32)
    hi = lax.bitcast_convert_type(x[:, w:].astype(BF16).astype(F32), jnp.uint32)
    return lax.bitcast_convert_type((lo >> 16) | (hi & jnp.uint32(0xFFFF0000)), jnp.int32)


def _unpack_rows(p):
    u = lax.bitcast_convert_type(p, jnp.uint32)
    lo = lax.bitcast_convert_type(u << 16, F32)
    hi = lax.bitcast_convert_type(u & jnp.uint32(0xFFFF0000), F32)
    return jnp.concatenate([lo, hi], axis=1)


def _nt_dot(a, b):
    return lax.dot_general(a, b, (((1,), (1,)), ((), ())), preferred_element_type=F32)


def _tn_dot(a, b):
    return lax.dot_general(a, b, (((0,), (0,)), ((), ())), preferred_element_type=F32)


def _ada_kernel(c_ref, w_ref, b_ref, o_ref):
    c = c_ref[...]
    sc = (c * _sigmoid(c)).astype(BF16)
    o_ref[...] = jnp.dot(sc, w_ref[...].astype(BF16), preferred_element_type=F32) + b_ref[...]


def _ada(c, w_ada, b_ada):
    bsz, d = c.shape
    n = w_ada.shape[1]
    return pl.pallas_call(
        _ada_kernel,
        out_shape=jax.ShapeDtypeStruct((bsz, n), F32),
        grid=(n // d,),
        in_specs=[pl.BlockSpec((bsz, d), lambda j: (0, 0)),
                  pl.BlockSpec((d, d), lambda j: (0, j)),
                  pl.BlockSpec((1, d), lambda j: (0, j))],
        out_specs=pl.BlockSpec((bsz, d), lambda j: (0, j)),
        name="ada",
    )(c, w_ada, b_ada.reshape(1, n))


def _inproj_kernel(x_ref, mod_ref, nw_ref, w_ref, o_ref, u_scr):
    y = _rms(x_ref[...], nw_ref[...])
    shift = mod_ref[0, 0:1, :]
    scale = mod_ref[0, 1:2, :]
    u_scr[...] = (y * (1.0 + scale) + shift).astype(BF16)

    step = 512
    for j in range(IN_COLS // step):
        o_ref[:, j * step:(j + 1) * step] = jnp.dot(
            u_scr[...], w_ref[:, j * step:(j + 1) * step], preferred_element_type=F32).astype(BF16)


def _inproj(x2, mod, norm_w, w_in_bf, seq, tm):
    n = x2.shape[0]
    per_b = seq // tm
    return pl.pallas_call(
        _inproj_kernel,
        out_shape=jax.ShapeDtypeStruct((n, IN_COLS), BF16),
        grid=(n // tm,),
        in_specs=[pl.BlockSpec((tm, D_MODEL), lambda i: (i, 0)),
                  pl.BlockSpec((1, 6, D_MODEL), lambda i: (i // per_b, 0, 0)),
                  pl.BlockSpec((1, D_MODEL), lambda i: (0, 0)),
                  pl.BlockSpec((D_MODEL, IN_COLS), lambda i: (0, 0), pipeline_mode=pl.Buffered(1))],
        out_specs=pl.BlockSpec((tm, IN_COLS), lambda i: (i, 0)),
        scratch_shapes=[pltpu.VMEM((tm, D_MODEL), BF16)],
        compiler_params=pltpu.CompilerParams(dimension_semantics=("arbitrary",),
                                             vmem_limit_bytes=VMEM_LIMIT),
        name="inproj",
    )(x2, mod, norm_w.reshape(1, D_MODEL), w_in_bf)


_HEADS = [slice(h * HEAD_DIM, (h + 1) * HEAD_DIM) for h in range(HGRN_HEADS)]


def _chunk_rows(c):
    if isinstance(c, int):
        return pl.ds(c * CHUNK, CHUNK)
    return pl.ds(pl.multiple_of(c * CHUNK, CHUNK), CHUNK)


def _hgrn_gates(c, lb, tri, f_ref, b_scr, k_scr, f_scr):
    fx = f_ref[_chunk_rows(c), :].astype(F32)
    f = lb + (1.0 - lb) * _sigmoid(fx)
    g = jnp.maximum(jnp.log(f), -128.0) * LOG2_E
    g1 = g.astype(BF16)
    r1 = g - g1.astype(F32)
    g2 = r1.astype(BF16)
    g3 = (r1 - g2.astype(F32)).astype(BF16)
    b_scr[...] = (jnp.dot(tri, g1, preferred_element_type=F32)
                  + jnp.dot(tri, g2, preferred_element_type=F32)
                  + jnp.dot(tri, g3, preferred_element_type=F32))
    f_scr[...] = f
    k_scr[...] = 1.0 - f


def _hgrn_scores(c, q_ref, b_scr, k_scr, f_scr):
    rows = _chunk_rows(c)
    qs = [q_ref[rows, hs].astype(F32) for hs in _HEADS]
    s_mats = [_level_scores(1, qs[h], _HEADS[h], b_scr, k_scr, f_scr) for h in range(HGRN_HEADS)]
    for lvl in range(2, 7):
        for h in range(HGRN_HEADS):
            s_mats[h] = s_mats[h] + _level_scores(lvl, qs[h], _HEADS[h], b_scr, k_scr, f_scr)
    return s_mats


def _hgrn_outputs(c, s_mats, q_ref, v_ref, st_scr, o_scr, b_scr, k_scr):
    rows = _chunk_rows(c)
    for h, hs in enumerate(_HEADS):
        q = q_ref[rows, hs].astype(F32)
        v_bf = v_ref[rows, hs]
        b = b_scr[:, hs]
        kk = k_scr[:, hs]
        b_last = b_scr[CHUNK - 1:CHUNK, hs]
        st = st_scr[h]
        qd = (q * jnp.exp2(b)).astype(BF16)
        kdec = (kk * jnp.exp2(b_last - b)).astype(BF16)
        diag = jnp.sum(q * kk, axis=-1, keepdims=True)
        o_scr[rows, hs] = (_nt_dot(qd, st.astype(BF16))
                           + jnp.dot(s_mats[h].astype(BF16), v_bf, preferred_element_type=F32)
                           + diag * v_bf.astype(F32))
        st_scr[h] = jnp.exp2(b_last) * st + _tn_dot(v_bf, kdec)


def _level_scores(lvl, q, hs, b_scr, k_scr, f_scr):
    row = lax.broadcasted_iota(jnp.int32, (CHUNK, 1), 0)
    col = lax.broadcasted_iota(jnp.int32, (1, CHUNK), 1)
    b = b_scr[:, hs]
    kk = k_scr[:, hs]
    blk = 1 << lvl
    half = blk // 2
    if lvl == 1:
        odd = (row & 1) == 1
        qx = jnp.where(odd, q * f_scr[:, hs], 0.0).astype(BF16)
        kx = jnp.where(odd, 0.0, kk).astype(BF16)
        return jnp.where((row >> 1) == (col >> 1), _nt_dot(qx, kx), 0.0)
    if half < 8:
        groups = []
        sub = lax.broadcasted_iota(jnp.int32, (8, 1), 0)
        for j in range(CHUNK // 8):
            rj = None
            for k in reversed(range(8 // blk)):
                m = 8 * j + k * blk + half - 1
                bm = jnp.broadcast_to(b_scr[m:m + 1, hs], (8, HEAD_DIM))
                rj = bm if rj is None else jnp.where(sub < (k + 1) * blk, bm, rj)
            groups.append(rj)
        ref = jnp.concatenate(groups, axis=0)
        second = (row & (blk - 1)) >= half
        qx = (q * jnp.exp2(jnp.where(second, b - ref, -jnp.inf))).astype(BF16)
        kx = (kk * jnp.exp2(jnp.where(second, -jnp.inf, ref - b))).astype(BF16)
        return jnp.where((row >> lvl) == (col >> lvl), _nt_dot(qx, kx), 0.0)
    n_blk = CHUNK // blk
    qparts, kparts = [], []
    for j in range(n_blk):
        m = j * blk + half - 1
        bm = b_scr[m:m + 1, hs]
        tq = slice(j * blk + half, (j + 1) * blk)
        tk = slice(j * blk, j * blk + half)
        qparts.append(q[tq] * jnp.exp2(b[tq] - bm))
        kparts.append(kk[tk] * jnp.exp2(bm - b[tk]))
        kparts.append(jnp.zeros((half, HEAD_DIM), F32))
    qx = jnp.concatenate(qparts, axis=0).astype(BF16)
    kx = jnp.concatenate(kparts, axis=0).astype(BF16)
    sc = _nt_dot(qx, kx)
    if n_blk > 1:
        crow = lax.broadcasted_iota(jnp.int32, (CHUNK // 2, 1), 0)
        sc = jnp.where((crow // half) == (col >> lvl), sc, 0.0)
    pieces = []
    for j in range(n_blk):
        pieces.append(jnp.zeros((half, CHUNK), F32))
        pieces.append(sc[j * half:(j + 1) * half])
    return jnp.concatenate(pieces, axis=0)


def _mix_kernel(lbt_ref, q_ref, f_ref, i_ref, g_ref, cb_ref, cc_ref, ch_ref, ga0_ref, ga1_ref,
                gb0_ref, gb1_ref, x_ref, mod_ref, hnw_ref, cw_ref, wa_ref, wb_ref, wm_ref,
                o_ref, st_scr, o_scr, carry_scr, b_scr, k_scr, f_scr, *, layer, rows_per_step, epi_rows):
    @pl.when(pl.program_id(1) == 0)
    def _():
        st_scr[...] = jnp.zeros_like(st_scr)
        carry_scr[0:8, :] = jnp.zeros((8, CONV_WIDTH), F32)

    tab = lbt_ref[...]
    tmax = jnp.max(tab, axis=0, keepdims=True)
    te = jnp.exp(tab - tmax)
    lb = jnp.sum(te[0:layer + 1], axis=0, keepdims=True) / jnp.sum(te, axis=0, keepdims=True)

    ri = lax.broadcasted_iota(jnp.int32, (CHUNK, CHUNK), 0)
    ci = lax.broadcasted_iota(jnp.int32, (CHUNK, CHUNK), 1)
    tri = jnp.where(ci <= ri, 1.0, 0.0).astype(BF16)

    n_chunks = rows_per_step // CHUNK

    def slot(s):
        return b_scr.at[s], k_scr.at[s], f_scr.at[s]

    def chunk(c, c_next, s):
        bs, ks, fs = slot(s)
        s_mats = _hgrn_scores(c, q_ref, bs, ks, fs)
        _hgrn_gates(c_next, lb, tri, f_ref, *slot(1 - s))
        _hgrn_outputs(c, s_mats, q_ref, i_ref, st_scr, o_scr, bs, ks)

    def pair_body(i, carry):
        chunk(2 * i, 2 * i + 1, 0)
        chunk(2 * i + 1, jnp.minimum(2 * i + 2, n_chunks - 1), 1)
        return carry

    _hgrn_gates(0, lb, tri, f_ref, *slot(0))
    lax.fori_loop(0, n_chunks // 2, pair_body, 0)

    gate_m = mod_ref[0, 2:3, :]
    hnw = hnw_ref[...]
    cw0 = cw_ref[0:1, :]
    cw1 = cw_ref[1:2, :]
    cw2 = cw_ref[2:3, :]
    for r in range(rows_per_step // epi_rows):
        rs = slice(r * epi_rows, (r + 1) * epi_rows)
        parts = []
        for h in range(HGRN_HEADS):
            hs = slice(h * HEAD_DIM, (h + 1) * HEAD_DIM)
            oh = _rms(o_scr[rs, hs], hnw)
            go = g_ref[rs, hs].astype(F32)
            parts.append((oh * (go * _sigmoid(go))).astype(BF16))
        ya = jnp.dot(jnp.concatenate(parts, axis=1), wa_ref[...], preferred_element_type=F32)
        uc = cc_ref[rs, :].astype(F32) * ch_ref[rs, :].astype(F32)
        carry_scr[8:8 + epi_rows, :] = uc
        s1 = carry_scr[7:7 + epi_rows, :]
        s2 = carry_scr[6:6 + epi_rows, :]
        carry_scr[0:8, :] = uc[epi_rows - 8:epi_rows, :]
        yc = cb_ref[rs, :].astype(F32) * (cw2 * uc + cw1 * s1 + cw0 * s2)
        yb = jnp.dot(yc.astype(BF16), wb_ref[...], preferred_element_type=F32)
        ga = jnp.concatenate([ga0_ref[rs, :], ga1_ref[rs, :]], axis=1).astype(F32)
        gb = jnp.concatenate([gb0_ref[rs, :], gb1_ref[rs, :]], axis=1).astype(F32)
        merged = (_sigmoid(ga) * ya + _sigmoid(gb) * yb).astype(BF16)
        o_ref[rs, :] = x_ref[rs, :] + gate_m * jnp.dot(merged, wm_ref[...], preferred_element_type=F32)


def _mix(proj, x2, mod, lb_table, hgrn_norm_w, conv_w, wa, wb, wm, layer, bsz, seq, tt):
    n = x2.shape[0]
    per_b = seq // tt

    def col(col_block):
        return pl.BlockSpec((tt, 512), lambda b, t: (b * per_b + t, col_block))

    const = lambda shape: pl.BlockSpec(shape, lambda b, t: (0,) * len(shape))
    in_specs = [
        const(lb_table.shape),
        col(0), col(1), col(2), col(3),
        col(4), col(5), col(6),
        col(7), col(8), col(9), col(10),
        pl.BlockSpec((tt, D_MODEL), lambda b, t: (b * per_b + t, 0)),
        pl.BlockSpec((1, 6, D_MODEL), lambda b, t: (b, 0, 0)),
        const((1, HEAD_DIM)), const((CONV_K, CONV_WIDTH)),
        const((HGRN_WIDTH, D_MODEL)), const((CONV_WIDTH, D_MODEL)), const((D_MODEL, D_MODEL)),
    ]
    kern = functools.partial(_mix_kernel, layer=layer, rows_per_step=tt, epi_rows=tt)
    return pl.pallas_call(
        kern,
        out_shape=jax.ShapeDtypeStruct((n, D_MODEL), F32),
        grid=(bsz, per_b),
        in_specs=in_specs,
        out_specs=pl.BlockSpec((tt, D_MODEL), lambda b, t: (b * per_b + t, 0)),
        scratch_shapes=[pltpu.VMEM((HGRN_HEADS, HEAD_DIM, HEAD_DIM), F32),
                        pltpu.VMEM((tt, HGRN_WIDTH), F32),
                        pltpu.VMEM((8 + tt, CONV_WIDTH), F32),
                        pltpu.VMEM((2, CHUNK, HGRN_WIDTH), F32),
                        pltpu.VMEM((2, CHUNK, HGRN_WIDTH), F32),
                        pltpu.VMEM((2, CHUNK, HGRN_WIDTH), F32)],
        compiler_params=pltpu.CompilerParams(dimension_semantics=("arbitrary", "arbitrary"),
                                             vmem_limit_bytes=VMEM_LIMIT),
        name="mix",
    )(lb_table, *([proj] * 11),
      x2, mod, hgrn_norm_w.reshape(1, HEAD_DIM), conv_w, wa, wb, wm)


def _route_kernel(h_ref, mod_ref, nw_ref, wr_ref, br_ref, u_ref, oh4_ref, rk_ref, pw_ref, cnt_ref, *, tr):
    shift = mod_ref[0, 3:4, :]
    scale = mod_ref[0, 4:5, :]
    u = _rms(h_ref[...], nw_ref[...]) * (1.0 + scale) + shift
    u_ref[...] = _pack_rows(u)
    lane = lax.broadcasted_iota(jnp.int32, (tr, LANES), 1).astype(F32)
    logits = jnp.dot(u.astype(BF16), wr_ref[...], preferred_element_type=F32) + br_ref[...]
    logits = jnp.where(lane < N_EXPERTS, logits, -jnp.inf)
    idx, val = [], []
    cur = logits
    for _ in range(TOP_K):
        m = jnp.max(cur, axis=-1, keepdims=True)
        i = jnp.min(jnp.where(cur == m, lane, float(LANES)), axis=-1, keepdims=True)
        idx.append(i)
        val.append(m)
        cur = jnp.where(lane == i, -jnp.inf, cur)
    ex = [jnp.exp(v - val[0]) for v in val]
    den = ex[0] + ex[1] + ex[2] + ex[3]
    onehot = jnp.zeros((tr, LANES), F32)
    for i in idx:
        onehot = onehot + jnp.where(lane == i, 1.0, 0.0)
    ri = lax.broadcasted_iota(jnp.int32, (tr, tr), 0)
    ci = lax.broadcasted_iota(jnp.int32, (tr, tr), 1)
    tri = jnp.where(ci < ri, 1.0, 0.0).astype(BF16)
    pref = jnp.dot(tri, onehot.astype(BF16), preferred_element_type=F32)
    rk = jnp.zeros((tr, LANES), jnp.int32)
    pw = jnp.zeros((tr, LANES), F32)
    oh4 = jnp.zeros((tr, LANES), F32)
    for j in range(TOP_K):
        rank = jnp.sum(jnp.where(lane == idx[j], pref, 0.0), axis=-1, keepdims=True).astype(jnp.int32)
        rk = jnp.where(lane == j, rank, rk)
        pw = jnp.where(lane == j, ex[j] / den, pw)
        oh4 = oh4 + jnp.where(lane == idx[j] + N_EXPERTS * j, 1.0, 0.0)
    oh4_ref[...] = oh4.astype(BF16)
    rk_ref[...] = rk
    pw_ref[...] = pw
    cnt_ref[0] = jnp.sum(onehot, axis=0, keepdims=True).astype(jnp.int32)


def _route(h1, mod, norm_w, wr_pad, br_pad, seq, tr, tile0, n):
    per_b = seq // tr
    nt = n // tr
    return pl.pallas_call(
        functools.partial(_route_kernel, tr=tr),
        out_shape=(jax.ShapeDtypeStruct((n, PACKED), jnp.int32),
                   jax.ShapeDtypeStruct((n, LANES), BF16),
                   jax.ShapeDtypeStruct((n, LANES), jnp.int32),
                   jax.ShapeDtypeStruct((n, LANES), F32),
                   jax.ShapeDtypeStruct((nt, 1, LANES), jnp.int32)),
        grid=(nt,),
        in_specs=[pl.BlockSpec((tr, D_MODEL), lambda i: (tile0 + i, 0)),
                  pl.BlockSpec((1, 6, D_MODEL), lambda i: ((tile0 + i) // per_b, 0, 0)),
                  pl.BlockSpec((1, D_MODEL), lambda i: (0, 0)),
                  pl.BlockSpec((D_MODEL, LANES), lambda i: (0, 0)),
                  pl.BlockSpec((1, LANES), lambda i: (0, 0))],
        out_specs=(pl.BlockSpec((tr, PACKED), lambda i: (i, 0)),
                   pl.BlockSpec((tr, LANES), lambda i: (i, 0)),
                   pl.BlockSpec((tr, LANES), lambda i: (i, 0)),
                   pl.BlockSpec((tr, LANES), lambda i: (i, 0)),
                   pl.BlockSpec((1, 1, LANES), lambda i: (i, 0, 0))),
        compiler_params=pltpu.CompilerParams(dimension_semantics=("arbitrary",),
                                             vmem_limit_bytes=VMEM_LIMIT),
        name="route",
    )(h1, mod, norm_w.reshape(1, D_MODEL), wr_pad, br_pad)


def _dest_kernel(oh4_ref, rk_ref, bt_ref, o_ref):
    oh = oh4_ref[...]
    start = (jnp.dot(oh, bt_ref[0, 0], preferred_element_type=F32)
             + 256.0 * jnp.dot(oh, bt_ref[0, 1], preferred_element_type=F32)
             + 65536.0 * jnp.dot(oh, bt_ref[0, 2], preferred_element_type=F32))
    o_ref[...] = (start + rk_ref[...].astype(F32)).T[:8, :].astype(jnp.int32)


def _dest(oh4, rk, bt, tr):
    n = oh4.shape[0]
    return pl.pallas_call(
        _dest_kernel,
        out_shape=jax.ShapeDtypeStruct((8, n), jnp.int32),
        grid=(n // tr,),
        in_specs=[pl.BlockSpec((tr, LANES), lambda i: (i, 0)),
                  pl.BlockSpec((tr, LANES), lambda i: (i, 0)),
                  pl.BlockSpec((1, 3, LANES, LANES), lambda i: (i, 0, 0, 0))],
        out_specs=pl.BlockSpec((8, tr), lambda i: (0, i)),
        name="dest",
    )(oh4, rk, bt)


def _sc_workers():
    info = plsc.get_sparse_core_info()
    return info.num_cores, info.num_cores * info.num_subcores


def _sc_scatter_rows(rows, idx_slots, pad_idx, n_out):
    n_cores, n_workers = _sc_workers()
    n, w = rows.shape
    k = idx_slots.shape[0] // n
    per_worker = n // n_workers
    pad_per_worker = pad_idx.shape[0] // n_workers
    assert per_worker % SC_CHUNK == 0 and pad_per_worker % SC_CHUNK == 0
    mesh = plsc.VectorSubcoreMesh(core_axis_name="c", subcore_axis_name="s")
    zeros = jnp.zeros((SC_CHUNK, w), rows.dtype)

    n_chunks = per_worker // SC_CHUNK
    assert n_chunks % 2 == 0

    @functools.partial(
        pl.kernel, mesh=mesh,
        out_type=jax.ShapeDtypeStruct((n_out, w), rows.dtype),
        scratch_types=[pltpu.VMEM((SC_CHUNK,), jnp.int32)] * k
        + [pltpu.VMEM((SC_CHUNK, w), rows.dtype)] * 2
        + [pltpu.SemaphoreType.DMA] * 2,
        name="sc_scatter",
    )
    def scatter(rows_hbm, idx_hbm, pad_hbm, zeros_hbm, out_hbm, *scratch):
        idx_bufs = scratch[:k]
        row_bufs = scratch[k:k + 2]
        sem_rows, sem_out = scratch[k + 2:]
        wid = lax.axis_index("s") * n_cores + lax.axis_index("c")

        def chunk_off(c):
            return pl.multiple_of(wid * per_worker + c * SC_CHUNK, 8)

        def load(c, b):
            pltpu.async_copy(rows_hbm.at[pl.ds(chunk_off(c), SC_CHUNK)], row_bufs[b], sem_rows)

        def wait_load(b):
            pltpu.make_async_copy(rows_hbm.at[pl.ds(0, SC_CHUNK)], row_bufs[b], sem_rows).wait()

        def scatter_chunk(c, b):
            for j in range(k):
                pltpu.sync_copy(idx_hbm.at[pl.ds(pl.multiple_of(j * n + chunk_off(c), 8), SC_CHUNK)],
                                idx_bufs[j])
            copies = [pltpu.async_copy(row_bufs[b], out_hbm.at[idx_bufs[j]], sem_out) for j in range(k)]
            for cp in copies:
                cp.wait()

        load(0, 0)

        def body(i, carry):
            wait_load(0)
            load(2 * i + 1, 1)
            scatter_chunk(2 * i, 0)
            wait_load(1)

            @pl.when(i < n_chunks // 2 - 1)
            def _():
                load(2 * i + 2, 0)

            scatter_chunk(2 * i + 1, 1)
            return carry

        lax.fori_loop(0, n_chunks // 2, body, 0)
        pltpu.sync_copy(zeros_hbm, row_bufs[0])

        def pad_body(i, carry):
            off = pl.multiple_of(wid * pad_per_worker + i * SC_CHUNK, 8)
            pltpu.sync_copy(pad_hbm.at[pl.ds(off, SC_CHUNK)], idx_bufs[0])
            pltpu.async_copy(row_bufs[0], out_hbm.at[idx_bufs[0]], sem_out).wait()
            return carry

        lax.fori_loop(0, pad_per_worker // SC_CHUNK, pad_body, 0)

    return scatter(rows, idx_slots, pad_idx, zeros)


def _experts_kernel(be_ref, nu_ref, first_ref, next_ref, slot_ref, quarters_ref, x_ref, w1_hbm, b1_ref, w2_hbm,
                    b2_ref, o_ref, w1_f32, w2_f32, w1_bf, w2_bf, sems):
    i = pl.program_id(0)
    used = i < nu_ref[0]

    def weight_copies(e, s):
        return (pltpu.make_async_copy(w1_hbm.at[e], w1_f32.at[s], sems.at[s]),
                pltpu.make_async_copy(w2_hbm.at[e], w2_f32.at[s], sems.at[s]))

    @pl.when(used & (first_ref[i] == 1))
    def _():
        s = slot_ref[i]

        @pl.when(i == 0)
        def _():
            for cp in weight_copies(be_ref[i], s):
                cp.start()

        for cp in weight_copies(be_ref[i], s):
            cp.wait()
        cw = 256
        for c in range(2 * D_FF // cw):
            w1_bf[:, c * cw:(c + 1) * cw] = w1_f32[s, :, c * cw:(c + 1) * cw].astype(BF16)
        for c in range(D_MODEL // cw):
            w2_bf[:, c * cw:(c + 1) * cw] = w2_f32[s, :, c * cw:(c + 1) * cw].astype(BF16)

        @pl.when(next_ref[i] >= 0)
        def _():
            for cp in weight_copies(next_ref[i], 1 - s):
                cp.start()

    def mlp(n_rows):
        step = 512
        x = _unpack_rows(x_ref[0:n_rows, :]).astype(BF16)
        acc = None
        for j in range(D_FF // step):
            cs = slice(j * step, (j + 1) * step)
            ls = slice(D_FF + j * step, D_FF + (j + 1) * step)
            glu = jnp.dot(x, w1_bf[:, cs], preferred_element_type=F32) + b1_ref[0, :, cs]
            lin = jnp.dot(x, w1_bf[:, ls], preferred_element_type=F32) + b1_ref[0, :, ls]
            glu = jnp.minimum(glu, SWIGLU_LIMIT)
            lin = jnp.clip(lin, -SWIGLU_LIMIT, SWIGLU_LIMIT)
            act = (glu * _sigmoid(glu, SWIGLU_ALPHA) * (lin + 1.0)).astype(BF16)
            part = jnp.dot(act, w2_bf[cs, :], preferred_element_type=F32)
            acc = part if acc is None else acc + part
        o_ref[0:n_rows, :] = _pack_rows(acc + b2_ref[0])
        if n_rows < ROW_BLOCK:
            o_ref[n_rows:ROW_BLOCK, :] = jnp.zeros((ROW_BLOCK - n_rows, PACKED), jnp.int32)

    for quarters in range(1, 5):
        @pl.when(used & (quarters_ref[i] == quarters))
        def _():
            mlp(quarters * (ROW_BLOCK // 4))

    @pl.when(jnp.logical_not(used))
    def _():
        o_ref[...] = jnp.zeros_like(o_ref)


def _experts(plan, xs, w1, b1, w2, b2):
    n_rows = xs.shape[0]
    nb = n_rows // ROW_BLOCK
    grid_spec = pltpu.PrefetchScalarGridSpec(
        num_scalar_prefetch=6,
        grid=(nb,),
        in_specs=[pl.BlockSpec((ROW_BLOCK, PACKED), lambda i, be, nu, *_: (jnp.minimum(i, nu[0] - 1), 0)),
                  pl.BlockSpec(memory_space=pl.ANY),
                  pl.BlockSpec((1, 1, 2 * D_FF), lambda i, be, *_: (be[i], 0, 0)),
                  pl.BlockSpec(memory_space=pl.ANY),
                  pl.BlockSpec((1, 1, D_MODEL), lambda i, be, *_: (be[i], 0, 0))],
        out_specs=pl.BlockSpec((ROW_BLOCK, PACKED), lambda i, *_: (i, 0)),
        scratch_shapes=[pltpu.VMEM((2, D_MODEL, 2 * D_FF), F32), pltpu.VMEM((2, D_FF, D_MODEL), F32),
                        pltpu.VMEM((D_MODEL, 2 * D_FF), BF16), pltpu.VMEM((D_FF, D_MODEL), BF16),
                        pltpu.SemaphoreType.DMA((2,))],
    )
    return pl.pallas_call(
        _experts_kernel,
        out_shape=jax.ShapeDtypeStruct((n_rows, PACKED), jnp.int32),
        grid_spec=grid_spec,
        compiler_params=pltpu.CompilerParams(dimension_semantics=("arbitrary",),
                                             vmem_limit_bytes=VMEM_LIMIT),
        name="experts",
    )(*plan, xs, w1, b1.reshape(N_EXPERTS, 1, 2 * D_FF), w2, b2.reshape(N_EXPERTS, 1, D_MODEL))


def _sc_gather_rows(table, idx_flat):
    n_cores, n_workers = _sc_workers()
    n_idx = idx_flat.shape[0]
    w = table.shape[1]
    per_worker = n_idx // n_workers
    n_chunks = per_worker // SC_CHUNK
    assert per_worker * n_workers == n_idx and n_chunks * SC_CHUNK == per_worker and n_chunks % 2 == 0
    mesh = plsc.VectorSubcoreMesh(core_axis_name="c", subcore_axis_name="s")

    @functools.partial(
        pl.kernel, mesh=mesh,
        out_type=jax.ShapeDtypeStruct((n_idx, w), table.dtype),
        scratch_types=[pltpu.VMEM((SC_CHUNK,), jnp.int32), pltpu.VMEM((SC_CHUNK,), jnp.int32),
                       pltpu.VMEM((SC_CHUNK, w), table.dtype), pltpu.VMEM((SC_CHUNK, w), table.dtype),
                       pltpu.SemaphoreType.DMA, pltpu.SemaphoreType.DMA],
        name="sc_gather",
    )
    def gather(table_hbm, idx_hbm, out_hbm, idx0, idx1, rows0, rows1, sem0, sem1):
        wid = lax.axis_index("s") * n_cores + lax.axis_index("c")
        bufs = ((idx0, rows0, sem0), (idx1, rows1, sem1))

        def chunk_off(c):
            return pl.multiple_of(wid * per_worker + c * SC_CHUNK, 8)

        def start(c, b):
            idx_v, rows_v, sem = bufs[b]
            pltpu.sync_copy(idx_hbm.at[pl.ds(chunk_off(c), SC_CHUNK)], idx_v)
            pltpu.async_copy(table_hbm.at[idx_v], rows_v, sem)

        def finish(c, b):
            idx_v, rows_v, sem = bufs[b]
            pltpu.make_async_copy(table_hbm.at[idx_v], rows_v, sem).wait()
            pltpu.sync_copy(rows_v, out_hbm.at[pl.ds(chunk_off(c), SC_CHUNK)])

        start(0, 0)

        def body(i, carry):
            start(2 * i + 1, 1)
            finish(2 * i, 0)

            @pl.when(i < n_chunks // 2 - 1)
            def _():
                start(2 * i + 2, 0)

            finish(2 * i + 1, 1)
            return carry

        lax.fori_loop(0, n_chunks // 2, body, 0)

    return gather(table, idx_flat)


def _finish_kernel(h_ref, pw_ref, mod_ref, nw_ref, y0_ref, y1_ref, y2_ref, y3_ref, *rest):
    o_ref = rest[-1]
    pw = pw_ref[...]
    moe = pw[:, 0:1] * _unpack_rows(y0_ref[0])
    for j, y_ref in enumerate((y1_ref, y2_ref, y3_ref), start=1):
        moe = moe + pw[:, j:j + 1] * _unpack_rows(y_ref[0])
    gate_f = mod_ref[0, 5:6, :]
    o_ref[...] = _rms(h_ref[...] + gate_f * moe, nw_ref[...])


def _finish(h1, pw, mod, norm_w, y4, seq, tc, tile0, prev_out):
    n = pw.shape[0]
    per_b = seq // tc
    slot = lambda j: pl.BlockSpec((1, tc, PACKED), lambda i: (j, i, 0))
    in_specs = [pl.BlockSpec((tc, D_MODEL), lambda i: (tile0 + i, 0)),
                pl.BlockSpec((tc, LANES), lambda i: (i, 0)),
                pl.BlockSpec((1, 6, D_MODEL), lambda i: ((tile0 + i) // per_b, 0, 0)),
                pl.BlockSpec((1, D_MODEL), lambda i: (0, 0)),
                slot(0), slot(1), slot(2), slot(3)]
    args = [h1, pw, mod, norm_w.reshape(1, D_MODEL), y4, y4, y4, y4]
    aliases = {}
    if prev_out is not None:
        in_specs.append(pl.BlockSpec(memory_space=pl.ANY))
        args.append(prev_out)
        aliases = {len(args) - 1: 0}
    return pl.pallas_call(
        _finish_kernel,
        out_shape=jax.ShapeDtypeStruct(h1.shape, F32),
        grid=(n // tc,),
        in_specs=in_specs,
        out_specs=pl.BlockSpec((tc, D_MODEL), lambda i: (tile0 + i, 0)),
        input_output_aliases=aliases,
        compiler_params=pltpu.CompilerParams(dimension_semantics=("arbitrary",),
                                             vmem_limit_bytes=VMEM_LIMIT),
        name="finish",
    )(*args)


def _moe_plan(counts, n_assign):
    cnt = counts[:, 0, :N_EXPERTS]
    sizes = jnp.sum(cnt, axis=0)
    padded = (sizes + ROW_BLOCK - 1) // ROW_BLOCK * ROW_BLOCK
    pad_end = jnp.cumsum(padded)
    pad_start = pad_end - padded
    tile_base = pad_start[None, :] + jnp.cumsum(cnt, axis=0) - cnt
    digits = jnp.stack([tile_base % 256, (tile_base // 256) % 256, tile_base // 65536], axis=1)
    rows = jnp.tile(digits, (1, 1, TOP_K))
    slot_of_row = jnp.arange(LANES, dtype=jnp.int32) // N_EXPERTS
    col = jnp.arange(LANES, dtype=jnp.int32)
    bt = jnp.where(slot_of_row[:, None] == col[None, :], rows[..., None], 0).astype(BF16)
    nb = n_assign // ROW_BLOCK + N_EXPERTS
    block_start = jnp.arange(nb, dtype=jnp.int32) * ROW_BLOCK
    block_e = jnp.minimum(jnp.sum(pad_end[None, :] <= block_start[:, None], axis=1),
                          N_EXPERTS - 1).astype(jnp.int32)
    n_used = (pad_end[-1] // ROW_BLOCK).astype(jnp.int32).reshape(1)
    nonempty = padded > 0
    ids = jnp.arange(N_EXPERTS, dtype=jnp.int32)
    suffix_min = lax.cummin(jnp.where(nonempty, ids, N_EXPERTS), reverse=True)
    after = jnp.concatenate([suffix_min[1:], jnp.full((1,), N_EXPERTS, jnp.int32)])
    next_nonempty = jnp.where(after < N_EXPERTS, after, -1).astype(jnp.int32)
    slot_of = ((jnp.cumsum(nonempty) - 1) % 2).astype(jnp.int32)
    first = jnp.concatenate([jnp.ones((1,), jnp.int32),
                             (block_e[1:] != block_e[:-1]).astype(jnp.int32)])
    valid_rows = (pad_start + sizes)[block_e] - block_start
    quarter = ROW_BLOCK // 4
    quarters = jnp.clip((valid_rows + quarter - 1) // quarter, 1, 4).astype(jnp.int32)
    expert_plan = (block_e, n_used, first, next_nonempty[block_e], slot_of[block_e], quarters)
    r = jnp.arange(ROW_BLOCK, dtype=jnp.int32)[None, :]
    pad_idx = jnp.where(r < (padded - sizes)[:, None], (pad_start + sizes)[:, None] + r,
                        (nb - 1) * ROW_BLOCK + r).astype(jnp.int32).reshape(PAD_SLOTS)
    return bt, expert_plan, pad_idx


def kernel(x, c, w_ada, b_ada, norm_mix_w, w_in, hgrn_lower_bounds, hgrn_norm_w, conv_w,
           w_hgrn_out, w_conv_out, w_mix_out, norm_ffn_w, w_router, b_router, w1, b1, w2, b2,
           norm_final_w):
    bsz, seq, d = x.shape
    assert d == D_MODEL and seq % CHUNK == 0
    n = bsz * seq
    depth = w_ada.shape[0]
    tile = min(512, seq)
    assert seq % tile == 0 and n % (MOE_GROUPS * tile) == 0 and (n // MOE_GROUPS * TOP_K) % ROW_BLOCK == 0
    h = x.reshape(n, d)
    wr_pad = jnp.zeros((depth, D_MODEL, LANES), BF16).at[:, :, :N_EXPERTS].set(w_router.astype(BF16))
    br_pad = jnp.zeros((depth, 1, LANES), F32).at[:, 0, :N_EXPERTS].set(b_router)
    for layer in range(depth):
        mod = _ada(c, w_ada[layer], b_ada[layer]).reshape(bsz, 6, d)
        proj = _inproj(h, mod, norm_mix_w[layer], w_in[layer].astype(BF16), seq, min(DENSE_ROWS, seq))
        h = _mix(proj, h, mod, hgrn_lower_bounds, hgrn_norm_w[layer], conv_w[layer],
                 w_hgrn_out[layer].astype(BF16), w_conv_out[layer].astype(BF16),
                 w_mix_out[layer].astype(BF16), layer, bsz, seq, min(DENSE_ROWS, seq))
        assert layer == depth - 1, "only the last layer applies the final norm"
        ng = n // MOE_GROUPS
        n_rows = (ng * TOP_K // ROW_BLOCK + N_EXPERTS) * ROW_BLOCK
        routed = [_route(h, mod, norm_ffn_w[layer], wr_pad[layer], br_pad[layer], seq, tile,
                         grp * (ng // tile), ng) for grp in range(MOE_GROUPS)]
        plans = jax.vmap(lambda cnt: _moe_plan(cnt, ng * TOP_K))(jnp.stack([r[4] for r in routed]))
        out = None
        for grp in range(MOE_GROUPS):
            tile0 = grp * (ng // tile)
            u2, oh4, rk, pw, _ = routed[grp]
            bt, expert_plan, pad_idx = jax.tree.map(lambda a: a[grp], plans)
            dest_slots = _dest(oh4, rk, bt, tile)[:TOP_K].reshape(TOP_K * ng)
            xs = _sc_scatter_rows(u2, dest_slots, pad_idx, n_rows)
            ys = _experts(expert_plan, xs, w1[layer], b1[layer], w2[layer], b2[layer])
            y4 = _sc_gather_rows(ys, dest_slots).reshape(TOP_K, ng, PACKED)
            out = _finish(h, pw, mod, norm_final_w, y4, seq, tile, tile0, out)
        h = out
    return h.reshape(bsz, seq, d)
```

```python
import functools

import jax
import jax.numpy as jnp
from jax import lax
from jax.experimental import pallas as pl
from jax.experimental.pallas import tpu as pltpu
from jax.experimental.pallas import tpu_sc as plsc

F32 = jnp.float32
BF16 = jnp.bfloat16

D_MODEL = 1024
HGRN_HEADS = 4
HEAD_DIM = 128
HGRN_WIDTH = HGRN_HEADS * HEAD_DIM
CONV_WIDTH = 512
CONV_K = 3
CHUNK = 64
N_EXPERTS = 32
TOP_K = 4
D_FF = 1024
SWIGLU_LIMIT = 7.0
SWIGLU_ALPHA = 1.702
EPS = 1e-6
LOG2_E = 1.4426950408889634
IN_COLS = 4 * HGRN_WIDTH + 3 * CONV_WIDTH + 2 * D_MODEL
LANES = 128
MIX_ROWS = 1024
ROW_BLOCK = 1024
PACKED = D_MODEL // 2
MOE_GROUPS = 2
SC_CHUNK = 64
PAD_SLOTS = N_EXPERTS * ROW_BLOCK
VMEM_LIMIT = 56 * 1024 * 1024


def _sigmoid(x, scale=1.0):
    return 1.0 / (1.0 + jnp.exp2(x * (-scale * LOG2_E)))


def _rms(x, w):
    ms = jnp.mean(x * x, axis=-1, keepdims=True)
    return x * lax.rsqrt(ms + EPS) * w


def _pack_rows(x):
    w = x.shape[1] // 2
    lo = lax.bitcast_convert_type(x[:, :w].astype(BF16).astype(F32), jnp.uint32)
    hi = lax.bitcast_convert_type(x[:, w:].astype(BF16).astype(F32), jnp.uint32)
    return lax.bitcast_convert_type((lo >> 16) | (hi & jnp.uint32(0xFFFF0000)), jnp.int32)


def _unpack_rows(p):
    u = lax.bitcast_convert_type(p, jnp.uint32)
    lo = lax.bitcast_convert_type(u << 16, F32)
    hi = lax.bitcast_convert_type(u & jnp.uint32(0xFFFF0000), F32)
    return jnp.concatenate([lo, hi], axis=1)


def _nt_dot(a, b):
    return lax.dot_general(a, b, (((1,), (1,)), ((), ())), preferred_element_type=F32)


def _tn_dot(a, b):
    return lax.dot_general(a, b, (((0,), (0,)), ((), ())), preferred_element_type=F32)


def _ada_kernel(c_ref, w_ref, b_ref, o_ref):
    c = c_ref[...]
    sc = (c * _sigmoid(c)).astype(BF16)
    o_ref[...] = jnp.dot(sc, w_ref[...].astype(BF16), preferred_element_type=F32) + b_ref[...]


def _ada(c, w_ada, b_ada):
    bsz, d = c.shape
    n = w_ada.shape[1]
    return pl.pallas_call(
        _ada_kernel,
        out_shape=jax.ShapeDtypeStruct((bsz, n), F32),
        grid=(n // d,),
        in_specs=[pl.BlockSpec((bsz, d), lambda j: (0, 0)),
                  pl.BlockSpec((d, d), lambda j: (0, j)),
                  pl.BlockSpec((1, d), lambda j: (0, j))],
        out_specs=pl.BlockSpec((bsz, d), lambda j: (0, j)),
        name="ada",
    )(c, w_ada, b_ada.reshape(1, n))


def _inproj_kernel(x_ref, mod_ref, nw_ref, w_ref, o_ref, u_scr):
    y = _rms(x_ref[...], nw_ref[...])
    shift = mod_ref[0, 0:1, :]
    scale = mod_ref[0, 1:2, :]
    u_scr[...] = (y * (1.0 + scale) + shift).astype(BF16)

    step = 512
    for j in range(IN_COLS // step):
        o_ref[:, j * step:(j + 1) * step] = jnp.dot(
            u_scr[...], w_ref[:, j * step:(j + 1) * step], preferred_element_type=F32).astype(BF16)


def _inproj(x2, mod, norm_w, w_in_bf, seq, tm):
    n = x2.shape[0]
    per_b = seq // tm
    return pl.pallas_call(
        _inproj_kernel,
        out_shape=jax.ShapeDtypeStruct((n, IN_COLS), BF16),
        grid=(n // tm,),
        in_specs=[pl.BlockSpec((tm, D_MODEL), lambda i: (i, 0)),
                  pl.BlockSpec((1, 6, D_MODEL), lambda i: (i // per_b, 0, 0)),
                  pl.BlockSpec((1, D_MODEL), lambda i: (0, 0)),
                  pl.BlockSpec((D_MODEL, IN_COLS), lambda i: (0, 0))],
        out_specs=pl.BlockSpec((tm, IN_COLS), lambda i: (i, 0)),
        scratch_shapes=[pltpu.VMEM((tm, D_MODEL), BF16)],
        compiler_params=pltpu.CompilerParams(dimension_semantics=("arbitrary",),
                                             vmem_limit_bytes=VMEM_LIMIT),
        name="inproj",
    )(x2, mod, norm_w.reshape(1, D_MODEL), w_in_bf)


_HEADS = [slice(h * HEAD_DIM, (h + 1) * HEAD_DIM) for h in range(HGRN_HEADS)]


def _chunk_rows(c):
    if isinstance(c, int):
        return pl.ds(c * CHUNK, CHUNK)
    return pl.ds(pl.multiple_of(c * CHUNK, CHUNK), CHUNK)


def _hgrn_gates(c, lb, tri, f_ref, b_scr, k_scr, f_scr):
    fx = f_ref[_chunk_rows(c), :].astype(F32)
    f = lb + (1.0 - lb) * _sigmoid(fx)
    g = jnp.maximum(jnp.log(f), -128.0) * LOG2_E
    g1 = g.astype(BF16)
    r1 = g - g1.astype(F32)
    g2 = r1.astype(BF16)
    g3 = (r1 - g2.astype(F32)).astype(BF16)
    b_scr[...] = (jnp.dot(tri, g1, preferred_element_type=F32)
                  + jnp.dot(tri, g2, preferred_element_type=F32)
                  + jnp.dot(tri, g3, preferred_element_type=F32))
    f_scr[...] = f
    k_scr[...] = 1.0 - f


def _hgrn_scores(c, q_ref, b_scr, k_scr, f_scr):
    rows = _chunk_rows(c)
    qs = [q_ref[rows, hs].astype(F32) for hs in _HEADS]
    s_mats = [_level_scores(1, qs[h], _HEADS[h], b_scr, k_scr, f_scr) for h in range(HGRN_HEADS)]
    for lvl in range(2, 7):
        for h in range(HGRN_HEADS):
            s_mats[h] = s_mats[h] + _level_scores(lvl, qs[h], _HEADS[h], b_scr, k_scr, f_scr)
    return s_mats


def _hgrn_outputs(c, s_mats, q_ref, v_ref, st_scr, o_scr, b_scr, k_scr):
    rows = _chunk_rows(c)
    for h, hs in enumerate(_HEADS):
        q = q_ref[rows, hs].astype(F32)
        v_bf = v_ref[rows, hs]
        b = b_scr[:, hs]
        kk = k_scr[:, hs]
        b_last = b_scr[CHUNK - 1:CHUNK, hs]
        st = st_scr[h]
        qd = (q * jnp.exp2(b)).astype(BF16)
        kdec = (kk * jnp.exp2(b_last - b)).astype(BF16)
        diag = jnp.sum(q * kk, axis=-1, keepdims=True)
        o_scr[rows, hs] = (_nt_dot(qd, st.astype(BF16))
                           + jnp.dot(s_mats[h].astype(BF16), v_bf, preferred_element_type=F32)
                           + diag * v_bf.astype(F32))
        st_scr[h] = jnp.exp2(b_last) * st + _tn_dot(v_bf, kdec)


def _level_scores(lvl, q, hs, b_scr, k_scr, f_scr):
    row = lax.broadcasted_iota(jnp.int32, (CHUNK, 1), 0)
    col = lax.broadcasted_iota(jnp.int32, (1, CHUNK), 1)
    b = b_scr[:, hs]
    kk = k_scr[:, hs]
    blk = 1 << lvl
    half = blk // 2
    if lvl == 1:
        odd = (row & 1) == 1
        qx = jnp.where(odd, q * f_scr[:, hs], 0.0).astype(BF16)
        kx = jnp.where(odd, 0.0, kk).astype(BF16)
        return jnp.where((row >> 1) == (col >> 1), _nt_dot(qx, kx), 0.0)
    if half < 8:
        groups = []
        sub = lax.broadcasted_iota(jnp.int32, (8, 1), 0)
        for j in range(CHUNK // 8):
            rj = None
            for k in reversed(range(8 // blk)):
                m = 8 * j + k * blk + half - 1
                bm = jnp.broadcast_to(b_scr[m:m + 1, hs], (8, HEAD_DIM))
                rj = bm if rj is None else jnp.where(sub < (k + 1) * blk, bm, rj)
            groups.append(rj)
        ref = jnp.concatenate(groups, axis=0)
        second = (row & (blk - 1)) >= half
        qx = (q * jnp.exp2(jnp.where(second, b - ref, -jnp.inf))).astype(BF16)
        kx = (kk * jnp.exp2(jnp.where(second, -jnp.inf, ref - b))).astype(BF16)
        return jnp.where((row >> lvl) == (col >> lvl), _nt_dot(qx, kx), 0.0)
    n_blk = CHUNK // blk
    qparts, kparts = [], []
    for j in range(n_blk):
        m = j * blk + half - 1
        bm = b_scr[m:m + 1, hs]
        tq = slice(j * blk + half, (j + 1) * blk)
        tk = slice(j * blk, j * blk + half)
        qparts.append(q[tq] * jnp.exp2(b[tq] - bm))
        kparts.append(kk[tk] * jnp.exp2(bm - b[tk]))
        kparts.append(jnp.zeros((half, HEAD_DIM), F32))
    qx = jnp.concatenate(qparts, axis=0).astype(BF16)
    kx = jnp.concatenate(kparts, axis=0).astype(BF16)
    sc = _nt_dot(qx, kx)
    if n_blk > 1:
        crow = lax.broadcasted_iota(jnp.int32, (CHUNK // 2, 1), 0)
        sc = jnp.where((crow // half) == (col >> lvl), sc, 0.0)
    pieces = []
    for j in range(n_blk):
        pieces.append(jnp.zeros((half, CHUNK), F32))
        pieces.append(sc[j * half:(j + 1) * half])
    return jnp.concatenate(pieces, axis=0)


def _mix_kernel(lbt_ref, q_ref, f_ref, i_ref, g_ref, cb_ref, cc_ref, ch_ref, ga0_ref, ga1_ref,
                gb0_ref, gb1_ref, x_ref, mod_ref, hnw_ref, cw_ref, wa_ref, wb_ref, wm_ref,
                o_ref, st_scr, o_scr, carry_scr, b_scr, k_scr, f_scr, *, layer, rows_per_step, epi_rows):
    @pl.when(pl.program_id(1) == 0)
    def _():
        st_scr[...] = jnp.zeros_like(st_scr)
        carry_scr[0:8, :] = jnp.zeros((8, CONV_WIDTH), F32)

    tab = lbt_ref[...]
    tmax = jnp.max(tab, axis=0, keepdims=True)
    te = jnp.exp(tab - tmax)
    lb = jnp.sum(te[0:layer + 1], axis=0, keepdims=True) / jnp.sum(te, axis=0, keepdims=True)

    ri = lax.broadcasted_iota(jnp.int32, (CHUNK, CHUNK), 0)
    ci = lax.broadcasted_iota(jnp.int32, (CHUNK, CHUNK), 1)
    tri = jnp.where(ci <= ri, 1.0, 0.0).astype(BF16)

    n_chunks = rows_per_step // CHUNK

    def slot(s):
        return b_scr.at[s], k_scr.at[s], f_scr.at[s]

    def chunk(c, c_next, s):
        bs, ks, fs = slot(s)
        s_mats = _hgrn_scores(c, q_ref, bs, ks, fs)
        _hgrn_gates(c_next, lb, tri, f_ref, *slot(1 - s))
        _hgrn_outputs(c, s_mats, q_ref, i_ref, st_scr, o_scr, bs, ks)

    def pair_body(i, carry):
        chunk(2 * i, 2 * i + 1, 0)
        chunk(2 * i + 1, jnp.minimum(2 * i + 2, n_chunks - 1), 1)
        return carry

    _hgrn_gates(0, lb, tri, f_ref, *slot(0))
    lax.fori_loop(0, n_chunks // 2, pair_body, 0)

    gate_m = mod_ref[0, 2:3, :]
    hnw = hnw_ref[...]
    cw0 = cw_ref[0:1, :]
    cw1 = cw_ref[1:2, :]
    cw2 = cw_ref[2:3, :]
    for r in range(rows_per_step // epi_rows):
        rs = slice(r * epi_rows, (r + 1) * epi_rows)
        parts = []
        for h in range(HGRN_HEADS):
            hs = slice(h * HEAD_DIM, (h + 1) * HEAD_DIM)
            oh = _rms(o_scr[rs, hs], hnw)
            go = g_ref[rs, hs].astype(F32)
            parts.append((oh * (go * _sigmoid(go))).astype(BF16))
        ya = jnp.dot(jnp.concatenate(parts, axis=1), wa_ref[...], preferred_element_type=F32)
        uc = cc_ref[rs, :].astype(F32) * ch_ref[rs, :].astype(F32)
        carry_scr[8:8 + epi_rows, :] = uc
        s1 = carry_scr[7:7 + epi_rows, :]
        s2 = carry_scr[6:6 + epi_rows, :]
        carry_scr[0:8, :] = uc[epi_rows - 8:epi_rows, :]
        yc = cb_ref[rs, :].astype(F32) * (cw2 * uc + cw1 * s1 + cw0 * s2)
        yb = jnp.dot(yc.astype(BF16), wb_ref[...], preferred_element_type=F32)
        ga = jnp.concatenate([ga0_ref[rs, :], ga1_ref[rs, :]], axis=1).astype(F32)
        gb = jnp.concatenate([gb0_ref[rs, :], gb1_ref[rs, :]], axis=1).astype(F32)
        merged = (_sigmoid(ga) * ya + _sigmoid(gb) * yb).astype(BF16)
        o_ref[rs, :] = x_ref[rs, :] + gate_m * jnp.dot(merged, wm_ref[...], preferred_element_type=F32)


def _mix(proj, x2, mod, lb_table, hgrn_norm_w, conv_w, wa, wb, wm, layer, bsz, seq, tt):
    n = x2.shape[0]
    per_b = seq // tt

    def col(col_block):
        return pl.BlockSpec((tt, 512), lambda b, t: (b * per_b + t, col_block))

    const = lambda shape: pl.BlockSpec(shape, lambda b, t: (0,) * len(shape))
    in_specs = [
        const(lb_table.shape),
        col(0), col(1), col(2), col(3),
        col(4), col(5), col(6),
        col(7), col(8), col(9), col(10),
        pl.BlockSpec((tt, D_MODEL), lambda b, t: (b * per_b + t, 0)),
        pl.BlockSpec((1, 6, D_MODEL), lambda b, t: (b, 0, 0)),
        const((1, HEAD_DIM)), const((CONV_K, CONV_WIDTH)),
        const((HGRN_WIDTH, D_MODEL)), const((CONV_WIDTH, D_MODEL)), const((D_MODEL, D_MODEL)),
    ]
    kern = functools.partial(_mix_kernel, layer=layer, rows_per_step=tt, epi_rows=tt)
    return pl.pallas_call(
        kern,
        out_shape=jax.ShapeDtypeStruct((n, D_MODEL), F32),
        grid=(bsz, per_b),
        in_specs=in_specs,
        out_specs=pl.BlockSpec((tt, D_MODEL), lambda b, t: (b * per_b + t, 0)),
        scratch_shapes=[pltpu.VMEM((HGRN_HEADS, HEAD_DIM, HEAD_DIM), F32),
                        pltpu.VMEM((tt, HGRN_WIDTH), F32),
                        pltpu.VMEM((8 + tt, CONV_WIDTH), F32),
                        pltpu.VMEM((2, CHUNK, HGRN_WIDTH), F32),
                        pltpu.VMEM((2, CHUNK, HGRN_WIDTH), F32),
                        pltpu.VMEM((2, CHUNK, HGRN_WIDTH), F32)],
        compiler_params=pltpu.CompilerParams(dimension_semantics=("arbitrary", "arbitrary"),
                                             vmem_limit_bytes=VMEM_LIMIT),
        name="mix",
    )(lb_table, *([proj] * 11),
      x2, mod, hgrn_norm_w.reshape(1, HEAD_DIM), conv_w, wa, wb, wm)


def _route_kernel(h_ref, mod_ref, nw_ref, wr_ref, br_ref, u_ref, oh4_ref, rk_ref, pw_ref, cnt_ref, *, tr):
    shift = mod_ref[0, 3:4, :]
    scale = mod_ref[0, 4:5, :]
    u = _rms(h_ref[...], nw_ref[...]) * (1.0 + scale) + shift
    u_ref[...] = _pack_rows(u)
    lane = lax.broadcasted_iota(jnp.int32, (tr, LANES), 1).astype(F32)
    logits = jnp.dot(u.astype(BF16), wr_ref[...], preferred_element_type=F32) + br_ref[...]
    logits = jnp.where(lane < N_EXPERTS, logits, -jnp.inf)
    idx, val = [], []
    cur = logits
    for _ in range(TOP_K):
        m = jnp.max(cur, axis=-1, keepdims=True)
        i = jnp.min(jnp.where(cur == m, lane, float(LANES)), axis=-1, keepdims=True)
        idx.append(i)
        val.append(m)
        cur = jnp.where(lane == i, -jnp.inf, cur)
    ex = [jnp.exp(v - val[0]) for v in val]
    den = ex[0] + ex[1] + ex[2] + ex[3]
    onehot = jnp.zeros((tr, LANES), F32)
    for i in idx:
        onehot = onehot + jnp.where(lane == i, 1.0, 0.0)
    ri = lax.broadcasted_iota(jnp.int32, (tr, tr), 0)
    ci = lax.broadcasted_iota(jnp.int32, (tr, tr), 1)
    tri = jnp.where(ci < ri, 1.0, 0.0).astype(BF16)
    pref = jnp.dot(tri, onehot.astype(BF16), preferred_element_type=F32)
    rk = jnp.zeros((tr, LANES), jnp.int32)
    pw = jnp.zeros((tr, LANES), F32)
    oh4 = jnp.zeros((tr, LANES), F32)
    for j in range(TOP_K):
        rank = jnp.sum(jnp.where(lane == idx[j], pref, 0.0), axis=-1, keepdims=True).astype(jnp.int32)
        rk = jnp.where(lane == j, rank, rk)
        pw = jnp.where(lane == j, ex[j] / den, pw)
        oh4 = oh4 + jnp.where(lane == idx[j] + N_EXPERTS * j, 1.0, 0.0)
    oh4_ref[...] = oh4.astype(BF16)
    rk_ref[...] = rk
    pw_ref[...] = pw
    cnt_ref[0] = jnp.sum(onehot, axis=0, keepdims=True).astype(jnp.int32)


def _route(h1, mod, norm_w, wr_pad, br_pad, seq, tr, tile0, n):
    per_b = seq // tr
    nt = n // tr
    return pl.pallas_call(
        functools.partial(_route_kernel, tr=tr),
        out_shape=(jax.ShapeDtypeStruct((n, PACKED), jnp.int32),
                   jax.ShapeDtypeStruct((n, LANES), BF16),
                   jax.ShapeDtypeStruct((n, LANES), jnp.int32),
                   jax.ShapeDtypeStruct((n, LANES), F32),
                   jax.ShapeDtypeStruct((nt, 1, LANES), jnp.int32)),
        grid=(nt,),
        in_specs=[pl.BlockSpec((tr, D_MODEL), lambda i: (tile0 + i, 0)),
                  pl.BlockSpec((1, 6, D_MODEL), lambda i: ((tile0 + i) // per_b, 0, 0)),
                  pl.BlockSpec((1, D_MODEL), lambda i: (0, 0)),
                  pl.BlockSpec((D_MODEL, LANES), lambda i: (0, 0)),
                  pl.BlockSpec((1, LANES), lambda i: (0, 0))],
        out_specs=(pl.BlockSpec((tr, PACKED), lambda i: (i, 0)),
                   pl.BlockSpec((tr, LANES), lambda i: (i, 0)),
                   pl.BlockSpec((tr, LANES), lambda i: (i, 0)),
                   pl.BlockSpec((tr, LANES), lambda i: (i, 0)),
                   pl.BlockSpec((1, 1, LANES), lambda i: (i, 0, 0))),
        compiler_params=pltpu.CompilerParams(dimension_semantics=("arbitrary",),
                                             vmem_limit_bytes=VMEM_LIMIT),
        name="route",
    )(h1, mod, norm_w.reshape(1, D_MODEL), wr_pad, br_pad)


def _dest_kernel(oh4_ref, rk_ref, bt_ref, o_ref):
    oh = oh4_ref[...]
    start = (jnp.dot(oh, bt_ref[0, 0], preferred_element_type=F32)
             + 256.0 * jnp.dot(oh, bt_ref[0, 1], preferred_element_type=F32)
             + 65536.0 * jnp.dot(oh, bt_ref[0, 2], preferred_element_type=F32))
    o_ref[...] = (start + rk_ref[...].astype(F32)).T[:8, :].astype(jnp.int32)


def _dest(oh4, rk, bt, tr):
    n = oh4.shape[0]
    return pl.pallas_call(
        _dest_kernel,
        out_shape=jax.ShapeDtypeStruct((8, n), jnp.int32),
        grid=(n // tr,),
        in_specs=[pl.BlockSpec((tr, LANES), lambda i: (i, 0)),
                  pl.BlockSpec((tr, LANES), lambda i: (i, 0)),
                  pl.BlockSpec((1, 3, LANES, LANES), lambda i: (i, 0, 0, 0))],
        out_specs=pl.BlockSpec((8, tr), lambda i: (0, i)),
        name="dest",
    )(oh4, rk, bt)


def _sc_workers():
    info = plsc.get_sparse_core_info()
    return info.num_cores, info.num_cores * info.num_subcores


def _sc_scatter_rows(rows, idx_slots, pad_idx, n_out):
    n_cores, n_workers = _sc_workers()
    n, w = rows.shape
    k = idx_slots.shape[0] // n
    per_worker = n // n_workers
    pad_per_worker = pad_idx.shape[0] // n_workers
    assert per_worker % SC_CHUNK == 0 and pad_per_worker % SC_CHUNK == 0
    mesh = plsc.VectorSubcoreMesh(core_axis_name="c", subcore_axis_name="s")
    zeros = jnp.zeros((SC_CHUNK, w), rows.dtype)

    n_chunks = per_worker // SC_CHUNK
    assert n_chunks % 2 == 0

    @functools.partial(
        pl.kernel, mesh=mesh,
        out_type=jax.ShapeDtypeStruct((n_out, w), rows.dtype),
        scratch_types=[pltpu.VMEM((SC_CHUNK,), jnp.int32)] * k
        + [pltpu.VMEM((SC_CHUNK, w), rows.dtype)] * 2
        + [pltpu.SemaphoreType.DMA] * 2,
        name="sc_scatter",
    )
    def scatter(rows_hbm, idx_hbm, pad_hbm, zeros_hbm, out_hbm, *scratch):
        idx_bufs = scratch[:k]
        row_bufs = scratch[k:k + 2]
        sem_rows, sem_out = scratch[k + 2:]
        wid = lax.axis_index("s") * n_cores + lax.axis_index("c")

        def chunk_off(c):
            return pl.multiple_of(wid * per_worker + c * SC_CHUNK, 8)

        def load(c, b):
            pltpu.async_copy(rows_hbm.at[pl.ds(chunk_off(c), SC_CHUNK)], row_bufs[b], sem_rows)

        def wait_load(b):
            pltpu.make_async_copy(rows_hbm.at[pl.ds(0, SC_CHUNK)], row_bufs[b], sem_rows).wait()

        def scatter_chunk(c, b):
            for j in range(k):
                pltpu.sync_copy(idx_hbm.at[pl.ds(pl.multiple_of(j * n + chunk_off(c), 8), SC_CHUNK)],
                                idx_bufs[j])
            copies = [pltpu.async_copy(row_bufs[b], out_hbm.at[idx_bufs[j]], sem_out) for j in range(k)]
            for cp in copies:
                cp.wait()

        load(0, 0)

        def body(i, carry):
            wait_load(0)
            load(2 * i + 1, 1)
            scatter_chunk(2 * i, 0)
            wait_load(1)

            @pl.when(i < n_chunks // 2 - 1)
            def _():
                load(2 * i + 2, 0)

            scatter_chunk(2 * i + 1, 1)
            return carry

        lax.fori_loop(0, n_chunks // 2, body, 0)
        pltpu.sync_copy(zeros_hbm, row_bufs[0])

        def pad_body(i, carry):
            off = pl.multiple_of(wid * pad_per_worker + i * SC_CHUNK, 8)
            pltpu.sync_copy(pad_hbm.at[pl.ds(off, SC_CHUNK)], idx_bufs[0])
            pltpu.async_copy(row_bufs[0], out_hbm.at[idx_bufs[0]], sem_out).wait()
            return carry

        lax.fori_loop(0, pad_per_worker // SC_CHUNK, pad_body, 0)

    return scatter(rows, idx_slots, pad_idx, zeros)


def _experts_kernel(be_ref, nu_ref, first_ref, next_ref, slot_ref, quarters_ref, x_ref, w1_hbm, b1_ref, w2_hbm,
                    b2_ref, o_ref, w1_f32, w2_f32, w1_bf, w2_bf, sems):
    i = pl.program_id(0)
    used = i < nu_ref[0]

    def weight_copies(e, s):
        return (pltpu.make_async_copy(w1_hbm.at[e], w1_f32.at[s], sems.at[s]),
                pltpu.make_async_copy(w2_hbm.at[e], w2_f32.at[s], sems.at[s]))

    @pl.when(used & (first_ref[i] == 1))
    def _():
        s = slot_ref[i]

        @pl.when(i == 0)
        def _():
            for cp in weight_copies(be_ref[i], s):
                cp.start()

        for cp in weight_copies(be_ref[i], s):
            cp.wait()
        cw = 256
        for c in range(2 * D_FF // cw):
            w1_bf[:, c * cw:(c + 1) * cw] = w1_f32[s, :, c * cw:(c + 1) * cw].astype(BF16)
        for c in range(D_MODEL // cw):
            w2_bf[:, c * cw:(c + 1) * cw] = w2_f32[s, :, c * cw:(c + 1) * cw].astype(BF16)

        @pl.when(next_ref[i] >= 0)
        def _():
            for cp in weight_copies(next_ref[i], 1 - s):
                cp.start()

    def mlp(n_rows):
        step = 512
        x = _unpack_rows(x_ref[0:n_rows, :]).astype(BF16)
        acc = None
        for j in range(D_FF // step):
            cs = slice(j * step, (j + 1) * step)
            ls = slice(D_FF + j * step, D_FF + (j + 1) * step)
            glu = jnp.dot(x, w1_bf[:, cs], preferred_element_type=F32) + b1_ref[0, :, cs]
            lin = jnp.dot(x, w1_bf[:, ls], preferred_element_type=F32) + b1_ref[0, :, ls]
            glu = jnp.minimum(glu, SWIGLU_LIMIT)
            lin = jnp.clip(lin, -SWIGLU_LIMIT, SWIGLU_LIMIT)
            act = (glu * _sigmoid(glu, SWIGLU_ALPHA) * (lin + 1.0)).astype(BF16)
            part = jnp.dot(act, w2_bf[cs, :], preferred_element_type=F32)
            acc = part if acc is None else acc + part
        o_ref[0:n_rows, :] = _pack_rows(acc + b2_ref[0])
        if n_rows < ROW_BLOCK:
            o_ref[n_rows:ROW_BLOCK, :] = jnp.zeros((ROW_BLOCK - n_rows, PACKED), jnp.int32)

    for quarters in range(1, 5):
        @pl.when(used & (quarters_ref[i] == quarters))
        def _():
            mlp(quarters * (ROW_BLOCK // 4))

    @pl.when(jnp.logical_not(used))
    def _():
        o_ref[...] = jnp.zeros_like(o_ref)


def _experts(plan, xs, w1, b1, w2, b2):
    n_rows = xs.shape[0]
    nb = n_rows // ROW_BLOCK
    grid_spec = pltpu.PrefetchScalarGridSpec(
        num_scalar_prefetch=6,
        grid=(nb,),
        in_specs=[pl.BlockSpec((ROW_BLOCK, PACKED), lambda i, be, nu, *_: (jnp.minimum(i, nu[0] - 1), 0)),
                  pl.BlockSpec(memory_space=pl.ANY),
                  pl.BlockSpec((1, 1, 2 * D_FF), lambda i, be, *_: (be[i], 0, 0)),
                  pl.BlockSpec(memory_space=pl.ANY),
                  pl.BlockSpec((1, 1, D_MODEL), lambda i, be, *_: (be[i], 0, 0))],
        out_specs=pl.BlockSpec((ROW_BLOCK, PACKED), lambda i, *_: (i, 0)),
        scratch_shapes=[pltpu.VMEM((2, D_MODEL, 2 * D_FF), F32), pltpu.VMEM((2, D_FF, D_MODEL), F32),
                        pltpu.VMEM((D_MODEL, 2 * D_FF), BF16), pltpu.VMEM((D_FF, D_MODEL), BF16),
                        pltpu.SemaphoreType.DMA((2,))],
    )
    return pl.pallas_call(
        _experts_kernel,
        out_shape=jax.ShapeDtypeStruct((n_rows, PACKED), jnp.int32),
        grid_spec=grid_spec,
        compiler_params=pltpu.CompilerParams(dimension_semantics=("arbitrary",),
                                             vmem_limit_bytes=VMEM_LIMIT),
        name="experts",
    )(*plan, xs, w1, b1.reshape(N_EXPERTS, 1, 2 * D_FF), w2, b2.reshape(N_EXPERTS, 1, D_MODEL))


def _sc_gather_rows(table, idx_flat):
    n_cores, n_workers = _sc_workers()
    n_idx = idx_flat.shape[0]
    w = table.shape[1]
    per_worker = n_idx // n_workers
    n_chunks = per_worker // SC_CHUNK
    assert per_worker * n_workers == n_idx and n_chunks * SC_CHUNK == per_worker and n_chunks % 2 == 0
    mesh = plsc.VectorSubcoreMesh(core_axis_name="c", subcore_axis_name="s")

    @functools.partial(
        pl.kernel, mesh=mesh,
        out_type=jax.ShapeDtypeStruct((n_idx, w), table.dtype),
        scratch_types=[pltpu.VMEM((SC_CHUNK,), jnp.int32), pltpu.VMEM((SC_CHUNK,), jnp.int32),
                       pltpu.VMEM((SC_CHUNK, w), table.dtype), pltpu.VMEM((SC_CHUNK, w), table.dtype),
                       pltpu.SemaphoreType.DMA, pltpu.SemaphoreType.DMA],
        name="sc_gather",
    )
    def gather(table_hbm, idx_hbm, out_hbm, idx0, idx1, rows0, rows1, sem0, sem1):
        wid = lax.axis_index("s") * n_cores + lax.axis_index("c")
        bufs = ((idx0, rows0, sem0), (idx1, rows1, sem1))

        def chunk_off(c):
            return pl.multiple_of(wid * per_worker + c * SC_CHUNK, 8)

        def start(c, b):
            idx_v, rows_v, sem = bufs[b]
            pltpu.sync_copy(idx_hbm.at[pl.ds(chunk_off(c), SC_CHUNK)], idx_v)
            pltpu.async_copy(table_hbm.at[idx_v], rows_v, sem)

        def finish(c, b):
            idx_v, rows_v, sem = bufs[b]
            pltpu.make_async_copy(table_hbm.at[idx_v], rows_v, sem).wait()
            pltpu.sync_copy(rows_v, out_hbm.at[pl.ds(chunk_off(c), SC_CHUNK)])

        start(0, 0)

        def body(i, carry):
            start(2 * i + 1, 1)
            finish(2 * i, 0)

            @pl.when(i < n_chunks // 2 - 1)
            def _():
                start(2 * i + 2, 0)

            finish(2 * i + 1, 1)
            return carry

        lax.fori_loop(0, n_chunks // 2, body, 0)

    return gather(table, idx_flat)


def _finish_kernel(h_ref, pw_ref, mod_ref, nw_ref, y0_ref, y1_ref, y2_ref, y3_ref, *rest):
    o_ref = rest[-1]
    pw = pw_ref[...]
    moe = pw[:, 0:1] * _unpack_rows(y0_ref[0])
    for j, y_ref in enumerate((y1_ref, y2_ref, y3_ref), start=1):
        moe = moe + pw[:, j:j + 1] * _unpack_rows(y_ref[0])
    gate_f = mod_ref[0, 5:6, :]
    o_ref[...] = _rms(h_ref[...] + gate_f * moe, nw_ref[...])


def _finish(h1, pw, mod, norm_w, y4, seq, tc, tile0, prev_out):
    n = pw.shape[0]
    per_b = seq // tc
    slot = lambda j: pl.BlockSpec((1, tc, PACKED), lambda i: (j, i, 0))
    in_specs = [pl.BlockSpec((tc, D_MODEL), lambda i: (tile0 + i, 0)),
                pl.BlockSpec((tc, LANES), lambda i: (i, 0)),
                pl.BlockSpec((1, 6, D_MODEL), lambda i: ((tile0 + i) // per_b, 0, 0)),
                pl.BlockSpec((1, D_MODEL), lambda i: (0, 0)),
                slot(0), slot(1), slot(2), slot(3)]
    args = [h1, pw, mod, norm_w.reshape(1, D_MODEL), y4, y4, y4, y4]
    aliases = {}
    if prev_out is not None:
        in_specs.append(pl.BlockSpec(memory_space=pl.ANY))
        args.append(prev_out)
        aliases = {len(args) - 1: 0}
    return pl.pallas_call(
        _finish_kernel,
        out_shape=jax.ShapeDtypeStruct(h1.shape, F32),
        grid=(n // tc,),
        in_specs=in_specs,
        out_specs=pl.BlockSpec((tc, D_MODEL), lambda i: (tile0 + i, 0)),
        input_output_aliases=aliases,
        compiler_params=pltpu.CompilerParams(dimension_semantics=("arbitrary",),
                                             vmem_limit_bytes=VMEM_LIMIT),
        name="finish",
    )(*args)


def _moe_plan(counts, n_assign):
    cnt = counts[:, 0, :N_EXPERTS]
    sizes = jnp.sum(cnt, axis=0)
    padded = (sizes + ROW_BLOCK - 1) // ROW_BLOCK * ROW_BLOCK
    pad_end = jnp.cumsum(padded)
    pad_start = pad_end - padded
    tile_base = pad_start[None, :] + jnp.cumsum(cnt, axis=0) - cnt
    digits = jnp.stack([tile_base % 256, (tile_base // 256) % 256, tile_base // 65536], axis=1)
    rows = jnp.tile(digits, (1, 1, TOP_K))
    slot_of_row = jnp.arange(LANES, dtype=jnp.int32) // N_EXPERTS
    col = jnp.arange(LANES, dtype=jnp.int32)
    bt = jnp.where(slot_of_row[:, None] == col[None, :], rows[..., None], 0).astype(BF16)
    nb = n_assign // ROW_BLOCK + N_EXPERTS
    block_start = jnp.arange(nb, dtype=jnp.int32) * ROW_BLOCK
    block_e = jnp.minimum(jnp.sum(pad_end[None, :] <= block_start[:, None], axis=1),
                          N_EXPERTS - 1).astype(jnp.int32)
    n_used = (pad_end[-1] // ROW_BLOCK).astype(jnp.int32).reshape(1)
    nonempty = padded > 0
    ids = jnp.arange(N_EXPERTS, dtype=jnp.int32)
    suffix_min = lax.cummin(jnp.where(nonempty, ids, N_EXPERTS), reverse=True)
    after = jnp.concatenate([suffix_min[1:], jnp.full((1,), N_EXPERTS, jnp.int32)])
    next_nonempty = jnp.where(after < N_EXPERTS, after, -1).astype(jnp.int32)
    slot_of = ((jnp.cumsum(nonempty) - 1) % 2).astype(jnp.int32)
    first = jnp.concatenate([jnp.ones((1,), jnp.int32),
                             (block_e[1:] != block_e[:-1]).astype(jnp.int32)])
    valid_rows = (pad_start + sizes)[block_e] - block_start
    quarter = ROW_BLOCK // 4
    quarters = jnp.clip((valid_rows + quarter - 1) // quarter, 1, 4).astype(jnp.int32)
    expert_plan = (block_e, n_used, first, next_nonempty[block_e], slot_of[block_e], quarters)
    r = jnp.arange(ROW_BLOCK, dtype=jnp.int32)[None, :]
    pad_idx = jnp.where(r < (padded - sizes)[:, None], (pad_start + sizes)[:, None] + r,
                        (nb - 1) * ROW_BLOCK + r).astype(jnp.int32).reshape(PAD_SLOTS)
    return bt, expert_plan, pad_idx


def kernel(x, c, w_ada, b_ada, norm_mix_w, w_in, hgrn_lower_bounds, hgrn_norm_w, conv_w,
           w_hgrn_out, w_conv_out, w_mix_out, norm_ffn_w, w_router, b_router, w1, b1, w2, b2,
           norm_final_w):
    bsz, seq, d = x.shape
    assert d == D_MODEL and seq % CHUNK == 0
    n = bsz * seq
    depth = w_ada.shape[0]
    tile = min(512, seq)
    assert seq % tile == 0 and n % (MOE_GROUPS * tile) == 0 and (n // MOE_GROUPS * TOP_K) % ROW_BLOCK == 0
    h = x.reshape(n, d)
    wr_pad = jnp.zeros((depth, D_MODEL, LANES), BF16).at[:, :, :N_EXPERTS].set(w_router.astype(BF16))
    br_pad = jnp.zeros((depth, 1, LANES), F32).at[:, 0, :N_EXPERTS].set(b_router)
    for layer in range(depth):
        mod = _ada(c, w_ada[layer], b_ada[layer]).reshape(bsz, 6, d)
        proj = _inproj(h, mod, norm_mix_w[layer], w_in[layer].astype(BF16), seq, tile)
        h = _mix(proj, h, mod, hgrn_lower_bounds, hgrn_norm_w[layer], conv_w[layer],
                 w_hgrn_out[layer].astype(BF16), w_conv_out[layer].astype(BF16),
                 w_mix_out[layer].astype(BF16), layer, bsz, seq, min(MIX_ROWS, seq))
        assert layer == depth - 1, "only the last layer applies the final norm"
        ng = n // MOE_GROUPS
        n_rows = (ng * TOP_K // ROW_BLOCK + N_EXPERTS) * ROW_BLOCK
        routed = [_route(h, mod, norm_ffn_w[layer], wr_pad[layer], br_pad[layer], seq, tile,
                         grp * (ng // tile), ng) for grp in range(MOE_GROUPS)]
        plans = jax.vmap(lambda cnt: _moe_plan(cnt, ng * TOP_K))(jnp.stack([r[4] for r in routed]))
        out = None
        for grp in range(MOE_GROUPS):
            tile0 = grp * (ng // tile)
            u2, oh4, rk, pw, _ = routed[grp]
            bt, expert_plan, pad_idx = jax.tree.map(lambda a: a[grp], plans)
            dest_slots = _dest(oh4, rk, bt, tile)[:TOP_K].reshape(TOP_K * ng)
            xs = _sc_scatter_rows(u2, dest_slots, pad_idx, n_rows)
            ys = _experts(expert_plan, xs, w1[layer], b1[layer], w2[layer], b2[layer])
            y4 = _sc_gather_rows(ys, dest_slots).reshape(TOP_K, ng, PACKED)
            fin = min(MIX_ROWS, seq)
            out = _finish(h, pw, mod, norm_final_w, y4, seq, fin, grp * (ng // fin), out)
        h = out
    return h.reshape(bsz, seq, d)
```

```python
import functools

import jax
import jax.numpy as jnp
from jax import lax
from jax.experimental import pallas as pl
from jax.experimental.pallas import tpu as pltpu
from jax.experimental.pallas import tpu_sc as plsc

F32 = jnp.float32
BF16 = jnp.bfloat16

D_MODEL = 1024
HGRN_HEADS = 4
HEAD_DIM = 128
HGRN_WIDTH = HGRN_HEADS * HEAD_DIM
CONV_WIDTH = 512
CONV_K = 3
CHUNK = 64
N_EXPERTS = 32
TOP_K = 4
D_FF = 1024
SWIGLU_LIMIT = 7.0
SWIGLU_ALPHA = 1.702
EPS = 1e-6
LOG2_E = 1.4426950408889634
IN_COLS = 4 * HGRN_WIDTH + 3 * CONV_WIDTH + 2 * D_MODEL
LANES = 128
MIX_ROWS = 1024
ROW_BLOCK = 1024
PACKED = D_MODEL // 2
MOE_GROUPS = 2
SC_CHUNK = 64
PAD_SLOTS = N_EXPERTS * ROW_BLOCK
VMEM_LIMIT = 56 * 1024 * 1024


def _sigmoid(x, scale=1.0):
    return 1.0 / (1.0 + jnp.exp2(x * (-scale * LOG2_E)))


def _rms(x, w):
    ms = jnp.mean(x * x, axis=-1, keepdims=True)
    return x * lax.rsqrt(ms + EPS) * w


def _pack_rows(x):
    w = x.shape[1] // 2
    lo = lax.bitcast_convert_type(x[:, :w].astype(BF16).astype(F32), jnp.uint32)
    hi = lax.bitcast_convert_type(x[:, w:].astype(BF16).astype(F32), jnp.uint32)
    return lax.bitcast_convert_type((lo >> 16) | (hi & jnp.uint32(0xFFFF0000)), jnp.int32)


def _unpack_rows(p):
    u = lax.bitcast_convert_type(p, jnp.uint32)
    lo = lax.bitcast_convert_type(u << 16, F32)
    hi = lax.bitcast_convert_type(u & jnp.uint32(0xFFFF0000), F32)
    return jnp.concatenate([lo, hi], axis=1)


def _nt_dot(a, b):
    return lax.dot_general(a, b, (((1,), (1,)), ((), ())), preferred_element_type=F32)


def _tn_dot(a, b):
    return lax.dot_general(a, b, (((0,), (0,)), ((), ())), preferred_element_type=F32)


def _ada_kernel(c_ref, w_ref, b_ref, o_ref):
    c = c_ref[...]
    sc = (c * _sigmoid(c)).astype(BF16)
    o_ref[...] = jnp.dot(sc, w_ref[...].astype(BF16), preferred_element_type=F32) + b_ref[...]


def _ada(c, w_ada, b_ada):
    bsz, d = c.shape
    n = w_ada.shape[1]
    return pl.pallas_call(
        _ada_kernel,
        out_shape=jax.ShapeDtypeStruct((bsz, n), F32),
        grid=(n // d,),
        in_specs=[pl.BlockSpec((bsz, d), lambda j: (0, 0)),
                  pl.BlockSpec((d, d), lambda j: (0, j)),
                  pl.BlockSpec((1, d), lambda j: (0, j))],
        out_specs=pl.BlockSpec((bsz, d), lambda j: (0, j)),
        name="ada",
    )(c, w_ada, b_ada.reshape(1, n))


def _inproj_kernel(x_ref, mod_ref, nw_ref, w_ref, o_ref, u_scr):
    y = _rms(x_ref[...], nw_ref[...])
    shift = mod_ref[0, 0:1, :]
    scale = mod_ref[0, 1:2, :]
    u_scr[...] = (y * (1.0 + scale) + shift).astype(BF16)

    step = 512
    for j in range(IN_COLS // step):
        o_ref[:, j * step:(j + 1) * step] = jnp.dot(
            u_scr[...], w_ref[:, j * step:(j + 1) * step], preferred_element_type=F32).astype(BF16)


def _inproj(x2, mod, norm_w, w_in_bf, seq, tm):
    n = x2.shape[0]
    per_b = seq // tm
    return pl.pallas_call(
        _inproj_kernel,
        out_shape=jax.ShapeDtypeStruct((n, IN_COLS), BF16),
        grid=(n // tm,),
        in_specs=[pl.BlockSpec((tm, D_MODEL), lambda i: (i, 0)),
                  pl.BlockSpec((1, 6, D_MODEL), lambda i: (i // per_b, 0, 0)),
                  pl.BlockSpec((1, D_MODEL), lambda i: (0, 0)),
                  pl.BlockSpec((D_MODEL, IN_COLS), lambda i: (0, 0), pipeline_mode=pl.Buffered(1))],
        out_specs=pl.BlockSpec((tm, IN_COLS), lambda i: (i, 0)),
        scratch_shapes=[pltpu.VMEM((tm, D_MODEL), BF16)],
        compiler_params=pltpu.CompilerParams(dimension_semantics=("arbitrary",),
                                             vmem_limit_bytes=VMEM_LIMIT),
        name="inproj",
    )(x2, mod, norm_w.reshape(1, D_MODEL), w_in_bf)


_HEADS = [slice(h * HEAD_DIM, (h + 1) * HEAD_DIM) for h in range(HGRN_HEADS)]


def _chunk_rows(c):
    if isinstance(c, int):
        return pl.ds(c * CHUNK, CHUNK)
    return pl.ds(pl.multiple_of(c * CHUNK, CHUNK), CHUNK)


def _hgrn_gates(c, lb, tri, f_ref, b_scr, k_scr, f_scr):
    fx = f_ref[_chunk_rows(c), :].astype(F32)
    f = lb + (1.0 - lb) * _sigmoid(fx)
    g = jnp.maximum(jnp.log(f), -128.0) * LOG2_E
    g1 = g.astype(BF16)
    r1 = g - g1.astype(F32)
    g2 = r1.astype(BF16)
    g3 = (r1 - g2.astype(F32)).astype(BF16)
    b_scr[...] = (jnp.dot(tri, g1, preferred_element_type=F32)
                  + jnp.dot(tri, g2, preferred_element_type=F32)
                  + jnp.dot(tri, g3, preferred_element_type=F32))
    f_scr[...] = f
    k_scr[...] = 1.0 - f


def _hgrn_scores(c, q_ref, b_scr, k_scr, f_scr):
    rows = _chunk_rows(c)
    qs = [q_ref[rows, hs].astype(F32) for hs in _HEADS]
    s_mats = [_level_scores(1, qs[h], _HEADS[h], b_scr, k_scr, f_scr) for h in range(HGRN_HEADS)]
    for lvl in range(2, 7):
        for h in range(HGRN_HEADS):
            s_mats[h] = s_mats[h] + _level_scores(lvl, qs[h], _HEADS[h], b_scr, k_scr, f_scr)
    return s_mats


def _hgrn_outputs(c, s_mats, q_ref, v_ref, st_scr, o_scr, b_scr, k_scr):
    rows = _chunk_rows(c)
    for h, hs in enumerate(_HEADS):
        q = q_ref[rows, hs].astype(F32)
        v_bf = v_ref[rows, hs]
        b = b_scr[:, hs]
        kk = k_scr[:, hs]
        b_last = b_scr[CHUNK - 1:CHUNK, hs]
        st = st_scr[h]
        qd = (q * jnp.exp2(b)).astype(BF16)
        kdec = (kk * jnp.exp2(b_last - b)).astype(BF16)
        diag = jnp.sum(q * kk, axis=-1, keepdims=True)
        o_scr[rows, hs] = (_nt_dot(qd, st.astype(BF16))
                           + jnp.dot(s_mats[h].astype(BF16), v_bf, preferred_element_type=F32)
                           + diag * v_bf.astype(F32))
        st_scr[h] = jnp.exp2(b_last) * st + _tn_dot(v_bf, kdec)


def _level_scores(lvl, q, hs, b_scr, k_scr, f_scr):
    row = lax.broadcasted_iota(jnp.int32, (CHUNK, 1), 0)
    col = lax.broadcasted_iota(jnp.int32, (1, CHUNK), 1)
    b = b_scr[:, hs]
    kk = k_scr[:, hs]
    blk = 1 << lvl
    half = blk // 2
    if lvl == 1:
        odd = (row & 1) == 1
        qx = jnp.where(odd, q * f_scr[:, hs], 0.0).astype(BF16)
        kx = jnp.where(odd, 0.0, kk).astype(BF16)
        return jnp.where((row >> 1) == (col >> 1), _nt_dot(qx, kx), 0.0)
    if half < 8:
        groups = []
        sub = lax.broadcasted_iota(jnp.int32, (8, 1), 0)
        for j in range(CHUNK // 8):
            rj = None
            for k in reversed(range(8 // blk)):
                m = 8 * j + k * blk + half - 1
                bm = jnp.broadcast_to(b_scr[m:m + 1, hs], (8, HEAD_DIM))
                rj = bm if rj is None else jnp.where(sub < (k + 1) * blk, bm, rj)
            groups.append(rj)
        ref = jnp.concatenate(groups, axis=0)
        second = (row & (blk - 1)) >= half
        qx = (q * jnp.exp2(jnp.where(second, b - ref, -jnp.inf))).astype(BF16)
        kx = (kk * jnp.exp2(jnp.where(second, -jnp.inf, ref - b))).astype(BF16)
        return jnp.where((row >> lvl) == (col >> lvl), _nt_dot(qx, kx), 0.0)
    n_blk = CHUNK // blk
    qparts, kparts = [], []
    for j in range(n_blk):
        m = j * blk + half - 1
        bm = b_scr[m:m + 1, hs]
        tq = slice(j * blk + half, (j + 1) * blk)
        tk = slice(j * blk, j * blk + half)
        qparts.append(q[tq] * jnp.exp2(b[tq] - bm))
        kparts.append(kk[tk] * jnp.exp2(bm - b[tk]))
        kparts.append(jnp.zeros((half, HEAD_DIM), F32))
    qx = jnp.concatenate(qparts, axis=0).astype(BF16)
    kx = jnp.concatenate(kparts, axis=0).astype(BF16)
    sc = _nt_dot(qx, kx)
    if n_blk > 1:
        crow = lax.broadcasted_iota(jnp.int32, (CHUNK // 2, 1), 0)
        sc = jnp.where((crow // half) == (col >> lvl), sc, 0.0)
    pieces = []
    for j in range(n_blk):
        pieces.append(jnp.zeros((half, CHUNK), F32))
        pieces.append(sc[j * half:(j + 1) * half])
    return jnp.concatenate(pieces, axis=0)


def _mix_kernel(lbt_ref, q_ref, f_ref, i_ref, g_ref, cb_ref, cc_ref, ch_ref, ga0_ref, ga1_ref,
                gb0_ref, gb1_ref, x_ref, mod_ref, hnw_ref, cw_ref, wa_ref, wb_ref, wm_ref,
                o_ref, st_scr, o_scr, carry_scr, b_scr, k_scr, f_scr, *, layer, rows_per_step, epi_rows):
    @pl.when(pl.program_id(1) == 0)
    def _():
        st_scr[...] = jnp.zeros_like(st_scr)
        carry_scr[0:8, :] = jnp.zeros((8, CONV_WIDTH), F32)

    tab = lbt_ref[...]
    tmax = jnp.max(tab, axis=0, keepdims=True)
    te = jnp.exp(tab - tmax)
    lb = jnp.sum(te[0:layer + 1], axis=0, keepdims=True) / jnp.sum(te, axis=0, keepdims=True)

    ri = lax.broadcasted_iota(jnp.int32, (CHUNK, CHUNK), 0)
    ci = lax.broadcasted_iota(jnp.int32, (CHUNK, CHUNK), 1)
    tri = jnp.where(ci <= ri, 1.0, 0.0).astype(BF16)

    n_chunks = rows_per_step // CHUNK

    def slot(s):
        return b_scr.at[s], k_scr.at[s], f_scr.at[s]

    def chunk(c, c_next, s):
        bs, ks, fs = slot(s)
        s_mats = _hgrn_scores(c, q_ref, bs, ks, fs)
        _hgrn_gates(c_next, lb, tri, f_ref, *slot(1 - s))
        _hgrn_outputs(c, s_mats, q_ref, i_ref, st_scr, o_scr, bs, ks)

    def pair_body(i, carry):
        chunk(2 * i, 2 * i + 1, 0)
        chunk(2 * i + 1, jnp.minimum(2 * i + 2, n_chunks - 1), 1)
        return carry

    _hgrn_gates(0, lb, tri, f_ref, *slot(0))
    lax.fori_loop(0, n_chunks // 2, pair_body, 0)

    gate_m = mod_ref[0, 2:3, :]
    hnw = hnw_ref[...]
    cw0 = cw_ref[0:1, :]
    cw1 = cw_ref[1:2, :]
    cw2 = cw_ref[2:3, :]
    for r in range(rows_per_step // epi_rows):
        rs = slice(r * epi_rows, (r + 1) * epi_rows)
        parts = []
        for h in range(HGRN_HEADS):
            hs = slice(h * HEAD_DIM, (h + 1) * HEAD_DIM)
            oh = _rms(o_scr[rs, hs], hnw)
            go = g_ref[rs, hs].astype(F32)
            parts.append((oh * (go * _sigmoid(go))).astype(BF16))
        ya = jnp.dot(jnp.concatenate(parts, axis=1), wa_ref[...], preferred_element_type=F32)
        uc = cc_ref[rs, :].astype(F32) * ch_ref[rs, :].astype(F32)
        carry_scr[8:8 + epi_rows, :] = uc
        s1 = carry_scr[7:7 + epi_rows, :]
        s2 = carry_scr[6:6 + epi_rows, :]
        carry_scr[0:8, :] = uc[epi_rows - 8:epi_rows, :]
        yc = cb_ref[rs, :].astype(F32) * (cw2 * uc + cw1 * s1 + cw0 * s2)
        yb = jnp.dot(yc.astype(BF16), wb_ref[...], preferred_element_type=F32)
        ga = jnp.concatenate([ga0_ref[rs, :], ga1_ref[rs, :]], axis=1).astype(F32)
        gb = jnp.concatenate([gb0_ref[rs, :], gb1_ref[rs, :]], axis=1).astype(F32)
        merged = (_sigmoid(ga) * ya + _sigmoid(gb) * yb).astype(BF16)
        o_ref[rs, :] = x_ref[rs, :] + gate_m * jnp.dot(merged, wm_ref[...], preferred_element_type=F32)


def _mix(proj, x2, mod, lb_table, hgrn_norm_w, conv_w, wa, wb, wm, layer, bsz, seq, tt):
    n = x2.shape[0]
    per_b = seq // tt

    def col(col_block):
        return pl.BlockSpec((tt, 512), lambda b, t: (b * per_b + t, col_block))

    const = lambda shape: pl.BlockSpec(shape, lambda b, t: (0,) * len(shape))
    in_specs = [
        const(lb_table.shape),
        col(0), col(1), col(2), col(3),
        col(4), col(5), col(6),
        col(7), col(8), col(9), col(10),
        pl.BlockSpec((tt, D_MODEL), lambda b, t: (b * per_b + t, 0)),
        pl.BlockSpec((1, 6, D_MODEL), lambda b, t: (b, 0, 0)),
        const((1, HEAD_DIM)), const((CONV_K, CONV_WIDTH)),
        const((HGRN_WIDTH, D_MODEL)), const((CONV_WIDTH, D_MODEL)), const((D_MODEL, D_MODEL)),
    ]
    kern = functools.partial(_mix_kernel, layer=layer, rows_per_step=tt, epi_rows=tt)
    return pl.pallas_call(
        kern,
        out_shape=jax.ShapeDtypeStruct((n, D_MODEL), F32),
        grid=(bsz, per_b),
        in_specs=in_specs,
        out_specs=pl.BlockSpec((tt, D_MODEL), lambda b, t: (b * per_b + t, 0)),
        scratch_shapes=[pltpu.VMEM((HGRN_HEADS, HEAD_DIM, HEAD_DIM), F32),
                        pltpu.VMEM((tt, HGRN_WIDTH), F32),
                        pltpu.VMEM((8 + tt, CONV_WIDTH), F32),
                        pltpu.VMEM((2, CHUNK, HGRN_WIDTH), F32),
                        pltpu.VMEM((2, CHUNK, HGRN_WIDTH), F32),
                        pltpu.VMEM((2, CHUNK, HGRN_WIDTH), F32)],
        compiler_params=pltpu.CompilerParams(dimension_semantics=("arbitrary", "arbitrary"),
                                             vmem_limit_bytes=VMEM_LIMIT),
        name="mix",
    )(lb_table, *([proj] * 11),
      x2, mod, hgrn_norm_w.reshape(1, HEAD_DIM), conv_w, wa, wb, wm)


def _route_kernel(h_ref, mod_ref, nw_ref, wr_ref, br_ref, u_ref, oh4_ref, rk_ref, pw_ref, cnt_ref, *, tr):
    shift = mod_ref[0, 3:4, :]
    scale = mod_ref[0, 4:5, :]
    u = _rms(h_ref[...], nw_ref[...]) * (1.0 + scale) + shift
    u_ref[...] = _pack_rows(u)
    lane = lax.broadcasted_iota(jnp.int32, (tr, LANES), 1).astype(F32)
    logits = jnp.dot(u.astype(BF16), wr_ref[...], preferred_element_type=F32) + br_ref[...]
    logits = jnp.where(lane < N_EXPERTS, logits, -jnp.inf)
    idx, val = [], []
    cur = logits
    for _ in range(TOP_K):
        m = jnp.max(cur, axis=-1, keepdims=True)
        i = jnp.min(jnp.where(cur == m, lane, float(LANES)), axis=-1, keepdims=True)
        idx.append(i)
        val.append(m)
        cur = jnp.where(lane == i, -jnp.inf, cur)
    ex = [jnp.exp(v - val[0]) for v in val]
    den = ex[0] + ex[1] + ex[2] + ex[3]
    onehot = jnp.zeros((tr, LANES), F32)
    for i in idx:
        onehot = onehot + jnp.where(lane == i, 1.0, 0.0)
    ri = lax.broadcasted_iota(jnp.int32, (tr, tr), 0)
    ci = lax.broadcasted_iota(jnp.int32, (tr, tr), 1)
    tri = jnp.where(ci < ri, 1.0, 0.0).astype(BF16)
    pref = jnp.dot(tri, onehot.astype(BF16), preferred_element_type=F32)
    rk = jnp.zeros((tr, LANES), jnp.int32)
    pw = jnp.zeros((tr, LANES), F32)
    oh4 = jnp.zeros((tr, LANES), F32)
    for j in range(TOP_K):
        rank = jnp.sum(jnp.where(lane == idx[j], pref, 0.0), axis=-1, keepdims=True).astype(jnp.int32)
        rk = jnp.where(lane == j, rank, rk)
        pw = jnp.where(lane == j, ex[j] / den, pw)
        oh4 = oh4 + jnp.where(lane == idx[j] + N_EXPERTS * j, 1.0, 0.0)
    oh4_ref[...] = oh4.astype(BF16)
    rk_ref[...] = rk
    pw_ref[...] = pw
    cnt_ref[0] = jnp.sum(onehot, axis=0, keepdims=True).astype(jnp.int32)


def _route(h1, mod, norm_w, wr_pad, br_pad, seq, tr, tile0, n):
    per_b = seq // tr
    nt = n // tr
    return pl.pallas_call(
        functools.partial(_route_kernel, tr=tr),
        out_shape=(jax.ShapeDtypeStruct((n, PACKED), jnp.int32),
                   jax.ShapeDtypeStruct((n, LANES), BF16),
                   jax.ShapeDtypeStruct((n, LANES), jnp.int32),
                   jax.ShapeDtypeStruct((n, LANES), F32),
                   jax.ShapeDtypeStruct((nt, 1, LANES), jnp.int32)),
        grid=(nt,),
        in_specs=[pl.BlockSpec((tr, D_MODEL), lambda i: (tile0 + i, 0)),
                  pl.BlockSpec((1, 6, D_MODEL), lambda i: ((tile0 + i) // per_b, 0, 0)),
                  pl.BlockSpec((1, D_MODEL), lambda i: (0, 0)),
                  pl.BlockSpec((D_MODEL, LANES), lambda i: (0, 0)),
                  pl.BlockSpec((1, LANES), lambda i: (0, 0))],
        out_specs=(pl.BlockSpec((tr, PACKED), lambda i: (i, 0)),
                   pl.BlockSpec((tr, LANES), lambda i: (i, 0)),
                   pl.BlockSpec((tr, LANES), lambda i: (i, 0)),
                   pl.BlockSpec((tr, LANES), lambda i: (i, 0)),
                   pl.BlockSpec((1, 1, LANES), lambda i: (i, 0, 0))),
        compiler_params=pltpu.CompilerParams(dimension_semantics=("arbitrary",),
                                             vmem_limit_bytes=VMEM_LIMIT),
        name="route",
    )(h1, mod, norm_w.reshape(1, D_MODEL), wr_pad, br_pad)


def _dest_kernel(oh4_ref, rk_ref, bt_ref, o_ref):
    oh = oh4_ref[...]
    start = (jnp.dot(oh, bt_ref[0, 0], preferred_element_type=F32)
             + 256.0 * jnp.dot(oh, bt_ref[0, 1], preferred_element_type=F32)
             + 65536.0 * jnp.dot(oh, bt_ref[0, 2], preferred_element_type=F32))
    o_ref[...] = (start + rk_ref[...].astype(F32)).T[:8, :].astype(jnp.int32)


def _dest(oh4, rk, bt, tr):
    n = oh4.shape[0]
    return pl.pallas_call(
        _dest_kernel,
        out_shape=jax.ShapeDtypeStruct((8, n), jnp.int32),
        grid=(n // tr,),
        in_specs=[pl.BlockSpec((tr, LANES), lambda i: (i, 0)),
                  pl.BlockSpec((tr, LANES), lambda i: (i, 0)),
                  pl.BlockSpec((1, 3, LANES, LANES), lambda i: (i, 0, 0, 0))],
        out_specs=pl.BlockSpec((8, tr), lambda i: (0, i)),
        name="dest",
    )(oh4, rk, bt)


def _sc_workers():
    info = plsc.get_sparse_core_info()
    return info.num_cores, info.num_cores * info.num_subcores


def _sc_scatter_rows(rows, idx_slots, pad_idx, n_out):
    n_cores, n_workers = _sc_workers()
    n, w = rows.shape
    k = idx_slots.shape[0] // n
    per_worker = n // n_workers
    pad_per_worker = pad_idx.shape[0] // n_workers
    assert per_worker % SC_CHUNK == 0 and pad_per_worker % SC_CHUNK == 0
    mesh = plsc.VectorSubcoreMesh(core_axis_name="c", subcore_axis_name="s")
    zeros = jnp.zeros((SC_CHUNK, w), rows.dtype)

    n_chunks = per_worker // SC_CHUNK
    assert n_chunks % 2 == 0

    @functools.partial(
        pl.kernel, mesh=mesh,
        out_type=jax.ShapeDtypeStruct((n_out, w), rows.dtype),
        scratch_types=[pltpu.VMEM((SC_CHUNK,), jnp.int32)] * k
        + [pltpu.VMEM((SC_CHUNK, w), rows.dtype)] * 2
        + [pltpu.SemaphoreType.DMA] * 2,
        name="sc_scatter",
    )
    def scatter(rows_hbm, idx_hbm, pad_hbm, zeros_hbm, out_hbm, *scratch):
        idx_bufs = scratch[:k]
        row_bufs = scratch[k:k + 2]
        sem_rows, sem_out = scratch[k + 2:]
        wid = lax.axis_index("s") * n_cores + lax.axis_index("c")

        def chunk_off(c):
            return pl.multiple_of(wid * per_worker + c * SC_CHUNK, 8)

        def load(c, b):
            pltpu.async_copy(rows_hbm.at[pl.ds(chunk_off(c), SC_CHUNK)], row_bufs[b], sem_rows)

        def wait_load(b):
            pltpu.make_async_copy(rows_hbm.at[pl.ds(0, SC_CHUNK)], row_bufs[b], sem_rows).wait()

        def scatter_chunk(c, b):
            for j in range(k):
                pltpu.sync_copy(idx_hbm.at[pl.ds(pl.multiple_of(j * n + chunk_off(c), 8), SC_CHUNK)],
                                idx_bufs[j])
            copies = [pltpu.async_copy(row_bufs[b], out_hbm.at[idx_bufs[j]], sem_out) for j in range(k)]
            for cp in copies:
                cp.wait()

        load(0, 0)

        def body(i, carry):
            wait_load(0)
            load(2 * i + 1, 1)
            scatter_chunk(2 * i, 0)
            wait_load(1)

            @pl.when(i < n_chunks // 2 - 1)
            def _():
                load(2 * i + 2, 0)

            scatter_chunk(2 * i + 1, 1)
            return carry

        lax.fori_loop(0, n_chunks // 2, body, 0)
        pltpu.sync_copy(zeros_hbm, row_bufs[0])

        def pad_body(i, carry):
            off = pl.multiple_of(wid * pad_per_worker + i * SC_CHUNK, 8)
            pltpu.sync_copy(pad_hbm.at[pl.ds(off, SC_CHUNK)], idx_bufs[0])
            pltpu.async_copy(row_bufs[0], out_hbm.at[idx_bufs[0]], sem_out).wait()
            return carry

        lax.fori_loop(0, pad_per_worker // SC_CHUNK, pad_body, 0)

    return scatter(rows, idx_slots, pad_idx, zeros)


def _experts_kernel(be_ref, nu_ref, first_ref, next_ref, slot_ref, quarters_ref, x_ref, w1_hbm, b1_ref, w2_hbm,
                    b2_ref, o_ref, w1_f32, w2_f32, w1_bf, w2_bf, sems):
    i = pl.program_id(0)
    used = i < nu_ref[0]

    def weight_copies(e, s):
        return (pltpu.make_async_copy(w1_hbm.at[e], w1_f32.at[s], sems.at[s]),
                pltpu.make_async_copy(w2_hbm.at[e], w2_f32.at[s], sems.at[s]))

    @pl.when(used & (first_ref[i] == 1))
    def _():
        s = slot_ref[i]

        @pl.when(i == 0)
        def _():
            for cp in weight_copies(be_ref[i], s):
                cp.start()

        for cp in weight_copies(be_ref[i], s):
            cp.wait()
        cw = 256
        for c in range(2 * D_FF // cw):
            w1_bf[:, c * cw:(c + 1) * cw] = w1_f32[s, :, c * cw:(c + 1) * cw].astype(BF16)
        for c in range(D_MODEL // cw):
            w2_bf[:, c * cw:(c + 1) * cw] = w2_f32[s, :, c * cw:(c + 1) * cw].astype(BF16)

        @pl.when(next_ref[i] >= 0)
        def _():
            for cp in weight_copies(next_ref[i], 1 - s):
                cp.start()

    def mlp(n_rows):
        step = 512
        x = _unpack_rows(x_ref[0:n_rows, :]).astype(BF16)
        acc = None
        for j in range(D_FF // step):
            cs = slice(j * step, (j + 1) * step)
            ls = slice(D_FF + j * step, D_FF + (j + 1) * step)
            glu = jnp.dot(x, w1_bf[:, cs], preferred_element_type=F32) + b1_ref[0, :, cs]
            lin = jnp.dot(x, w1_bf[:, ls], preferred_element_type=F32) + b1_ref[0, :, ls]
            glu = jnp.minimum(glu, SWIGLU_LIMIT)
            lin = jnp.clip(lin, -SWIGLU_LIMIT, SWIGLU_LIMIT)
            act = (glu * _sigmoid(glu, SWIGLU_ALPHA) * (lin + 1.0)).astype(BF16)
            part = jnp.dot(act, w2_bf[cs, :], preferred_element_type=F32)
            acc = part if acc is None else acc + part
        o_ref[0:n_rows, :] = _pack_rows(acc + b2_ref[0])
        if n_rows < ROW_BLOCK:
            o_ref[n_rows:ROW_BLOCK, :] = jnp.zeros((ROW_BLOCK - n_rows, PACKED), jnp.int32)

    for quarters in range(1, 5):
        @pl.when(used & (quarters_ref[i] == quarters))
        def _():
            mlp(quarters * (ROW_BLOCK // 4))

    @pl.when(jnp.logical_not(used))
    def _():
        o_ref[...] = jnp.zeros_like(o_ref)


def _experts(plan, xs, w1, b1, w2, b2):
    n_rows = xs.shape[0]
    nb = n_rows // ROW_BLOCK
    grid_spec = pltpu.PrefetchScalarGridSpec(
        num_scalar_prefetch=6,
        grid=(nb,),
        in_specs=[pl.BlockSpec((ROW_BLOCK, PACKED), lambda i, be, nu, *_: (jnp.minimum(i, nu[0] - 1), 0)),
                  pl.BlockSpec(memory_space=pl.ANY),
                  pl.BlockSpec((1, 1, 2 * D_FF), lambda i, be, *_: (be[i], 0, 0)),
                  pl.BlockSpec(memory_space=pl.ANY),
                  pl.BlockSpec((1, 1, D_MODEL), lambda i, be, *_: (be[i], 0, 0))],
        out_specs=pl.BlockSpec((ROW_BLOCK, PACKED), lambda i, *_: (i, 0)),
        scratch_shapes=[pltpu.VMEM((2, D_MODEL, 2 * D_FF), F32), pltpu.VMEM((2, D_FF, D_MODEL), F32),
                        pltpu.VMEM((D_MODEL, 2 * D_FF), BF16), pltpu.VMEM((D_FF, D_MODEL), BF16),
                        pltpu.SemaphoreType.DMA((2,))],
    )
    return pl.pallas_call(
        _experts_kernel,
        out_shape=jax.ShapeDtypeStruct((n_rows, PACKED), jnp.int32),
        grid_spec=grid_spec,
        compiler_params=pltpu.CompilerParams(dimension_semantics=("arbitrary",),
                                             vmem_limit_bytes=VMEM_LIMIT),
        name="experts",
    )(*plan, xs, w1, b1.reshape(N_EXPERTS, 1, 2 * D_FF), w2, b2.reshape(N_EXPERTS, 1, D_MODEL))


def _sc_gather_rows(table, idx_flat):
    n_cores, n_workers = _sc_workers()
    n_idx = idx_flat.shape[0]
    w = table.shape[1]
    per_worker = n_idx // n_workers
    n_chunks = per_worker // SC_CHUNK
    assert per_worker * n_workers == n_idx and n_chunks * SC_CHUNK == per_worker and n_chunks % 2 == 0
    mesh = plsc.VectorSubcoreMesh(core_axis_name="c", subcore_axis_name="s")

    @functools.partial(
        pl.kernel, mesh=mesh,
        out_type=jax.ShapeDtypeStruct((n_idx, w), table.dtype),
        scratch_types=[pltpu.VMEM((SC_CHUNK,), jnp.int32), pltpu.VMEM((SC_CHUNK,), jnp.int32),
                       pltpu.VMEM((SC_CHUNK, w), table.dtype), pltpu.VMEM((SC_CHUNK, w), table.dtype),
                       pltpu.SemaphoreType.DMA, pltpu.SemaphoreType.DMA],
        name="sc_gather",
    )
    def gather(table_hbm, idx_hbm, out_hbm, idx0, idx1, rows0, rows1, sem0, sem1):
        wid = lax.axis_index("s") * n_cores + lax.axis_index("c")
        bufs = ((idx0, rows0, sem0), (idx1, rows1, sem1))

        def chunk_off(c):
            return pl.multiple_of(wid * per_worker + c * SC_CHUNK, 8)

        def start(c, b):
            idx_v, rows_v, sem = bufs[b]
            pltpu.sync_copy(idx_hbm.at[pl.ds(chunk_off(c), SC_CHUNK)], idx_v)
            pltpu.async_copy(table_hbm.at[idx_v], rows_v, sem)

        def finish(c, b):
            idx_v, rows_v, sem = bufs[b]
            pltpu.make_async_copy(table_hbm.at[idx_v], rows_v, sem).wait()
            pltpu.sync_copy(rows_v, out_hbm.at[pl.ds(chunk_off(c), SC_CHUNK)])

        start(0, 0)

        def body(i, carry):
            start(2 * i + 1, 1)
            finish(2 * i, 0)

            @pl.when(i < n_chunks // 2 - 1)
            def _():
                start(2 * i + 2, 0)

            finish(2 * i + 1, 1)
            return carry

        lax.fori_loop(0, n_chunks // 2, body, 0)

    return gather(table, idx_flat)


def _finish_kernel(h_ref, pw_ref, mod_ref, nw_ref, y0_ref, y1_ref, y2_ref, y3_ref, *rest):
    o_ref = rest[-1]
    pw = pw_ref[...]
    moe = pw[:, 0:1] * _unpack_rows(y0_ref[0])
    for j, y_ref in enumerate((y1_ref, y2_ref, y3_ref), start=1):
        moe = moe + pw[:, j:j + 1] * _unpack_rows(y_ref[0])
    gate_f = mod_ref[0, 5:6, :]
    o_ref[...] = _rms(h_ref[...] + gate_f * moe, nw_ref[...])


def _finish(h1, pw, mod, norm_w, y4, seq, tc, tile0, prev_out):
    n = pw.shape[0]
    per_b = seq // tc
    slot = lambda j: pl.BlockSpec((1, tc, PACKED), lambda i: (j, i, 0))
    in_specs = [pl.BlockSpec((tc, D_MODEL), lambda i: (tile0 + i, 0)),
                pl.BlockSpec((tc, LANES), lambda i: (i, 0)),
                pl.BlockSpec((1, 6, D_MODEL), lambda i: ((tile0 + i) // per_b, 0, 0)),
                pl.BlockSpec((1, D_MODEL), lambda i: (0, 0)),
                slot(0), slot(1), slot(2), slot(3)]
    args = [h1, pw, mod, norm_w.reshape(1, D_MODEL), y4, y4, y4, y4]
    aliases = {}
    if prev_out is not None:
        in_specs.append(pl.BlockSpec(memory_space=pl.ANY))
        args.append(prev_out)
        aliases = {len(args) - 1: 0}
    return pl.pallas_call(
        _finish_kernel,
        out_shape=jax.ShapeDtypeStruct(h1.shape, F32),
        grid=(n // tc,),
        in_specs=in_specs,
        out_specs=pl.BlockSpec((tc, D_MODEL), lambda i: (tile0 + i, 0)),
        input_output_aliases=aliases,
        compiler_params=pltpu.CompilerParams(dimension_semantics=("arbitrary",),
                                             vmem_limit_bytes=VMEM_LIMIT),
        name="finish",
    )(*args)


def _moe_plan(counts, n_assign):
    cnt = counts[:, 0, :N_EXPERTS]
    sizes = jnp.sum(cnt, axis=0)
    padded = (sizes + ROW_BLOCK - 1) // ROW_BLOCK * ROW_BLOCK
    pad_end = jnp.cumsum(padded)
    pad_start = pad_end - padded
    tile_base = pad_start[None, :] + jnp.cumsum(cnt, axis=0) - cnt
    digits = jnp.stack([tile_base % 256, (tile_base // 256) % 256, tile_base // 65536], axis=1)
    rows = jnp.tile(digits, (1, 1, TOP_K))
    slot_of_row = jnp.arange(LANES, dtype=jnp.int32) // N_EXPERTS
    col = jnp.arange(LANES, dtype=jnp.int32)
    bt = jnp.where(slot_of_row[:, None] == col[None, :], rows[..., None], 0).astype(BF16)
    nb = n_assign // ROW_BLOCK + N_EXPERTS
    block_start = jnp.arange(nb, dtype=jnp.int32) * ROW_BLOCK
    block_e = jnp.minimum(jnp.sum(pad_end[None, :] <= block_start[:, None], axis=1),
                          N_EXPERTS - 1).astype(jnp.int32)
    n_used = (pad_end[-1] // ROW_BLOCK).astype(jnp.int32).reshape(1)
    nonempty = padded > 0
    ids = jnp.arange(N_EXPERTS, dtype=jnp.int32)
    suffix_min = lax.cummin(jnp.where(nonempty, ids, N_EXPERTS), reverse=True)
    after = jnp.concatenate([suffix_min[1:], jnp.full((1,), N_EXPERTS, jnp.int32)])
    next_nonempty = jnp.where(after < N_EXPERTS, after, -1).astype(jnp.int32)
    slot_of = ((jnp.cumsum(nonempty) - 1) % 2).astype(jnp.int32)
    first = jnp.concatenate([jnp.ones((1,), jnp.int32),
                             (block_e[1:] != block_e[:-1]).astype(jnp.int32)])
    valid_rows = (pad_start + sizes)[block_e] - block_start
    quarter = ROW_BLOCK // 4
    quarters = jnp.clip((valid_rows + quarter - 1) // quarter, 1, 4).astype(jnp.int32)
    expert_plan = (block_e, n_used, first, next_nonempty[block_e], slot_of[block_e], quarters)
    r = jnp.arange(ROW_BLOCK, dtype=jnp.int32)[None, :]
    pad_idx = jnp.where(r < (padded - sizes)[:, None], (pad_start + sizes)[:, None] + r,
                        (nb - 1) * ROW_BLOCK + r).astype(jnp.int32).reshape(PAD_SLOTS)
    return bt, expert_plan, pad_idx


def kernel(x, c, w_ada, b_ada, norm_mix_w, w_in, hgrn_lower_bounds, hgrn_norm_w, conv_w,
           w_hgrn_out, w_conv_out, w_mix_out, norm_ffn_w, w_router, b_router, w1, b1, w2, b2,
           norm_final_w):
    bsz, seq, d = x.shape
    assert d == D_MODEL and seq % CHUNK == 0
    n = bsz * seq
    depth = w_ada.shape[0]
    tile = min(512, seq)
    assert seq % tile == 0 and n % (MOE_GROUPS * tile) == 0 and (n // MOE_GROUPS * TOP_K) % ROW_BLOCK == 0
    h = x.reshape(n, d)
    wr_pad = jnp.zeros((depth, D_MODEL, LANES), BF16).at[:, :, :N_EXPERTS].set(w_router.astype(BF16))
    br_pad = jnp.zeros((depth, 1, LANES), F32).at[:, 0, :N_EXPERTS].set(b_router)
    for layer in range(depth):
        mod = _ada(c, w_ada[layer], b_ada[layer]).reshape(bsz, 6, d)
        proj = _inproj(h, mod, norm_mix_w[layer], w_in[layer].astype(BF16), seq, min(MIX_ROWS, seq))
        h = _mix(proj, h, mod, hgrn_lower_bounds, hgrn_norm_w[layer], conv_w[layer],
                 w_hgrn_out[layer].astype(BF16), w_conv_out[layer].astype(BF16),
                 w_mix_out[layer].astype(BF16), layer, bsz, seq, min(MIX_ROWS, seq))
        assert layer == depth - 1, "only the last layer applies the final norm"
        ng = n // MOE_GROUPS
        n_rows = (ng * TOP_K // ROW_BLOCK + N_EXPERTS) * ROW_BLOCK
        routed = [_route(h, mod, norm_ffn_w[layer], wr_pad[layer], br_pad[layer], seq, tile,
                         grp * (ng // tile), ng) for grp in range(MOE_GROUPS)]
        plans = jax.vmap(lambda cnt: _moe_plan(cnt, ng * TOP_K))(jnp.stack([r[4] for r in routed]))
        out = None
        for grp in range(MOE_GROUPS):
            tile0 = grp * (ng // tile)
            u2, oh4, rk, pw, _ = routed[grp]
            bt, expert_plan, pad_idx = jax.tree.map(lambda a: a[grp], plans)
            dest_slots = _dest(oh4, rk, bt, tile)[:TOP_K].reshape(TOP_K * ng)
            xs = _sc_scatter_rows(u2, dest_slots, pad_idx, n_rows)
            ys = _experts(expert_plan, xs, w1[layer], b1[layer], w2[layer], b2[layer])
            y4 = _sc_gather_rows(ys, dest_slots).reshape(TOP_K, ng, PACKED)
            fin = min(MIX_ROWS, seq)
            out = _finish(h, pw, mod, norm_final_w, y4, seq, fin, grp * (ng // fin), out)
        h = out
    return h.reshape(bsz, seq, d)
```

```python
import functools

import jax
import jax.numpy as jnp
from jax import lax
from jax.experimental import pallas as pl
from jax.experimental.pallas import tpu as pltpu
from jax.experimental.pallas import tpu_sc as plsc

F32 = jnp.float32
BF16 = jnp.bfloat16

D_MODEL = 1024
HGRN_HEADS = 4
HEAD_DIM = 128
HGRN_WIDTH = HGRN_HEADS * HEAD_DIM
CONV_WIDTH = 512
CONV_K = 3
CHUNK = 64
N_EXPERTS = 32
TOP_K = 4
D_FF = 1024
SWIGLU_LIMIT = 7.0
SWIGLU_ALPHA = 1.702
EPS = 1e-6
LOG2_E = 1.4426950408889634
IN_COLS = 4 * HGRN_WIDTH + 3 * CONV_WIDTH + 2 * D_MODEL
LANES = 128
MIX_ROWS = 1024
ROW_BLOCK = 1024
PACKED = D_MODEL // 2
MOE_GROUPS = 2
SC_CHUNK = 64
PAD_SLOTS = N_EXPERTS * ROW_BLOCK
VMEM_LIMIT = 56 * 1024 * 1024


def _sigmoid(x, scale=1.0):
    return 1.0 / (1.0 + jnp.exp2(x * (-scale * LOG2_E)))


def _rms(x, w):
    ms = jnp.mean(x * x, axis=-1, keepdims=True)
    return x * lax.rsqrt(ms + EPS) * w


def _pack_rows(x):
    w = x.shape[1] // 2
    lo = lax.bitcast_convert_type(x[:, :w].astype(BF16).astype(F32), jnp.uint32)
    hi = lax.bitcast_convert_type(x[:, w:].astype(BF16).astype(F32), jnp.uint32)
    return lax.bitcast_convert_type((lo >> 16) | (hi & jnp.uint32(0xFFFF0000)), jnp.int32)


def _unpack_rows(p):
    u = lax.bitcast_convert_type(p, jnp.uint32)
    lo = lax.bitcast_convert_type(u << 16, F32)
    hi = lax.bitcast_convert_type(u & jnp.uint32(0xFFFF0000), F32)
    return jnp.concatenate([lo, hi], axis=1)


def _nt_dot(a, b):
    return lax.dot_general(a, b, (((1,), (1,)), ((), ())), preferred_element_type=F32)


def _tn_dot(a, b):
    return lax.dot_general(a, b, (((0,), (0,)), ((), ())), preferred_element_type=F32)


def _ada_kernel(c_ref, w_ref, b_ref, o_ref):
    c = c_ref[...]
    sc = (c * _sigmoid(c)).astype(BF16)
    o_ref[...] = jnp.dot(sc, w_ref[...].astype(BF16), preferred_element_type=F32) + b_ref[...]


def _ada(c, w_ada, b_ada):
    bsz, d = c.shape
    n = w_ada.shape[1]
    return pl.pallas_call(
        _ada_kernel,
        out_shape=jax.ShapeDtypeStruct((bsz, n), F32),
        grid=(n // d,),
        in_specs=[pl.BlockSpec((bsz, d), lambda j: (0, 0)),
                  pl.BlockSpec((d, d), lambda j: (0, j)),
                  pl.BlockSpec((1, d), lambda j: (0, j))],
        out_specs=pl.BlockSpec((bsz, d), lambda j: (0, j)),
        name="ada",
    )(c, w_ada, b_ada.reshape(1, n))


def _inproj_kernel(x_ref, mod_ref, nw_ref, w_ref, o_ref, u_scr):
    y = _rms(x_ref[...], nw_ref[...])
    shift = mod_ref[0, 0:1, :]
    scale = mod_ref[0, 1:2, :]
    u_scr[...] = (y * (1.0 + scale) + shift).astype(BF16)

    step = 512
    for j in range(IN_COLS // step):
        o_ref[:, j * step:(j + 1) * step] = jnp.dot(
            u_scr[...], w_ref[:, j * step:(j + 1) * step], preferred_element_type=F32).astype(BF16)


def _inproj(x2, mod, norm_w, w_in_bf, seq, tm):
    n = x2.shape[0]
    per_b = seq // tm
    return pl.pallas_call(
        _inproj_kernel,
        out_shape=jax.ShapeDtypeStruct((n, IN_COLS), BF16),
        grid=(n // tm,),
        in_specs=[pl.BlockSpec((tm, D_MODEL), lambda i: (i, 0)),
                  pl.BlockSpec((1, 6, D_MODEL), lambda i: (i // per_b, 0, 0)),
                  pl.BlockSpec((1, D_MODEL), lambda i: (0, 0)),
                  pl.BlockSpec((D_MODEL, IN_COLS), lambda i: (0, 0))],
        out_specs=pl.BlockSpec((tm, IN_COLS), lambda i: (i, 0)),
        scratch_shapes=[pltpu.VMEM((tm, D_MODEL), BF16)],
        compiler_params=pltpu.CompilerParams(dimension_semantics=("arbitrary",),
                                             vmem_limit_bytes=VMEM_LIMIT),
        name="inproj",
    )(x2, mod, norm_w.reshape(1, D_MODEL), w_in_bf)


_HEADS = [slice(h * HEAD_DIM, (h + 1) * HEAD_DIM) for h in range(HGRN_HEADS)]


def _chunk_rows(c):
    if isinstance(c, int):
        return pl.ds(c * CHUNK, CHUNK)
    return pl.ds(pl.multiple_of(c * CHUNK, CHUNK), CHUNK)


def _hgrn_gates(c, lb, tri, f_ref, b_scr, k_scr, f_scr):
    fx = f_ref[_chunk_rows(c), :].astype(F32)
    f = lb + (1.0 - lb) * _sigmoid(fx)
    g = jnp.maximum(jnp.log(f), -128.0) * LOG2_E
    g1 = g.astype(BF16)
    r1 = g - g1.astype(F32)
    g2 = r1.astype(BF16)
    g3 = (r1 - g2.astype(F32)).astype(BF16)
    b_scr[...] = (jnp.dot(tri, g1, preferred_element_type=F32)
                  + jnp.dot(tri, g2, preferred_element_type=F32)
                  + jnp.dot(tri, g3, preferred_element_type=F32))
    f_scr[...] = f
    k_scr[...] = 1.0 - f


def _hgrn_scores(c, q_ref, b_scr, k_scr, f_scr):
    rows = _chunk_rows(c)
    qs = [q_ref[rows, hs].astype(F32) for hs in _HEADS]
    s_mats = [_level_scores(1, qs[h], _HEADS[h], b_scr, k_scr, f_scr) for h in range(HGRN_HEADS)]
    for lvl in range(2, 7):
        for h in range(HGRN_HEADS):
            s_mats[h] = s_mats[h] + _level_scores(lvl, qs[h], _HEADS[h], b_scr, k_scr, f_scr)
    return s_mats


def _hgrn_outputs(c, s_mats, q_ref, v_ref, st_scr, o_scr, b_scr, k_scr):
    rows = _chunk_rows(c)
    for h, hs in enumerate(_HEADS):
        q = q_ref[rows, hs].astype(F32)
        v_bf = v_ref[rows, hs]
        b = b_scr[:, hs]
        kk = k_scr[:, hs]
        b_last = b_scr[CHUNK - 1:CHUNK, hs]
        st = st_scr[h]
        qd = (q * jnp.exp2(b)).astype(BF16)
        kdec = (kk * jnp.exp2(b_last - b)).astype(BF16)
        diag = jnp.sum(q * kk, axis=-1, keepdims=True)
        o_scr[rows, hs] = (_nt_dot(qd, st.astype(BF16))
                           + jnp.dot(s_mats[h].astype(BF16), v_bf, preferred_element_type=F32)
                           + diag * v_bf.astype(F32))
        st_scr[h] = jnp.exp2(b_last) * st + _tn_dot(v_bf, kdec)


def _level_scores(lvl, q, hs, b_scr, k_scr, f_scr):
    row = lax.broadcasted_iota(jnp.int32, (CHUNK, 1), 0)
    col = lax.broadcasted_iota(jnp.int32, (1, CHUNK), 1)
    b = b_scr[:, hs]
    kk = k_scr[:, hs]
    blk = 1 << lvl
    half = blk // 2
    if lvl == 1:
        odd = (row & 1) == 1
        qx = jnp.where(odd, q * f_scr[:, hs], 0.0).astype(BF16)
        kx = jnp.where(odd, 0.0, kk).astype(BF16)
        return jnp.where((row >> 1) == (col >> 1), _nt_dot(qx, kx), 0.0)
    if half < 8:
        groups = []
        sub = lax.broadcasted_iota(jnp.int32, (8, 1), 0)
        for j in range(CHUNK // 8):
            rj = None
            for k in reversed(range(8 // blk)):
                m = 8 * j + k * blk + half - 1
                bm = jnp.broadcast_to(b_scr[m:m + 1, hs], (8, HEAD_DIM))
                rj = bm if rj is None else jnp.where(sub < (k + 1) * blk, bm, rj)
            groups.append(rj)
        ref = jnp.concatenate(groups, axis=0)
        second = (row & (blk - 1)) >= half
        qx = (q * jnp.exp2(jnp.where(second, b - ref, -jnp.inf))).astype(BF16)
        kx = (kk * jnp.exp2(jnp.where(second, -jnp.inf, ref - b))).astype(BF16)
        return jnp.where((row >> lvl) == (col >> lvl), _nt_dot(qx, kx), 0.0)
    n_blk = CHUNK // blk
    qparts, kparts = [], []
    for j in range(n_blk):
        m = j * blk + half - 1
        bm = b_scr[m:m + 1, hs]
        tq = slice(j * blk + half, (j + 1) * blk)
        tk = slice(j * blk, j * blk + half)
        qparts.append(q[tq] * jnp.exp2(b[tq] - bm))
        kparts.append(kk[tk] * jnp.exp2(bm - b[tk]))
        kparts.append(jnp.zeros((half, HEAD_DIM), F32))
    qx = jnp.concatenate(qparts, axis=0).astype(BF16)
    kx = jnp.concatenate(kparts, axis=0).astype(BF16)
    sc = _nt_dot(qx, kx)
    if n_blk > 1:
        crow = lax.broadcasted_iota(jnp.int32, (CHUNK // 2, 1), 0)
        sc = jnp.where((crow // half) == (col >> lvl), sc, 0.0)
    pieces = []
    for j in range(n_blk):
        pieces.append(jnp.zeros((half, CHUNK), F32))
        pieces.append(sc[j * half:(j + 1) * half])
    return jnp.concatenate(pieces, axis=0)


def _mix_kernel(lbt_ref, q_ref, f_ref, i_ref, g_ref, cb_ref, cc_ref, ch_ref, ga0_ref, ga1_ref,
                gb0_ref, gb1_ref, x_ref, mod_ref, hnw_ref, cw_ref, wa_ref, wb_ref, wm_ref,
                o_ref, st_scr, o_scr, carry_scr, b_scr, k_scr, f_scr, *, layer, rows_per_step, epi_rows):
    @pl.when(pl.program_id(1) == 0)
    def _():
        st_scr[...] = jnp.zeros_like(st_scr)
        carry_scr[0:8, :] = jnp.zeros((8, CONV_WIDTH), F32)

    tab = lbt_ref[...]
    tmax = jnp.max(tab, axis=0, keepdims=True)
    te = jnp.exp(tab - tmax)
    lb = jnp.sum(te[0:layer + 1], axis=0, keepdims=True) / jnp.sum(te, axis=0, keepdims=True)

    ri = lax.broadcasted_iota(jnp.int32, (CHUNK, CHUNK), 0)
    ci = lax.broadcasted_iota(jnp.int32, (CHUNK, CHUNK), 1)
    tri = jnp.where(ci <= ri, 1.0, 0.0).astype(BF16)

    n_chunks = rows_per_step // CHUNK

    def slot(s):
        return b_scr.at[s], k_scr.at[s], f_scr.at[s]

    def chunk(c, c_next, s):
        bs, ks, fs = slot(s)
        s_mats = _hgrn_scores(c, q_ref, bs, ks, fs)
        _hgrn_gates(c_next, lb, tri, f_ref, *slot(1 - s))
        _hgrn_outputs(c, s_mats, q_ref, i_ref, st_scr, o_scr, bs, ks)

    def pair_body(i, carry):
        chunk(2 * i, 2 * i + 1, 0)
        chunk(2 * i + 1, jnp.minimum(2 * i + 2, n_chunks - 1), 1)
        return carry

    _hgrn_gates(0, lb, tri, f_ref, *slot(0))
    lax.fori_loop(0, n_chunks // 2, pair_body, 0)

    gate_m = mod_ref[0, 2:3, :]
    hnw = hnw_ref[...]
    cw0 = cw_ref[0:1, :]
    cw1 = cw_ref[1:2, :]
    cw2 = cw_ref[2:3, :]
    for r in range(rows_per_step // epi_rows):
        rs = slice(r * epi_rows, (r + 1) * epi_rows)
        parts = []
        for h in range(HGRN_HEADS):
            hs = slice(h * HEAD_DIM, (h + 1) * HEAD_DIM)
            oh = _rms(o_scr[rs, hs], hnw)
            go = g_ref[rs, hs].astype(F32)
            parts.append((oh * (go * _sigmoid(go))).astype(BF16))
        ya = jnp.dot(jnp.concatenate(parts, axis=1), wa_ref[...], preferred_element_type=F32)
        uc = cc_ref[rs, :].astype(F32) * ch_ref[rs, :].astype(F32)
        carry_scr[8:8 + epi_rows, :] = uc
        s1 = carry_scr[7:7 + epi_rows, :]
        s2 = carry_scr[6:6 + epi_rows, :]
        carry_scr[0:8, :] = uc[epi_rows - 8:epi_rows, :]
        yc = cb_ref[rs, :].astype(F32) * (cw2 * uc + cw1 * s1 + cw0 * s2)
        yb = jnp.dot(yc.astype(BF16), wb_ref[...], preferred_element_type=F32)
        ga = jnp.concatenate([ga0_ref[rs, :], ga1_ref[rs, :]], axis=1).astype(F32)
        gb = jnp.concatenate([gb0_ref[rs, :], gb1_ref[rs, :]], axis=1).astype(F32)
        merged = (_sigmoid(ga) * ya + _sigmoid(gb) * yb).astype(BF16)
        o_ref[rs, :] = x_ref[rs, :] + gate_m * jnp.dot(merged, wm_ref[...], preferred_element_type=F32)


def _mix(proj, x2, mod, lb_table, hgrn_norm_w, conv_w, wa, wb, wm, layer, bsz, seq, tt):
    n = x2.shape[0]
    per_b = seq // tt

    def col(col_block):
        return pl.BlockSpec((tt, 512), lambda b, t: (b * per_b + t, col_block))

    const = lambda shape: pl.BlockSpec(shape, lambda b, t: (0,) * len(shape))
    in_specs = [
        const(lb_table.shape),
        col(0), col(1), col(2), col(3),
        col(4), col(5), col(6),
        col(7), col(8), col(9), col(10),
        pl.BlockSpec((tt, D_MODEL), lambda b, t: (b * per_b + t, 0)),
        pl.BlockSpec((1, 6, D_MODEL), lambda b, t: (b, 0, 0)),
        const((1, HEAD_DIM)), const((CONV_K, CONV_WIDTH)),
        const((HGRN_WIDTH, D_MODEL)), const((CONV_WIDTH, D_MODEL)), const((D_MODEL, D_MODEL)),
    ]
    kern = functools.partial(_mix_kernel, layer=layer, rows_per_step=tt, epi_rows=tt)
    return pl.pallas_call(
        kern,
        out_shape=jax.ShapeDtypeStruct((n, D_MODEL), F32),
        grid=(bsz, per_b),
        in_specs=in_specs,
        out_specs=pl.BlockSpec((tt, D_MODEL), lambda b, t: (b * per_b + t, 0)),
        scratch_shapes=[pltpu.VMEM((HGRN_HEADS, HEAD_DIM, HEAD_DIM), F32),
                        pltpu.VMEM((tt, HGRN_WIDTH), F32),
                        pltpu.VMEM((8 + tt, CONV_WIDTH), F32),
                        pltpu.VMEM((2, CHUNK, HGRN_WIDTH), F32),
                        pltpu.VMEM((2, CHUNK, HGRN_WIDTH), F32),
                        pltpu.VMEM((2, CHUNK, HGRN_WIDTH), F32)],
        compiler_params=pltpu.CompilerParams(dimension_semantics=("arbitrary", "arbitrary"),
                                             vmem_limit_bytes=VMEM_LIMIT),
        name="mix",
    )(lb_table, *([proj] * 11),
      x2, mod, hgrn_norm_w.reshape(1, HEAD_DIM), conv_w, wa, wb, wm)


def _route_kernel(h_ref, mod_ref, nw_ref, wr_ref, br_ref, u_ref, oh4_ref, rk_ref, pw_ref, cnt_ref, *, tr):
    shift = mod_ref[0, 3:4, :]
    scale = mod_ref[0, 4:5, :]
    u = _rms(h_ref[...], nw_ref[...]) * (1.0 + scale) + shift
    u_ref[...] = _pack_rows(u)
    lane = lax.broadcasted_iota(jnp.int32, (tr, LANES), 1).astype(F32)
    logits = jnp.dot(u.astype(BF16), wr_ref[...], preferred_element_type=F32) + br_ref[...]
    logits = jnp.where(lane < N_EXPERTS, logits, -jnp.inf)
    idx, val = [], []
    cur = logits
    for _ in range(TOP_K):
        m = jnp.max(cur, axis=-1, keepdims=True)
        i = jnp.min(jnp.where(cur == m, lane, float(LANES)), axis=-1, keepdims=True)
        idx.append(i)
        val.append(m)
        cur = jnp.where(lane == i, -jnp.inf, cur)
    ex = [jnp.exp(v - val[0]) for v in val]
    den = ex[0] + ex[1] + ex[2] + ex[3]
    onehot = jnp.zeros((tr, LANES), F32)
    for i in idx:
        onehot = onehot + jnp.where(lane == i, 1.0, 0.0)
    ri = lax.broadcasted_iota(jnp.int32, (tr, tr), 0)
    ci = lax.broadcasted_iota(jnp.int32, (tr, tr), 1)
    tri = jnp.where(ci < ri, 1.0, 0.0).astype(BF16)
    pref = jnp.dot(tri, onehot.astype(BF16), preferred_element_type=F32)
    rk = jnp.zeros((tr, LANES), jnp.int32)
    pw = jnp.zeros((tr, LANES), F32)
    oh4 = jnp.zeros((tr, LANES), F32)
    for j in range(TOP_K):
        rank = jnp.sum(jnp.where(lane == idx[j], pref, 0.0), axis=-1, keepdims=True).astype(jnp.int32)
        rk = jnp.where(lane == j, rank, rk)
        pw = jnp.where(lane == j, ex[j] / den, pw)
        oh4 = oh4 + jnp.where(lane == idx[j] + N_EXPERTS * j, 1.0, 0.0)
    oh4_ref[...] = oh4.astype(BF16)
    rk_ref[...] = rk
    pw_ref[...] = pw
    cnt_ref[0] = jnp.sum(onehot, axis=0, keepdims=True).astype(jnp.int32)


def _route(h1, mod, norm_w, wr_pad, br_pad, seq, tr, tile0, n):
    per_b = seq // tr
    nt = n // tr
    return pl.pallas_call(
        functools.partial(_route_kernel, tr=tr),
        out_shape=(jax.ShapeDtypeStruct((n, PACKED), jnp.int32),
                   jax.ShapeDtypeStruct((n, LANES), BF16),
                   jax.ShapeDtypeStruct((n, LANES), jnp.int32),
                   jax.ShapeDtypeStruct((n, LANES), F32),
                   jax.ShapeDtypeStruct((nt, 1, LANES), jnp.int32)),
        grid=(nt,),
        in_specs=[pl.BlockSpec((tr, D_MODEL), lambda i: (tile0 + i, 0)),
                  pl.BlockSpec((1, 6, D_MODEL), lambda i: ((tile0 + i) // per_b, 0, 0)),
                  pl.BlockSpec((1, D_MODEL), lambda i: (0, 0)),
                  pl.BlockSpec((D_MODEL, LANES), lambda i: (0, 0)),
                  pl.BlockSpec((1, LANES), lambda i: (0, 0))],
        out_specs=(pl.BlockSpec((tr, PACKED), lambda i: (i, 0)),
                   pl.BlockSpec((tr, LANES), lambda i: (i, 0)),
                   pl.BlockSpec((tr, LANES), lambda i: (i, 0)),
                   pl.BlockSpec((tr, LANES), lambda i: (i, 0)),
                   pl.BlockSpec((1, 1, LANES), lambda i: (i, 0, 0))),
        compiler_params=pltpu.CompilerParams(dimension_semantics=("arbitrary",),
                                             vmem_limit_bytes=VMEM_LIMIT),
        name="route",
    )(h1, mod, norm_w.reshape(1, D_MODEL), wr_pad, br_pad)


def _dest_kernel(oh4_ref, rk_ref, bt_ref, o_ref):
    oh = oh4_ref[...]
    start = (jnp.dot(oh, bt_ref[0, 0], preferred_element_type=F32)
             + 256.0 * jnp.dot(oh, bt_ref[0, 1], preferred_element_type=F32)
             + 65536.0 * jnp.dot(oh, bt_ref[0, 2], preferred_element_type=F32))
    o_ref[...] = (start + rk_ref[...].astype(F32)).T[:8, :].astype(jnp.int32)


def _dest(oh4, rk, bt, tr):
    n = oh4.shape[0]
    return pl.pallas_call(
        _dest_kernel,
        out_shape=jax.ShapeDtypeStruct((8, n), jnp.int32),
        grid=(n // tr,),
        in_specs=[pl.BlockSpec((tr, LANES), lambda i: (i, 0)),
                  pl.BlockSpec((tr, LANES), lambda i: (i, 0)),
                  pl.BlockSpec((1, 3, LANES, LANES), lambda i: (i, 0, 0, 0))],
        out_specs=pl.BlockSpec((8, tr), lambda i: (0, i)),
        name="dest",
    )(oh4, rk, bt)


def _sc_workers():
    info = plsc.get_sparse_core_info()
    return info.num_cores, info.num_cores * info.num_subcores


def _sc_scatter_rows(rows, idx_slots, pad_idx, n_out):
    n_cores, n_workers = _sc_workers()
    n, w = rows.shape
    k = idx_slots.shape[0] // n
    per_worker = n // n_workers
    pad_per_worker = pad_idx.shape[0] // n_workers
    assert per_worker % SC_CHUNK == 0 and pad_per_worker % SC_CHUNK == 0
    mesh = plsc.VectorSubcoreMesh(core_axis_name="c", subcore_axis_name="s")
    zeros = jnp.zeros((SC_CHUNK, w), rows.dtype)

    n_chunks = per_worker // SC_CHUNK
    assert n_chunks % 2 == 0

    @functools.partial(
        pl.kernel, mesh=mesh,
        out_type=jax.ShapeDtypeStruct((n_out, w), rows.dtype),
        scratch_types=[pltpu.VMEM((SC_CHUNK,), jnp.int32)] * k
        + [pltpu.VMEM((SC_CHUNK, w), rows.dtype)] * 2
        + [pltpu.SemaphoreType.DMA] * 2,
        name="sc_scatter",
    )
    def scatter(rows_hbm, idx_hbm, pad_hbm, zeros_hbm, out_hbm, *scratch):
        idx_bufs = scratch[:k]
        row_bufs = scratch[k:k + 2]
        sem_rows, sem_out = scratch[k + 2:]
        wid = lax.axis_index("s") * n_cores + lax.axis_index("c")

        def chunk_off(c):
            return pl.multiple_of(wid * per_worker + c * SC_CHUNK, 8)

        def load(c, b):
            pltpu.async_copy(rows_hbm.at[pl.ds(chunk_off(c), SC_CHUNK)], row_bufs[b], sem_rows)

        def wait_load(b):
            pltpu.make_async_copy(rows_hbm.at[pl.ds(0, SC_CHUNK)], row_bufs[b], sem_rows).wait()

        def scatter_chunk(c, b):
            for j in range(k):
                pltpu.sync_copy(idx_hbm.at[pl.ds(pl.multiple_of(j * n + chunk_off(c), 8), SC_CHUNK)],
                                idx_bufs[j])
            copies = [pltpu.async_copy(row_bufs[b], out_hbm.at[idx_bufs[j]], sem_out) for j in range(k)]
            for cp in copies:
                cp.wait()

        load(0, 0)

        def body(i, carry):
            wait_load(0)
            load(2 * i + 1, 1)
            scatter_chunk(2 * i, 0)
            wait_load(1)

            @pl.when(i < n_chunks // 2 - 1)
            def _():
                load(2 * i + 2, 0)

            scatter_chunk(2 * i + 1, 1)
            return carry

        lax.fori_loop(0, n_chunks // 2, body, 0)
        pltpu.sync_copy(zeros_hbm, row_bufs[0])

        def pad_body(i, carry):
            off = pl.multiple_of(wid * pad_per_worker + i * SC_CHUNK, 8)
            pltpu.sync_copy(pad_hbm.at[pl.ds(off, SC_CHUNK)], idx_bufs[0])
            pltpu.async_copy(row_bufs[0], out_hbm.at[idx_bufs[0]], sem_out).wait()
            return carry

        lax.fori_loop(0, pad_per_worker // SC_CHUNK, pad_body, 0)

    return scatter(rows, idx_slots, pad_idx, zeros)


def _experts_kernel(be_ref, nu_ref, first_ref, next_ref, slot_ref, quarters_ref, x_ref, w1_hbm, b1_ref, w2_hbm,
                    b2_ref, o_ref, w1_f32, w2_f32, w1_bf, w2_bf, sems):
    i = pl.program_id(0)
    used = i < nu_ref[0]

    def weight_copies(e, s):
        return (pltpu.make_async_copy(w1_hbm.at[e], w1_f32.at[s], sems.at[s]),
                pltpu.make_async_copy(w2_hbm.at[e], w2_f32.at[s], sems.at[s]))

    @pl.when(used & (first_ref[i] == 1))
    def _():
        s = slot_ref[i]

        @pl.when(i == 0)
        def _():
            for cp in weight_copies(be_ref[i], s):
                cp.start()

        for cp in weight_copies(be_ref[i], s):
            cp.wait()
        cw = 256
        for c in range(2 * D_FF // cw):
            w1_bf[:, c * cw:(c + 1) * cw] = w1_f32[s, :, c * cw:(c + 1) * cw].astype(BF16)
        for c in range(D_MODEL // cw):
            w2_bf[:, c * cw:(c + 1) * cw] = w2_f32[s, :, c * cw:(c + 1) * cw].astype(BF16)

        @pl.when(next_ref[i] >= 0)
        def _():
            for cp in weight_copies(next_ref[i], 1 - s):
                cp.start()

    def mlp(n_rows):
        step = 512
        x = _unpack_rows(x_ref[0:n_rows, :]).astype(BF16)
        acc = None
        for j in range(D_FF // step):
            cs = slice(j * step, (j + 1) * step)
            ls = slice(D_FF + j * step, D_FF + (j + 1) * step)
            glu = jnp.dot(x, w1_bf[:, cs], preferred_element_type=F32) + b1_ref[0, :, cs]
            lin = jnp.dot(x, w1_bf[:, ls], preferred_element_type=F32) + b1_ref[0, :, ls]
            glu = jnp.minimum(glu, SWIGLU_LIMIT)
            lin = jnp.clip(lin, -SWIGLU_LIMIT, SWIGLU_LIMIT)
            act = (glu * _sigmoid(glu, SWIGLU_ALPHA) * (lin + 1.0)).astype(BF16)
            part = jnp.dot(act, w2_bf[cs, :], preferred_element_type=F32)
            acc = part if acc is None else acc + part
        o_ref[0:n_rows, :] = _pack_rows(acc + b2_ref[0])
        if n_rows < ROW_BLOCK:
            o_ref[n_rows:ROW_BLOCK, :] = jnp.zeros((ROW_BLOCK - n_rows, PACKED), jnp.int32)

    for quarters in range(1, 5):
        @pl.when(used & (quarters_ref[i] == quarters))
        def _():
            mlp(quarters * (ROW_BLOCK // 4))


def _experts(plan, xs, w1, b1, w2, b2):
    n_rows = xs.shape[0]
    nb = n_rows // ROW_BLOCK
    grid_spec = pltpu.PrefetchScalarGridSpec(
        num_scalar_prefetch=6,
        grid=(nb,),
        in_specs=[pl.BlockSpec((ROW_BLOCK, PACKED), lambda i, be, nu, *_: (jnp.minimum(i, nu[0] - 1), 0)),
                  pl.BlockSpec(memory_space=pl.ANY),
                  pl.BlockSpec((1, 1, 2 * D_FF), lambda i, be, *_: (be[i], 0, 0)),
                  pl.BlockSpec(memory_space=pl.ANY),
                  pl.BlockSpec((1, 1, D_MODEL), lambda i, be, *_: (be[i], 0, 0))],
        out_specs=pl.BlockSpec((ROW_BLOCK, PACKED), lambda i, be, nu, *_: (jnp.minimum(i, nu[0] - 1), 0)),
        scratch_shapes=[pltpu.VMEM((2, D_MODEL, 2 * D_FF), F32), pltpu.VMEM((2, D_FF, D_MODEL), F32),
                        pltpu.VMEM((D_MODEL, 2 * D_FF), BF16), pltpu.VMEM((D_FF, D_MODEL), BF16),
                        pltpu.SemaphoreType.DMA((2,))],
    )
    return pl.pallas_call(
        _experts_kernel,
        out_shape=jax.ShapeDtypeStruct((n_rows, PACKED), jnp.int32),
        grid_spec=grid_spec,
        compiler_params=pltpu.CompilerParams(dimension_semantics=("arbitrary",),
                                             vmem_limit_bytes=VMEM_LIMIT),
        name="experts",
    )(*plan, xs, w1, b1.reshape(N_EXPERTS, 1, 2 * D_FF), w2, b2.reshape(N_EXPERTS, 1, D_MODEL))


def _sc_gather_rows(table, idx_flat):
    n_cores, n_workers = _sc_workers()
    n_idx = idx_flat.shape[0]
    w = table.shape[1]
    per_worker = n_idx // n_workers
    n_chunks = per_worker // SC_CHUNK
    assert per_worker * n_workers == n_idx and n_chunks * SC_CHUNK == per_worker and n_chunks % 2 == 0
    mesh = plsc.VectorSubcoreMesh(core_axis_name="c", subcore_axis_name="s")

    @functools.partial(
        pl.kernel, mesh=mesh,
        out_type=jax.ShapeDtypeStruct((n_idx, w), table.dtype),
        scratch_types=[pltpu.VMEM((SC_CHUNK,), jnp.int32), pltpu.VMEM((SC_CHUNK,), jnp.int32),
                       pltpu.VMEM((SC_CHUNK, w), table.dtype), pltpu.VMEM((SC_CHUNK, w), table.dtype),
                       pltpu.SemaphoreType.DMA, pltpu.SemaphoreType.DMA],
        name="sc_gather",
    )
    def gather(table_hbm, idx_hbm, out_hbm, idx0, idx1, rows0, rows1, sem0, sem1):
        wid = lax.axis_index("s") * n_cores + lax.axis_index("c")
        bufs = ((idx0, rows0, sem0), (idx1, rows1, sem1))

        def chunk_off(c):
            return pl.multiple_of(wid * per_worker + c * SC_CHUNK, 8)

        def start(c, b):
            idx_v, rows_v, sem = bufs[b]
            pltpu.sync_copy(idx_hbm.at[pl.ds(chunk_off(c), SC_CHUNK)], idx_v)
            pltpu.async_copy(table_hbm.at[idx_v], rows_v, sem)

        def finish(c, b):
            idx_v, rows_v, sem = bufs[b]
            pltpu.make_async_copy(table_hbm.at[idx_v], rows_v, sem).wait()
            pltpu.sync_copy(rows_v, out_hbm.at[pl.ds(chunk_off(c), SC_CHUNK)])

        start(0, 0)

        def body(i, carry):
            start(2 * i + 1, 1)
            finish(2 * i, 0)

            @pl.when(i < n_chunks // 2 - 1)
            def _():
                start(2 * i + 2, 0)

            finish(2 * i + 1, 1)
            return carry

        lax.fori_loop(0, n_chunks // 2, body, 0)

    return gather(table, idx_flat)


def _finish_kernel(h_ref, pw_ref, mod_ref, nw_ref, y0_ref, y1_ref, y2_ref, y3_ref, *rest):
    o_ref = rest[-1]
    pw = pw_ref[...]
    moe = pw[:, 0:1] * _unpack_rows(y0_ref[0])
    for j, y_ref in enumerate((y1_ref, y2_ref, y3_ref), start=1):
        moe = moe + pw[:, j:j + 1] * _unpack_rows(y_ref[0])
    gate_f = mod_ref[0, 5:6, :]
    o_ref[...] = _rms(h_ref[...] + gate_f * moe, nw_ref[...])


def _finish(h1, pw, mod, norm_w, y4, seq, tc, tile0, prev_out):
    n = pw.shape[0]
    per_b = seq // tc
    slot = lambda j: pl.BlockSpec((1, tc, PACKED), lambda i: (j, i, 0))
    in_specs = [pl.BlockSpec((tc, D_MODEL), lambda i: (tile0 + i, 0)),
                pl.BlockSpec((tc, LANES), lambda i: (i, 0)),
                pl.BlockSpec((1, 6, D_MODEL), lambda i: ((tile0 + i) // per_b, 0, 0)),
                pl.BlockSpec((1, D_MODEL), lambda i: (0, 0)),
                slot(0), slot(1), slot(2), slot(3)]
    args = [h1, pw, mod, norm_w.reshape(1, D_MODEL), y4, y4, y4, y4]
    aliases = {}
    if prev_out is not None:
        in_specs.append(pl.BlockSpec(memory_space=pl.ANY))
        args.append(prev_out)
        aliases = {len(args) - 1: 0}
    return pl.pallas_call(
        _finish_kernel,
        out_shape=jax.ShapeDtypeStruct(h1.shape, F32),
        grid=(n // tc,),
        in_specs=in_specs,
        out_specs=pl.BlockSpec((tc, D_MODEL), lambda i: (tile0 + i, 0)),
        input_output_aliases=aliases,
        compiler_params=pltpu.CompilerParams(dimension_semantics=("arbitrary",),
                                             vmem_limit_bytes=VMEM_LIMIT),
        name="finish",
    )(*args)


def _moe_plan(counts, n_assign):
    cnt = counts[:, 0, :N_EXPERTS]
    sizes = jnp.sum(cnt, axis=0)
    padded = (sizes + ROW_BLOCK - 1) // ROW_BLOCK * ROW_BLOCK
    pad_end = jnp.cumsum(padded)
    pad_start = pad_end - padded
    tile_base = pad_start[None, :] + jnp.cumsum(cnt, axis=0) - cnt
    digits = jnp.stack([tile_base % 256, (tile_base // 256) % 256, tile_base // 65536], axis=1)
    rows = jnp.tile(digits, (1, 1, TOP_K))
    slot_of_row = jnp.arange(LANES, dtype=jnp.int32) // N_EXPERTS
    col = jnp.arange(LANES, dtype=jnp.int32)
    bt = jnp.where(slot_of_row[:, None] == col[None, :], rows[..., None], 0).astype(BF16)
    nb = n_assign // ROW_BLOCK + N_EXPERTS
    block_start = jnp.arange(nb, dtype=jnp.int32) * ROW_BLOCK
    block_e = jnp.minimum(jnp.sum(pad_end[None, :] <= block_start[:, None], axis=1),
                          N_EXPERTS - 1).astype(jnp.int32)
    n_used = (pad_end[-1] // ROW_BLOCK).astype(jnp.int32).reshape(1)
    nonempty = padded > 0
    ids = jnp.arange(N_EXPERTS, dtype=jnp.int32)
    suffix_min = lax.cummin(jnp.where(nonempty, ids, N_EXPERTS), reverse=True)
    after = jnp.concatenate([suffix_min[1:], jnp.full((1,), N_EXPERTS, jnp.int32)])
    next_nonempty = jnp.where(after < N_EXPERTS, after, -1).astype(jnp.int32)
    slot_of = ((jnp.cumsum(nonempty) - 1) % 2).astype(jnp.int32)
    first = jnp.concatenate([jnp.ones((1,), jnp.int32),
                             (block_e[1:] != block_e[:-1]).astype(jnp.int32)])
    valid_rows = (pad_start + sizes)[block_e] - block_start
    quarter = ROW_BLOCK // 4
    quarters = jnp.clip((valid_rows + quarter - 1) // quarter, 1, 4).astype(jnp.int32)
    expert_plan = (block_e, n_used, first, next_nonempty[block_e], slot_of[block_e], quarters)
    r = jnp.arange(ROW_BLOCK, dtype=jnp.int32)[None, :]
    pad_idx = jnp.where(r < (padded - sizes)[:, None], (pad_start + sizes)[:, None] + r,
                        (nb - 1) * ROW_BLOCK + r).astype(jnp.int32).reshape(PAD_SLOTS)
    return bt, expert_plan, pad_idx


def kernel(x, c, w_ada, b_ada, norm_mix_w, w_in, hgrn_lower_bounds, hgrn_norm_w, conv_w,
           w_hgrn_out, w_conv_out, w_mix_out, norm_ffn_w, w_router, b_router, w1, b1, w2, b2,
           norm_final_w):
    bsz, seq, d = x.shape
    assert d == D_MODEL and seq % CHUNK == 0
    n = bsz * seq
    depth = w_ada.shape[0]
    tile = min(512, seq)
    assert seq % tile == 0 and n % (MOE_GROUPS * tile) == 0 and (n // MOE_GROUPS * TOP_K) % ROW_BLOCK == 0
    h = x.reshape(n, d)
    wr_pad = jnp.zeros((depth, D_MODEL, LANES), BF16).at[:, :, :N_EXPERTS].set(w_router.astype(BF16))
    br_pad = jnp.zeros((depth, 1, LANES), F32).at[:, 0, :N_EXPERTS].set(b_router)
    for layer in range(depth):
        mod = _ada(c, w_ada[layer], b_ada[layer]).reshape(bsz, 6, d)
        proj = _inproj(h, mod, norm_mix_w[layer], w_in[layer].astype(BF16), seq, tile)
        h = _mix(proj, h, mod, hgrn_lower_bounds, hgrn_norm_w[layer], conv_w[layer],
                 w_hgrn_out[layer].astype(BF16), w_conv_out[layer].astype(BF16),
                 w_mix_out[layer].astype(BF16), layer, bsz, seq, min(MIX_ROWS, seq))
        assert layer == depth - 1, "only the last layer applies the final norm"
        ng = n // MOE_GROUPS
        n_rows = (ng * TOP_K // ROW_BLOCK + N_EXPERTS) * ROW_BLOCK
        routed = [_route(h, mod, norm_ffn_w[layer], wr_pad[layer], br_pad[layer], seq, tile,
                         grp * (ng // tile), ng) for grp in range(MOE_GROUPS)]
        plans = jax.vmap(lambda cnt: _moe_plan(cnt, ng * TOP_K))(jnp.stack([r[4] for r in routed]))
        out = None
        for grp in range(MOE_GROUPS):
            tile0 = grp * (ng // tile)
            u2, oh4, rk, pw, _ = routed[grp]
            bt, expert_plan, pad_idx = jax.tree.map(lambda a: a[grp], plans)
            dest_slots = _dest(oh4, rk, bt, tile)[:TOP_K].reshape(TOP_K * ng)
            xs = _sc_scatter_rows(u2, dest_slots, pad_idx, n_rows)
            ys = _experts(expert_plan, xs, w1[layer], b1[layer], w2[layer], b2[layer])
            y4 = _sc_gather_rows(ys, dest_slots).reshape(TOP_K, ng, PACKED)
            fin = min(MIX_ROWS, seq)
            out = _finish(h, pw, mod, norm_final_w, y4, seq, fin, grp * (ng // fin), out)
        h = out
    return h.reshape(bsz, seq, d)
```

```python
import functools

import jax
import jax.numpy as jnp
from jax import lax
from jax.experimental import pallas as pl
from jax.experimental.pallas import tpu as pltpu
from jax.experimental.pallas import tpu_sc as plsc

F32 = jnp.float32
BF16 = jnp.bfloat16

D_MODEL = 1024
HGRN_HEADS = 4
HEAD_DIM = 128
HGRN_WIDTH = HGRN_HEADS * HEAD_DIM
CONV_WIDTH = 512
CONV_K = 3
CHUNK = 64
N_EXPERTS = 32
TOP_K = 4
D_FF = 1024
SWIGLU_LIMIT = 7.0
SWIGLU_ALPHA = 1.702
EPS = 1e-6
LOG2_E = 1.4426950408889634
IN_COLS = 4 * HGRN_WIDTH + 3 * CONV_WIDTH + 2 * D_MODEL
LANES = 128
MIX_ROWS = 1024
ROW_BLOCK = 1024
PACKED = D_MODEL // 2
MOE_GROUPS = 2
SC_CHUNK = 64
PAD_SLOTS = N_EXPERTS * (ROW_BLOCK // 4)
VMEM_LIMIT = 56 * 1024 * 1024


def _sigmoid(x, scale=1.0):
    return 1.0 / (1.0 + jnp.exp2(x * (-scale * LOG2_E)))


def _rms(x, w):
    ms = jnp.mean(x * x, axis=-1, keepdims=True)
    return x * lax.rsqrt(ms + EPS) * w


def _pack_rows(x):
    w = x.shape[1] // 2
    lo = lax.bitcast_convert_type(x[:, :w].astype(BF16).astype(F32), jnp.uint32)
    hi = lax.bitcast_convert_type(x[:, w:].astype(BF16).astype(F32), jnp.uint32)
    return lax.bitcast_convert_type((lo >> 16) | (hi & jnp.uint32(0xFFFF0000)), jnp.int32)


def _unpack_rows(p):
    u = lax.bitcast_convert_type(p, jnp.uint32)
    lo = lax.bitcast_convert_type(u << 16, F32)
    hi = lax.bitcast_convert_type(u & jnp.uint32(0xFFFF0000), F32)
    return jnp.concatenate([lo, hi], axis=1)


def _nt_dot(a, b):
    return lax.dot_general(a, b, (((1,), (1,)), ((), ())), preferred_element_type=F32)


def _tn_dot(a, b):
    return lax.dot_general(a, b, (((0,), (0,)), ((), ())), preferred_element_type=F32)


def _ada_kernel(c_ref, w_ref, b_ref, o_ref):
    c = c_ref[...]
    sc = (c * _sigmoid(c)).astype(BF16)
    o_ref[...] = jnp.dot(sc, w_ref[...].astype(BF16), preferred_element_type=F32) + b_ref[...]


def _ada(c, w_ada, b_ada):
    bsz, d = c.shape
    n = w_ada.shape[1]
    return pl.pallas_call(
        _ada_kernel,
        out_shape=jax.ShapeDtypeStruct((bsz, n), F32),
        grid=(n // d,),
        in_specs=[pl.BlockSpec((bsz, d), lambda j: (0, 0)),
                  pl.BlockSpec((d, d), lambda j: (0, j)),
                  pl.BlockSpec((1, d), lambda j: (0, j))],
        out_specs=pl.BlockSpec((bsz, d), lambda j: (0, j)),
        name="ada",
    )(c, w_ada, b_ada.reshape(1, n))


def _inproj_kernel(x_ref, mod_ref, nw_ref, w_ref, o_ref, u_scr):
    y = _rms(x_ref[...], nw_ref[...])
    shift = mod_ref[0, 0:1, :]
    scale = mod_ref[0, 1:2, :]
    u_scr[...] = (y * (1.0 + scale) + shift).astype(BF16)

    step = 512
    for j in range(IN_COLS // step):
        o_ref[:, j * step:(j + 1) * step] = jnp.dot(
            u_scr[...], w_ref[:, j * step:(j + 1) * step], preferred_element_type=F32).astype(BF16)


def _inproj(x2, mod, norm_w, w_in_bf, seq, tm):
    n = x2.shape[0]
    per_b = seq // tm
    return pl.pallas_call(
        _inproj_kernel,
        out_shape=jax.ShapeDtypeStruct((n, IN_COLS), BF16),
        grid=(n // tm,),
        in_specs=[pl.BlockSpec((tm, D_MODEL), lambda i: (i, 0)),
                  pl.BlockSpec((1, 6, D_MODEL), lambda i: (i // per_b, 0, 0)),
                  pl.BlockSpec((1, D_MODEL), lambda i: (0, 0)),
                  pl.BlockSpec((D_MODEL, IN_COLS), lambda i: (0, 0))],
        out_specs=pl.BlockSpec((tm, IN_COLS), lambda i: (i, 0)),
        scratch_shapes=[pltpu.VMEM((tm, D_MODEL), BF16)],
        compiler_params=pltpu.CompilerParams(dimension_semantics=("arbitrary",),
                                             vmem_limit_bytes=VMEM_LIMIT),
        name="inproj",
    )(x2, mod, norm_w.reshape(1, D_MODEL), w_in_bf)


_HEADS = [slice(h * HEAD_DIM, (h + 1) * HEAD_DIM) for h in range(HGRN_HEADS)]


def _chunk_rows(c):
    if isinstance(c, int):
        return pl.ds(c * CHUNK, CHUNK)
    return pl.ds(pl.multiple_of(c * CHUNK, CHUNK), CHUNK)


def _hgrn_gates(c, lb, tri, f_ref, b_scr, k_scr, f_scr):
    fx = f_ref[_chunk_rows(c), :].astype(F32)
    f = lb + (1.0 - lb) * _sigmoid(fx)
    g = jnp.maximum(jnp.log(f), -128.0) * LOG2_E
    g1 = g.astype(BF16)
    r1 = g - g1.astype(F32)
    g2 = r1.astype(BF16)
    g3 = (r1 - g2.astype(F32)).astype(BF16)
    b_scr[...] = (jnp.dot(tri, g1, preferred_element_type=F32)
                  + jnp.dot(tri, g2, preferred_element_type=F32)
                  + jnp.dot(tri, g3, preferred_element_type=F32))
    f_scr[...] = f
    k_scr[...] = 1.0 - f


def _hgrn_scores(c, q_ref, b_scr, k_scr, f_scr):
    rows = _chunk_rows(c)
    qs = [q_ref[rows, hs].astype(F32) for hs in _HEADS]
    s_mats = [_level_scores(1, qs[h], _HEADS[h], b_scr, k_scr, f_scr) for h in range(HGRN_HEADS)]
    for lvl in range(2, 7):
        for h in range(HGRN_HEADS):
            s_mats[h] = s_mats[h] + _level_scores(lvl, qs[h], _HEADS[h], b_scr, k_scr, f_scr)
    return s_mats


def _hgrn_outputs(c, s_mats, q_ref, v_ref, st_scr, o_scr, b_scr, k_scr):
    rows = _chunk_rows(c)
    for h, hs in enumerate(_HEADS):
        q = q_ref[rows, hs].astype(F32)
        v_bf = v_ref[rows, hs]
        b = b_scr[:, hs]
        kk = k_scr[:, hs]
        b_last = b_scr[CHUNK - 1:CHUNK, hs]
        st = st_scr[h]
        qd = (q * jnp.exp2(b)).astype(BF16)
        kdec = (kk * jnp.exp2(b_last - b)).astype(BF16)
        diag = jnp.sum(q * kk, axis=-1, keepdims=True)
        o_scr[rows, hs] = (_nt_dot(qd, st.astype(BF16))
                           + jnp.dot(s_mats[h].astype(BF16), v_bf, preferred_element_type=F32)
                           + diag * v_bf.astype(F32))
        st_scr[h] = jnp.exp2(b_last) * st + _tn_dot(v_bf, kdec)


def _level_scores(lvl, q, hs, b_scr, k_scr, f_scr):
    row = lax.broadcasted_iota(jnp.int32, (CHUNK, 1), 0)
    col = lax.broadcasted_iota(jnp.int32, (1, CHUNK), 1)
    b = b_scr[:, hs]
    kk = k_scr[:, hs]
    blk = 1 << lvl
    half = blk // 2
    if lvl == 1:
        odd = (row & 1) == 1
        qx = jnp.where(odd, q * f_scr[:, hs], 0.0).astype(BF16)
        kx = jnp.where(odd, 0.0, kk).astype(BF16)
        return jnp.where((row >> 1) == (col >> 1), _nt_dot(qx, kx), 0.0)
    if half < 8:
        groups = []
        sub = lax.broadcasted_iota(jnp.int32, (8, 1), 0)
        for j in range(CHUNK // 8):
            rj = None
            for k in reversed(range(8 // blk)):
                m = 8 * j + k * blk + half - 1
                bm = jnp.broadcast_to(b_scr[m:m + 1, hs], (8, HEAD_DIM))
                rj = bm if rj is None else jnp.where(sub < (k + 1) * blk, bm, rj)
            groups.append(rj)
        ref = jnp.concatenate(groups, axis=0)
        second = (row & (blk - 1)) >= half
        qx = (q * jnp.exp2(jnp.where(second, b - ref, -jnp.inf))).astype(BF16)
        kx = (kk * jnp.exp2(jnp.where(second, -jnp.inf, ref - b))).astype(BF16)
        return jnp.where((row >> lvl) == (col >> lvl), _nt_dot(qx, kx), 0.0)
    n_blk = CHUNK // blk
    qparts, kparts = [], []
    for j in range(n_blk):
        m = j * blk + half - 1
        bm = b_scr[m:m + 1, hs]
        tq = slice(j * blk + half, (j + 1) * blk)
        tk = slice(j * blk, j * blk + half)
        qparts.append(q[tq] * jnp.exp2(b[tq] - bm))
        kparts.append(kk[tk] * jnp.exp2(bm - b[tk]))
        kparts.append(jnp.zeros((half, HEAD_DIM), F32))
    qx = jnp.concatenate(qparts, axis=0).astype(BF16)
    kx = jnp.concatenate(kparts, axis=0).astype(BF16)
    sc = _nt_dot(qx, kx)
    if n_blk > 1:
        crow = lax.broadcasted_iota(jnp.int32, (CHUNK // 2, 1), 0)
        sc = jnp.where((crow // half) == (col >> lvl), sc, 0.0)
    pieces = []
    for j in range(n_blk):
        pieces.append(jnp.zeros((half, CHUNK), F32))
        pieces.append(sc[j * half:(j + 1) * half])
    return jnp.concatenate(pieces, axis=0)


def _mix_kernel(lbt_ref, q_ref, f_ref, i_ref, g_ref, cb_ref, cc_ref, ch_ref, ga0_ref, ga1_ref,
                gb0_ref, gb1_ref, x_ref, mod_ref, hnw_ref, cw_ref, wa_ref, wb_ref, wm_ref,
                o_ref, st_scr, o_scr, carry_scr, b_scr, k_scr, f_scr, *, layer, rows_per_step, epi_rows):
    @pl.when(pl.program_id(1) == 0)
    def _():
        st_scr[...] = jnp.zeros_like(st_scr)
        carry_scr[0:8, :] = jnp.zeros((8, CONV_WIDTH), F32)

    tab = lbt_ref[...]
    tmax = jnp.max(tab, axis=0, keepdims=True)
    te = jnp.exp(tab - tmax)
    lb = jnp.sum(te[0:layer + 1], axis=0, keepdims=True) / jnp.sum(te, axis=0, keepdims=True)

    ri = lax.broadcasted_iota(jnp.int32, (CHUNK, CHUNK), 0)
    ci = lax.broadcasted_iota(jnp.int32, (CHUNK, CHUNK), 1)
    tri = jnp.where(ci <= ri, 1.0, 0.0).astype(BF16)

    n_chunks = rows_per_step // CHUNK

    def slot(s):
        return b_scr.at[s], k_scr.at[s], f_scr.at[s]

    def chunk(c, c_next, s):
        bs, ks, fs = slot(s)
        s_mats = _hgrn_scores(c, q_ref, bs, ks, fs)
        _hgrn_gates(c_next, lb, tri, f_ref, *slot(1 - s))
        _hgrn_outputs(c, s_mats, q_ref, i_ref, st_scr, o_scr, bs, ks)

    def pair_body(i, carry):
        chunk(2 * i, 2 * i + 1, 0)
        chunk(2 * i + 1, jnp.minimum(2 * i + 2, n_chunks - 1), 1)
        return carry

    _hgrn_gates(0, lb, tri, f_ref, *slot(0))
    lax.fori_loop(0, n_chunks // 2, pair_body, 0)

    gate_m = mod_ref[0, 2:3, :]
    hnw = hnw_ref[...]
    cw0 = cw_ref[0:1, :]
    cw1 = cw_ref[1:2, :]
    cw2 = cw_ref[2:3, :]
    for r in range(rows_per_step // epi_rows):
        rs = slice(r * epi_rows, (r + 1) * epi_rows)
        parts = []
        for h in range(HGRN_HEADS):
            hs = slice(h * HEAD_DIM, (h + 1) * HEAD_DIM)
            oh = _rms(o_scr[rs, hs], hnw)
            go = g_ref[rs, hs].astype(F32)
            parts.append((oh * (go * _sigmoid(go))).astype(BF16))
        ya = jnp.dot(jnp.concatenate(parts, axis=1), wa_ref[...], preferred_element_type=F32)
        uc = cc_ref[rs, :].astype(F32) * ch_ref[rs, :].astype(F32)
        carry_scr[8:8 + epi_rows, :] = uc
        s1 = carry_scr[7:7 + epi_rows, :]
        s2 = carry_scr[6:6 + epi_rows, :]
        carry_scr[0:8, :] = uc[epi_rows - 8:epi_rows, :]
        yc = cb_ref[rs, :].astype(F32) * (cw2 * uc + cw1 * s1 + cw0 * s2)
        yb = jnp.dot(yc.astype(BF16), wb_ref[...], preferred_element_type=F32)
        ga = jnp.concatenate([ga0_ref[rs, :], ga1_ref[rs, :]], axis=1).astype(F32)
        gb = jnp.concatenate([gb0_ref[rs, :], gb1_ref[rs, :]], axis=1).astype(F32)
        merged = (_sigmoid(ga) * ya + _sigmoid(gb) * yb).astype(BF16)
        o_ref[rs, :] = x_ref[rs, :] + gate_m * jnp.dot(merged, wm_ref[...], preferred_element_type=F32)


def _mix(proj, x2, mod, lb_table, hgrn_norm_w, conv_w, wa, wb, wm, layer, bsz, seq, tt):
    n = x2.shape[0]
    per_b = seq // tt

    def col(col_block):
        return pl.BlockSpec((tt, 512), lambda b, t: (b * per_b + t, col_block))

    const = lambda shape: pl.BlockSpec(shape, lambda b, t: (0,) * len(shape))
    in_specs = [
        const(lb_table.shape),
        col(0), col(1), col(2), col(3),
        col(4), col(5), col(6),
        col(7), col(8), col(9), col(10),
        pl.BlockSpec((tt, D_MODEL), lambda b, t: (b * per_b + t, 0)),
        pl.BlockSpec((1, 6, D_MODEL), lambda b, t: (b, 0, 0)),
        const((1, HEAD_DIM)), const((CONV_K, CONV_WIDTH)),
        const((HGRN_WIDTH, D_MODEL)), const((CONV_WIDTH, D_MODEL)), const((D_MODEL, D_MODEL)),
    ]
    kern = functools.partial(_mix_kernel, layer=layer, rows_per_step=tt, epi_rows=tt)
    return pl.pallas_call(
        kern,
        out_shape=jax.ShapeDtypeStruct((n, D_MODEL), F32),
        grid=(bsz, per_b),
        in_specs=in_specs,
        out_specs=pl.BlockSpec((tt, D_MODEL), lambda b, t: (b * per_b + t, 0)),
        scratch_shapes=[pltpu.VMEM((HGRN_HEADS, HEAD_DIM, HEAD_DIM), F32),
                        pltpu.VMEM((tt, HGRN_WIDTH), F32),
                        pltpu.VMEM((8 + tt, CONV_WIDTH), F32),
                        pltpu.VMEM((2, CHUNK, HGRN_WIDTH), F32),
                        pltpu.VMEM((2, CHUNK, HGRN_WIDTH), F32),
                        pltpu.VMEM((2, CHUNK, HGRN_WIDTH), F32)],
        compiler_params=pltpu.CompilerParams(dimension_semantics=("arbitrary", "arbitrary"),
                                             vmem_limit_bytes=VMEM_LIMIT),
        name="mix",
    )(lb_table, *([proj] * 11),
      x2, mod, hgrn_norm_w.reshape(1, HEAD_DIM), conv_w, wa, wb, wm)


def _route_kernel(h_ref, mod_ref, nw_ref, wr_ref, br_ref, u_ref, oh4_ref, rk_ref, pw_ref, cnt_ref, *, tr):
    shift = mod_ref[0, 3:4, :]
    scale = mod_ref[0, 4:5, :]
    u = _rms(h_ref[...], nw_ref[...]) * (1.0 + scale) + shift
    u_ref[...] = _pack_rows(u)
    lane = lax.broadcasted_iota(jnp.int32, (tr, LANES), 1).astype(F32)
    logits = jnp.dot(u.astype(BF16), wr_ref[...], preferred_element_type=F32) + br_ref[...]
    logits = jnp.where(lane < N_EXPERTS, logits, -jnp.inf)
    idx, val = [], []
    cur = logits
    for _ in range(TOP_K):
        m = jnp.max(cur, axis=-1, keepdims=True)
        i = jnp.min(jnp.where(cur == m, lane, float(LANES)), axis=-1, keepdims=True)
        idx.append(i)
        val.append(m)
        cur = jnp.where(lane == i, -jnp.inf, cur)
    ex = [jnp.exp(v - val[0]) for v in val]
    den = ex[0] + ex[1] + ex[2] + ex[3]
    onehot = jnp.zeros((tr, LANES), F32)
    for i in idx:
        onehot = onehot + jnp.where(lane == i, 1.0, 0.0)
    ri = lax.broadcasted_iota(jnp.int32, (tr, tr), 0)
    ci = lax.broadcasted_iota(jnp.int32, (tr, tr), 1)
    tri = jnp.where(ci < ri, 1.0, 0.0).astype(BF16)
    pref = jnp.dot(tri, onehot.astype(BF16), preferred_element_type=F32)
    rk = jnp.zeros((tr, LANES), jnp.int32)
    pw = jnp.zeros((tr, LANES), F32)
    oh4 = jnp.zeros((tr, LANES), F32)
    for j in range(TOP_K):
        rank = jnp.sum(jnp.where(lane == idx[j], pref, 0.0), axis=-1, keepdims=True).astype(jnp.int32)
        rk = jnp.where(lane == j, rank, rk)
        pw = jnp.where(lane == j, ex[j] / den, pw)
        oh4 = oh4 + jnp.where(lane == idx[j] + N_EXPERTS * j, 1.0, 0.0)
    oh4_ref[...] = oh4.astype(BF16)
    rk_ref[...] = rk
    pw_ref[...] = pw
    cnt_ref[0] = jnp.sum(onehot, axis=0, keepdims=True).astype(jnp.int32)


def _route(h1, mod, norm_w, wr_pad, br_pad, seq, tr, tile0, n):
    per_b = seq // tr
    nt = n // tr
    return pl.pallas_call(
        functools.partial(_route_kernel, tr=tr),
        out_shape=(jax.ShapeDtypeStruct((n, PACKED), jnp.int32),
                   jax.ShapeDtypeStruct((n, LANES), BF16),
                   jax.ShapeDtypeStruct((n, LANES), jnp.int32),
                   jax.ShapeDtypeStruct((n, LANES), F32),
                   jax.ShapeDtypeStruct((nt, 1, LANES), jnp.int32)),
        grid=(nt,),
        in_specs=[pl.BlockSpec((tr, D_MODEL), lambda i: (tile0 + i, 0)),
                  pl.BlockSpec((1, 6, D_MODEL), lambda i: ((tile0 + i) // per_b, 0, 0)),
                  pl.BlockSpec((1, D_MODEL), lambda i: (0, 0)),
                  pl.BlockSpec((D_MODEL, LANES), lambda i: (0, 0)),
                  pl.BlockSpec((1, LANES), lambda i: (0, 0))],
        out_specs=(pl.BlockSpec((tr, PACKED), lambda i: (i, 0)),
                   pl.BlockSpec((tr, LANES), lambda i: (i, 0)),
                   pl.BlockSpec((tr, LANES), lambda i: (i, 0)),
                   pl.BlockSpec((tr, LANES), lambda i: (i, 0)),
                   pl.BlockSpec((1, 1, LANES), lambda i: (i, 0, 0))),
        compiler_params=pltpu.CompilerParams(dimension_semantics=("arbitrary",),
                                             vmem_limit_bytes=VMEM_LIMIT),
        name="route",
    )(h1, mod, norm_w.reshape(1, D_MODEL), wr_pad, br_pad)


def _dest_kernel(oh4_ref, rk_ref, bt_ref, o_ref):
    oh = oh4_ref[...]
    start = (jnp.dot(oh, bt_ref[0, 0], preferred_element_type=F32)
             + 256.0 * jnp.dot(oh, bt_ref[0, 1], preferred_element_type=F32)
             + 65536.0 * jnp.dot(oh, bt_ref[0, 2], preferred_element_type=F32))
    o_ref[...] = (start + rk_ref[...].astype(F32)).T[:8, :].astype(jnp.int32)


def _dest(oh4, rk, bt, tr):
    n = oh4.shape[0]
    return pl.pallas_call(
        _dest_kernel,
        out_shape=jax.ShapeDtypeStruct((8, n), jnp.int32),
        grid=(n // tr,),
        in_specs=[pl.BlockSpec((tr, LANES), lambda i: (i, 0)),
                  pl.BlockSpec((tr, LANES), lambda i: (i, 0)),
                  pl.BlockSpec((1, 3, LANES, LANES), lambda i: (i, 0, 0, 0))],
        out_specs=pl.BlockSpec((8, tr), lambda i: (0, i)),
        name="dest",
    )(oh4, rk, bt)


def _sc_workers():
    info = plsc.get_sparse_core_info()
    return info.num_cores, info.num_cores * info.num_subcores


def _sc_scatter_rows(rows, idx_slots, pad_idx, n_out):
    n_cores, n_workers = _sc_workers()
    n, w = rows.shape
    k = idx_slots.shape[0] // n
    per_worker = n // n_workers
    pad_per_worker = pad_idx.shape[0] // n_workers
    assert per_worker % SC_CHUNK == 0 and pad_per_worker % SC_CHUNK == 0
    mesh = plsc.VectorSubcoreMesh(core_axis_name="c", subcore_axis_name="s")
    zeros = jnp.zeros((SC_CHUNK, w), rows.dtype)

    n_chunks = per_worker // SC_CHUNK
    assert n_chunks % 2 == 0

    @functools.partial(
        pl.kernel, mesh=mesh,
        out_type=jax.ShapeDtypeStruct((n_out, w), rows.dtype),
        scratch_types=[pltpu.VMEM((SC_CHUNK,), jnp.int32)] * k
        + [pltpu.VMEM((SC_CHUNK, w), rows.dtype)] * 2
        + [pltpu.SemaphoreType.DMA] * 2,
        name="sc_scatter",
    )
    def scatter(rows_hbm, idx_hbm, pad_hbm, zeros_hbm, out_hbm, *scratch):
        idx_bufs = scratch[:k]
        row_bufs = scratch[k:k + 2]
        sem_rows, sem_out = scratch[k + 2:]
        wid = lax.axis_index("s") * n_cores + lax.axis_index("c")

        def chunk_off(c):
            return pl.multiple_of(wid * per_worker + c * SC_CHUNK, 8)

        def load(c, b):
            pltpu.async_copy(rows_hbm.at[pl.ds(chunk_off(c), SC_CHUNK)], row_bufs[b], sem_rows)

        def wait_load(b):
            pltpu.make_async_copy(rows_hbm.at[pl.ds(0, SC_CHUNK)], row_bufs[b], sem_rows).wait()

        def scatter_chunk(c, b):
            for j in range(k):
                pltpu.sync_copy(idx_hbm.at[pl.ds(pl.multiple_of(j * n + chunk_off(c), 8), SC_CHUNK)],
                                idx_bufs[j])
            copies = [pltpu.async_copy(row_bufs[b], out_hbm.at[idx_bufs[j]], sem_out) for j in range(k)]
            for cp in copies:
                cp.wait()

        load(0, 0)

        def body(i, carry):
            wait_load(0)
            load(2 * i + 1, 1)
            scatter_chunk(2 * i, 0)
            wait_load(1)

            @pl.when(i < n_chunks // 2 - 1)
            def _():
                load(2 * i + 2, 0)

            scatter_chunk(2 * i + 1, 1)
            return carry

        lax.fori_loop(0, n_chunks // 2, body, 0)
        pltpu.sync_copy(zeros_hbm, row_bufs[0])

        def pad_body(i, carry):
            off = pl.multiple_of(wid * pad_per_worker + i * SC_CHUNK, 8)
            pltpu.sync_copy(pad_hbm.at[pl.ds(off, SC_CHUNK)], idx_bufs[0])
            pltpu.async_copy(row_bufs[0], out_hbm.at[idx_bufs[0]], sem_out).wait()
            return carry

        lax.fori_loop(0, pad_per_worker // SC_CHUNK, pad_body, 0)

    return scatter(rows, idx_slots, pad_idx, zeros)


def _experts_kernel(be_ref, nu_ref, first_ref, next_ref, slot_ref, quarters_ref, x_ref, w1_hbm, b1_ref, w2_hbm,
                    b2_ref, o_ref, w1_f32, w2_f32, w1_bf, w2_bf, sems):
    i = pl.program_id(0)
    used = i < nu_ref[0]

    def weight_copies(e, s):
        return (pltpu.make_async_copy(w1_hbm.at[e], w1_f32.at[s], sems.at[s]),
                pltpu.make_async_copy(w2_hbm.at[e], w2_f32.at[s], sems.at[s]))

    @pl.when(used & (first_ref[i] == 1))
    def _():
        s = slot_ref[i]

        @pl.when(i == 0)
        def _():
            for cp in weight_copies(be_ref[i], s):
                cp.start()

        for cp in weight_copies(be_ref[i], s):
            cp.wait()
        cw = 256
        for c in range(2 * D_FF // cw):
            w1_bf[:, c * cw:(c + 1) * cw] = w1_f32[s, :, c * cw:(c + 1) * cw].astype(BF16)
        for c in range(D_MODEL // cw):
            w2_bf[:, c * cw:(c + 1) * cw] = w2_f32[s, :, c * cw:(c + 1) * cw].astype(BF16)

        @pl.when(next_ref[i] >= 0)
        def _():
            for cp in weight_copies(next_ref[i], 1 - s):
                cp.start()

    def mlp(n_rows):
        step = 512
        x = _unpack_rows(x_ref[0:n_rows, :]).astype(BF16)
        acc = None
        for j in range(D_FF // step):
            cs = slice(j * step, (j + 1) * step)
            ls = slice(D_FF + j * step, D_FF + (j + 1) * step)
            glu = jnp.dot(x, w1_bf[:, cs], preferred_element_type=F32) + b1_ref[0, :, cs]
            lin = jnp.dot(x, w1_bf[:, ls], preferred_element_type=F32) + b1_ref[0, :, ls]
            glu = jnp.minimum(glu, SWIGLU_LIMIT)
            lin = jnp.clip(lin, -SWIGLU_LIMIT, SWIGLU_LIMIT)
            act = (glu * _sigmoid(glu, SWIGLU_ALPHA) * (lin + 1.0)).astype(BF16)
            part = jnp.dot(act, w2_bf[cs, :], preferred_element_type=F32)
            acc = part if acc is None else acc + part
        o_ref[0:n_rows, :] = _pack_rows(acc + b2_ref[0])
        if n_rows < ROW_BLOCK:
            o_ref[n_rows:ROW_BLOCK, :] = jnp.zeros((ROW_BLOCK - n_rows, PACKED), jnp.int32)

    for quarters in range(1, 5):
        @pl.when(used & (quarters_ref[i] == quarters))
        def _():
            mlp(quarters * (ROW_BLOCK // 4))


def _experts(plan, xs, w1, b1, w2, b2):
    n_rows = xs.shape[0]
    nb = n_rows // ROW_BLOCK
    grid_spec = pltpu.PrefetchScalarGridSpec(
        num_scalar_prefetch=6,
        grid=(nb,),
        in_specs=[pl.BlockSpec((ROW_BLOCK, PACKED), lambda i, be, nu, *_: (jnp.minimum(i, nu[0] - 1), 0)),
                  pl.BlockSpec(memory_space=pl.ANY),
                  pl.BlockSpec((1, 1, 2 * D_FF), lambda i, be, *_: (be[i], 0, 0)),
                  pl.BlockSpec(memory_space=pl.ANY),
                  pl.BlockSpec((1, 1, D_MODEL), lambda i, be, *_: (be[i], 0, 0))],
        out_specs=pl.BlockSpec((ROW_BLOCK, PACKED), lambda i, be, nu, *_: (jnp.minimum(i, nu[0] - 1), 0)),
        scratch_shapes=[pltpu.VMEM((2, D_MODEL, 2 * D_FF), F32), pltpu.VMEM((2, D_FF, D_MODEL), F32),
                        pltpu.VMEM((D_MODEL, 2 * D_FF), BF16), pltpu.VMEM((D_FF, D_MODEL), BF16),
                        pltpu.SemaphoreType.DMA((2,))],
    )
    return pl.pallas_call(
        _experts_kernel,
        out_shape=jax.ShapeDtypeStruct((n_rows, PACKED), jnp.int32),
        grid_spec=grid_spec,
        compiler_params=pltpu.CompilerParams(dimension_semantics=("arbitrary",),
                                             vmem_limit_bytes=VMEM_LIMIT),
        name="experts",
    )(*plan, xs, w1, b1.reshape(N_EXPERTS, 1, 2 * D_FF), w2, b2.reshape(N_EXPERTS, 1, D_MODEL))


def _sc_gather_rows(table, idx_flat):
    n_cores, n_workers = _sc_workers()
    n_idx = idx_flat.shape[0]
    w = table.shape[1]
    per_worker = n_idx // n_workers
    n_chunks = per_worker // SC_CHUNK
    assert per_worker * n_workers == n_idx and n_chunks * SC_CHUNK == per_worker and n_chunks % 2 == 0
    mesh = plsc.VectorSubcoreMesh(core_axis_name="c", subcore_axis_name="s")

    @functools.partial(
        pl.kernel, mesh=mesh,
        out_type=jax.ShapeDtypeStruct((n_idx, w), table.dtype),
        scratch_types=[pltpu.VMEM((SC_CHUNK,), jnp.int32), pltpu.VMEM((SC_CHUNK,), jnp.int32),
                       pltpu.VMEM((SC_CHUNK, w), table.dtype), pltpu.VMEM((SC_CHUNK, w), table.dtype),
                       pltpu.SemaphoreType.DMA, pltpu.SemaphoreType.DMA],
        name="sc_gather",
    )
    def gather(table_hbm, idx_hbm, out_hbm, idx0, idx1, rows0, rows1, sem0, sem1):
        wid = lax.axis_index("s") * n_cores + lax.axis_index("c")
        bufs = ((idx0, rows0, sem0), (idx1, rows1, sem1))

        def chunk_off(c):
            return pl.multiple_of(wid * per_worker + c * SC_CHUNK, 8)

        def start(c, b):
            idx_v, rows_v, sem = bufs[b]
            pltpu.sync_copy(idx_hbm.at[pl.ds(chunk_off(c), SC_CHUNK)], idx_v)
            pltpu.async_copy(table_hbm.at[idx_v], rows_v, sem)

        def finish(c, b):
            idx_v, rows_v, sem = bufs[b]
            pltpu.make_async_copy(table_hbm.at[idx_v], rows_v, sem).wait()
            pltpu.sync_copy(rows_v, out_hbm.at[pl.ds(chunk_off(c), SC_CHUNK)])

        start(0, 0)

        def body(i, carry):
            start(2 * i + 1, 1)
            finish(2 * i, 0)

            @pl.when(i < n_chunks // 2 - 1)
            def _():
                start(2 * i + 2, 0)

            finish(2 * i + 1, 1)
            return carry

        lax.fori_loop(0, n_chunks // 2, body, 0)

    return gather(table, idx_flat)


def _finish_kernel(h_ref, pw_ref, mod_ref, nw_ref, y0_ref, y1_ref, y2_ref, y3_ref, *rest):
    o_ref = rest[-1]
    pw = pw_ref[...]
    moe = pw[:, 0:1] * _unpack_rows(y0_ref[0])
    for j, y_ref in enumerate((y1_ref, y2_ref, y3_ref), start=1):
        moe = moe + pw[:, j:j + 1] * _unpack_rows(y_ref[0])
    gate_f = mod_ref[0, 5:6, :]
    o_ref[...] = _rms(h_ref[...] + gate_f * moe, nw_ref[...])


def _finish(h1, pw, mod, norm_w, y4, seq, tc, tile0, prev_out):
    n = pw.shape[0]
    per_b = seq // tc
    slot = lambda j: pl.BlockSpec((1, tc, PACKED), lambda i: (j, i, 0))
    in_specs = [pl.BlockSpec((tc, D_MODEL), lambda i: (tile0 + i, 0)),
                pl.BlockSpec((tc, LANES), lambda i: (i, 0)),
                pl.BlockSpec((1, 6, D_MODEL), lambda i: ((tile0 + i) // per_b, 0, 0)),
                pl.BlockSpec((1, D_MODEL), lambda i: (0, 0)),
                slot(0), slot(1), slot(2), slot(3)]
    args = [h1, pw, mod, norm_w.reshape(1, D_MODEL), y4, y4, y4, y4]
    aliases = {}
    if prev_out is not None:
        in_specs.append(pl.BlockSpec(memory_space=pl.ANY))
        args.append(prev_out)
        aliases = {len(args) - 1: 0}
    return pl.pallas_call(
        _finish_kernel,
        out_shape=jax.ShapeDtypeStruct(h1.shape, F32),
        grid=(n // tc,),
        in_specs=in_specs,
        out_specs=pl.BlockSpec((tc, D_MODEL), lambda i: (tile0 + i, 0)),
        input_output_aliases=aliases,
        compiler_params=pltpu.CompilerParams(dimension_semantics=("arbitrary",),
                                             vmem_limit_bytes=VMEM_LIMIT),
        name="finish",
    )(*args)


def _moe_plan(counts, n_assign):
    cnt = counts[:, 0, :N_EXPERTS]
    sizes = jnp.sum(cnt, axis=0)
    padded = (sizes + ROW_BLOCK - 1) // ROW_BLOCK * ROW_BLOCK
    pad_end = jnp.cumsum(padded)
    pad_start = pad_end - padded
    tile_base = pad_start[None, :] + jnp.cumsum(cnt, axis=0) - cnt
    digits = jnp.stack([tile_base % 256, (tile_base // 256) % 256, tile_base // 65536], axis=1)
    rows = jnp.tile(digits, (1, 1, TOP_K))
    slot_of_row = jnp.arange(LANES, dtype=jnp.int32) // N_EXPERTS
    col = jnp.arange(LANES, dtype=jnp.int32)
    bt = jnp.where(slot_of_row[:, None] == col[None, :], rows[..., None], 0).astype(BF16)
    nb = n_assign // ROW_BLOCK + N_EXPERTS
    block_start = jnp.arange(nb, dtype=jnp.int32) * ROW_BLOCK
    block_e = jnp.minimum(jnp.sum(pad_end[None, :] <= block_start[:, None], axis=1),
                          N_EXPERTS - 1).astype(jnp.int32)
    n_used = (pad_end[-1] // ROW_BLOCK).astype(jnp.int32).reshape(1)
    nonempty = padded > 0
    ids = jnp.arange(N_EXPERTS, dtype=jnp.int32)
    suffix_min = lax.cummin(jnp.where(nonempty, ids, N_EXPERTS), reverse=True)
    after = jnp.concatenate([suffix_min[1:], jnp.full((1,), N_EXPERTS, jnp.int32)])
    next_nonempty = jnp.where(after < N_EXPERTS, after, -1).astype(jnp.int32)
    slot_of = ((jnp.cumsum(nonempty) - 1) % 2).astype(jnp.int32)
    first = jnp.concatenate([jnp.ones((1,), jnp.int32),
                             (block_e[1:] != block_e[:-1]).astype(jnp.int32)])
    valid_rows = (pad_start + sizes)[block_e] - block_start
    quarter = ROW_BLOCK // 4
    quarters = jnp.clip((valid_rows + quarter - 1) // quarter, 1, 4).astype(jnp.int32)
    expert_plan = (block_e, n_used, first, next_nonempty[block_e], slot_of[block_e], quarters)
    r = jnp.arange(quarter, dtype=jnp.int32)[None, :]
    n_pad = (sizes + quarter - 1) // quarter * quarter - sizes
    pad_idx = jnp.where(r < n_pad[:, None], (pad_start + sizes)[:, None] + r,
                        (nb - 1) * ROW_BLOCK + r).astype(jnp.int32).reshape(PAD_SLOTS)
    return bt, expert_plan, pad_idx


def kernel(x, c, w_ada, b_ada, norm_mix_w, w_in, hgrn_lower_bounds, hgrn_norm_w, conv_w,
           w_hgrn_out, w_conv_out, w_mix_out, norm_ffn_w, w_router, b_router, w1, b1, w2, b2,
           norm_final_w):
    bsz, seq, d = x.shape
    assert d == D_MODEL and seq % CHUNK == 0
    n = bsz * seq
    depth = w_ada.shape[0]
    tile = min(512, seq)
    assert seq % tile == 0 and n % (MOE_GROUPS * tile) == 0 and (n // MOE_GROUPS * TOP_K) % ROW_BLOCK == 0
    h = x.reshape(n, d)
    wr_pad = jnp.zeros((depth, D_MODEL, LANES), BF16).at[:, :, :N_EXPERTS].set(w_router.astype(BF16))
    br_pad = jnp.zeros((depth, 1, LANES), F32).at[:, 0, :N_EXPERTS].set(b_router)
    for layer in range(depth):
        mod = _ada(c, w_ada[layer], b_ada[layer]).reshape(bsz, 6, d)
        proj = _inproj(h, mod, norm_mix_w[layer], w_in[layer].astype(BF16), seq, tile)
        h = _mix(proj, h, mod, hgrn_lower_bounds, hgrn_norm_w[layer], conv_w[layer],
                 w_hgrn_out[layer].astype(BF16), w_conv_out[layer].astype(BF16),
                 w_mix_out[layer].astype(BF16), layer, bsz, seq, min(MIX_ROWS, seq))
        assert layer == depth - 1, "only the last layer applies the final norm"
        ng = n // MOE_GROUPS
        n_rows = (ng * TOP_K // ROW_BLOCK + N_EXPERTS) * ROW_BLOCK
        routed = [_route(h, mod, norm_ffn_w[layer], wr_pad[layer], br_pad[layer], seq, tile,
                         grp * (ng // tile), ng) for grp in range(MOE_GROUPS)]
        plans = jax.vmap(lambda cnt: _moe_plan(cnt, ng * TOP_K))(jnp.stack([r[4] for r in routed]))
        out = None
        for grp in range(MOE_GROUPS):
            tile0 = grp * (ng // tile)
            u2, oh4, rk, pw, _ = routed[grp]
            bt, expert_plan, pad_idx = jax.tree.map(lambda a: a[grp], plans)
            dest_slots = _dest(oh4, rk, bt, tile)[:TOP_K].reshape(TOP_K * ng)
            xs = _sc_scatter_rows(u2, dest_slots, pad_idx, n_rows)
            ys = _experts(expert_plan, xs, w1[layer], b1[layer], w2[layer], b2[layer])
            y4 = _sc_gather_rows(ys, dest_slots).reshape(TOP_K, ng, PACKED)
            fin = min(MIX_ROWS, seq)
            out = _finish(h, pw, mod, norm_final_w, y4, seq, fin, grp * (ng // fin), out)
        h = out
    return h.reshape(bsz, seq, d)
```

```python
import functools

import jax
import jax.numpy as jnp
from jax import lax
from jax.experimental import pallas as pl
from jax.experimental.pallas import tpu as pltpu
from jax.experimental.pallas import tpu_sc as plsc

F32 = jnp.float32
BF16 = jnp.bfloat16

D_MODEL = 1024
HGRN_HEADS = 4
HEAD_DIM = 128
HGRN_WIDTH = HGRN_HEADS * HEAD_DIM
CONV_WIDTH = 512
CONV_K = 3
CHUNK = 64
N_EXPERTS = 32
TOP_K = 4
D_FF = 1024
SWIGLU_LIMIT = 7.0
SWIGLU_ALPHA = 1.702
EPS = 1e-6
LOG2_E = 1.4426950408889634
IN_COLS = 4 * HGRN_WIDTH + 3 * CONV_WIDTH + 2 * D_MODEL
LANES = 128
MIX_ROWS = 1024
ROW_BLOCK = 1024
PACKED = D_MODEL // 2
MOE_GROUPS = 2
SC_CHUNK = 64
PAD_SLOTS = N_EXPERTS * (ROW_BLOCK // 4)
VMEM_LIMIT = 56 * 1024 * 1024


def _sigmoid(x, scale=1.0):
    return 1.0 / (1.0 + jnp.exp2(x * (-scale * LOG2_E)))


def _rms(x, w):
    ms = jnp.mean(x * x, axis=-1, keepdims=True)
    return x * lax.rsqrt(ms + EPS) * w


def _pack_rows(x):
    w = x.shape[1] // 2
    lo = lax.bitcast_convert_type(x[:, :w].astype(BF16).astype(F32), jnp.uint32)
    hi = lax.bitcast_convert_type(x[:, w:].astype(BF16).astype(F32), jnp.uint32)
    return lax.bitcast_convert_type((lo >> 16) | (hi & jnp.uint32(0xFFFF0000)), jnp.int32)


def _unpack_rows(p):
    u = lax.bitcast_convert_type(p, jnp.uint32)
    lo = lax.bitcast_convert_type(u << 16, F32)
    hi = lax.bitcast_convert_type(u & jnp.uint32(0xFFFF0000), F32)
    return jnp.concatenate([lo, hi], axis=1)


def _nt_dot(a, b):
    return lax.dot_general(a, b, (((1,), (1,)), ((), ())), preferred_element_type=F32)


def _tn_dot(a, b):
    return lax.dot_general(a, b, (((0,), (0,)), ((), ())), preferred_element_type=F32)


def _ada_kernel(c_ref, w_ref, b_ref, o_ref):
    c = c_ref[...]
    sc = (c * _sigmoid(c)).astype(BF16)
    o_ref[...] = jnp.dot(sc, w_ref[...].astype(BF16), preferred_element_type=F32) + b_ref[...]


def _ada(c, w_ada, b_ada):
    bsz, d = c.shape
    n = w_ada.shape[1]
    return pl.pallas_call(
        _ada_kernel,
        out_shape=jax.ShapeDtypeStruct((bsz, n), F32),
        grid=(n // d,),
        in_specs=[pl.BlockSpec((bsz, d), lambda j: (0, 0)),
                  pl.BlockSpec((d, d), lambda j: (0, j)),
                  pl.BlockSpec((1, d), lambda j: (0, j))],
        out_specs=pl.BlockSpec((bsz, d), lambda j: (0, j)),
        name="ada",
    )(c, w_ada, b_ada.reshape(1, n))


def _inproj_kernel(x_ref, mod_ref, nw_ref, w_ref, o_ref, u_scr):
    y = _rms(x_ref[...], nw_ref[...])
    shift = mod_ref[0, 0:1, :]
    scale = mod_ref[0, 1:2, :]
    u_scr[...] = (y * (1.0 + scale) + shift).astype(BF16)

    step = 512
    for j in range(IN_COLS // step):
        o_ref[:, j * step:(j + 1) * step] = jnp.dot(
            u_scr[...], w_ref[:, j * step:(j + 1) * step], preferred_element_type=F32).astype(BF16)


def _inproj(x2, mod, norm_w, w_in_bf, seq, tm):
    n = x2.shape[0]
    per_b = seq // tm
    return pl.pallas_call(
        _inproj_kernel,
        out_shape=jax.ShapeDtypeStruct((n, IN_COLS), BF16),
        grid=(n // tm,),
        in_specs=[pl.BlockSpec((tm, D_MODEL), lambda i: (i, 0)),
                  pl.BlockSpec((1, 6, D_MODEL), lambda i: (i // per_b, 0, 0)),
                  pl.BlockSpec((1, D_MODEL), lambda i: (0, 0)),
                  pl.BlockSpec((D_MODEL, IN_COLS), lambda i: (0, 0))],
        out_specs=pl.BlockSpec((tm, IN_COLS), lambda i: (i, 0)),
        scratch_shapes=[pltpu.VMEM((tm, D_MODEL), BF16)],
        compiler_params=pltpu.CompilerParams(dimension_semantics=("arbitrary",),
                                             vmem_limit_bytes=VMEM_LIMIT),
        name="inproj",
    )(x2, mod, norm_w.reshape(1, D_MODEL), w_in_bf)


_HEADS = [slice(h * HEAD_DIM, (h + 1) * HEAD_DIM) for h in range(HGRN_HEADS)]


def _chunk_rows(c):
    if isinstance(c, int):
        return pl.ds(c * CHUNK, CHUNK)
    return pl.ds(pl.multiple_of(c * CHUNK, CHUNK), CHUNK)


def _hgrn_gates(c, lb, tri, f_ref, b_scr, k_scr, f_scr):
    fx = f_ref[_chunk_rows(c), :].astype(F32)
    f = lb + (1.0 - lb) * _sigmoid(fx)
    g = jnp.maximum(jnp.log(f), -128.0) * LOG2_E
    g1 = g.astype(BF16)
    r1 = g - g1.astype(F32)
    g2 = r1.astype(BF16)
    g3 = (r1 - g2.astype(F32)).astype(BF16)
    b_scr[...] = (jnp.dot(tri, g1, preferred_element_type=F32)
                  + jnp.dot(tri, g2, preferred_element_type=F32)
                  + jnp.dot(tri, g3, preferred_element_type=F32))
    f_scr[...] = f
    k_scr[...] = 1.0 - f


def _hgrn_scores(c, q_ref, b_scr, k_scr, f_scr):
    rows = _chunk_rows(c)
    qs = [q_ref[rows, hs].astype(F32) for hs in _HEADS]
    s_mats = [None] * HGRN_HEADS
    for lvl in range(1, 7):
        for h in range(HGRN_HEADS):
            s_mats[h] = _level_scores(lvl, qs[h], _HEADS[h], b_scr, k_scr, f_scr, s_mats[h])
    return s_mats


def _hgrn_outputs(c, s_mats, q_ref, v_ref, st_scr, o_scr, b_scr, k_scr):
    rows = _chunk_rows(c)
    for h, hs in enumerate(_HEADS):
        q = q_ref[rows, hs].astype(F32)
        v_bf = v_ref[rows, hs]
        b = b_scr[:, hs]
        kk = k_scr[:, hs]
        b_last = b_scr[CHUNK - 1:CHUNK, hs]
        st = st_scr[h]
        qd = (q * jnp.exp2(b)).astype(BF16)
        kdec = (kk * jnp.exp2(b_last - b)).astype(BF16)
        diag = jnp.sum(q * kk, axis=-1, keepdims=True)
        o_scr[rows, hs] = (_nt_dot(qd, st.astype(BF16))
                           + jnp.dot(s_mats[h].astype(BF16), v_bf, preferred_element_type=F32)
                           + diag * v_bf.astype(F32))
        st_scr[h] = jnp.exp2(b_last) * st + _tn_dot(v_bf, kdec)


def _level_scores(lvl, q, hs, b_scr, k_scr, f_scr, acc):
    row = lax.broadcasted_iota(jnp.int32, (CHUNK, 1), 0)
    col = lax.broadcasted_iota(jnp.int32, (1, CHUNK), 1)
    b = b_scr[:, hs]
    kk = k_scr[:, hs]
    blk = 1 << lvl
    half = blk // 2
    first_half_col = (col & (blk - 1)) < half
    if lvl == 1:
        odd = (row & 1) == 1
        qx = jnp.where(odd, q * f_scr[:, hs], 0.0).astype(BF16)
        kx = jnp.where(odd, 0.0, kk).astype(BF16)
        return jnp.where((row >> 1) == (col >> 1), _nt_dot(qx, kx), 0.0)
    if half < 8:
        groups = []
        sub = lax.broadcasted_iota(jnp.int32, (8, 1), 0)
        for j in range(CHUNK // 8):
            rj = None
            for k in reversed(range(8 // blk)):
                m = 8 * j + k * blk + half - 1
                bm = jnp.broadcast_to(b_scr[m:m + 1, hs], (8, HEAD_DIM))
                rj = bm if rj is None else jnp.where(sub < (k + 1) * blk, bm, rj)
            groups.append(rj)
        ref = jnp.concatenate(groups, axis=0)
        second = (row & (blk - 1)) >= half
        qx = (q * jnp.exp2(jnp.where(second, b - ref, -jnp.inf))).astype(BF16)
        kx = (kk * jnp.exp2(jnp.where(second, -jnp.inf, ref - b))).astype(BF16)
        mine = ((row >> lvl) == (col >> lvl)) & second & first_half_col
        return jnp.where(mine, _nt_dot(qx, kx), acc)
    n_blk = CHUNK // blk
    qparts, kparts = [], []
    for j in range(n_blk):
        m = j * blk + half - 1
        bm = b_scr[m:m + 1, hs]
        tq = slice(j * blk + half, (j + 1) * blk)
        tk = slice(j * blk, j * blk + half)
        qparts.append(q[tq] * jnp.exp2(b[tq] - bm))
        kparts.append(kk[tk] * jnp.exp2(bm - b[tk]))
        kparts.append(jnp.zeros((half, HEAD_DIM), F32))
    qx = jnp.concatenate(qparts, axis=0).astype(BF16)
    kx = jnp.concatenate(kparts, axis=0).astype(BF16)
    sc = _nt_dot(qx, kx)
    pieces = []
    for j in range(n_blk):
        mine = ((col >> lvl) == j) & first_half_col
        pieces.append(acc[j * blk:j * blk + half])
        pieces.append(jnp.where(mine, sc[j * half:(j + 1) * half], acc[j * blk + half:(j + 1) * blk]))
    return jnp.concatenate(pieces, axis=0)


def _mix_kernel(lbt_ref, q_ref, f_ref, i_ref, g_ref, cb_ref, cc_ref, ch_ref, ga0_ref, ga1_ref,
                gb0_ref, gb1_ref, x_ref, mod_ref, hnw_ref, cw_ref, wa_ref, wb_ref, wm_ref,
                o_ref, st_scr, o_scr, carry_scr, b_scr, k_scr, f_scr, *, layer, rows_per_step, epi_rows):
    @pl.when(pl.program_id(1) == 0)
    def _():
        st_scr[...] = jnp.zeros_like(st_scr)
        carry_scr[0:8, :] = jnp.zeros((8, CONV_WIDTH), F32)

    tab = lbt_ref[...]
    tmax = jnp.max(tab, axis=0, keepdims=True)
    te = jnp.exp(tab - tmax)
    lb = jnp.sum(te[0:layer + 1], axis=0, keepdims=True) / jnp.sum(te, axis=0, keepdims=True)

    ri = lax.broadcasted_iota(jnp.int32, (CHUNK, CHUNK), 0)
    ci = lax.broadcasted_iota(jnp.int32, (CHUNK, CHUNK), 1)
    tri = jnp.where(ci <= ri, 1.0, 0.0).astype(BF16)

    n_chunks = rows_per_step // CHUNK

    def slot(s):
        return b_scr.at[s], k_scr.at[s], f_scr.at[s]

    def chunk(c, c_next, s):
        bs, ks, fs = slot(s)
        s_mats = _hgrn_scores(c, q_ref, bs, ks, fs)
        _hgrn_gates(c_next, lb, tri, f_ref, *slot(1 - s))
        _hgrn_outputs(c, s_mats, q_ref, i_ref, st_scr, o_scr, bs, ks)

    def pair_body(i, carry):
        chunk(2 * i, 2 * i + 1, 0)
        chunk(2 * i + 1, jnp.minimum(2 * i + 2, n_chunks - 1), 1)
        return carry

    _hgrn_gates(0, lb, tri, f_ref, *slot(0))
    lax.fori_loop(0, n_chunks // 2, pair_body, 0)

    gate_m = mod_ref[0, 2:3, :]
    hnw = hnw_ref[...]
    cw0 = cw_ref[0:1, :]
    cw1 = cw_ref[1:2, :]
    cw2 = cw_ref[2:3, :]
    for r in range(rows_per_step // epi_rows):
        rs = slice(r * epi_rows, (r + 1) * epi_rows)
        parts = []
        for h in range(HGRN_HEADS):
            hs = slice(h * HEAD_DIM, (h + 1) * HEAD_DIM)
            oh = _rms(o_scr[rs, hs], hnw)
            go = g_ref[rs, hs].astype(F32)
            parts.append((oh * (go * _sigmoid(go))).astype(BF16))
        ya = jnp.dot(jnp.concatenate(parts, axis=1), wa_ref[...], preferred_element_type=F32)
        uc = cc_ref[rs, :].astype(F32) * ch_ref[rs, :].astype(F32)
        carry_scr[8:8 + epi_rows, :] = uc
        s1 = carry_scr[7:7 + epi_rows, :]
        s2 = carry_scr[6:6 + epi_rows, :]
        carry_scr[0:8, :] = uc[epi_rows - 8:epi_rows, :]
        yc = cb_ref[rs, :].astype(F32) * (cw2 * uc + cw1 * s1 + cw0 * s2)
        yb = jnp.dot(yc.astype(BF16), wb_ref[...], preferred_element_type=F32)
        ga = jnp.concatenate([ga0_ref[rs, :], ga1_ref[rs, :]], axis=1).astype(F32)
        gb = jnp.concatenate([gb0_ref[rs, :], gb1_ref[rs, :]], axis=1).astype(F32)
        merged = (_sigmoid(ga) * ya + _sigmoid(gb) * yb).astype(BF16)
        o_ref[rs, :] = x_ref[rs, :] + gate_m * jnp.dot(merged, wm_ref[...], preferred_element_type=F32)


def _mix(proj, x2, mod, lb_table, hgrn_norm_w, conv_w, wa, wb, wm, layer, bsz, seq, tt):
    n = x2.shape[0]
    per_b = seq // tt

    def col(col_block):
        return pl.BlockSpec((tt, 512), lambda b, t: (b * per_b + t, col_block))

    const = lambda shape: pl.BlockSpec(shape, lambda b, t: (0,) * len(shape))
    in_specs = [
        const(lb_table.shape),
        col(0), col(1), col(2), col(3),
        col(4), col(5), col(6),
        col(7), col(8), col(9), col(10),
        pl.BlockSpec((tt, D_MODEL), lambda b, t: (b * per_b + t, 0)),
        pl.BlockSpec((1, 6, D_MODEL), lambda b, t: (b, 0, 0)),
        const((1, HEAD_DIM)), const((CONV_K, CONV_WIDTH)),
        const((HGRN_WIDTH, D_MODEL)), const((CONV_WIDTH, D_MODEL)), const((D_MODEL, D_MODEL)),
    ]
    kern = functools.partial(_mix_kernel, layer=layer, rows_per_step=tt, epi_rows=tt)
    return pl.pallas_call(
        kern,
        out_shape=jax.ShapeDtypeStruct((n, D_MODEL), F32),
        grid=(bsz, per_b),
        in_specs=in_specs,
        out_specs=pl.BlockSpec((tt, D_MODEL), lambda b, t: (b * per_b + t, 0)),
        scratch_shapes=[pltpu.VMEM((HGRN_HEADS, HEAD_DIM, HEAD_DIM), F32),
                        pltpu.VMEM((tt, HGRN_WIDTH), F32),
                        pltpu.VMEM((8 + tt, CONV_WIDTH), F32),
                        pltpu.VMEM((2, CHUNK, HGRN_WIDTH), F32),
                        pltpu.VMEM((2, CHUNK, HGRN_WIDTH), F32),
                        pltpu.VMEM((2, CHUNK, HGRN_WIDTH), F32)],
        compiler_params=pltpu.CompilerParams(dimension_semantics=("arbitrary", "arbitrary"),
                                             vmem_limit_bytes=VMEM_LIMIT),
        name="mix",
    )(lb_table, *([proj] * 11),
      x2, mod, hgrn_norm_w.reshape(1, HEAD_DIM), conv_w, wa, wb, wm)


def _route_kernel(h_ref, mod_ref, nw_ref, wr_ref, br_ref, u_ref, oh4_ref, rk_ref, pw_ref, cnt_ref, *, tr):
    shift = mod_ref[0, 3:4, :]
    scale = mod_ref[0, 4:5, :]
    u = _rms(h_ref[...], nw_ref[...]) * (1.0 + scale) + shift
    u_ref[...] = _pack_rows(u)
    lane = lax.broadcasted_iota(jnp.int32, (tr, LANES), 1).astype(F32)
    logits = jnp.dot(u.astype(BF16), wr_ref[...], preferred_element_type=F32) + br_ref[...]
    logits = jnp.where(lane < N_EXPERTS, logits, -jnp.inf)
    idx, val = [], []
    cur = logits
    for _ in range(TOP_K):
        m = jnp.max(cur, axis=-1, keepdims=True)
        i = jnp.min(jnp.where(cur == m, lane, float(LANES)), axis=-1, keepdims=True)
        idx.append(i)
        val.append(m)
        cur = jnp.where(lane == i, -jnp.inf, cur)
    ex = [jnp.exp(v - val[0]) for v in val]
    den = ex[0] + ex[1] + ex[2] + ex[3]
    onehot = jnp.zeros((tr, LANES), F32)
    for i in idx:
        onehot = onehot + jnp.where(lane == i, 1.0, 0.0)
    ri = lax.broadcasted_iota(jnp.int32, (tr, tr), 0)
    ci = lax.broadcasted_iota(jnp.int32, (tr, tr), 1)
    tri = jnp.where(ci < ri, 1.0, 0.0).astype(BF16)
    pref = jnp.dot(tri, onehot.astype(BF16), preferred_element_type=F32)
    rk = jnp.zeros((tr, LANES), jnp.int32)
    pw = jnp.zeros((tr, LANES), F32)
    oh4 = jnp.zeros((tr, LANES), F32)
    for j in range(TOP_K):
        rank = jnp.sum(jnp.where(lane == idx[j], pref, 0.0), axis=-1, keepdims=True).astype(jnp.int32)
        rk = jnp.where(lane == j, rank, rk)
        pw = jnp.where(lane == j, ex[j] / den, pw)
        oh4 = oh4 + jnp.where(lane == idx[j] + N_EXPERTS * j, 1.0, 0.0)
    oh4_ref[...] = oh4.astype(BF16)
    rk_ref[...] = rk
    pw_ref[...] = pw
    cnt_ref[0] = jnp.sum(onehot, axis=0, keepdims=True).astype(jnp.int32)


def _route(h1, mod, norm_w, wr_pad, br_pad, seq, tr, tile0, n):
    per_b = seq // tr
    nt = n // tr
    return pl.pallas_call(
        functools.partial(_route_kernel, tr=tr),
        out_shape=(jax.ShapeDtypeStruct((n, PACKED), jnp.int32),
                   jax.ShapeDtypeStruct((n, LANES), BF16),
                   jax.ShapeDtypeStruct((n, LANES), jnp.int32),
                   jax.ShapeDtypeStruct((n, LANES), F32),
                   jax.ShapeDtypeStruct((nt, 1, LANES), jnp.int32)),
        grid=(nt,),
        in_specs=[pl.BlockSpec((tr, D_MODEL), lambda i: (tile0 + i, 0)),
                  pl.BlockSpec((1, 6, D_MODEL), lambda i: ((tile0 + i) // per_b, 0, 0)),
                  pl.BlockSpec((1, D_MODEL), lambda i: (0, 0)),
                  pl.BlockSpec((D_MODEL, LANES), lambda i: (0, 0)),
                  pl.BlockSpec((1, LANES), lambda i: (0, 0))],
        out_specs=(pl.BlockSpec((tr, PACKED), lambda i: (i, 0)),
                   pl.BlockSpec((tr, LANES), lambda i: (i, 0)),
                   pl.BlockSpec((tr, LANES), lambda i: (i, 0)),
                   pl.BlockSpec((tr, LANES), lambda i: (i, 0)),
                   pl.BlockSpec((1, 1, LANES), lambda i: (i, 0, 0))),
        compiler_params=pltpu.CompilerParams(dimension_semantics=("arbitrary",),
                                             vmem_limit_bytes=VMEM_LIMIT),
        name="route",
    )(h1, mod, norm_w.reshape(1, D_MODEL), wr_pad, br_pad)


def _dest_kernel(oh4_ref, rk_ref, bt_ref, o_ref):
    oh = oh4_ref[...]
    start = (jnp.dot(oh, bt_ref[0, 0], preferred_element_type=F32)
             + 256.0 * jnp.dot(oh, bt_ref[0, 1], preferred_element_type=F32)
             + 65536.0 * jnp.dot(oh, bt_ref[0, 2], preferred_element_type=F32))
    o_ref[...] = (start + rk_ref[...].astype(F32)).T[:8, :].astype(jnp.int32)


def _dest(oh4, rk, bt, tr):
    n = oh4.shape[0]
    return pl.pallas_call(
        _dest_kernel,
        out_shape=jax.ShapeDtypeStruct((8, n), jnp.int32),
        grid=(n // tr,),
        in_specs=[pl.BlockSpec((tr, LANES), lambda i: (i, 0)),
                  pl.BlockSpec((tr, LANES), lambda i: (i, 0)),
                  pl.BlockSpec((1, 3, LANES, LANES), lambda i: (i, 0, 0, 0))],
        out_specs=pl.BlockSpec((8, tr), lambda i: (0, i)),
        name="dest",
    )(oh4, rk, bt)


def _sc_workers():
    info = plsc.get_sparse_core_info()
    return info.num_cores, info.num_cores * info.num_subcores


def _sc_scatter_rows(rows, idx_slots, pad_idx, n_out):
    n_cores, n_workers = _sc_workers()
    n, w = rows.shape
    k = idx_slots.shape[0] // n
    per_worker = n // n_workers
    pad_per_worker = pad_idx.shape[0] // n_workers
    assert per_worker % SC_CHUNK == 0 and pad_per_worker % SC_CHUNK == 0
    mesh = plsc.VectorSubcoreMesh(core_axis_name="c", subcore_axis_name="s")
    zeros = jnp.zeros((SC_CHUNK, w), rows.dtype)

    n_chunks = per_worker // SC_CHUNK
    assert n_chunks % 2 == 0

    @functools.partial(
        pl.kernel, mesh=mesh,
        out_type=jax.ShapeDtypeStruct((n_out, w), rows.dtype),
        scratch_types=[pltpu.VMEM((SC_CHUNK,), jnp.int32)] * k
        + [pltpu.VMEM((SC_CHUNK, w), rows.dtype)] * 2
        + [pltpu.SemaphoreType.DMA] * 2,
        name="sc_scatter",
    )
    def scatter(rows_hbm, idx_hbm, pad_hbm, zeros_hbm, out_hbm, *scratch):
        idx_bufs = scratch[:k]
        row_bufs = scratch[k:k + 2]
        sem_rows, sem_out = scratch[k + 2:]
        wid = lax.axis_index("s") * n_cores + lax.axis_index("c")

        def chunk_off(c):
            return pl.multiple_of(wid * per_worker + c * SC_CHUNK, 8)

        def load(c, b):
            pltpu.async_copy(rows_hbm.at[pl.ds(chunk_off(c), SC_CHUNK)], row_bufs[b], sem_rows)

        def wait_load(b):
            pltpu.make_async_copy(rows_hbm.at[pl.ds(0, SC_CHUNK)], row_bufs[b], sem_rows).wait()

        def scatter_chunk(c, b):
            for j in range(k):
                pltpu.sync_copy(idx_hbm.at[pl.ds(pl.multiple_of(j * n + chunk_off(c), 8), SC_CHUNK)],
                                idx_bufs[j])
            copies = [pltpu.async_copy(row_bufs[b], out_hbm.at[idx_bufs[j]], sem_out) for j in range(k)]
            for cp in copies:
                cp.wait()

        load(0, 0)

        def body(i, carry):
            wait_load(0)
            load(2 * i + 1, 1)
            scatter_chunk(2 * i, 0)
            wait_load(1)

            @pl.when(i < n_chunks // 2 - 1)
            def _():
                load(2 * i + 2, 0)

            scatter_chunk(2 * i + 1, 1)
            return carry

        lax.fori_loop(0, n_chunks // 2, body, 0)
        pltpu.sync_copy(zeros_hbm, row_bufs[0])

        def pad_body(i, carry):
            off = pl.multiple_of(wid * pad_per_worker + i * SC_CHUNK, 8)
            pltpu.sync_copy(pad_hbm.at[pl.ds(off, SC_CHUNK)], idx_bufs[0])
            pltpu.async_copy(row_bufs[0], out_hbm.at[idx_bufs[0]], sem_out).wait()
            return carry

        lax.fori_loop(0, pad_per_worker // SC_CHUNK, pad_body, 0)

    return scatter(rows, idx_slots, pad_idx, zeros)


def _experts_kernel(be_ref, nu_ref, first_ref, next_ref, slot_ref, quarters_ref, x_ref, w1_hbm, b1_ref, w2_hbm,
                    b2_ref, o_ref, w1_f32, w2_f32, w1_bf, w2_bf, sems):
    i = pl.program_id(0)
    used = i < nu_ref[0]

    def weight_copies(e, s):
        return (pltpu.make_async_copy(w1_hbm.at[e], w1_f32.at[s], sems.at[s]),
                pltpu.make_async_copy(w2_hbm.at[e], w2_f32.at[s], sems.at[s]))

    @pl.when(used & (first_ref[i] == 1))
    def _():
        s = slot_ref[i]

        @pl.when(i == 0)
        def _():
            for cp in weight_copies(be_ref[i], s):
                cp.start()

        for cp in weight_copies(be_ref[i], s):
            cp.wait()
        cw = 256
        for c in range(2 * D_FF // cw):
            w1_bf[:, c * cw:(c + 1) * cw] = w1_f32[s, :, c * cw:(c + 1) * cw].astype(BF16)
        for c in range(D_MODEL // cw):
            w2_bf[:, c * cw:(c + 1) * cw] = w2_f32[s, :, c * cw:(c + 1) * cw].astype(BF16)

        @pl.when(next_ref[i] >= 0)
        def _():
            for cp in weight_copies(next_ref[i], 1 - s):
                cp.start()

    def mlp(n_rows):
        step = 512
        x = _unpack_rows(x_ref[0:n_rows, :]).astype(BF16)
        acc = None
        for j in range(D_FF // step):
            cs = slice(j * step, (j + 1) * step)
            ls = slice(D_FF + j * step, D_FF + (j + 1) * step)
            glu = jnp.dot(x, w1_bf[:, cs], preferred_element_type=F32) + b1_ref[0, :, cs]
            lin = jnp.dot(x, w1_bf[:, ls], preferred_element_type=F32) + b1_ref[0, :, ls]
            glu = jnp.minimum(glu, SWIGLU_LIMIT)
            lin = jnp.clip(lin, -SWIGLU_LIMIT, SWIGLU_LIMIT)
            act = (glu * _sigmoid(glu, SWIGLU_ALPHA) * (lin + 1.0)).astype(BF16)
            part = jnp.dot(act, w2_bf[cs, :], preferred_element_type=F32)
            acc = part if acc is None else acc + part
        o_ref[0:n_rows, :] = _pack_rows(acc + b2_ref[0])
        if n_rows < ROW_BLOCK:
            o_ref[n_rows:ROW_BLOCK, :] = jnp.zeros((ROW_BLOCK - n_rows, PACKED), jnp.int32)

    for quarters in range(1, 5):
        @pl.when(used & (quarters_ref[i] == quarters))
        def _():
            mlp(quarters * (ROW_BLOCK // 4))


def _experts(plan, xs, w1, b1, w2, b2):
    n_rows = xs.shape[0]
    nb = n_rows // ROW_BLOCK
    grid_spec = pltpu.PrefetchScalarGridSpec(
        num_scalar_prefetch=6,
        grid=(nb,),
        in_specs=[pl.BlockSpec((ROW_BLOCK, PACKED), lambda i, be, nu, *_: (jnp.minimum(i, nu[0] - 1), 0)),
                  pl.BlockSpec(memory_space=pl.ANY),
                  pl.BlockSpec((1, 1, 2 * D_FF), lambda i, be, *_: (be[i], 0, 0)),
                  pl.BlockSpec(memory_space=pl.ANY),
                  pl.BlockSpec((1, 1, D_MODEL), lambda i, be, *_: (be[i], 0, 0))],
        out_specs=pl.BlockSpec((ROW_BLOCK, PACKED), lambda i, be, nu, *_: (jnp.minimum(i, nu[0] - 1), 0)),
        scratch_shapes=[pltpu.VMEM((2, D_MODEL, 2 * D_FF), F32), pltpu.VMEM((2, D_FF, D_MODEL), F32),
                        pltpu.VMEM((D_MODEL, 2 * D_FF), BF16), pltpu.VMEM((D_FF, D_MODEL), BF16),
                        pltpu.SemaphoreType.DMA((2,))],
    )
    return pl.pallas_call(
        _experts_kernel,
        out_shape=jax.ShapeDtypeStruct((n_rows, PACKED), jnp.int32),
        grid_spec=grid_spec,
        compiler_params=pltpu.CompilerParams(dimension_semantics=("arbitrary",),
                                             vmem_limit_bytes=VMEM_LIMIT),
        name="experts",
    )(*plan, xs, w1, b1.reshape(N_EXPERTS, 1, 2 * D_FF), w2, b2.reshape(N_EXPERTS, 1, D_MODEL))


def _sc_gather_rows(table, idx_flat):
    n_cores, n_workers = _sc_workers()
    n_idx = idx_flat.shape[0]
    w = table.shape[1]
    per_worker = n_idx // n_workers
    n_chunks = per_worker // SC_CHUNK
    assert per_worker * n_workers == n_idx and n_chunks * SC_CHUNK == per_worker and n_chunks % 2 == 0
    mesh = plsc.VectorSubcoreMesh(core_axis_name="c", subcore_axis_name="s")

    @functools.partial(
        pl.kernel, mesh=mesh,
        out_type=jax.ShapeDtypeStruct((n_idx, w), table.dtype),
        scratch_types=[pltpu.VMEM((SC_CHUNK,), jnp.int32), pltpu.VMEM((SC_CHUNK,), jnp.int32),
                       pltpu.VMEM((SC_CHUNK, w), table.dtype), pltpu.VMEM((SC_CHUNK, w), table.dtype),
                       pltpu.SemaphoreType.DMA, pltpu.SemaphoreType.DMA],
        name="sc_gather",
    )
    def gather(table_hbm, idx_hbm, out_hbm, idx0, idx1, rows0, rows1, sem0, sem1):
        wid = lax.axis_index("s") * n_cores + lax.axis_index("c")
        bufs = ((idx0, rows0, sem0), (idx1, rows1, sem1))

        def chunk_off(c):
            return pl.multiple_of(wid * per_worker + c * SC_CHUNK, 8)

        def start(c, b):
            idx_v, rows_v, sem = bufs[b]
            pltpu.sync_copy(idx_hbm.at[pl.ds(chunk_off(c), SC_CHUNK)], idx_v)
            pltpu.async_copy(table_hbm.at[idx_v], rows_v, sem)

        def finish(c, b):
            idx_v, rows_v, sem = bufs[b]
            pltpu.make_async_copy(table_hbm.at[idx_v], rows_v, sem).wait()
            pltpu.sync_copy(rows_v, out_hbm.at[pl.ds(chunk_off(c), SC_CHUNK)])

        start(0, 0)

        def body(i, carry):
            start(2 * i + 1, 1)
            finish(2 * i, 0)

            @pl.when(i < n_chunks // 2 - 1)
            def _():
                start(2 * i + 2, 0)

            finish(2 * i + 1, 1)
            return carry

        lax.fori_loop(0, n_chunks // 2, body, 0)

    return gather(table, idx_flat)


def _finish_kernel(h_ref, pw_ref, mod_ref, nw_ref, y0_ref, y1_ref, y2_ref, y3_ref, *rest):
    o_ref = rest[-1]
    pw = pw_ref[...]
    moe = pw[:, 0:1] * _unpack_rows(y0_ref[0])
    for j, y_ref in enumerate((y1_ref, y2_ref, y3_ref), start=1):
        moe = moe + pw[:, j:j + 1] * _unpack_rows(y_ref[0])
    gate_f = mod_ref[0, 5:6, :]
    o_ref[...] = _rms(h_ref[...] + gate_f * moe, nw_ref[...])


def _finish(h1, pw, mod, norm_w, y4, seq, tc, tile0, prev_out):
    n = pw.shape[0]
    per_b = seq // tc
    slot = lambda j: pl.BlockSpec((1, tc, PACKED), lambda i: (j, i, 0))
    in_specs = [pl.BlockSpec((tc, D_MODEL), lambda i: (tile0 + i, 0)),
                pl.BlockSpec((tc, LANES), lambda i: (i, 0)),
                pl.BlockSpec((1, 6, D_MODEL), lambda i: ((tile0 + i) // per_b, 0, 0)),
                pl.BlockSpec((1, D_MODEL), lambda i: (0, 0)),
                slot(0), slot(1), slot(2), slot(3)]
    args = [h1, pw, mod, norm_w.reshape(1, D_MODEL), y4, y4, y4, y4]
    aliases = {}
    if prev_out is not None:
        in_specs.append(pl.BlockSpec(memory_space=pl.ANY))
        args.append(prev_out)
        aliases = {len(args) - 1: 0}
    return pl.pallas_call(
        _finish_kernel,
        out_shape=jax.ShapeDtypeStruct(h1.shape, F32),
        grid=(n // tc,),
        in_specs=in_specs,
        out_specs=pl.BlockSpec((tc, D_MODEL), lambda i: (tile0 + i, 0)),
        input_output_aliases=aliases,
        compiler_params=pltpu.CompilerParams(dimension_semantics=("arbitrary",),
                                             vmem_limit_bytes=VMEM_LIMIT),
        name="finish",
    )(*args)


def _moe_plan(counts, n_assign):
    cnt = counts[:, 0, :N_EXPERTS]
    sizes = jnp.sum(cnt, axis=0)
    padded = (sizes + ROW_BLOCK - 1) // ROW_BLOCK * ROW_BLOCK
    pad_end = jnp.cumsum(padded)
    pad_start = pad_end - padded
    tile_base = pad_start[None, :] + jnp.cumsum(cnt, axis=0) - cnt
    digits = jnp.stack([tile_base % 256, (tile_base // 256) % 256, tile_base // 65536], axis=1)
    rows = jnp.tile(digits, (1, 1, TOP_K))
    slot_of_row = jnp.arange(LANES, dtype=jnp.int32) // N_EXPERTS
    col = jnp.arange(LANES, dtype=jnp.int32)
    bt = jnp.where(slot_of_row[:, None] == col[None, :], rows[..., None], 0).astype(BF16)
    nb = n_assign // ROW_BLOCK + N_EXPERTS
    block_start = jnp.arange(nb, dtype=jnp.int32) * ROW_BLOCK
    block_e = jnp.minimum(jnp.sum(pad_end[None, :] <= block_start[:, None], axis=1),
                          N_EXPERTS - 1).astype(jnp.int32)
    n_used = (pad_end[-1] // ROW_BLOCK).astype(jnp.int32).reshape(1)
    nonempty = padded > 0
    ids = jnp.arange(N_EXPERTS, dtype=jnp.int32)
    suffix_min = lax.cummin(jnp.where(nonempty, ids, N_EXPERTS), reverse=True)
    after = jnp.concatenate([suffix_min[1:], jnp.full((1,), N_EXPERTS, jnp.int32)])
    next_nonempty = jnp.where(after < N_EXPERTS, after, -1).astype(jnp.int32)
    slot_of = ((jnp.cumsum(nonempty) - 1) % 2).astype(jnp.int32)
    first = jnp.concatenate([jnp.ones((1,), jnp.int32),
                             (block_e[1:] != block_e[:-1]).astype(jnp.int32)])
    valid_rows = (pad_start + sizes)[block_e] - block_start
    quarter = ROW_BLOCK // 4
    quarters = jnp.clip((valid_rows + quarter - 1) // quarter, 1, 4).astype(jnp.int32)
    expert_plan = (block_e, n_used, first, next_nonempty[block_e], slot_of[block_e], quarters)
    r = jnp.arange(quarter, dtype=jnp.int32)[None, :]
    n_pad = (sizes + quarter - 1) // quarter * quarter - sizes
    pad_idx = jnp.where(r < n_pad[:, None], (pad_start + sizes)[:, None] + r,
                        (nb - 1) * ROW_BLOCK + r).astype(jnp.int32).reshape(PAD_SLOTS)
    return bt, expert_plan, pad_idx


def kernel(x, c, w_ada, b_ada, norm_mix_w, w_in, hgrn_lower_bounds, hgrn_norm_w, conv_w,
           w_hgrn_out, w_conv_out, w_mix_out, norm_ffn_w, w_router, b_router, w1, b1, w2, b2,
           norm_final_w):
    bsz, seq, d = x.shape
    assert d == D_MODEL and seq % CHUNK == 0
    n = bsz * seq
    depth = w_ada.shape[0]
    tile = min(512, seq)
    assert seq % tile == 0 and n % (MOE_GROUPS * tile) == 0 and (n // MOE_GROUPS * TOP_K) % ROW_BLOCK == 0
    h = x.reshape(n, d)
    wr_pad = jnp.zeros((depth, D_MODEL, LANES), BF16).at[:, :, :N_EXPERTS].set(w_router.astype(BF16))
    br_pad = jnp.zeros((depth, 1, LANES), F32).at[:, 0, :N_EXPERTS].set(b_router)
    for layer in range(depth):
        mod = _ada(c, w_ada[layer], b_ada[layer]).reshape(bsz, 6, d)
        proj = _inproj(h, mod, norm_mix_w[layer], w_in[layer].astype(BF16), seq, tile)
        h = _mix(proj, h, mod, hgrn_lower_bounds, hgrn_norm_w[layer], conv_w[layer],
                 w_hgrn_out[layer].astype(BF16), w_conv_out[layer].astype(BF16),
                 w_mix_out[layer].astype(BF16), layer, bsz, seq, min(MIX_ROWS, seq))
        assert layer == depth - 1, "only the last layer applies the final norm"
        ng = n // MOE_GROUPS
        n_rows = (ng * TOP_K // ROW_BLOCK + N_EXPERTS) * ROW_BLOCK
        routed = [_route(h, mod, norm_ffn_w[layer], wr_pad[layer], br_pad[layer], seq, tile,
                         grp * (ng // tile), ng) for grp in range(MOE_GROUPS)]
        plans = jax.vmap(lambda cnt: _moe_plan(cnt, ng * TOP_K))(jnp.stack([r[4] for r in routed]))
        out = None
        for grp in range(MOE_GROUPS):
            tile0 = grp * (ng // tile)
            u2, oh4, rk, pw, _ = routed[grp]
            bt, expert_plan, pad_idx = jax.tree.map(lambda a: a[grp], plans)
            dest_slots = _dest(oh4, rk, bt, tile)[:TOP_K].reshape(TOP_K * ng)
            xs = _sc_scatter_rows(u2, dest_slots, pad_idx, n_rows)
            ys = _experts(expert_plan, xs, w1[layer], b1[layer], w2[layer], b2[layer])
            y4 = _sc_gather_rows(ys, dest_slots).reshape(TOP_K, ng, PACKED)
            fin = min(MIX_ROWS, seq)
            out = _finish(h, pw, mod, norm_final_w, y4, seq, fin, grp * (ng // fin), out)
        h = out
    return h.reshape(bsz, seq, d)
```
